```python
import math
import jax
import jax.numpy as jnp
from jax import lax
import numpy as np

D_MODEL = 2048
BATCH = 8
SEQ = 2048
DEPTH = 2

GRID_W = 64
CTX_LEN = 256
HEAD_DIM = 128
ROPE_THETA = 10000.0
NORM_EPS = 1e-6
Q_BLOCK = 128

A_HEADS = 8
A_KV_HEADS = 2
B_HEADS = 8
B_Q_LORA = 512
B_KV_LORA = 512
B_NOPE = 128
B_ROPE = 64
B_V = 128
C_HEADS = 8
C_HD = 64
C_V = 2 * C_HD
D_HEADS = 8
D_KV_HEADS = 2
WINDOW = 128

EVEN_SIZES = [A_HEADS * HEAD_DIM, A_KV_HEADS * HEAD_DIM, A_KV_HEADS * HEAD_DIM, B_Q_LORA, B_KV_LORA, B_ROPE]
EVEN_IN = sum(EVEN_SIZES)
EVEN_MIX = A_HEADS * HEAD_DIM + B_HEADS * B_V
ODD_SIZES = [C_HEADS * 2 * C_HD, C_HEADS * 2 * C_HD, C_HEADS * C_V, D_HEADS * HEAD_DIM, D_KV_HEADS * HEAD_DIM, D_KV_HEADS * HEAD_DIM]
ODD_IN = sum(ODD_SIZES)
ODD_MIX = C_HEADS * C_V + D_HEADS * HEAD_DIM

N_GROUPS = 4
EXPERTS_PER_GROUP = 8
N_EXPERTS = N_GROUPS * EXPERTS_PER_GROUP
TOP_K_IN_GROUP = 2
D_EXPERT = 1024
MOE_BLOCK = 128

N_EVEN = (DEPTH + 1) // 2
N_ODD = DEPTH // 2

kernel_name = 'hybrid_diffusion_trunk'


def rms_norm(x, gain=None):
    xf = x.astype(jnp.float32)
    y = xf * lax.rsqrt(jnp.mean(xf * xf, axis=-1, keepdims=True) + NORM_EPS)
    if gain is not None:
        y = y * gain.astype(jnp.float32)
    return y.astype(x.dtype)


def modulate(x, shift, scale):
    return rms_norm(x) * (1 + scale) + shift


def split_cols(p, sizes):
    return jnp.split(p, np.cumsum(sizes)[:-1].tolist(), axis=-1)


def axial_rope_tables(row, col, dim):
    n_freq = dim // 4
    inv = 1.0 / (ROPE_THETA ** (jnp.arange(n_freq, dtype=jnp.float32) / n_freq))
    ang_r = row.astype(jnp.float32)[:, None] * inv
    ang_c = col.astype(jnp.float32)[:, None] * inv
    ang = jnp.concatenate([ang_r, ang_r, ang_c, ang_c], axis=-1)
    return jnp.cos(ang), jnp.sin(ang)


def rotate_half(u):
    u1, u2 = jnp.split(u, 2, axis=-1)
    return jnp.concatenate([-u2, u1], axis=-1)


def apply_axial_rope(x, cos, sin):
    xr, xc = jnp.split(x, 2, axis=-1)
    rot = jnp.concatenate([rotate_half(xr), rotate_half(xc)], axis=-1)
    y = x.astype(jnp.float32) * cos[:, None, :] + rot.astype(jnp.float32) * sin[:, None, :]
    return y.astype(x.dtype)


def sweep_query_blocks(fn, *qs):
    bsz, seq = qs[0].shape[:2]
    nb = seq // Q_BLOCK
    blocks = tuple(jnp.moveaxis(q.reshape((bsz, nb, Q_BLOCK) + q.shape[2:]), 1, 0) for q in qs)
    out = lax.map(lambda args: fn(*args), blocks)
    out = jnp.moveaxis(out, 0, 1)
    return out.reshape((bsz, seq) + out.shape[3:])


def gqa_attend(q, k, v, scale):
    bsz, nq, nh, d = q.shape
    kvh = k.shape[2]
    qg = q.reshape(bsz, nq, kvh, nh // kvh, d)
    s = jnp.einsum('bqkgd,btkd->bkgqt', qg, k).astype(jnp.float32) * scale
    p = jax.nn.softmax(s, axis=-1).astype(v.dtype)
    o = jnp.einsum('bkgqt,btkd->bqkgd', p, v)
    return o.reshape(bsz, nq, nh, v.shape[-1])


def diff_attend(q, k, v, lam, scale):
    s = jnp.einsum('bqhcd,bthcd->bhcqt', q, k).astype(jnp.float32) * scale
    p = jax.nn.softmax(s, axis=-1)
    pd = (p[:, :, 0] - lam * p[:, :, 1]).astype(v.dtype)
    return jnp.einsum('bhqt,bthd->bqhd', pd, v)


def window_sink_attend(q, k, v, kc, vc, sink, scale):
    bsz, seq, nh, d = q.shape
    kvh = k.shape[2]
    g = nh // kvh
    nb = seq // Q_BLOCK
    n_ctx = kc.shape[1]
    qb = q.reshape(bsz, nb, Q_BLOCK, kvh, g, d)

    def band(u):
        ub = u.reshape(bsz, nb, Q_BLOCK, kvh, u.shape[-1])
        up = jnp.pad(ub, ((0, 0), (1, 1), (0, 0), (0, 0), (0, 0)))
        return jnp.concatenate([up[:, :-2], up[:, 1:-1], up[:, 2:]], axis=2)

    kband, vband = band(k), band(v)
    qi = jnp.arange(Q_BLOCK)
    kj = jnp.arange(3 * Q_BLOCK) - Q_BLOCK
    near = jnp.abs(kj[None, :] - qi[:, None]) <= WINDOW
    absj = jnp.arange(nb)[:, None] * Q_BLOCK + kj[None, :]
    inside = (absj >= 0) & (absj < seq)
    mask = near[None, :, :] & inside[:, None, :]
    s_loc = jnp.einsum('bnqkgd,bntkd->bnkgqt', qb, kband).astype(jnp.float32) * scale
    s_loc = jnp.where(mask[None, :, None, None], s_loc, -jnp.inf)
    s_ctx = jnp.einsum('bnqkgd,btkd->bnkgqt', qb, kc).astype(jnp.float32) * scale
    s_sink = jnp.broadcast_to(sink.astype(jnp.float32).reshape(1, 1, kvh, g, 1, 1), s_ctx.shape[:-1] + (1,))
    p = jax.nn.softmax(jnp.concatenate([s_ctx, s_loc, s_sink], axis=-1), axis=-1).astype(v.dtype)
    o = (jnp.einsum('bnkgqt,btkd->bnqkgd', p[..., :n_ctx], vc)
         + jnp.einsum('bnkgqt,bntkd->bnqkgd', p[..., n_ctx:n_ctx + 3 * Q_BLOCK], vband))
    return o.reshape(bsz, seq, nh, v.shape[-1])


def ctx_sink_attend(q, kc, vc, sink, scale):
    bsz, n_ctx, nh, d = q.shape
    kvh = kc.shape[2]
    g = nh // kvh
    qg = q.reshape(bsz, n_ctx, kvh, g, d)
    s = jnp.einsum('bqkgd,btkd->bkgqt', qg, kc).astype(jnp.float32) * scale
    s_sink = jnp.broadcast_to(sink.astype(jnp.float32).reshape(1, kvh, g, 1, 1), s.shape[:-1] + (1,))
    p = jax.nn.softmax(jnp.concatenate([s, s_sink], axis=-1), axis=-1)[..., :-1].astype(vc.dtype)
    o = jnp.einsum('bkgqt,btkd->bqkgd', p, vc)
    return o.reshape(bsz, n_ctx, nh, vc.shape[-1])


def even_mixer(ax, ac, w_in, w_out, a_qn, a_kn, b_cqn, b_wuq, b_ckvn, b_wukv, b_qn, b_kn, rope_128, rope_64, with_ctx):
    scale_a = HEAD_DIM ** -0.5
    scale_b = (B_NOPE + B_ROPE) ** -0.5

    def project(h, use_rope):
        bsz, t, _ = h.shape
        qa, ka, va, cq, ckv, kr = split_cols(h @ w_in, EVEN_SIZES)
        qa = rms_norm(qa.reshape(bsz, t, A_HEADS, HEAD_DIM), a_qn)
        ka = rms_norm(ka.reshape(bsz, t, A_KV_HEADS, HEAD_DIM), a_kn)
        va = va.reshape(bsz, t, A_KV_HEADS, HEAD_DIM)
        qb = (rms_norm(cq, b_cqn) @ b_wuq).reshape(bsz, t, B_HEADS, B_NOPE + B_ROPE)
        kv = (rms_norm(ckv, b_ckvn) @ b_wukv).reshape(bsz, t, B_HEADS, B_NOPE + B_V)
        kn, vb = jnp.split(kv, [B_NOPE], axis=-1)
        kb = jnp.concatenate([kn, jnp.broadcast_to(kr[:, :, None, :], (bsz, t, B_HEADS, B_ROPE))], axis=-1)
        qb = rms_norm(qb, b_qn)
        kb = rms_norm(kb, b_kn)
        if use_rope:
            qa = apply_axial_rope(qa, *rope_128)
            ka = apply_axial_rope(ka, *rope_128)
            qb = jnp.concatenate([qb[..., :B_NOPE], apply_axial_rope(qb[..., B_NOPE:], *rope_64)], axis=-1)
            kb = jnp.concatenate([kb[..., :B_NOPE], apply_axial_rope(kb[..., B_NOPE:], *rope_64)], axis=-1)
        return qa, ka, va, qb, kb, vb

    qax, kax, vax, qbx, kbx, vbx = project(ax, True)
    qac, kac, vac, qbc, kbc, vbc = project(ac, False)
    ka_all = jnp.concatenate([kac, kax], axis=1)
    va_all = jnp.concatenate([vac, vax], axis=1)
    kb_all = jnp.concatenate([kbc, kbx], axis=1)
    vb_all = jnp.concatenate([vbc, vbx], axis=1)

    def block(qa_blk, qb_blk):
        oa = gqa_attend(qa_blk, ka_all, va_all, scale_a)
        ob = gqa_attend(qb_blk, kb_all, vb_all, scale_b)
        return jnp.concatenate([oa.reshape(oa.shape[:2] + (-1,)), ob.reshape(ob.shape[:2] + (-1,))], axis=-1)

    ox = sweep_query_blocks(block, qax, qbx) @ w_out
    oc = None
    if with_ctx:
        oa = gqa_attend(qac, kac, vac, scale_a)
        ob = gqa_attend(qbc, kbc, vbc, scale_b)
        oc = jnp.concatenate([oa.reshape(oa.shape[:2] + (-1,)), ob.reshape(ob.shape[:2] + (-1,))], axis=-1) @ w_out
    return ox, oc


def odd_mixer(ax, ac, w_in, w_out, c_qn, c_kn, lq1, lk1, lq2, lk2, c_subln, d_qn, d_kn, d_sink,
              lambda_init, rope_64, rope_128, with_ctx):
    scale_c = C_HD ** -0.5
    scale_d = HEAD_DIM ** -0.5

    def project(h, use_rope):
        bsz, t, _ = h.shape
        qc, kc, vc, qd, kd, vd = split_cols(h @ w_in, ODD_SIZES)
        qc = rms_norm(qc.reshape(bsz, t, 2 * C_HEADS, C_HD), c_qn)
        kc = rms_norm(kc.reshape(bsz, t, 2 * C_HEADS, C_HD), c_kn)
        qd = rms_norm(qd.reshape(bsz, t, D_HEADS, HEAD_DIM), d_qn)
        kd = rms_norm(kd.reshape(bsz, t, D_KV_HEADS, HEAD_DIM), d_kn)
        if use_rope:
            qc = apply_axial_rope(qc, *rope_64)
            kc = apply_axial_rope(kc, *rope_64)
            qd = apply_axial_rope(qd, *rope_128)
            kd = apply_axial_rope(kd, *rope_128)
        return (qc.reshape(bsz, t, C_HEADS, 2, C_HD), kc.reshape(bsz, t, C_HEADS, 2, C_HD),
                vc.reshape(bsz, t, C_HEADS, C_V), qd, kd, vd.reshape(bsz, t, D_KV_HEADS, HEAD_DIM))

    lam = (jnp.exp(jnp.sum(lq1.astype(jnp.float32) * lk1.astype(jnp.float32)))
           - jnp.exp(jnp.sum(lq2.astype(jnp.float32) * lk2.astype(jnp.float32))) + lambda_init)
    qcx, kcx, vcx, qdx, kdx, vdx = project(ax, True)
    qcc, kcc, vcc, qdc, kdc, vdc = project(ac, False)
    kc_all = jnp.concatenate([kcc, kcx], axis=1)
    vc_all = jnp.concatenate([vcc, vcx], axis=1)

    def diff_head_out(o):
        o = rms_norm(o, c_subln) * (1 - lambda_init)
        return o.reshape(o.shape[:2] + (-1,))

    def diff_block(q_blk):
        return diff_head_out(diff_attend(q_blk, kc_all, vc_all, lam, scale_c))

    oc_x = sweep_query_blocks(diff_block, qcx)
    od_x = window_sink_attend(qdx, kdx, vdx, kdc, vdc, d_sink, scale_d)
    ox = jnp.concatenate([oc_x, od_x.reshape(od_x.shape[:2] + (-1,))], axis=-1) @ w_out
    oc = None
    if with_ctx:
        oc_c = diff_head_out(diff_attend(qcc, kcc, vcc, lam, scale_c))
        od_c = ctx_sink_attend(qdc, kdc, vdc, d_sink, scale_d)
        oc = jnp.concatenate([oc_c, od_c.reshape(od_c.shape[:2] + (-1,))], axis=-1) @ w_out
    return ox, oc


def routed_experts(h, expert_idx, gates, w_gate, w_up, w_down):
    n_tok, d = h.shape
    k = expert_idx.shape[1]
    n_assign = n_tok * k
    flat_e = expert_idx.reshape(-1)
    order = jnp.argsort(flat_e)
    sorted_e = flat_e[order]
    tok = order // k
    counts = jnp.bincount(flat_e, length=N_EXPERTS)
    padded = (counts + MOE_BLOCK - 1) // MOE_BLOCK * MOE_BLOCK
    start = jnp.cumsum(counts) - counts
    pad_end = jnp.cumsum(padded)
    pad_start = pad_end - padded
    dest = pad_start[sorted_e] + jnp.arange(n_assign) - start[sorted_e]
    n_blocks = -(-n_assign // MOE_BLOCK) + N_EXPERTS
    slot_tok = jnp.full((n_blocks * MOE_BLOCK,), n_tok, jnp.int32).at[dest].set(tok.astype(jnp.int32))
    block_expert = jnp.minimum(jnp.searchsorted(pad_end, jnp.arange(n_blocks) * MOE_BLOCK, side='right'), N_EXPERTS - 1)
    h_pad = jnp.concatenate([h, jnp.zeros((1, d), h.dtype)], axis=0)

    def expert_block(args):
        toks, e = args
        xb = h_pad[toks]
        u = jax.nn.silu(xb @ w_gate[e]) * (xb @ w_up[e])
        return u @ w_down[e]

    y_slots = lax.map(expert_block, (slot_tok.reshape(n_blocks, MOE_BLOCK), block_expert)).reshape(-1, d)
    w_sorted = gates.reshape(-1)[order].astype(h.dtype)
    return jnp.zeros_like(h).at[tok].add(y_slots[dest] * w_sorted[:, None])


def hier_moe(h, wr_g, br_g, wr_e, br_e, w_gate, w_up, w_down):
    n_tok = h.shape[0]
    p_group = jax.nn.softmax((h @ wr_g).astype(jnp.float32) + br_g.astype(jnp.float32), axis=-1)
    p_top, g_idx = lax.top_k(p_group, 1)
    e_logits = ((h @ wr_e).astype(jnp.float32) + br_e.astype(jnp.float32)).reshape(n_tok, N_GROUPS, EXPERTS_PER_GROUP)
    e_in = e_logits[jnp.arange(n_tok), g_idx[:, 0]]
    l_top, e_loc = lax.top_k(e_in, TOP_K_IN_GROUP)
    gates = p_top * jax.nn.softmax(l_top, axis=-1)
    expert_idx = g_idx * EXPERTS_PER_GROUP + e_loc
    return routed_experts(h, expert_idx, gates, w_gate, w_up, w_down)


def setup_inputs(seed: int = 0) -> dict:
    key = jax.random.key(seed)
    keys = jax.random.split(key, 64)
    counter = [0]

    def normal(shape, scale):
        counter[0] += 1
        return jax.random.normal(keys[counter[0] - 1], shape, dtype=jnp.float32) * scale

    def gain(shape):
        return 1.0 + normal(shape, 0.05)

    d = D_MODEL
    return {
        'x': normal((BATCH, SEQ, d), 1.0),
        'c': normal((BATCH, d), 1.0),
        'ctx': normal((BATCH, CTX_LEN, d), 1.0),
        'c_ctx': normal((d,), 1.0),
        'mod_w': normal((DEPTH, d, 6 * d), 0.5 * d ** -0.5),
        'mod_b': normal((DEPTH, 6 * d), 0.01),
        'even_w_in': normal((N_EVEN, d, EVEN_IN), d ** -0.5),
        'even_w_out': normal((N_EVEN, EVEN_MIX, d), EVEN_MIX ** -0.5),
        'a_q_norm': gain((N_EVEN, HEAD_DIM)),
        'a_k_norm': gain((N_EVEN, HEAD_DIM)),
        'b_cq_norm': gain((N_EVEN, B_Q_LORA)),
        'b_w_uq': normal((N_EVEN, B_Q_LORA, B_HEADS * (B_NOPE + B_ROPE)), B_Q_LORA ** -0.5),
        'b_ckv_norm': gain((N_EVEN, B_KV_LORA)),
        'b_w_ukv': normal((N_EVEN, B_KV_LORA, B_HEADS * (B_NOPE + B_V)), B_KV_LORA ** -0.5),
        'b_q_norm': gain((N_EVEN, B_NOPE + B_ROPE)),
        'b_k_norm': gain((N_EVEN, B_NOPE + B_ROPE)),
        'odd_w_in': normal((N_ODD, d, ODD_IN), d ** -0.5),
        'odd_w_out': normal((N_ODD, ODD_MIX, d), ODD_MIX ** -0.5),
        'c_q_norm': gain((N_ODD, C_HD)),
        'c_k_norm': gain((N_ODD, C_HD)),
        'c_lambda_q1': normal((N_ODD, C_HD), 0.1),
        'c_lambda_k1': normal((N_ODD, C_HD), 0.1),
        'c_lambda_q2': normal((N_ODD, C_HD), 0.1),
        'c_lambda_k2': normal((N_ODD, C_HD), 0.1),
        'c_subln': gain((N_ODD, C_V)),
        'd_q_norm': gain((N_ODD, HEAD_DIM)),
        'd_k_norm': gain((N_ODD, HEAD_DIM)),
        'd_sink': normal((N_ODD, D_HEADS), 0.5),
        'moe_wr_group': normal((DEPTH, d, N_GROUPS), d ** -0.5),
        'moe_br_group': normal((DEPTH, N_GROUPS), 0.01),
        'moe_wr_expert': normal((DEPTH, d, N_EXPERTS), d ** -0.5),
        'moe_br_expert': normal((DEPTH, N_EXPERTS), 0.01),
        'moe_w_gate': normal((DEPTH, N_EXPERTS, d, D_EXPERT), d ** -0.5),
        'moe_w_up': normal((DEPTH, N_EXPERTS, d, D_EXPERT), d ** -0.5),
        'moe_w_down': normal((DEPTH, N_EXPERTS, D_EXPERT, d), D_EXPERT ** -0.5),
    }


def reference(x, c, ctx, c_ctx, mod_w, mod_b, even_w_in, even_w_out, a_q_norm, a_k_norm, b_cq_norm, b_w_uq,
              b_ckv_norm, b_w_ukv, b_q_norm, b_k_norm, odd_w_in, odd_w_out, c_q_norm, c_k_norm, c_lambda_q1,
              c_lambda_k1, c_lambda_q2, c_lambda_k2, c_subln, d_q_norm, d_k_norm, d_sink, moe_wr_group,
              moe_br_group, moe_wr_expert, moe_br_expert, moe_w_gate, moe_w_up, moe_w_down):
    bsz, seq, d = x.shape
    rows = seq // GRID_W
    row = jnp.repeat(jnp.arange(rows), GRID_W)
    col = jnp.tile(jnp.arange(GRID_W), rows)
    rope_128 = axial_rope_tables(row, col, HEAD_DIM)
    rope_64 = axial_rope_tables(row, col, B_ROPE)

    h_x, h_c = x, ctx
    for layer in range(DEPTH):
        last = layer == DEPTH - 1
        i = layer // 2
        mod_x = jax.nn.silu(c) @ mod_w[layer] + mod_b[layer]
        mod_c = jax.nn.silu(c_ctx) @ mod_w[layer] + mod_b[layer]
        sh1, sc1, g1, sh2, sc2, g2 = jnp.split(mod_x[:, None, :], 6, axis=-1)
        csh1, csc1, cg1, csh2, csc2, cg2 = jnp.split(mod_c, 6, axis=-1)

        ax = modulate(h_x, sh1, sc1)
        ac = modulate(h_c, csh1, csc1)
        if layer % 2 == 0:
            ox, oc = even_mixer(ax, ac, even_w_in[i], even_w_out[i], a_q_norm[i], a_k_norm[i], b_cq_norm[i],
                                b_w_uq[i], b_ckv_norm[i], b_w_ukv[i], b_q_norm[i], b_k_norm[i],
                                rope_128, rope_64, not last)
        else:
            lambda_init = 0.8 - 0.6 * math.exp(-0.3 * layer)
            ox, oc = odd_mixer(ax, ac, odd_w_in[i], odd_w_out[i], c_q_norm[i], c_k_norm[i], c_lambda_q1[i],
                               c_lambda_k1[i], c_lambda_q2[i], c_lambda_k2[i], c_subln[i], d_q_norm[i],
                               d_k_norm[i], d_sink[i], lambda_init, rope_64, rope_128, not last)
        h_x = h_x + g1 * ox
        fx = modulate(h_x, sh2, sc2)
        moe_args = (moe_wr_group[layer], moe_br_group[layer], moe_wr_expert[layer], moe_br_expert[layer],
                    moe_w_gate[layer], moe_w_up[layer], moe_w_down[layer])
        if last:
            yx = hier_moe(fx.reshape(-1, d), *moe_args)
            h_x = h_x + g2 * yx.reshape(bsz, seq, d)
        else:
            h_c = h_c + cg1 * oc
            fc = modulate(h_c, csh2, csc2)
            y = hier_moe(jnp.concatenate([fx.reshape(-1, d), fc.reshape(-1, d)], axis=0), *moe_args)
            h_x = h_x + g2 * y[:bsz * seq].reshape(bsz, seq, d)
            h_c = h_c + cg2 * y[bsz * seq:].reshape(h_c.shape)
    return h_x
```

```python
import functools
import math

import jax
import jax.numpy as jnp
from jax import lax
from jax.experimental import pallas as pl
from jax.experimental.pallas import tpu as pltpu

F32 = jnp.float32
BF16 = jnp.bfloat16

D_MODEL = 2048
BATCH = 8
SEQ = 2048
DEPTH = 2
GRID_W = 64
CTX_LEN = 256
TOK = SEQ + CTX_LEN
HEAD_DIM = 128
ROPE_THETA = 10000.0
NORM_EPS = 1e-6
A_HEADS = 8
A_KV_HEADS = 2
B_HEADS = 8
B_Q_LORA = 512
B_KV_LORA = 512
B_NOPE = 128
B_ROPE = 64
B_V = 128
B_QK_PAD = 256
C_HEADS = 8
C_HD = 64
C_V = 128
D_HEADS = 8
D_KV_HEADS = 2
WINDOW = 128
N_GROUPS = 4
EXPERTS_PER_GROUP = 8
N_EXPERTS = 32
D_EXPERT = 1024
MOE_BLOCK = 128
LANES = 128

ROW_TILE = 256
N_ROW_TILES = TOK // ROW_TILE
CTX_TILE = SEQ // ROW_TILE
MOD_ROWS = 16
MOD_CTX_ROW = BATCH
VMEM_LIMIT = 56 * 1024 * 1024


def _dot(a, b):
    return jnp.dot(a, b, preferred_element_type=F32)


def _dot_nt(a, b):
    return lax.dot_general(a, b, (((1,), (1,)), ((), ())), preferred_element_type=F32)


def _silu(x):
    return x / (1.0 + jnp.exp(-x))


def _rms(x, n):
    return x * lax.rsqrt(jnp.sum(x * x, axis=-1, keepdims=True) * (1.0 / n) + NORM_EPS)


def _modulate(x, shift, scale):
    return _rms(x, x.shape[-1]) * (1.0 + scale) + shift


def _rope(x, cos, sin_signed, quarter, lane):
    fwd = pltpu.roll(x, LANES - quarter, axis=1)
    bwd = pltpu.roll(x, quarter, axis=1)
    rot = jnp.where((lane // quarter) % 2 == 0, fwd, bwd)
    return x * cos + rot * sin_signed


MOD_TN = 1024


def _mod_kernel(c_ref, w_ref, b_ref, o_ref):
    a = _silu(c_ref[...]).astype(BF16)
    o_ref[0] = _dot(a, w_ref[0].astype(BF16)) + b_ref[0]


def _mod_call(c_all, mod_w, mod_b):
    d6 = 6 * D_MODEL
    return pl.pallas_call(
        _mod_kernel,
        grid=(DEPTH, d6 // MOD_TN),
        in_specs=[
            pl.BlockSpec((MOD_ROWS, D_MODEL), lambda l, n: (0, 0)),
            pl.BlockSpec((1, D_MODEL, MOD_TN), lambda l, n: (l, 0, n)),
            pl.BlockSpec((1, 1, MOD_TN), lambda l, n: (l, 0, n)),
        ],
        out_specs=pl.BlockSpec((1, MOD_ROWS, MOD_TN), lambda l, n: (l, 0, n)),
        out_shape=jax.ShapeDtypeStruct((DEPTH, MOD_ROWS, d6), F32),
        compiler_params=pltpu.CompilerParams(
            dimension_semantics=("parallel", "parallel"), vmem_limit_bytes=VMEM_LIMIT),
        name="mod_vectors",
    )(c_all, mod_w, mod_b.reshape(DEPTH, 1, d6))


def _mod_row(b, j):
    return jnp.where(j == CTX_TILE, MOD_CTX_ROW, b)


def _resident(shape):
    nd = len(shape)
    return pl.BlockSpec(shape, lambda *_: (0,) * nd, pipeline_mode=pl.Buffered(1))


EVEN_IN_PAD = 2688


def _even_proj_kernel(h_ref, mod_ref, tab_ref, g_ref, win_ref, wuq_ref, wukv_ref,
                      qa_ref, ka_ref, va_ref, qb_ref, kb_ref, vb_ref):
    d = D_MODEL
    x = h_ref[0]
    a = _modulate(x, mod_ref[0, :, 0:d], mod_ref[0, :, d:2 * d]).astype(BF16)
    lane = lax.broadcasted_iota(jnp.int32, (ROW_TILE, LANES), 1)
    cos128, sin128 = tab_ref[:, 0:128], tab_ref[:, 128:256]
    cos64p, sin64p = tab_ref[:, 512:640], tab_ref[:, 640:768]

    z = _dot(a, win_ref[:, 0:1024])
    g_q = g_ref[0:1, 0:128]
    for h in range(A_HEADS):
        blk = _rms(z[:, h * 128:(h + 1) * 128], HEAD_DIM) * g_q
        qa_ref[0, :, h * 128:(h + 1) * 128] = _rope(blk, cos128, sin128, 32, lane).astype(BF16)
    z = _dot(a, win_ref[:, 1024:1536])
    g_k = g_ref[1:2, 0:128]
    for h in range(A_KV_HEADS):
        blk = _rms(z[:, h * 128:(h + 1) * 128], HEAD_DIM) * g_k
        ka_ref[0, :, h * 128:(h + 1) * 128] = _rope(blk, cos128, sin128, 32, lane).astype(BF16)
    va_ref[0] = z[:, 256:512].astype(BF16)

    z = _dot(a, win_ref[:, 1536:2048])
    cq = (_rms(z, B_Q_LORA) * g_ref[2:3, :]).astype(BF16)
    zq = _dot(cq, wuq_ref[...])
    gq0, gq1 = g_ref[4:5, 0:128], g_ref[4:5, 128:256]
    n_qk = float(B_NOPE + B_ROPE)
    for h in range(B_HEADS):
        b0 = zq[:, h * 256:h * 256 + 128]
        b1 = zq[:, h * 256 + 128:(h + 1) * 256]
        ss = jnp.sum(b0 * b0, axis=-1, keepdims=True) + jnp.sum(b1 * b1, axis=-1, keepdims=True)
        r = lax.rsqrt(ss * (1.0 / n_qk) + NORM_EPS)
        qb_ref[0, :, h * 256:h * 256 + 128] = (b0 * r * gq0).astype(BF16)
        qb_ref[0, :, h * 256 + 128:(h + 1) * 256] = _rope(b1 * r * gq1, cos64p, sin64p, 16, lane).astype(BF16)

    z = _dot(a, win_ref[:, 2048:2688])
    ckv = (_rms(z[:, 0:512], B_KV_LORA) * g_ref[3:4, :]).astype(BF16)
    kr = z[:, 512:640]
    ss_kr = jnp.sum(kr * kr, axis=-1, keepdims=True)
    kr_rot = _rope(kr * g_ref[5:6, 128:256], cos64p, sin64p, 16, lane)
    zkv = _dot(ckv, wukv_ref[...])
    vb_ref[0] = zkv[:, 1024:2048].astype(BF16)
    gk0 = g_ref[5:6, 0:128]
    for h in range(B_HEADS):
        kn = zkv[:, h * 128:(h + 1) * 128]
        ss = jnp.sum(kn * kn, axis=-1, keepdims=True) + ss_kr
        r = lax.rsqrt(ss * (1.0 / n_qk) + NORM_EPS)
        kb_ref[0, :, h * 256:h * 256 + 128] = (kn * r * gk0).astype(BF16)
        kb_ref[0, :, h * 256 + 128:(h + 1) * 256] = (kr_rot * r).astype(BF16)


def _even_proj_call(h, mod, tab, gains, win, wuq, wukv):
    bsz = h.shape[0]
    widths = (A_HEADS * HEAD_DIM, A_KV_HEADS * HEAD_DIM, A_KV_HEADS * HEAD_DIM,
              B_HEADS * B_QK_PAD, B_HEADS * B_QK_PAD, B_HEADS * B_V)
    tile = lambda w: pl.BlockSpec((1, ROW_TILE, w), lambda b, j: (b, j, 0))
    return pl.pallas_call(
        _even_proj_kernel,
        grid=(bsz, N_ROW_TILES),
        in_specs=[
            tile(D_MODEL),
            pl.BlockSpec((1, 1, 6 * D_MODEL), lambda b, j: (_mod_row(b, j), 0, 0)),
            pl.BlockSpec((ROW_TILE, 768), lambda b, j: (j, 0)),
            _resident(gains.shape), _resident(win.shape), _resident(wuq.shape), _resident(wukv.shape),
        ],
        out_specs=[tile(w) for w in widths],
        out_shape=[jax.ShapeDtypeStruct((bsz, TOK, w), BF16) for w in widths],
        compiler_params=pltpu.CompilerParams(
            dimension_semantics=("parallel", "parallel"), vmem_limit_bytes=VMEM_LIMIT),
        name="even_proj",
    )(h, mod, tab, gains, win, wuq, wukv)


def _rms_halves(x, lane):
    x2 = x * x
    s_lo = jnp.sum(jnp.where(lane < 64, x2, 0.0), axis=-1, keepdims=True)
    s_hi = jnp.sum(jnp.where(lane < 64, 0.0, x2), axis=-1, keepdims=True)
    r = jnp.where(lane < 64, lax.rsqrt(s_lo * (1.0 / C_HD) + NORM_EPS), lax.rsqrt(s_hi * (1.0 / C_HD) + NORM_EPS))
    return x * r


def _odd_proj_kernel(h_ref, mod_ref, tab_ref, g_ref, win_ref,
                     qc_ref, kc_ref, vc_ref, qd_ref, kd_ref, vd_ref):
    d = D_MODEL
    x = h_ref[0]
    a = _modulate(x, mod_ref[0, :, 0:d], mod_ref[0, :, d:2 * d]).astype(BF16)
    lane = lax.broadcasted_iota(jnp.int32, (ROW_TILE, LANES), 1)
    cos128, sin128 = tab_ref[:, 0:128], tab_ref[:, 128:256]
    cos64, sin64 = tab_ref[:, 256:384], tab_ref[:, 384:512]

    z = _dot(a, win_ref[:, 0:1024])
    g_q = g_ref[0:1, :]
    for h in range(C_HEADS):
        blk = _rope(_rms_halves(z[:, h * 128:(h + 1) * 128], lane) * g_q, cos64, sin64, 16, lane)
        qc_ref[0, :, h * 256:h * 256 + 128] = jnp.where(lane < 64, blk, 0.0).astype(BF16)
        qc_ref[0, :, h * 256 + 128:(h + 1) * 256] = jnp.where(lane < 64, 0.0, blk).astype(BF16)
    z = _dot(a, win_ref[:, 1024:2048])
    g_k = g_ref[1:2, :]
    for h in range(C_HEADS):
        blk = _rope(_rms_halves(z[:, h * 128:(h + 1) * 128], lane) * g_k, cos64, sin64, 16, lane)
        kc_ref[0, :, h * 128:(h + 1) * 128] = blk.astype(BF16)
    vc_ref[0] = _dot(a, win_ref[:, 2048:3072]).astype(BF16)

    z = _dot(a, win_ref[:, 3072:4096])
    g_q = g_ref[2:3, :]
    for h in range(D_HEADS):
        blk = _rms(z[:, h * 128:(h + 1) * 128], HEAD_DIM) * g_q
        qd_ref[0, :, h * 128:(h + 1) * 128] = _rope(blk, cos128, sin128, 32, lane).astype(BF16)
    z = _dot(a, win_ref[:, 4096:4608])
    g_k = g_ref[3:4, :]
    for h in range(D_KV_HEADS):
        blk = _rms(z[:, h * 128:(h + 1) * 128], HEAD_DIM) * g_k
        kd_ref[0, :, h * 128:(h + 1) * 128] = _rope(blk, cos128, sin128, 32, lane).astype(BF16)
    vd_ref[0] = z[:, 256:512].astype(BF16)


def _odd_proj_call(h, mod, tab, gains, win):
    bsz = h.shape[0]
    widths = (C_HEADS * 2 * LANES, C_HEADS * LANES, C_HEADS * C_V,
              D_HEADS * HEAD_DIM, D_KV_HEADS * HEAD_DIM, D_KV_HEADS * HEAD_DIM)
    tile = lambda w: pl.BlockSpec((1, ROW_TILE, w), lambda b, j: (b, j, 0))
    return pl.pallas_call(
        _odd_proj_kernel,
        grid=(bsz, N_ROW_TILES),
        in_specs=[
            tile(D_MODEL),
            pl.BlockSpec((1, 1, 6 * D_MODEL), lambda b, j: (_mod_row(b, j), 0, 0)),
            pl.BlockSpec((ROW_TILE, 768), lambda b, j: (j, 0)),
            _resident(gains.shape), _resident(win.shape),
        ],
        out_specs=[tile(w) for w in widths],
        out_shape=[jax.ShapeDtypeStruct((bsz, TOK, w), BF16) for w in widths],
        compiler_params=pltpu.CompilerParams(
            dimension_semantics=("parallel", "parallel"), vmem_limit_bytes=VMEM_LIMIT),
        name="odd_proj",
    )(h, mod, tab, gains, win)


KEY_CHUNK = 768


def _attn_kernel(*refs, nkv, g, dk, dv, tq, n_keys, diff_lambda_init):
    if diff_lambda_init is None:
        q_ref, k_ref, v_ref, o_ref = refs
    else:
        q_ref, k_ref, v_ref, aux_ref, o_ref = refs
    for kv in range(nkv):
        q = jnp.concatenate(
            [q_ref[0, :, (kv * g + gi) * dk:(kv * g + gi + 1) * dk] for gi in range(g)], axis=0)
        m = l = acc = None
        for c0 in range(0, n_keys, KEY_CHUNK):
            c1 = min(c0 + KEY_CHUNK, n_keys)
            s = _dot_nt(q, k_ref[0, c0:c1, kv * dk:(kv + 1) * dk])
            v = v_ref[0, c0:c1, kv * dv:(kv + 1) * dv]
            m_c = jnp.max(s, axis=-1, keepdims=True)
            if m is None:
                m = m_c
                p = jnp.exp(s - m)
                l = jnp.sum(p, axis=-1, keepdims=True)
                acc = _dot(p.astype(BF16), v)
            else:
                m_new = jnp.maximum(m, m_c)
                alpha = jnp.exp(m - m_new)
                p = jnp.exp(s - m_new)
                l = alpha * l + jnp.sum(p, axis=-1, keepdims=True)
                acc = alpha * acc + _dot(p.astype(BF16), v)
                m = m_new
        o = acc / l
        if diff_lambda_init is None:
            for gi in range(g):
                o_ref[0, :, (kv * g + gi) * dv:(kv * g + gi + 1) * dv] = o[gi * tq:(gi + 1) * tq].astype(BF16)
        else:
            lam = (jnp.exp(jnp.sum(aux_ref[0:1, :] * aux_ref[1:2, :], axis=-1, keepdims=True))
                   - jnp.exp(jnp.sum(aux_ref[2:3, :] * aux_ref[3:4, :], axis=-1, keepdims=True))
                   + diff_lambda_init)
            od = o[0:tq] - lam * o[tq:2 * tq]
            od = _rms(od, dv) * aux_ref[4:5, :] * (1.0 - diff_lambda_init)
            o_ref[0, :, kv * dv:(kv + 1) * dv] = od.astype(BF16)


def _attn_call(q, k, v, *, n_heads_kv, g, dk, dv, nkv, tq, q_tile0, n_q_tiles, key_tile0, n_keys,
               aux=None, diff_lambda_init=None, name="attn"):
    bsz = q.shape[0]
    n_out_heads = n_heads_kv * (g if diff_lambda_init is None else 1)
    out_w = nkv * (g if diff_lambda_init is None else 1) * dv
    kern = functools.partial(_attn_kernel, nkv=nkv, g=g, dk=dk, dv=dv, tq=tq, n_keys=n_keys,
                             diff_lambda_init=diff_lambda_init)
    in_specs = [
        pl.BlockSpec((1, tq, nkv * g * dk), lambda b, hh, qi: (b, qi + q_tile0, hh)),
        pl.BlockSpec((1, n_keys, nkv * dk), lambda b, hh, qi: (b, key_tile0, hh)),
        pl.BlockSpec((1, n_keys, nkv * dv), lambda b, hh, qi: (b, key_tile0, hh)),
    ]
    args = [q, k, v]
    if aux is not None:
        in_specs.append(pl.BlockSpec(aux.shape, lambda b, hh, qi: (0, 0)))
        args.append(aux)
    return pl.pallas_call(
        kern,
        grid=(bsz, n_heads_kv // nkv, n_q_tiles),
        in_specs=in_specs,
        out_specs=pl.BlockSpec((1, tq, out_w), lambda b, hh, qi: (b, qi + q_tile0, hh)),
        out_shape=jax.ShapeDtypeStruct((bsz, TOK, n_out_heads * dv), BF16),
        compiler_params=pltpu.CompilerParams(
            dimension_semantics=("parallel", "parallel", "parallel"), vmem_limit_bytes=VMEM_LIMIT),
        name=name,
    )(*args)


WIN_TQ = 256
WIN_BAND = WIN_TQ + 2 * WINDOW
NEG_BIG = -1e30


def _window_kernel(q_ref, k_ref, v_ref, sink_ref, o_ref):
    g = D_HEADS // D_KV_HEADS
    dk = HEAD_DIM
    qi = pl.program_id(2)
    q0 = qi * WIN_TQ
    start = pl.multiple_of(jnp.clip(q0 - WINDOW, 0, SEQ - WIN_BAND), WINDOW)
    q = jnp.concatenate([q_ref[0, :, gi * dk:(gi + 1) * dk] for gi in range(g)], axis=0)
    s_loc = _dot_nt(q, k_ref[0, pl.ds(start, WIN_BAND), :])
    s_ctx = _dot_nt(q, k_ref[0, SEQ:TOK, :])
    row = (lax.broadcasted_iota(jnp.int32, (g * WIN_TQ, WIN_BAND), 0) & (WIN_TQ - 1)) + q0
    col = lax.broadcasted_iota(jnp.int32, (g * WIN_TQ, WIN_BAND), 1) + start
    s_loc = jnp.where(jnp.abs(row - col) <= WINDOW, s_loc, NEG_BIG)
    sink = sink_ref[0]
    m = jnp.maximum(jnp.maximum(jnp.max(s_loc, axis=-1, keepdims=True),
                                jnp.max(s_ctx, axis=-1, keepdims=True)), sink)
    p_loc = jnp.exp(s_loc - m)
    p_ctx = jnp.exp(s_ctx - m)
    l = (jnp.sum(p_loc, axis=-1, keepdims=True) + jnp.sum(p_ctx, axis=-1, keepdims=True)
         + jnp.exp(sink - m))
    acc = (_dot(p_ctx.astype(BF16), v_ref[0, SEQ:TOK, :])
           + _dot(p_loc.astype(BF16), v_ref[0, pl.ds(start, WIN_BAND), :]))
    o = acc / l
    for gi in range(g):
        o_ref[0, :, gi * dk:(gi + 1) * dk] = o[gi * WIN_TQ:(gi + 1) * WIN_TQ].astype(BF16)


def _window_call(q, k, v, sink_col):
    bsz = q.shape[0]
    g = D_HEADS // D_KV_HEADS
    return pl.pallas_call(
        _window_kernel,
        grid=(bsz, D_KV_HEADS, SEQ // WIN_TQ),
        in_specs=[
            pl.BlockSpec((1, WIN_TQ, g * HEAD_DIM), lambda b, hh, qi: (b, qi, hh)),
            pl.BlockSpec((1, TOK, HEAD_DIM), lambda b, hh, qi: (b, 0, hh)),
            pl.BlockSpec((1, TOK, HEAD_DIM), lambda b, hh, qi: (b, 0, hh)),
            pl.BlockSpec((1, g * WIN_TQ, 1), lambda b, hh, qi: (hh, 0, 0)),
        ],
        out_specs=pl.BlockSpec((1, WIN_TQ, g * HEAD_DIM), lambda b, hh, qi: (b, qi, hh)),
        out_shape=jax.ShapeDtypeStruct((bsz, TOK, D_HEADS * HEAD_DIM), BF16),
        compiler_params=pltpu.CompilerParams(
            dimension_semantics=("parallel", "parallel", "parallel"), vmem_limit_bytes=VMEM_LIMIT),
        name="window_attn",
    )(q, k, v, sink_col)


def _route(logits, lane):
    lane_f = lane.astype(F32)
    lg = jnp.where(lane < N_GROUPS, logits, NEG_BIG)
    g_max = jnp.max(lg, axis=-1, keepdims=True)
    p_top = 1.0 / jnp.sum(jnp.exp(lg - g_max), axis=-1, keepdims=True)
    g_idx = jnp.min(jnp.where(lg == g_max, lane_f, float(LANES)), axis=-1, keepdims=True)
    e_lane = lane - N_GROUPS
    in_group = (e_lane >= 0) & (e_lane < N_EXPERTS) & ((e_lane // EXPERTS_PER_GROUP).astype(F32) == g_idx)
    le = jnp.where(in_group, logits, NEG_BIG)
    m1 = jnp.max(le, axis=-1, keepdims=True)
    i1 = jnp.min(jnp.where(le == m1, lane_f, float(LANES)), axis=-1, keepdims=True)
    le2 = jnp.where(lane_f == i1, NEG_BIG, le)
    m2 = jnp.max(le2, axis=-1, keepdims=True)
    i2 = jnp.min(jnp.where(le2 == m2, lane_f, float(LANES)), axis=-1, keepdims=True)
    t = jnp.exp(m2 - m1)
    w1 = p_top / (1.0 + t)
    w2 = p_top * t / (1.0 + t)
    out = jnp.where(lane == 0, i1 - N_GROUPS, 0.0)
    out = jnp.where(lane == 1, i2 - N_GROUPS, out)
    out = jnp.where(lane == 2, w1, out)
    out = jnp.where(lane == 3, w2, out)
    return out


def _outproj_kernel(o1_ref, o2_ref, h_ref, mod_ref, w1_ref, w2_ref, rwh_ref, rwl_ref, rb_ref,
                    hn_ref, fx_ref, rt_ref):
    d = D_MODEL
    acc = _dot(o1_ref[0], w1_ref[...]) + _dot(o2_ref[0], w2_ref[...])
    hn = h_ref[0] + mod_ref[0, :, 2 * d:3 * d] * acc
    hn_ref[0] = hn
    fx = _modulate(hn, mod_ref[0, :, 3 * d:4 * d], mod_ref[0, :, 4 * d:5 * d])
    fx_ref[0] = fx
    hi = fx.astype(BF16)
    lo = (fx - hi.astype(F32)).astype(BF16)
    logits = _dot(hi, rwh_ref[...]) + _dot(lo, rwh_ref[...]) + _dot(hi, rwl_ref[...]) + rb_ref[...]
    lane = lax.broadcasted_iota(jnp.int32, (ROW_TILE, LANES), 1)
    rt_ref[0] = _route(logits, lane)


def _outproj_call(o1, o2, h, mod, w1, w2, rwh, rwl, rb, n_tiles):
    bsz = h.shape[0]
    tile = lambda w: pl.BlockSpec((1, ROW_TILE, w), lambda b, j: (b, j, 0))
    return pl.pallas_call(
        _outproj_kernel,
        grid=(bsz, n_tiles),
        in_specs=[
            tile(o1.shape[-1]), tile(o2.shape[-1]), tile(D_MODEL),
            pl.BlockSpec((1, 1, 6 * D_MODEL), lambda b, j: (_mod_row(b, j), 0, 0)),
            _resident(w1.shape), _resident(w2.shape), _resident(rwh.shape), _resident(rwl.shape),
            _resident(rb.shape),
        ],
        out_specs=[tile(D_MODEL), tile(D_MODEL), tile(LANES)],
        out_shape=[jax.ShapeDtypeStruct((bsz, TOK, D_MODEL), F32),
                   jax.ShapeDtypeStruct((bsz, TOK, D_MODEL), F32),
                   jax.ShapeDtypeStruct((bsz, TOK, LANES), F32)],
        compiler_params=pltpu.CompilerParams(
            dimension_semantics=("parallel", "parallel"), vmem_limit_bytes=VMEM_LIMIT),
        name="outproj_router",
    )(o1, o2, h, mod, w1, w2, rwh, rwl, rb)


def _moe_kernel(be_ref, nv_ref, nu_ref, row_ref, dst_ref, fx_hbm, wg_ref, wu_ref, wd_ref,
                y2_hbm, xbuf, ybuf, gsem, ssem):
    del be_ref
    blk = pl.program_id(0)

    @pl.when(blk < nu_ref[0])
    def _():
        def gather(r, carry):
            pltpu.make_async_copy(fx_hbm.at[pl.ds(row_ref[0, 0, r], 1)], xbuf.at[pl.ds(r, 1)], gsem).start()
            return carry
        lax.fori_loop(0, MOE_BLOCK, gather, 0, unroll=8)
        pltpu.make_async_copy(fx_hbm.at[pl.ds(0, MOE_BLOCK)], xbuf, gsem).wait()

        x = xbuf[...].astype(BF16)
        gt = _dot(x, wg_ref[0])
        up = _dot(x, wu_ref[0])
        u = (_silu(gt) * up).astype(BF16)
        ybuf[...] = _dot(u, wd_ref[0])

        n_valid = nv_ref[blk]

        def scatter(r, carry):
            pltpu.make_async_copy(ybuf.at[pl.ds(r, 1)], y2_hbm.at[pl.ds(dst_ref[0, 0, r], 1)], ssem).start()
            return carry
        lax.fori_loop(0, n_valid, scatter, 0)

        def drain(r, carry):
            pltpu.make_async_copy(ybuf.at[pl.ds(0, 1)], y2_hbm.at[pl.ds(0, 1)], ssem).wait()
            return carry
        lax.fori_loop(0, n_valid, drain, 0)


def _moe_call(fx_flat, block_expert, n_valid, n_used, slot_row, slot_dst, wg, wu, wd):
    n_blocks = block_expert.shape[0]
    n_rows = fx_flat.shape[0]
    idx_spec = pl.BlockSpec((1, 1, MOE_BLOCK), lambda i, be, nv, nu: (i, 0, 0), memory_space=pltpu.SMEM)
    grid_spec = pltpu.PrefetchScalarGridSpec(
        num_scalar_prefetch=3,
        grid=(n_blocks,),
        in_specs=[
            idx_spec, idx_spec,
            pl.BlockSpec(memory_space=pl.ANY),
            pl.BlockSpec((1, D_MODEL, D_EXPERT), lambda i, be, nv, nu: (be[i], 0, 0)),
            pl.BlockSpec((1, D_MODEL, D_EXPERT), lambda i, be, nv, nu: (be[i], 0, 0)),
            pl.BlockSpec((1, D_EXPERT, D_MODEL), lambda i, be, nv, nu: (be[i], 0, 0)),
        ],
        out_specs=pl.BlockSpec(memory_space=pl.ANY),
        scratch_shapes=[
            pltpu.VMEM((MOE_BLOCK, D_MODEL), F32),
            pltpu.VMEM((MOE_BLOCK, D_MODEL), F32),
            pltpu.SemaphoreType.DMA(()),
            pltpu.SemaphoreType.DMA(()),
        ],
    )
    return pl.pallas_call(
        _moe_kernel,
        grid_spec=grid_spec,
        out_shape=jax.ShapeDtypeStruct((2 * n_rows, D_MODEL), F32),
        compiler_params=pltpu.CompilerParams(
            dimension_semantics=("arbitrary",), vmem_limit_bytes=VMEM_LIMIT),
        name="moe_experts",
    )(block_expert, n_valid, n_used, slot_row, slot_dst, fx_flat, wg, wu, wd)


def _dispatch_plan(expert_idx, row_ids):
    n_tok, k = expert_idx.shape
    n_assign = n_tok * k
    flat_e = expert_idx.reshape(-1)
    order = jnp.argsort(flat_e)
    sorted_e = flat_e[order]
    counts = jnp.bincount(flat_e, length=N_EXPERTS)
    padded = (counts + MOE_BLOCK - 1) // MOE_BLOCK * MOE_BLOCK
    start = jnp.cumsum(counts) - counts
    pad_end = jnp.cumsum(padded)
    pad_start = pad_end - padded
    dest = pad_start[sorted_e] + jnp.arange(n_assign) - start[sorted_e]
    n_blocks = -(-n_assign // MOE_BLOCK) + N_EXPERTS
    rows = row_ids[order // k].astype(jnp.int32)
    slot_row = jnp.zeros((n_blocks * MOE_BLOCK,), jnp.int32).at[dest].set(rows)
    slot_dst = jnp.full((n_blocks * MOE_BLOCK,), -1, jnp.int32).at[dest].set(rows * k + (order % k).astype(jnp.int32))
    block_expert = jnp.minimum(
        jnp.searchsorted(pad_end, jnp.arange(n_blocks) * MOE_BLOCK, side='right'), N_EXPERTS - 1).astype(jnp.int32)
    n_valid = jnp.sum((slot_dst >= 0).reshape(n_blocks, MOE_BLOCK), axis=1).astype(jnp.int32)
    n_used = (pad_end[-1:] // MOE_BLOCK).astype(jnp.int32)
    return (block_expert, n_valid, n_used,
            slot_row.reshape(n_blocks, 1, MOE_BLOCK), slot_dst.reshape(n_blocks, 1, MOE_BLOCK))


def _combine_kernel(h_ref, y_ref, rt_ref, mod_ref, o_ref):
    d = D_MODEL
    rt = rt_ref[0]
    y = rt[:, 2:3] * y_ref[0, :, 0:d] + rt[:, 3:4] * y_ref[0, :, d:2 * d]
    o_ref[0] = h_ref[0] + mod_ref[0, :, 5 * d:6 * d] * y


def _combine_call(h, y2, rt, mod, n_tiles):
    bsz = h.shape[0]
    tile = lambda w: pl.BlockSpec((1, ROW_TILE, w), lambda b, j: (b, j, 0))
    return pl.pallas_call(
        _combine_kernel,
        grid=(bsz, n_tiles),
        in_specs=[
            tile(D_MODEL), tile(2 * D_MODEL), tile(LANES),
            pl.BlockSpec((1, 1, 6 * D_MODEL), lambda b, j: (_mod_row(b, j), 0, 0)),
        ],
        out_specs=tile(D_MODEL),
        out_shape=jax.ShapeDtypeStruct((bsz, n_tiles * ROW_TILE, D_MODEL), F32),
        compiler_params=pltpu.CompilerParams(
            dimension_semantics=("parallel", "parallel"), vmem_limit_bytes=VMEM_LIMIT),
        name="moe_combine",
    )(h, y2, rt, mod)


def _rope_tables():
    rows = SEQ // GRID_W
    row = jnp.repeat(jnp.arange(rows), GRID_W).astype(F32)
    col = jnp.tile(jnp.arange(GRID_W), rows).astype(F32)

    def tables(dim):
        n_freq = dim // 4
        inv = 1.0 / (ROPE_THETA ** (jnp.arange(n_freq, dtype=F32) / n_freq))
        ang_r = row[:, None] * inv
        ang_c = col[:, None] * inv
        ang = jnp.concatenate([ang_r, ang_r, ang_c, ang_c], axis=-1)
        sign = jnp.tile(jnp.concatenate([-jnp.ones((n_freq,), F32), jnp.ones((n_freq,), F32)]), 2)
        return jnp.cos(ang), jnp.sin(ang) * sign

    cos128, sin128 = tables(HEAD_DIM)
    cos64, sin64 = tables(B_ROPE)
    ones, zeros = jnp.ones((SEQ, 64), F32), jnp.zeros((SEQ, 64), F32)
    lat = jnp.concatenate([cos128, sin128,
                           cos64, cos64, sin64, sin64,
                           cos64, ones, sin64, zeros], axis=-1)
    ident = jnp.concatenate([jnp.ones((CTX_LEN, 128), F32), jnp.zeros((CTX_LEN, 128), F32)], axis=-1)
    return jnp.concatenate([lat, jnp.tile(ident, (1, 3))], axis=0)


def _pad_lanes(v, n):
    return jnp.pad(v, (0, n - v.shape[0]))


def _router_weights(wr_g, br_g, wr_e, br_e):
    w = jnp.pad(jnp.concatenate([wr_g, wr_e], axis=1), ((0, 0), (0, LANES - N_GROUPS - N_EXPERTS)))
    hi = w.astype(BF16)
    lo = (w - hi.astype(F32)).astype(BF16)
    b = _pad_lanes(jnp.concatenate([br_g, br_e]), LANES).reshape(1, LANES)
    return hi, lo, b


def _moe_layer(h, fx, rt, mod, wg, wu, wd, n_tiles):
    bsz = h.shape[0]
    n_rows_used = n_tiles * ROW_TILE
    e_idx = rt[:, :n_rows_used, 0:2].astype(jnp.int32).reshape(-1, 2)
    row_ids = (jnp.arange(bsz)[:, None] * TOK + jnp.arange(n_rows_used)[None, :]).reshape(-1)
    block_expert, n_valid, n_used, slot_row, slot_dst = _dispatch_plan(e_idx, row_ids)
    y2 = _moe_call(fx.reshape(bsz * TOK, D_MODEL), block_expert, n_valid, n_used, slot_row, slot_dst,
                   wg.astype(BF16), wu.astype(BF16), wd.astype(BF16))
    return _combine_call(h, y2.reshape(bsz, TOK, 2 * D_MODEL), rt, mod, n_tiles)


def kernel(x, c, ctx, c_ctx, mod_w, mod_b, even_w_in, even_w_out, a_q_norm, a_k_norm, b_cq_norm, b_w_uq,
           b_ckv_norm, b_w_ukv, b_q_norm, b_k_norm, odd_w_in, odd_w_out, c_q_norm, c_k_norm, c_lambda_q1,
           c_lambda_k1, c_lambda_q2, c_lambda_k2, c_subln, d_q_norm, d_k_norm, d_sink, moe_wr_group,
           moe_br_group, moe_wr_expert, moe_br_expert, moe_w_gate, moe_w_up, moe_w_down):
    bsz = x.shape[0]
    h = jnp.concatenate([x, ctx], axis=1)
    c_all = jnp.zeros((MOD_ROWS, D_MODEL), F32).at[:bsz].set(c).at[MOD_CTX_ROW].set(c_ctx)
    mod_all = _mod_call(c_all, mod_w, mod_b)
    tab = _rope_tables()

    i = 0
    mod = mod_all[0].reshape(MOD_ROWS, 1, 6 * D_MODEL)
    scale_a = HEAD_DIM ** -0.5
    scale_b = (B_NOPE + B_ROPE) ** -0.5
    win = jnp.pad(even_w_in[i], ((0, 0), (0, EVEN_IN_PAD - even_w_in.shape[-1]))).astype(BF16)
    wuq = b_w_uq[i].reshape(B_Q_LORA, B_HEADS, B_NOPE + B_ROPE)
    wuq = jnp.pad(wuq, ((0, 0), (0, 0), (0, B_QK_PAD - B_NOPE - B_ROPE))).reshape(B_Q_LORA, -1).astype(BF16)
    wukv = b_w_ukv[i].reshape(B_KV_LORA, B_HEADS, B_NOPE + B_V)
    wukv = jnp.concatenate([wukv[:, :, :B_NOPE].reshape(B_KV_LORA, -1),
                            wukv[:, :, B_NOPE:].reshape(B_KV_LORA, -1)], axis=1).astype(BF16)
    gains = jnp.stack([
        _pad_lanes(a_q_norm[i] * scale_a, 512), _pad_lanes(a_k_norm[i], 512),
        b_cq_norm[i], b_ckv_norm[i],
        _pad_lanes(b_q_norm[i] * scale_b, 512), _pad_lanes(b_k_norm[i], 512),
        jnp.zeros((512,), F32), jnp.zeros((512,), F32)])
    qa, ka, va, qb, kb, vb = _even_proj_call(h, mod, tab, gains, win, wuq, wukv)

    g_a = A_HEADS // A_KV_HEADS
    lat = dict(q_tile0=0, key_tile0=0, n_keys=TOK)
    oa = _attn_call(qa, ka, va, n_heads_kv=A_KV_HEADS, g=g_a, dk=HEAD_DIM, dv=HEAD_DIM, nkv=1,
                    tq=256, n_q_tiles=SEQ // 256, name="gqa_latent", **lat)
    ob = _attn_call(qb, kb, vb, n_heads_kv=B_HEADS, g=1, dk=B_QK_PAD, dv=B_V, nkv=2,
                    tq=512, n_q_tiles=SEQ // 512, name="mla_latent", **lat)
    cx = dict(tq=CTX_LEN, q_tile0=SEQ // CTX_LEN, n_q_tiles=1, key_tile0=SEQ // CTX_LEN, n_keys=CTX_LEN)
    oa_c = _attn_call(qa, ka, va, n_heads_kv=A_KV_HEADS, g=g_a, dk=HEAD_DIM, dv=HEAD_DIM, nkv=1,
                      name="gqa_context", **cx)
    ob_c = _attn_call(qb, kb, vb, n_heads_kv=B_HEADS, g=1, dk=B_QK_PAD, dv=B_V, nkv=2,
                      name="mla_context", **cx)
    oa = lax.dynamic_update_slice(oa, oa_c[:, SEQ:], (0, SEQ, 0))
    ob = lax.dynamic_update_slice(ob, ob_c[:, SEQ:], (0, SEQ, 0))

    w_out = even_w_out[i].astype(BF16)
    rwh, rwl, rb = _router_weights(moe_wr_group[0], moe_br_group[0], moe_wr_expert[0], moe_br_expert[0])
    hn, fx, rt = _outproj_call(oa, ob, h, mod, w_out[:A_HEADS * HEAD_DIM], w_out[A_HEADS * HEAD_DIM:],
                               rwh, rwl, rb, N_ROW_TILES)
    h = _moe_layer(hn, fx, rt, mod, moe_w_gate[0], moe_w_up[0], moe_w_down[0], N_ROW_TILES)

    layer = 1
    mod = mod_all[1].reshape(MOD_ROWS, 1, 6 * D_MODEL)
    lambda_init = 0.8 - 0.6 * math.exp(-0.3 * layer)
    scale_c = C_HD ** -0.5
    scale_d = HEAD_DIM ** -0.5
    win = odd_w_in[i].astype(BF16)
    gains = jnp.stack([
        jnp.tile(c_q_norm[i] * scale_c, 2), jnp.tile(c_k_norm[i], 2),
        d_q_norm[i] * scale_d, d_k_norm[i],
        jnp.zeros((128,), F32), jnp.zeros((128,), F32), jnp.zeros((128,), F32), jnp.zeros((128,), F32)])
    qc, kc, vc, qd, kd, vd = _odd_proj_call(h, mod, tab, gains, win)

    aux = jnp.stack([
        _pad_lanes(c_lambda_q1[i], 128), _pad_lanes(c_lambda_k1[i], 128),
        _pad_lanes(c_lambda_q2[i], 128), _pad_lanes(c_lambda_k2[i], 128),
        c_subln[i], jnp.zeros((128,), F32), jnp.zeros((128,), F32), jnp.zeros((128,), F32)])
    oc = _attn_call(qc, kc, vc, n_heads_kv=C_HEADS, g=2, dk=LANES, dv=C_V, nkv=2,
                    tq=512, n_q_tiles=SEQ // 512, aux=aux, diff_lambda_init=lambda_init,
                    name="diff_latent", **lat)
    g_d = D_HEADS // D_KV_HEADS
    sink_col = jnp.repeat(d_sink[i].reshape(D_KV_HEADS, g_d), WIN_TQ, axis=1).reshape(D_KV_HEADS, g_d * WIN_TQ, 1)
    od = _window_call(qd, kd, vd, sink_col)

    w_out = odd_w_out[i].astype(BF16)
    rwh, rwl, rb = _router_weights(moe_wr_group[1], moe_br_group[1], moe_wr_expert[1], moe_br_expert[1])
    n_lat_tiles = SEQ // ROW_TILE
    hn, fx, rt = _outproj_call(oc, od, h, mod, w_out[:C_HEADS * C_V], w_out[C_HEADS * C_V:],
                               rwh, rwl, rb, n_lat_tiles)
    return _moe_layer(hn, fx, rt, mod, moe_w_gate[1], moe_w_up[1], moe_w_down[1], n_lat_tiles)
```

```python
import functools
import math

import jax
import jax.numpy as jnp
from jax import lax
from jax.experimental import pallas as pl
from jax.experimental.pallas import tpu as pltpu

F32 = jnp.float32
BF16 = jnp.bfloat16

D_MODEL = 2048
BATCH = 8
SEQ = 2048
DEPTH = 2
GRID_W = 64
CTX_LEN = 256
TOK = SEQ + CTX_LEN
HEAD_DIM = 128
ROPE_THETA = 10000.0
NORM_EPS = 1e-6
A_HEADS = 8
A_KV_HEADS = 2
B_HEADS = 8
B_Q_LORA = 512
B_KV_LORA = 512
B_NOPE = 128
B_ROPE = 64
B_V = 128
B_QK_PAD = 256
C_HEADS = 8
C_HD = 64
C_V = 128
D_HEADS = 8
D_KV_HEADS = 2
WINDOW = 128
N_GROUPS = 4
EXPERTS_PER_GROUP = 8
N_EXPERTS = 32
D_EXPERT = 1024
MOE_BLOCK = 128
LANES = 128

ROW_TILE = 256
N_ROW_TILES = TOK // ROW_TILE
CTX_TILE = SEQ // ROW_TILE
MOD_ROWS = 16
MOD_CTX_ROW = BATCH
VMEM_LIMIT = 56 * 1024 * 1024


def _dot(a, b):
    return jnp.dot(a, b, preferred_element_type=F32)


def _dot_nt(a, b):
    return lax.dot_general(a, b, (((1,), (1,)), ((), ())), preferred_element_type=F32)


def _silu(x):
    return x / (1.0 + jnp.exp(-x))


def _rms(x, n):
    return x * lax.rsqrt(jnp.sum(x * x, axis=-1, keepdims=True) * (1.0 / n) + NORM_EPS)


def _modulate(x, shift, scale):
    return _rms(x, x.shape[-1]) * (1.0 + scale) + shift


def _rope(x, cos, sin_signed, quarter, lane):
    fwd = pltpu.roll(x, LANES - quarter, axis=1)
    bwd = pltpu.roll(x, quarter, axis=1)
    rot = jnp.where((lane // quarter) % 2 == 0, fwd, bwd)
    return x * cos + rot * sin_signed


MOD_TN = 1024


def _mod_kernel(c_ref, w_ref, b_ref, o_ref):
    a = _silu(c_ref[...]).astype(BF16)
    o_ref[0] = _dot(a, w_ref[0].astype(BF16)) + b_ref[0]


def _mod_call(c_all, mod_w, mod_b):
    d6 = 6 * D_MODEL
    return pl.pallas_call(
        _mod_kernel,
        grid=(DEPTH, d6 // MOD_TN),
        in_specs=[
            pl.BlockSpec((MOD_ROWS, D_MODEL), lambda l, n: (0, 0)),
            pl.BlockSpec((1, D_MODEL, MOD_TN), lambda l, n: (l, 0, n)),
            pl.BlockSpec((1, 1, MOD_TN), lambda l, n: (l, 0, n)),
        ],
        out_specs=pl.BlockSpec((1, MOD_ROWS, MOD_TN), lambda l, n: (l, 0, n)),
        out_shape=jax.ShapeDtypeStruct((DEPTH, MOD_ROWS, d6), F32),
        compiler_params=pltpu.CompilerParams(
            dimension_semantics=("parallel", "parallel"), vmem_limit_bytes=VMEM_LIMIT),
        name="mod_vectors",
    )(c_all, mod_w, mod_b.reshape(DEPTH, 1, d6))


def _mod_row(b, j):
    return jnp.where(j == CTX_TILE, MOD_CTX_ROW, b)


def _resident(shape):
    nd = len(shape)
    return pl.BlockSpec(shape, lambda *_: (0,) * nd, pipeline_mode=pl.Buffered(1))


EVEN_IN_PAD = 2688


def _even_proj_kernel(h_ref, mod_ref, tab_ref, g_ref, win_ref, wuq_ref, wukv_ref,
                      qa_ref, ka_ref, va_ref, qb_ref, kb_ref, vb_ref):
    d = D_MODEL
    x = h_ref[0]
    a = _modulate(x, mod_ref[0, :, 0:d], mod_ref[0, :, d:2 * d]).astype(BF16)
    lane = lax.broadcasted_iota(jnp.int32, (ROW_TILE, LANES), 1)
    cos128, sin128 = tab_ref[:, 0:128], tab_ref[:, 128:256]
    cos64p, sin64p = tab_ref[:, 512:640], tab_ref[:, 640:768]

    z = _dot(a, win_ref[:, 0:1024])
    g_q = g_ref[0:1, 0:128]
    for h in range(A_HEADS):
        blk = _rms(z[:, h * 128:(h + 1) * 128], HEAD_DIM) * g_q
        qa_ref[0, :, h * 128:(h + 1) * 128] = _rope(blk, cos128, sin128, 32, lane).astype(BF16)
    z = _dot(a, win_ref[:, 1024:1536])
    g_k = g_ref[1:2, 0:128]
    for h in range(A_KV_HEADS):
        blk = _rms(z[:, h * 128:(h + 1) * 128], HEAD_DIM) * g_k
        ka_ref[0, :, h * 128:(h + 1) * 128] = _rope(blk, cos128, sin128, 32, lane).astype(BF16)
    va_ref[0] = z[:, 256:512].astype(BF16)

    z = _dot(a, win_ref[:, 1536:2048])
    cq = (_rms(z, B_Q_LORA) * g_ref[2:3, :]).astype(BF16)
    zq = _dot(cq, wuq_ref[...])
    gq0, gq1 = g_ref[4:5, 0:128], g_ref[4:5, 128:256]
    n_qk = float(B_NOPE + B_ROPE)
    for h in range(B_HEADS):
        b0 = zq[:, h * 256:h * 256 + 128]
        b1 = zq[:, h * 256 + 128:(h + 1) * 256]
        ss = jnp.sum(b0 * b0, axis=-1, keepdims=True) + jnp.sum(b1 * b1, axis=-1, keepdims=True)
        r = lax.rsqrt(ss * (1.0 / n_qk) + NORM_EPS)
        qb_ref[0, :, h * 256:h * 256 + 128] = (b0 * r * gq0).astype(BF16)
        qb_ref[0, :, h * 256 + 128:(h + 1) * 256] = _rope(b1 * r * gq1, cos64p, sin64p, 16, lane).astype(BF16)

    z = _dot(a, win_ref[:, 2048:2688])
    ckv = (_rms(z[:, 0:512], B_KV_LORA) * g_ref[3:4, :]).astype(BF16)
    kr = z[:, 512:640]
    ss_kr = jnp.sum(kr * kr, axis=-1, keepdims=True)
    kr_rot = _rope(kr * g_ref[5:6, 128:256], cos64p, sin64p, 16, lane)
    zkv = _dot(ckv, wukv_ref[...])
    vb_ref[0] = zkv[:, 1024:2048].astype(BF16)
    gk0 = g_ref[5:6, 0:128]
    for h in range(B_HEADS):
        kn = zkv[:, h * 128:(h + 1) * 128]
        ss = jnp.sum(kn * kn, axis=-1, keepdims=True) + ss_kr
        r = lax.rsqrt(ss * (1.0 / n_qk) + NORM_EPS)
        kb_ref[0, :, h * 256:h * 256 + 128] = (kn * r * gk0).astype(BF16)
        kb_ref[0, :, h * 256 + 128:(h + 1) * 256] = (kr_rot * r).astype(BF16)


def _even_proj_call(h, mod, tab, gains, win, wuq, wukv):
    bsz = h.shape[0]
    widths = (A_HEADS * HEAD_DIM, A_KV_HEADS * HEAD_DIM, A_KV_HEADS * HEAD_DIM,
              B_HEADS * B_QK_PAD, B_HEADS * B_QK_PAD, B_HEADS * B_V)
    tile = lambda w: pl.BlockSpec((1, ROW_TILE, w), lambda b, j: (b, j, 0))
    return pl.pallas_call(
        _even_proj_kernel,
        grid=(bsz, N_ROW_TILES),
        in_specs=[
            tile(D_MODEL),
            pl.BlockSpec((1, 1, 6 * D_MODEL), lambda b, j: (_mod_row(b, j), 0, 0)),
            pl.BlockSpec((ROW_TILE, 768), lambda b, j: (j, 0)),
            _resident(gains.shape), _resident(win.shape), _resident(wuq.shape), _resident(wukv.shape),
        ],
        out_specs=[tile(w) for w in widths],
        out_shape=[jax.ShapeDtypeStruct((bsz, TOK, w), BF16) for w in widths],
        compiler_params=pltpu.CompilerParams(
            dimension_semantics=("parallel", "parallel"), vmem_limit_bytes=VMEM_LIMIT),
        name="even_proj",
    )(h, mod, tab, gains, win, wuq, wukv)


def _rms_halves(x, lane):
    x2 = x * x
    s_lo = jnp.sum(jnp.where(lane < 64, x2, 0.0), axis=-1, keepdims=True)
    s_hi = jnp.sum(jnp.where(lane < 64, 0.0, x2), axis=-1, keepdims=True)
    r = jnp.where(lane < 64, lax.rsqrt(s_lo * (1.0 / C_HD) + NORM_EPS), lax.rsqrt(s_hi * (1.0 / C_HD) + NORM_EPS))
    return x * r


def _odd_proj_kernel(h_ref, mod_ref, tab_ref, g_ref, win_ref,
                     qc_ref, kc_ref, vc_ref, qd_ref, kd_ref, vd_ref):
    d = D_MODEL
    x = h_ref[0]
    a = _modulate(x, mod_ref[0, :, 0:d], mod_ref[0, :, d:2 * d]).astype(BF16)
    lane = lax.broadcasted_iota(jnp.int32, (ROW_TILE, LANES), 1)
    cos128, sin128 = tab_ref[:, 0:128], tab_ref[:, 128:256]
    cos64, sin64 = tab_ref[:, 256:384], tab_ref[:, 384:512]

    z = _dot(a, win_ref[:, 0:1024])
    g_q = g_ref[0:1, :]
    for h in range(C_HEADS):
        blk = _rope(_rms_halves(z[:, h * 128:(h + 1) * 128], lane) * g_q, cos64, sin64, 16, lane)
        qc_ref[0, :, h * 256:h * 256 + 128] = jnp.where(lane < 64, blk, 0.0).astype(BF16)
        qc_ref[0, :, h * 256 + 128:(h + 1) * 256] = jnp.where(lane < 64, 0.0, blk).astype(BF16)
    z = _dot(a, win_ref[:, 1024:2048])
    g_k = g_ref[1:2, :]
    for h in range(C_HEADS):
        blk = _rope(_rms_halves(z[:, h * 128:(h + 1) * 128], lane) * g_k, cos64, sin64, 16, lane)
        kc_ref[0, :, h * 128:(h + 1) * 128] = blk.astype(BF16)
    vc_ref[0] = _dot(a, win_ref[:, 2048:3072]).astype(BF16)

    z = _dot(a, win_ref[:, 3072:4096])
    g_q = g_ref[2:3, :]
    for h in range(D_HEADS):
        blk = _rms(z[:, h * 128:(h + 1) * 128], HEAD_DIM) * g_q
        qd_ref[0, :, h * 128:(h + 1) * 128] = _rope(blk, cos128, sin128, 32, lane).astype(BF16)
    z = _dot(a, win_ref[:, 4096:4608])
    g_k = g_ref[3:4, :]
    for h in range(D_KV_HEADS):
        blk = _rms(z[:, h * 128:(h + 1) * 128], HEAD_DIM) * g_k
        kd_ref[0, :, h * 128:(h + 1) * 128] = _rope(blk, cos128, sin128, 32, lane).astype(BF16)
    vd_ref[0] = z[:, 256:512].astype(BF16)


def _odd_proj_call(h, mod, tab, gains, win):
    bsz = h.shape[0]
    widths = (C_HEADS * 2 * LANES, C_HEADS * LANES, C_HEADS * C_V,
              D_HEADS * HEAD_DIM, D_KV_HEADS * HEAD_DIM, D_KV_HEADS * HEAD_DIM)
    tile = lambda w: pl.BlockSpec((1, ROW_TILE, w), lambda b, j: (b, j, 0))
    return pl.pallas_call(
        _odd_proj_kernel,
        grid=(bsz, N_ROW_TILES),
        in_specs=[
            tile(D_MODEL),
            pl.BlockSpec((1, 1, 6 * D_MODEL), lambda b, j: (_mod_row(b, j), 0, 0)),
            pl.BlockSpec((ROW_TILE, 768), lambda b, j: (j, 0)),
            _resident(gains.shape), _resident(win.shape),
        ],
        out_specs=[tile(w) for w in widths],
        out_shape=[jax.ShapeDtypeStruct((bsz, TOK, w), BF16) for w in widths],
        compiler_params=pltpu.CompilerParams(
            dimension_semantics=("parallel", "parallel"), vmem_limit_bytes=VMEM_LIMIT),
        name="odd_proj",
    )(h, mod, tab, gains, win)


KEY_CHUNK = 768


def _attn_kernel(*refs, nkv, g, dk, dv, tq, n_keys, diff_lambda_init):
    if diff_lambda_init is None:
        q_ref, k_ref, v_ref, o_ref = refs
    else:
        q_ref, k_ref, v_ref, aux_ref, o_ref = refs
    for kv in range(nkv):
        q = jnp.concatenate(
            [q_ref[0, :, (kv * g + gi) * dk:(kv * g + gi + 1) * dk] for gi in range(g)], axis=0)
        m = l = acc = None
        for c0 in range(0, n_keys, KEY_CHUNK):
            c1 = min(c0 + KEY_CHUNK, n_keys)
            s = _dot_nt(q, k_ref[0, c0:c1, kv * dk:(kv + 1) * dk])
            v = v_ref[0, c0:c1, kv * dv:(kv + 1) * dv]
            m_c = jnp.max(s, axis=-1, keepdims=True)
            if m is None:
                m = m_c
                p = jnp.exp(s - m)
                l = jnp.sum(p, axis=-1, keepdims=True)
                acc = _dot(p.astype(BF16), v)
            else:
                m_new = jnp.maximum(m, m_c)
                alpha = jnp.exp(m - m_new)
                p = jnp.exp(s - m_new)
                l = alpha * l + jnp.sum(p, axis=-1, keepdims=True)
                acc = alpha * acc + _dot(p.astype(BF16), v)
                m = m_new
        o = acc / l
        if diff_lambda_init is None:
            for gi in range(g):
                o_ref[0, :, (kv * g + gi) * dv:(kv * g + gi + 1) * dv] = o[gi * tq:(gi + 1) * tq].astype(BF16)
        else:
            lam = (jnp.exp(jnp.sum(aux_ref[0:1, :] * aux_ref[1:2, :], axis=-1, keepdims=True))
                   - jnp.exp(jnp.sum(aux_ref[2:3, :] * aux_ref[3:4, :], axis=-1, keepdims=True))
                   + diff_lambda_init)
            od = o[0:tq] - lam * o[tq:2 * tq]
            od = _rms(od, dv) * aux_ref[4:5, :] * (1.0 - diff_lambda_init)
            o_ref[0, :, kv * dv:(kv + 1) * dv] = od.astype(BF16)


def _attn_call(q, k, v, *, n_heads_kv, g, dk, dv, nkv, tq, q_tile0, n_q_tiles, key_tile0, n_keys,
               aux=None, diff_lambda_init=None, name="attn"):
    bsz = q.shape[0]
    n_out_heads = n_heads_kv * (g if diff_lambda_init is None else 1)
    out_w = nkv * (g if diff_lambda_init is None else 1) * dv
    kern = functools.partial(_attn_kernel, nkv=nkv, g=g, dk=dk, dv=dv, tq=tq, n_keys=n_keys,
                             diff_lambda_init=diff_lambda_init)
    in_specs = [
        pl.BlockSpec((1, tq, nkv * g * dk), lambda b, hh, qi: (b, qi + q_tile0, hh)),
        pl.BlockSpec((1, n_keys, nkv * dk), lambda b, hh, qi: (b, key_tile0, hh)),
        pl.BlockSpec((1, n_keys, nkv * dv), lambda b, hh, qi: (b, key_tile0, hh)),
    ]
    args = [q, k, v]
    if aux is not None:
        in_specs.append(pl.BlockSpec(aux.shape, lambda b, hh, qi: (0, 0)))
        args.append(aux)
    return pl.pallas_call(
        kern,
        grid=(bsz, n_heads_kv // nkv, n_q_tiles),
        in_specs=in_specs,
        out_specs=pl.BlockSpec((1, tq, out_w), lambda b, hh, qi: (b, qi, hh)),
        out_shape=jax.ShapeDtypeStruct((bsz, n_q_tiles * tq, n_out_heads * dv), BF16),
        compiler_params=pltpu.CompilerParams(
            dimension_semantics=("parallel", "parallel", "parallel"), vmem_limit_bytes=VMEM_LIMIT),
        name=name,
    )(*args)


WIN_TQ = 256
WIN_BAND = WIN_TQ + 2 * WINDOW
NEG_BIG = -1e30


def _window_kernel(q_ref, k_ref, v_ref, sink_ref, o_ref):
    g = D_HEADS // D_KV_HEADS
    dk = HEAD_DIM
    qi = pl.program_id(2)
    q0 = qi * WIN_TQ
    start = pl.multiple_of(jnp.clip(q0 - WINDOW, 0, SEQ - WIN_BAND), WINDOW)
    q = jnp.concatenate([q_ref[0, :, gi * dk:(gi + 1) * dk] for gi in range(g)], axis=0)
    s_loc = _dot_nt(q, k_ref[0, pl.ds(start, WIN_BAND), :])
    s_ctx = _dot_nt(q, k_ref[0, SEQ:TOK, :])
    row = (lax.broadcasted_iota(jnp.int32, (g * WIN_TQ, WIN_BAND), 0) & (WIN_TQ - 1)) + q0
    col = lax.broadcasted_iota(jnp.int32, (g * WIN_TQ, WIN_BAND), 1) + start
    s_loc = jnp.where(jnp.abs(row - col) <= WINDOW, s_loc, NEG_BIG)
    sink = sink_ref[0]
    m = jnp.maximum(jnp.maximum(jnp.max(s_loc, axis=-1, keepdims=True),
                                jnp.max(s_ctx, axis=-1, keepdims=True)), sink)
    p_loc = jnp.exp(s_loc - m)
    p_ctx = jnp.exp(s_ctx - m)
    l = (jnp.sum(p_loc, axis=-1, keepdims=True) + jnp.sum(p_ctx, axis=-1, keepdims=True)
         + jnp.exp(sink - m))
    acc = (_dot(p_ctx.astype(BF16), v_ref[0, SEQ:TOK, :])
           + _dot(p_loc.astype(BF16), v_ref[0, pl.ds(start, WIN_BAND), :]))
    o = acc / l
    for gi in range(g):
        o_ref[0, :, gi * dk:(gi + 1) * dk] = o[gi * WIN_TQ:(gi + 1) * WIN_TQ].astype(BF16)


def _window_call(q, k, v, sink_col):
    bsz = q.shape[0]
    g = D_HEADS // D_KV_HEADS
    return pl.pallas_call(
        _window_kernel,
        grid=(bsz, D_KV_HEADS, SEQ // WIN_TQ),
        in_specs=[
            pl.BlockSpec((1, WIN_TQ, g * HEAD_DIM), lambda b, hh, qi: (b, qi, hh)),
            pl.BlockSpec((1, TOK, HEAD_DIM), lambda b, hh, qi: (b, 0, hh)),
            pl.BlockSpec((1, TOK, HEAD_DIM), lambda b, hh, qi: (b, 0, hh)),
            pl.BlockSpec((1, g * WIN_TQ, 1), lambda b, hh, qi: (hh, 0, 0)),
        ],
        out_specs=pl.BlockSpec((1, WIN_TQ, g * HEAD_DIM), lambda b, hh, qi: (b, qi, hh)),
        out_shape=jax.ShapeDtypeStruct((bsz, SEQ, D_HEADS * HEAD_DIM), BF16),
        compiler_params=pltpu.CompilerParams(
            dimension_semantics=("parallel", "parallel", "parallel"), vmem_limit_bytes=VMEM_LIMIT),
        name="window_attn",
    )(q, k, v, sink_col)


def _route(logits, lane):
    lane_f = lane.astype(F32)
    lg = jnp.where(lane < N_GROUPS, logits, NEG_BIG)
    g_max = jnp.max(lg, axis=-1, keepdims=True)
    p_top = 1.0 / jnp.sum(jnp.exp(lg - g_max), axis=-1, keepdims=True)
    g_idx = jnp.min(jnp.where(lg == g_max, lane_f, float(LANES)), axis=-1, keepdims=True)
    e_lane = lane - N_GROUPS
    in_group = (e_lane >= 0) & (e_lane < N_EXPERTS) & ((e_lane // EXPERTS_PER_GROUP).astype(F32) == g_idx)
    le = jnp.where(in_group, logits, NEG_BIG)
    m1 = jnp.max(le, axis=-1, keepdims=True)
    i1 = jnp.min(jnp.where(le == m1, lane_f, float(LANES)), axis=-1, keepdims=True)
    le2 = jnp.where(lane_f == i1, NEG_BIG, le)
    m2 = jnp.max(le2, axis=-1, keepdims=True)
    i2 = jnp.min(jnp.where(le2 == m2, lane_f, float(LANES)), axis=-1, keepdims=True)
    t = jnp.exp(m2 - m1)
    w1 = p_top / (1.0 + t)
    w2 = p_top * t / (1.0 + t)
    return i1, i2, w1, w2


def _outproj_kernel(*refs, has_ctx):
    if has_ctx:
        (o1_ref, o2_ref, o1c_ref, o2c_ref, h_ref, mod_ref, w1_ref, w2_ref, rwh_ref, rwl_ref, rb_ref,
         hn_ref, fx_ref, rt_ref, cnt_ref, run_ref) = refs
    else:
        (o1_ref, o2_ref, h_ref, mod_ref, w1_ref, w2_ref, rwh_ref, rwl_ref, rb_ref,
         hn_ref, fx_ref, rt_ref, cnt_ref, run_ref) = refs
    d = D_MODEL
    b, j = pl.program_id(0), pl.program_id(1)

    @pl.when((b == 0) & (j == 0))
    def _():
        run_ref[...] = jnp.zeros_like(run_ref)

    o1, o2 = o1_ref[0], o2_ref[0]
    if has_ctx:
        is_ctx = j == CTX_TILE
        o1 = jnp.where(is_ctx, o1c_ref[0], o1)
        o2 = jnp.where(is_ctx, o2c_ref[0], o2)
    acc = _dot(o1, w1_ref[...]) + _dot(o2, w2_ref[...])
    hn = h_ref[0] + mod_ref[0, :, 2 * d:3 * d] * acc
    hn_ref[0] = hn
    fx = _modulate(hn, mod_ref[0, :, 3 * d:4 * d], mod_ref[0, :, 4 * d:5 * d])
    fx_ref[0] = fx
    hi = fx.astype(BF16)
    lo = (fx - hi.astype(F32)).astype(BF16)
    logits = _dot(hi, rwh_ref[...]) + _dot(lo, rwh_ref[...]) + _dot(hi, rwl_ref[...]) + rb_ref[...]
    lane = lax.broadcasted_iota(jnp.int32, (ROW_TILE, LANES), 1)
    lane_f = lane.astype(F32)
    i1, i2, w1, w2 = _route(logits, lane)

    hit1, hit2 = lane_f == i1, lane_f == i2
    multi_hot = jnp.where(hit1, 1.0, 0.0) + jnp.where(hit2, 1.0, 0.0)
    tri = (lax.broadcasted_iota(jnp.int32, (ROW_TILE, ROW_TILE), 0)
           > lax.broadcasted_iota(jnp.int32, (ROW_TILE, ROW_TILE), 1))
    before = _dot(jnp.where(tri, 1.0, 0.0).astype(BF16), multi_hot.astype(BF16)) + run_ref[...]
    r1 = jnp.sum(jnp.where(hit1, before, 0.0), axis=-1, keepdims=True)
    r2 = jnp.sum(jnp.where(hit2, before, 0.0), axis=-1, keepdims=True)
    run_ref[...] = run_ref[...] + jnp.sum(multi_hot, axis=0, keepdims=True)
    cnt_ref[...] = jnp.broadcast_to(run_ref[...], cnt_ref.shape)

    out = jnp.where(lane == 0, i1 - N_GROUPS, 0.0)
    out = jnp.where(lane == 1, i2 - N_GROUPS, out)
    out = jnp.where(lane == 2, w1, out)
    out = jnp.where(lane == 3, w2, out)
    out = jnp.where(lane == 4, r1, out)
    out = jnp.where(lane == 5, r2, out)
    rt_ref[0] = out


def _outproj_call(o1, o2, o_ctx, h, mod, w1, w2, rwh, rwl, rb, n_tiles):
    bsz = h.shape[0]
    has_ctx = o_ctx is not None
    tile = lambda w: pl.BlockSpec((1, ROW_TILE, w), lambda b, j: (b, j, 0))
    lat_tile = lambda w: pl.BlockSpec((1, ROW_TILE, w), lambda b, j: (b, jnp.minimum(j, CTX_TILE - 1), 0))
    ctx_tile = lambda w: pl.BlockSpec((1, ROW_TILE, w), lambda b, j: (b, 0, 0))
    in_specs = [lat_tile(o1.shape[-1]), lat_tile(o2.shape[-1])]
    args = [o1, o2]
    if has_ctx:
        in_specs += [ctx_tile(o_ctx[0].shape[-1]), ctx_tile(o_ctx[1].shape[-1])]
        args += list(o_ctx)
    in_specs += [
        tile(D_MODEL),
        pl.BlockSpec((1, 1, 6 * D_MODEL), lambda b, j: (_mod_row(b, j), 0, 0)),
        _resident(w1.shape), _resident(w2.shape), _resident(rwh.shape), _resident(rwl.shape),
        _resident(rb.shape),
    ]
    args += [h, mod, w1, w2, rwh, rwl, rb]
    n_rows = n_tiles * ROW_TILE
    return pl.pallas_call(
        functools.partial(_outproj_kernel, has_ctx=has_ctx),
        grid=(bsz, n_tiles),
        in_specs=in_specs,
        out_specs=[tile(D_MODEL), tile(D_MODEL), tile(LANES),
                   pl.BlockSpec((8, LANES), lambda b, j: (0, 0))],
        out_shape=[jax.ShapeDtypeStruct((bsz, n_rows, D_MODEL), F32),
                   jax.ShapeDtypeStruct((bsz, n_rows, D_MODEL), F32),
                   jax.ShapeDtypeStruct((bsz, n_rows, LANES), F32),
                   jax.ShapeDtypeStruct((8, LANES), F32)],
        scratch_shapes=[pltpu.VMEM((1, LANES), F32)],
        compiler_params=pltpu.CompilerParams(
            dimension_semantics=("arbitrary", "arbitrary"), vmem_limit_bytes=VMEM_LIMIT),
        name="outproj_router",
    )(*args)


TOP_K = 2


def _dispatch_kernel(lo_ref, hi_ref, slot_ref, fx_ref, xs_hbm, zbuf, sem, zsem):
    b, j = pl.program_id(0), pl.program_id(1)

    def issue(r, carry):
        for k in range(TOP_K):
            pltpu.make_async_copy(fx_ref.at[0, pl.ds(r, 1)], xs_hbm.at[pl.ds(slot_ref[0, k, r], 1)], sem).start()
        return carry
    lax.fori_loop(0, ROW_TILE, issue, 0, unroll=8)

    @pl.when((b == pl.num_programs(0) - 1) & (j == pl.num_programs(1) - 1))
    def _():
        zbuf[...] = jnp.zeros_like(zbuf)

        def fill(s, carry):
            pltpu.make_async_copy(zbuf.at[pl.ds(0, 1)], xs_hbm.at[pl.ds(s, 1)], zsem).start()
            return carry

        def drain(s, carry):
            pltpu.make_async_copy(zbuf.at[pl.ds(0, 1)], xs_hbm.at[pl.ds(0, 1)], zsem).wait()
            return carry
        for e in range(N_EXPERTS):
            lax.fori_loop(lo_ref[e], hi_ref[e], fill, 0)
        for e in range(N_EXPERTS):
            lax.fori_loop(lo_ref[e], hi_ref[e], drain, 0)

    for k in range(TOP_K):
        pltpu.make_async_copy(fx_ref.at[0], xs_hbm.at[pl.ds(0, ROW_TILE)], sem).wait()


def _dispatch_call(fx, slots, fill_lo, fill_hi, n_blocks):
    bsz, n_rows, _ = fx.shape
    n_tiles = n_rows // ROW_TILE
    grid_spec = pltpu.PrefetchScalarGridSpec(
        num_scalar_prefetch=2,
        grid=(bsz, n_tiles),
        in_specs=[
            pl.BlockSpec((1, TOP_K, ROW_TILE), lambda b, j, lo, hi: (b * n_tiles + j, 0, 0),
                         memory_space=pltpu.SMEM),
            pl.BlockSpec((1, ROW_TILE, D_MODEL), lambda b, j, lo, hi: (b, j, 0)),
        ],
        out_specs=pl.BlockSpec(memory_space=pl.ANY),
        scratch_shapes=[pltpu.VMEM((8, D_MODEL), F32), pltpu.SemaphoreType.DMA(()), pltpu.SemaphoreType.DMA(())],
    )
    return pl.pallas_call(
        _dispatch_kernel,
        grid_spec=grid_spec,
        out_shape=jax.ShapeDtypeStruct((n_blocks * MOE_BLOCK, D_MODEL), F32),
        compiler_params=pltpu.CompilerParams(
            dimension_semantics=("arbitrary", "arbitrary"), vmem_limit_bytes=VMEM_LIMIT),
        name="moe_dispatch",
    )(fill_lo, fill_hi, slots, fx)


CAST_ROWS = 256


def _moe_kernel(be_ref, nu_ref, nx_ref, x_ref, wg_hbm, wu_hbm, wd_hbm, y_ref,
                sg, su, sd, wg, wu, wd, sems, *, layer):
    i = pl.program_id(0)
    e = be_ref[i]
    used = i < nu_ref[0]
    staged = ((wg_hbm, sg, wg, 0), (wu_hbm, su, wu, 1), (wd_hbm, sd, wd, 2))

    def fetch(expert):
        for hbm, stage, _, s in staged:
            pltpu.make_async_copy(hbm.at[layer, expert], stage, sems.at[s]).start()

    @pl.when(i == 0)
    def _():
        fetch(e)

    first_of_expert = (i == 0) | (e != be_ref[jnp.maximum(i - 1, 0)])

    @pl.when(used & first_of_expert)
    def _():
        for hbm, stage, dst, s in staged:
            pltpu.make_async_copy(hbm.at[layer, 0], stage, sems.at[s]).wait()

            def cast(c, carry):
                rows = pl.ds(pl.multiple_of(c * CAST_ROWS, CAST_ROWS), CAST_ROWS)
                dst[rows, :] = stage[rows, :].astype(BF16)
                return carry
            lax.fori_loop(0, stage.shape[0] // CAST_ROWS, cast, 0)
        nxt = nx_ref[e]

        @pl.when(nxt >= 0)
        def _():
            fetch(nxt)

    @pl.when(used)
    def _():
        x = x_ref[...].astype(BF16)
        gt = _dot(x, wg[...])
        up = _dot(x, wu[...])
        u = (_silu(gt) * up).astype(BF16)
        y_ref[...] = _dot(u, wd[...])


def _moe_call(xs, block_expert, n_used, next_expert, wg, wu, wd, layer):
    n_blocks = block_expert.shape[0]
    row_blk = pl.BlockSpec((MOE_BLOCK, D_MODEL), lambda i, be, nu, nx: (jnp.minimum(i, nu[0] - 1), 0))
    hbm = pl.BlockSpec(memory_space=pl.ANY)
    grid_spec = pltpu.PrefetchScalarGridSpec(
        num_scalar_prefetch=3,
        grid=(n_blocks,),
        in_specs=[row_blk, hbm, hbm, hbm],
        out_specs=row_blk,
        scratch_shapes=[
            pltpu.VMEM((D_MODEL, D_EXPERT), F32), pltpu.VMEM((D_MODEL, D_EXPERT), F32),
            pltpu.VMEM((D_EXPERT, D_MODEL), F32),
            pltpu.VMEM((D_MODEL, D_EXPERT), BF16), pltpu.VMEM((D_MODEL, D_EXPERT), BF16),
            pltpu.VMEM((D_EXPERT, D_MODEL), BF16),
            pltpu.SemaphoreType.DMA((3,)),
        ],
    )
    return pl.pallas_call(
        functools.partial(_moe_kernel, layer=layer),
        grid_spec=grid_spec,
        out_shape=jax.ShapeDtypeStruct((n_blocks * MOE_BLOCK, D_MODEL), F32),
        compiler_params=pltpu.CompilerParams(
            dimension_semantics=("arbitrary",), vmem_limit_bytes=VMEM_LIMIT),
        name="moe_experts",
    )(block_expert, n_used, next_expert, xs, wg, wu, wd)


def _combine_kernel(slot_ref, h_ref, rt_ref, mod_ref, ys_hbm, o_ref, ybuf, sem):
    d = D_MODEL

    def issue(r, carry):
        for k in range(TOP_K):
            pltpu.make_async_copy(ys_hbm.at[pl.ds(slot_ref[0, k, r], 1)], ybuf.at[k, pl.ds(r, 1)], sem).start()
        return carry
    lax.fori_loop(0, ROW_TILE, issue, 0, unroll=8)
    for k in range(TOP_K):
        pltpu.make_async_copy(ys_hbm.at[pl.ds(0, ROW_TILE)], ybuf.at[k], sem).wait()
    rt = rt_ref[0]
    y = rt[:, 2:3] * ybuf[0] + rt[:, 3:4] * ybuf[1]
    o_ref[0] = h_ref[0] + mod_ref[0, :, 5 * d:6 * d] * y


def _combine_call(h, ys, slots, rt, mod, n_tiles):
    bsz = h.shape[0]
    tile = lambda w: pl.BlockSpec((1, ROW_TILE, w), lambda b, j: (b, j, 0))
    return pl.pallas_call(
        _combine_kernel,
        grid=(bsz, n_tiles),
        in_specs=[
            pl.BlockSpec((1, TOP_K, ROW_TILE), lambda b, j: (b * n_tiles + j, 0, 0), memory_space=pltpu.SMEM),
            tile(D_MODEL), tile(LANES),
            pl.BlockSpec((1, 1, 6 * D_MODEL), lambda b, j: (_mod_row(b, j), 0, 0)),
            pl.BlockSpec(memory_space=pl.ANY),
        ],
        out_specs=tile(D_MODEL),
        out_shape=jax.ShapeDtypeStruct((bsz, n_tiles * ROW_TILE, D_MODEL), F32),
        scratch_shapes=[pltpu.VMEM((TOP_K, ROW_TILE, D_MODEL), F32), pltpu.SemaphoreType.DMA(())],
        compiler_params=pltpu.CompilerParams(
            dimension_semantics=("arbitrary", "arbitrary"), vmem_limit_bytes=VMEM_LIMIT),
        name="moe_combine",
    )(slots, h, rt, mod, ys)


def _rope_tables():
    rows = SEQ // GRID_W
    row = jnp.repeat(jnp.arange(rows), GRID_W).astype(F32)
    col = jnp.tile(jnp.arange(GRID_W), rows).astype(F32)

    def tables(dim):
        n_freq = dim // 4
        inv = 1.0 / (ROPE_THETA ** (jnp.arange(n_freq, dtype=F32) / n_freq))
        ang_r = row[:, None] * inv
        ang_c = col[:, None] * inv
        ang = jnp.concatenate([ang_r, ang_r, ang_c, ang_c], axis=-1)
        sign = jnp.tile(jnp.concatenate([-jnp.ones((n_freq,), F32), jnp.ones((n_freq,), F32)]), 2)
        return jnp.cos(ang), jnp.sin(ang) * sign

    cos128, sin128 = tables(HEAD_DIM)
    cos64, sin64 = tables(B_ROPE)
    ones, zeros = jnp.ones((SEQ, 64), F32), jnp.zeros((SEQ, 64), F32)
    lat = jnp.concatenate([cos128, sin128,
                           cos64, cos64, sin64, sin64,
                           cos64, ones, sin64, zeros], axis=-1)
    ident = jnp.concatenate([jnp.ones((CTX_LEN, 128), F32), jnp.zeros((CTX_LEN, 128), F32)], axis=-1)
    return jnp.concatenate([lat, jnp.tile(ident, (1, 3))], axis=0)


def _pad_lanes(v, n):
    return jnp.pad(v, (0, n - v.shape[0]))


def _router_weights(wr_g, br_g, wr_e, br_e):
    w = jnp.pad(jnp.concatenate([wr_g, wr_e], axis=1), ((0, 0), (0, LANES - N_GROUPS - N_EXPERTS)))
    hi = w.astype(BF16)
    lo = (w - hi.astype(F32)).astype(BF16)
    b = _pad_lanes(jnp.concatenate([br_g, br_e]), LANES).reshape(1, LANES)
    return hi, lo, b


def _moe_layer(h, fx, rt, cnt, mod, wg, wu, wd, layer):
    bsz, n_rows, _ = fx.shape
    n_tiles = n_rows // ROW_TILE
    n_blocks = -(-(bsz * n_rows * TOP_K) // MOE_BLOCK) + N_EXPERTS
    experts = jnp.arange(N_EXPERTS, dtype=jnp.int32)
    counts = cnt[0, N_GROUPS:N_GROUPS + N_EXPERTS].astype(jnp.int32)
    padded = (counts + MOE_BLOCK - 1) // MOE_BLOCK * MOE_BLOCK
    pad_end = jnp.cumsum(padded)
    pad_start = pad_end - padded
    e_idx = rt[..., 0:TOP_K].astype(jnp.int32)
    rank = rt[..., 4:4 + TOP_K].astype(jnp.int32)
    slot = rank + jnp.sum(jnp.where(e_idx[..., None] == experts, pad_start, 0), axis=-1)
    slots = slot.reshape(bsz * n_tiles, ROW_TILE, TOP_K).transpose(0, 2, 1)
    block_start = jnp.arange(n_blocks, dtype=jnp.int32) * MOE_BLOCK
    block_expert = jnp.minimum(jnp.sum(pad_end[None, :] <= block_start[:, None], axis=1), N_EXPERTS - 1)
    n_used = pad_end[-1:] // MOE_BLOCK
    later = (experts[None, :] > experts[:, None]) & (counts[None, :] > 0)
    next_expert = jnp.where(jnp.any(later, axis=1), jnp.argmax(later, axis=1), -1)
    i32 = lambda a: a.astype(jnp.int32)
    xs = _dispatch_call(fx, i32(slots), i32(pad_start + counts), i32(pad_end), n_blocks)
    ys = _moe_call(xs, i32(block_expert), i32(n_used), i32(next_expert), wg, wu, wd, layer)
    return _combine_call(h, ys, i32(slots), rt, mod, n_tiles)


def kernel(x, c, ctx, c_ctx, mod_w, mod_b, even_w_in, even_w_out, a_q_norm, a_k_norm, b_cq_norm, b_w_uq,
           b_ckv_norm, b_w_ukv, b_q_norm, b_k_norm, odd_w_in, odd_w_out, c_q_norm, c_k_norm, c_lambda_q1,
           c_lambda_k1, c_lambda_q2, c_lambda_k2, c_subln, d_q_norm, d_k_norm, d_sink, moe_wr_group,
           moe_br_group, moe_wr_expert, moe_br_expert, moe_w_gate, moe_w_up, moe_w_down):
    bsz = x.shape[0]
    h = jnp.concatenate([x, ctx], axis=1)
    c_all = jnp.zeros((MOD_ROWS, D_MODEL), F32).at[:bsz].set(c).at[MOD_CTX_ROW].set(c_ctx)
    mod_all = _mod_call(c_all, mod_w, mod_b)
    tab = _rope_tables()

    i = 0
    mod = mod_all[0].reshape(MOD_ROWS, 1, 6 * D_MODEL)
    scale_a = HEAD_DIM ** -0.5
    scale_b = (B_NOPE + B_ROPE) ** -0.5
    win = jnp.pad(even_w_in[i], ((0, 0), (0, EVEN_IN_PAD - even_w_in.shape[-1]))).astype(BF16)
    wuq = b_w_uq[i].reshape(B_Q_LORA, B_HEADS, B_NOPE + B_ROPE)
    wuq = jnp.pad(wuq, ((0, 0), (0, 0), (0, B_QK_PAD - B_NOPE - B_ROPE))).reshape(B_Q_LORA, -1).astype(BF16)
    wukv = b_w_ukv[i].reshape(B_KV_LORA, B_HEADS, B_NOPE + B_V)
    wukv = jnp.concatenate([wukv[:, :, :B_NOPE].reshape(B_KV_LORA, -1),
                            wukv[:, :, B_NOPE:].reshape(B_KV_LORA, -1)], axis=1).astype(BF16)
    gains = jnp.stack([
        _pad_lanes(a_q_norm[i] * scale_a, 512), _pad_lanes(a_k_norm[i], 512),
        b_cq_norm[i], b_ckv_norm[i],
        _pad_lanes(b_q_norm[i] * scale_b, 512), _pad_lanes(b_k_norm[i], 512),
        jnp.zeros((512,), F32), jnp.zeros((512,), F32)])
    qa, ka, va, qb, kb, vb = _even_proj_call(h, mod, tab, gains, win, wuq, wukv)

    g_a = A_HEADS // A_KV_HEADS
    lat = dict(q_tile0=0, key_tile0=0, n_keys=TOK)
    oa = _attn_call(qa, ka, va, n_heads_kv=A_KV_HEADS, g=g_a, dk=HEAD_DIM, dv=HEAD_DIM, nkv=1,
                    tq=256, n_q_tiles=SEQ // 256, name="gqa_latent", **lat)
    ob = _attn_call(qb, kb, vb, n_heads_kv=B_HEADS, g=1, dk=B_QK_PAD, dv=B_V, nkv=2,
                    tq=512, n_q_tiles=SEQ // 512, name="mla_latent", **lat)
    cx = dict(tq=CTX_LEN, q_tile0=SEQ // CTX_LEN, n_q_tiles=1, key_tile0=SEQ // CTX_LEN, n_keys=CTX_LEN)
    oa_c = _attn_call(qa, ka, va, n_heads_kv=A_KV_HEADS, g=g_a, dk=HEAD_DIM, dv=HEAD_DIM, nkv=1,
                      name="gqa_context", **cx)
    ob_c = _attn_call(qb, kb, vb, n_heads_kv=B_HEADS, g=1, dk=B_QK_PAD, dv=B_V, nkv=2,
                      name="mla_context", **cx)

    w_out = even_w_out[i].astype(BF16)
    rwh, rwl, rb = _router_weights(moe_wr_group[0], moe_br_group[0], moe_wr_expert[0], moe_br_expert[0])
    hn, fx, rt, cnt = _outproj_call(oa, ob, (oa_c, ob_c), h, mod,
                                    w_out[:A_HEADS * HEAD_DIM], w_out[A_HEADS * HEAD_DIM:],
                                    rwh, rwl, rb, N_ROW_TILES)
    h = _moe_layer(hn, fx, rt, cnt, mod, moe_w_gate, moe_w_up, moe_w_down, 0)

    layer = 1
    mod = mod_all[1].reshape(MOD_ROWS, 1, 6 * D_MODEL)
    lambda_init = 0.8 - 0.6 * math.exp(-0.3 * layer)
    scale_c = C_HD ** -0.5
    scale_d = HEAD_DIM ** -0.5
    win = odd_w_in[i].astype(BF16)
    gains = jnp.stack([
        jnp.tile(c_q_norm[i] * scale_c, 2), jnp.tile(c_k_norm[i], 2),
        d_q_norm[i] * scale_d, d_k_norm[i],
        jnp.zeros((128,), F32), jnp.zeros((128,), F32), jnp.zeros((128,), F32), jnp.zeros((128,), F32)])
    qc, kc, vc, qd, kd, vd = _odd_proj_call(h, mod, tab, gains, win)

    aux = jnp.stack([
        _pad_lanes(c_lambda_q1[i], 128), _pad_lanes(c_lambda_k1[i], 128),
        _pad_lanes(c_lambda_q2[i], 128), _pad_lanes(c_lambda_k2[i], 128),
        c_subln[i], jnp.zeros((128,), F32), jnp.zeros((128,), F32), jnp.zeros((128,), F32)])
    oc = _attn_call(qc, kc, vc, n_heads_kv=C_HEADS, g=2, dk=LANES, dv=C_V, nkv=2,
                    tq=512, n_q_tiles=SEQ // 512, aux=aux, diff_lambda_init=lambda_init,
                    name="diff_latent", **lat)
    g_d = D_HEADS // D_KV_HEADS
    sink_col = jnp.repeat(d_sink[i].reshape(D_KV_HEADS, g_d), WIN_TQ, axis=1).reshape(D_KV_HEADS, g_d * WIN_TQ, 1)
    od = _window_call(qd, kd, vd, sink_col)

    w_out = odd_w_out[i].astype(BF16)
    rwh, rwl, rb = _router_weights(moe_wr_group[1], moe_br_group[1], moe_wr_expert[1], moe_br_expert[1])
    n_lat_tiles = SEQ // ROW_TILE
    hn, fx, rt, cnt = _outproj_call(oc, od, None, h, mod, w_out[:C_HEADS * C_V], w_out[C_HEADS * C_V:],
                                    rwh, rwl, rb, n_lat_tiles)
    return _moe_layer(hn, fx, rt, cnt, mod, moe_w_gate, moe_w_up, moe_w_down, 1)
```

```python
import functools
import math

import jax
import jax.numpy as jnp
from jax import lax
from jax.experimental import pallas as pl
from jax.experimental.pallas import tpu as pltpu

F32 = jnp.float32
BF16 = jnp.bfloat16

D_MODEL = 2048
BATCH = 8
SEQ = 2048
DEPTH = 2
GRID_W = 64
CTX_LEN = 256
TOK = SEQ + CTX_LEN
HEAD_DIM = 128
ROPE_THETA = 10000.0
NORM_EPS = 1e-6
A_HEADS = 8
A_KV_HEADS = 2
B_HEADS = 8
B_Q_LORA = 512
B_KV_LORA = 512
B_NOPE = 128
B_ROPE = 64
B_V = 128
B_QK_PAD = 256
C_HEADS = 8
C_HD = 64
C_V = 128
D_HEADS = 8
D_KV_HEADS = 2
WINDOW = 128
N_GROUPS = 4
EXPERTS_PER_GROUP = 8
N_EXPERTS = 32
D_EXPERT = 1024
MOE_BLOCK = 128
LANES = 128
LOG2_E = math.log2(math.e)

ROW_TILE = 256
N_ROW_TILES = TOK // ROW_TILE
CTX_TILE = SEQ // ROW_TILE
MOD_ROWS = 16
MOD_CTX_ROW = BATCH
VMEM_LIMIT = 56 * 1024 * 1024


def _dot(a, b):
    return jnp.dot(a, b, preferred_element_type=F32)


def _dot_nt(a, b):
    return lax.dot_general(a, b, (((1,), (1,)), ((), ())), preferred_element_type=F32)


def _silu(x):
    return x / (1.0 + jnp.exp(-x))


def _rms(x, n):
    return x * lax.rsqrt(jnp.sum(x * x, axis=-1, keepdims=True) * (1.0 / n) + NORM_EPS)


def _modulate(x, shift, scale):
    return _rms(x, x.shape[-1]) * (1.0 + scale) + shift


def _rope(x, cos, sin_signed, quarter, lane):
    fwd = pltpu.roll(x, LANES - quarter, axis=1)
    bwd = pltpu.roll(x, quarter, axis=1)
    rot = jnp.where((lane // quarter) % 2 == 0, fwd, bwd)
    return x * cos + rot * sin_signed


MOD_TN = 1024


def _mod_kernel(c_ref, w_ref, b_ref, o_ref):
    a = _silu(c_ref[...]).astype(BF16)
    o_ref[0] = _dot(a, w_ref[0].astype(BF16)) + b_ref[0]


def _mod_call(c_all, mod_w, mod_b):
    d6 = 6 * D_MODEL
    return pl.pallas_call(
        _mod_kernel,
        grid=(DEPTH, d6 // MOD_TN),
        in_specs=[
            pl.BlockSpec((MOD_ROWS, D_MODEL), lambda l, n: (0, 0)),
            pl.BlockSpec((1, D_MODEL, MOD_TN), lambda l, n: (l, 0, n)),
            pl.BlockSpec((1, 1, MOD_TN), lambda l, n: (l, 0, n)),
        ],
        out_specs=pl.BlockSpec((1, MOD_ROWS, MOD_TN), lambda l, n: (l, 0, n)),
        out_shape=jax.ShapeDtypeStruct((DEPTH, MOD_ROWS, d6), F32),
        compiler_params=pltpu.CompilerParams(
            dimension_semantics=("parallel", "parallel"), vmem_limit_bytes=VMEM_LIMIT),
        name="mod_vectors",
    )(c_all, mod_w, mod_b.reshape(DEPTH, 1, d6))


def _mod_row(b, j):
    return jnp.where(j == CTX_TILE, MOD_CTX_ROW, b)


def _resident(shape):
    nd = len(shape)
    return pl.BlockSpec(shape, lambda *_: (0,) * nd, pipeline_mode=pl.Buffered(1))


EVEN_IN_PAD = 2688


def _even_proj_kernel(hx_ref, hc_ref, mod_ref, tab_ref, g_ref, win_ref, wuq_ref, wukv_ref,
                      qa_ref, ka_ref, va_ref, qb_ref, kb_ref, vb_ref):
    d = D_MODEL
    x = jnp.where(pl.program_id(1) == CTX_TILE, hc_ref[0], hx_ref[0])
    a = _modulate(x, mod_ref[0, :, 0:d], mod_ref[0, :, d:2 * d]).astype(BF16)
    lane = lax.broadcasted_iota(jnp.int32, (ROW_TILE, LANES), 1)
    cos128, sin128 = tab_ref[:, 0:128], tab_ref[:, 128:256]
    cos64p, sin64p = tab_ref[:, 512:640], tab_ref[:, 640:768]

    z = _dot(a, win_ref[:, 0:1024])
    g_q = g_ref[0:1, 0:128]
    for h in range(A_HEADS):
        blk = _rms(z[:, h * 128:(h + 1) * 128], HEAD_DIM) * g_q
        qa_ref[0, :, h * 128:(h + 1) * 128] = _rope(blk, cos128, sin128, 32, lane).astype(BF16)
    z = _dot(a, win_ref[:, 1024:1536])
    g_k = g_ref[1:2, 0:128]
    for h in range(A_KV_HEADS):
        blk = _rms(z[:, h * 128:(h + 1) * 128], HEAD_DIM) * g_k
        ka_ref[0, :, h * 128:(h + 1) * 128] = _rope(blk, cos128, sin128, 32, lane).astype(BF16)
    va_ref[0] = z[:, 256:512].astype(BF16)

    z = _dot(a, win_ref[:, 1536:2048])
    cq = (_rms(z, B_Q_LORA) * g_ref[2:3, :]).astype(BF16)
    zq = _dot(cq, wuq_ref[...])
    gq0, gq1 = g_ref[4:5, 0:128], g_ref[4:5, 128:256]
    n_qk = float(B_NOPE + B_ROPE)
    for h in range(B_HEADS):
        b0 = zq[:, h * 256:h * 256 + 128]
        b1 = zq[:, h * 256 + 128:(h + 1) * 256]
        ss = jnp.sum(b0 * b0, axis=-1, keepdims=True) + jnp.sum(b1 * b1, axis=-1, keepdims=True)
        r = lax.rsqrt(ss * (1.0 / n_qk) + NORM_EPS)
        qb_ref[0, :, h * 256:h * 256 + 128] = (b0 * r * gq0).astype(BF16)
        qb_ref[0, :, h * 256 + 128:(h + 1) * 256] = _rope(b1 * r * gq1, cos64p, sin64p, 16, lane).astype(BF16)

    z = _dot(a, win_ref[:, 2048:2688])
    ckv = (_rms(z[:, 0:512], B_KV_LORA) * g_ref[3:4, :]).astype(BF16)
    kr = z[:, 512:640]
    ss_kr = jnp.sum(kr * kr, axis=-1, keepdims=True)
    kr_rot = _rope(kr * g_ref[5:6, 128:256], cos64p, sin64p, 16, lane)
    zkv = _dot(ckv, wukv_ref[...])
    vb_ref[0] = zkv[:, 1024:2048].astype(BF16)
    gk0 = g_ref[5:6, 0:128]
    for h in range(B_HEADS):
        kn = zkv[:, h * 128:(h + 1) * 128]
        ss = jnp.sum(kn * kn, axis=-1, keepdims=True) + ss_kr
        r = lax.rsqrt(ss * (1.0 / n_qk) + NORM_EPS)
        kb_ref[0, :, h * 256:h * 256 + 128] = (kn * r * gk0).astype(BF16)
        kb_ref[0, :, h * 256 + 128:(h + 1) * 256] = (kr_rot * r).astype(BF16)


def _lat_tile(w):
    return pl.BlockSpec((1, ROW_TILE, w), lambda b, j: (b, jnp.minimum(j, CTX_TILE - 1), 0))


def _ctx_tile(w):
    return pl.BlockSpec((1, ROW_TILE, w), lambda b, j: (b, 0, 0))


def _even_proj_call(h_lat, h_ctx, mod, tab, gains, win, wuq, wukv):
    bsz = h_lat.shape[0]
    widths = (A_HEADS * HEAD_DIM, A_KV_HEADS * HEAD_DIM, A_KV_HEADS * HEAD_DIM,
              B_HEADS * B_QK_PAD, B_HEADS * B_QK_PAD, B_HEADS * B_V)
    tile = lambda w: pl.BlockSpec((1, ROW_TILE, w), lambda b, j: (b, j, 0))
    return pl.pallas_call(
        _even_proj_kernel,
        grid=(bsz, N_ROW_TILES),
        in_specs=[
            _lat_tile(D_MODEL), _ctx_tile(D_MODEL),
            pl.BlockSpec((1, 1, 6 * D_MODEL), lambda b, j: (_mod_row(b, j), 0, 0)),
            pl.BlockSpec((ROW_TILE, 768), lambda b, j: (j, 0)),
            _resident(gains.shape), _resident(win.shape), _resident(wuq.shape), _resident(wukv.shape),
        ],
        out_specs=[tile(w) for w in widths],
        out_shape=[jax.ShapeDtypeStruct((bsz, TOK, w), BF16) for w in widths],
        compiler_params=pltpu.CompilerParams(
            dimension_semantics=("parallel", "parallel"), vmem_limit_bytes=VMEM_LIMIT),
        name="even_proj",
    )(h_lat, h_ctx, mod, tab, gains, win, wuq, wukv)


def _rms_halves(x, lane):
    x2 = x * x
    s_lo = jnp.sum(jnp.where(lane < 64, x2, 0.0), axis=-1, keepdims=True)
    s_hi = jnp.sum(jnp.where(lane < 64, 0.0, x2), axis=-1, keepdims=True)
    r = jnp.where(lane < 64, lax.rsqrt(s_lo * (1.0 / C_HD) + NORM_EPS), lax.rsqrt(s_hi * (1.0 / C_HD) + NORM_EPS))
    return x * r


def _odd_proj_kernel(h_ref, mod_ref, tab_ref, g_ref, win_ref,
                     qc_ref, kc_ref, vc_ref, qd_ref, kd_ref, vd_ref):
    d = D_MODEL
    x = h_ref[0]
    a = _modulate(x, mod_ref[0, :, 0:d], mod_ref[0, :, d:2 * d]).astype(BF16)
    lane = lax.broadcasted_iota(jnp.int32, (ROW_TILE, LANES), 1)
    cos128, sin128 = tab_ref[:, 0:128], tab_ref[:, 128:256]
    cos64, sin64 = tab_ref[:, 256:384], tab_ref[:, 384:512]

    z = _dot(a, win_ref[:, 0:1024])
    g_q = g_ref[0:1, :]
    for h in range(C_HEADS):
        blk = _rope(_rms_halves(z[:, h * 128:(h + 1) * 128], lane) * g_q, cos64, sin64, 16, lane)
        qc_ref[0, :, h * 256:h * 256 + 128] = jnp.where(lane < 64, blk, 0.0).astype(BF16)
        qc_ref[0, :, h * 256 + 128:(h + 1) * 256] = jnp.where(lane < 64, 0.0, blk).astype(BF16)
    z = _dot(a, win_ref[:, 1024:2048])
    g_k = g_ref[1:2, :]
    for h in range(C_HEADS):
        blk = _rope(_rms_halves(z[:, h * 128:(h + 1) * 128], lane) * g_k, cos64, sin64, 16, lane)
        kc_ref[0, :, h * 128:(h + 1) * 128] = blk.astype(BF16)
    vc_ref[0] = _dot(a, win_ref[:, 2048:3072]).astype(BF16)

    z = _dot(a, win_ref[:, 3072:4096])
    g_q = g_ref[2:3, :]
    for h in range(D_HEADS):
        blk = _rms(z[:, h * 128:(h + 1) * 128], HEAD_DIM) * g_q
        qd_ref[0, :, h * 128:(h + 1) * 128] = _rope(blk, cos128, sin128, 32, lane).astype(BF16)
    z = _dot(a, win_ref[:, 4096:4608])
    g_k = g_ref[3:4, :]
    for h in range(D_KV_HEADS):
        blk = _rms(z[:, h * 128:(h + 1) * 128], HEAD_DIM) * g_k
        kd_ref[0, :, h * 128:(h + 1) * 128] = _rope(blk, cos128, sin128, 32, lane).astype(BF16)
    vd_ref[0] = z[:, 256:512].astype(BF16)


def _odd_proj_call(h, mod, tab, gains, win):
    bsz = h.shape[0]
    widths = (C_HEADS * 2 * LANES, C_HEADS * LANES, C_HEADS * C_V,
              D_HEADS * HEAD_DIM, D_KV_HEADS * HEAD_DIM, D_KV_HEADS * HEAD_DIM)
    tile = lambda w: pl.BlockSpec((1, ROW_TILE, w), lambda b, j: (b, j, 0))
    return pl.pallas_call(
        _odd_proj_kernel,
        grid=(bsz, N_ROW_TILES),
        in_specs=[
            tile(D_MODEL),
            pl.BlockSpec((1, 1, 6 * D_MODEL), lambda b, j: (_mod_row(b, j), 0, 0)),
            pl.BlockSpec((ROW_TILE, 768), lambda b, j: (j, 0)),
            _resident(gains.shape), _resident(win.shape),
        ],
        out_specs=[tile(w) for w in widths],
        out_shape=[jax.ShapeDtypeStruct((bsz, TOK, w), BF16) for w in widths],
        compiler_params=pltpu.CompilerParams(
            dimension_semantics=("parallel", "parallel"), vmem_limit_bytes=VMEM_LIMIT),
        name="odd_proj",
    )(h, mod, tab, gains, win)


KEY_CHUNK = 768


def _attn_kernel(*refs, nkv, g, dk, dv, tq, n_keys, diff_lambda_init):
    if diff_lambda_init is None:
        q_ref, k_ref, v_ref, o_ref = refs
    else:
        q_ref, k_ref, v_ref, aux_ref, o_ref = refs
    for kv in range(nkv):
        q = jnp.concatenate(
            [q_ref[0, :, (kv * g + gi) * dk:(kv * g + gi + 1) * dk] for gi in range(g)], axis=0)
        m = l = acc = None
        for c0 in range(0, n_keys, KEY_CHUNK):
            c1 = min(c0 + KEY_CHUNK, n_keys)
            s = _dot_nt(q, k_ref[0, c0:c1, kv * dk:(kv + 1) * dk])
            v = v_ref[0, c0:c1, kv * dv:(kv + 1) * dv]
            m_c = jnp.max(s, axis=-1, keepdims=True)
            if m is None:
                m = m_c
                p = jnp.exp2(s - m)
                l = jnp.sum(p, axis=-1, keepdims=True)
                acc = _dot(p.astype(BF16), v)
            else:
                m_new = jnp.maximum(m, m_c)
                alpha = jnp.exp2(m - m_new)
                p = jnp.exp2(s - m_new)
                l = alpha * l + jnp.sum(p, axis=-1, keepdims=True)
                acc = alpha * acc + _dot(p.astype(BF16), v)
                m = m_new
        o = acc / l
        if diff_lambda_init is None:
            for gi in range(g):
                o_ref[0, :, (kv * g + gi) * dv:(kv * g + gi + 1) * dv] = o[gi * tq:(gi + 1) * tq].astype(BF16)
        else:
            lam = (jnp.exp(jnp.sum(aux_ref[0:1, :] * aux_ref[1:2, :], axis=-1, keepdims=True))
                   - jnp.exp(jnp.sum(aux_ref[2:3, :] * aux_ref[3:4, :], axis=-1, keepdims=True))
                   + diff_lambda_init)
            od = o[0:tq] - lam * o[tq:2 * tq]
            od = _rms(od, dv) * aux_ref[4:5, :] * (1.0 - diff_lambda_init)
            o_ref[0, :, kv * dv:(kv + 1) * dv] = od.astype(BF16)


def _attn_call(q, k, v, *, n_heads_kv, g, dk, dv, nkv, tq, q_tile0, n_q_tiles, key_tile0, n_keys,
               aux=None, diff_lambda_init=None, name="attn"):
    bsz = q.shape[0]
    n_out_heads = n_heads_kv * (g if diff_lambda_init is None else 1)
    out_w = nkv * (g if diff_lambda_init is None else 1) * dv
    kern = functools.partial(_attn_kernel, nkv=nkv, g=g, dk=dk, dv=dv, tq=tq, n_keys=n_keys,
                             diff_lambda_init=diff_lambda_init)
    in_specs = [
        pl.BlockSpec((1, tq, nkv * g * dk), lambda b, hh, qi: (b, qi + q_tile0, hh)),
        pl.BlockSpec((1, n_keys, nkv * dk), lambda b, hh, qi: (b, key_tile0, hh)),
        pl.BlockSpec((1, n_keys, nkv * dv), lambda b, hh, qi: (b, key_tile0, hh)),
    ]
    args = [q, k, v]
    if aux is not None:
        in_specs.append(pl.BlockSpec(aux.shape, lambda b, hh, qi: (0, 0)))
        args.append(aux)
    return pl.pallas_call(
        kern,
        grid=(bsz, n_heads_kv // nkv, n_q_tiles),
        in_specs=in_specs,
        out_specs=pl.BlockSpec((1, tq, out_w), lambda b, hh, qi: (b, qi, hh)),
        out_shape=jax.ShapeDtypeStruct((bsz, n_q_tiles * tq, n_out_heads * dv), BF16),
        compiler_params=pltpu.CompilerParams(
            dimension_semantics=("parallel", "parallel", "parallel"), vmem_limit_bytes=VMEM_LIMIT),
        name=name,
    )(*args)


WIN_TQ = 256
WIN_BAND = WIN_TQ + 2 * WINDOW
NEG_BIG = -1e30


def _window_kernel(q_ref, k_ref, v_ref, sink_ref, o_ref):
    g = D_HEADS // D_KV_HEADS
    dk = HEAD_DIM
    qi = pl.program_id(2)
    q0 = qi * WIN_TQ
    start = pl.multiple_of(jnp.clip(q0 - WINDOW, 0, SEQ - WIN_BAND), WINDOW)
    q = jnp.concatenate([q_ref[0, :, gi * dk:(gi + 1) * dk] for gi in range(g)], axis=0)
    s_loc = _dot_nt(q, k_ref[0, pl.ds(start, WIN_BAND), :])
    s_ctx = _dot_nt(q, k_ref[0, SEQ:TOK, :])
    row = (lax.broadcasted_iota(jnp.int32, (g * WIN_TQ, WIN_BAND), 0) & (WIN_TQ - 1)) + q0
    col = lax.broadcasted_iota(jnp.int32, (g * WIN_TQ, WIN_BAND), 1) + start
    s_loc = jnp.where(jnp.abs(row - col) <= WINDOW, s_loc, NEG_BIG)
    sink = sink_ref[0]
    m = jnp.maximum(jnp.maximum(jnp.max(s_loc, axis=-1, keepdims=True),
                                jnp.max(s_ctx, axis=-1, keepdims=True)), sink)
    p_loc = jnp.exp2(s_loc - m)
    p_ctx = jnp.exp2(s_ctx - m)
    l = (jnp.sum(p_loc, axis=-1, keepdims=True) + jnp.sum(p_ctx, axis=-1, keepdims=True)
         + jnp.exp2(sink - m))
    acc = (_dot(p_ctx.astype(BF16), v_ref[0, SEQ:TOK, :])
           + _dot(p_loc.astype(BF16), v_ref[0, pl.ds(start, WIN_BAND), :]))
    o = acc / l
    for gi in range(g):
        o_ref[0, :, gi * dk:(gi + 1) * dk] = o[gi * WIN_TQ:(gi + 1) * WIN_TQ].astype(BF16)


def _window_call(q, k, v, sink_col):
    bsz = q.shape[0]
    g = D_HEADS // D_KV_HEADS
    return pl.pallas_call(
        _window_kernel,
        grid=(bsz, D_KV_HEADS, SEQ // WIN_TQ),
        in_specs=[
            pl.BlockSpec((1, WIN_TQ, g * HEAD_DIM), lambda b, hh, qi: (b, qi, hh)),
            pl.BlockSpec((1, TOK, HEAD_DIM), lambda b, hh, qi: (b, 0, hh)),
            pl.BlockSpec((1, TOK, HEAD_DIM), lambda b, hh, qi: (b, 0, hh)),
            pl.BlockSpec((1, g * WIN_TQ, 1), lambda b, hh, qi: (hh, 0, 0)),
        ],
        out_specs=pl.BlockSpec((1, WIN_TQ, g * HEAD_DIM), lambda b, hh, qi: (b, qi, hh)),
        out_shape=jax.ShapeDtypeStruct((bsz, SEQ, D_HEADS * HEAD_DIM), BF16),
        compiler_params=pltpu.CompilerParams(
            dimension_semantics=("parallel", "parallel", "parallel"), vmem_limit_bytes=VMEM_LIMIT),
        name="window_attn",
    )(q, k, v, sink_col)


def _route(logits, lane):
    lane_f = lane.astype(F32)
    lg = jnp.where(lane < N_GROUPS, logits, NEG_BIG)
    g_max = jnp.max(lg, axis=-1, keepdims=True)
    p_top = 1.0 / jnp.sum(jnp.exp(lg - g_max), axis=-1, keepdims=True)
    g_idx = jnp.min(jnp.where(lg == g_max, lane_f, float(LANES)), axis=-1, keepdims=True)
    e_lane = lane - N_GROUPS
    in_group = (e_lane >= 0) & (e_lane < N_EXPERTS) & ((e_lane // EXPERTS_PER_GROUP).astype(F32) == g_idx)
    le = jnp.where(in_group, logits, NEG_BIG)
    m1 = jnp.max(le, axis=-1, keepdims=True)
    i1 = jnp.min(jnp.where(le == m1, lane_f, float(LANES)), axis=-1, keepdims=True)
    le2 = jnp.where(lane_f == i1, NEG_BIG, le)
    m2 = jnp.max(le2, axis=-1, keepdims=True)
    i2 = jnp.min(jnp.where(le2 == m2, lane_f, float(LANES)), axis=-1, keepdims=True)
    t = jnp.exp(m2 - m1)
    w1 = p_top / (1.0 + t)
    w2 = p_top * t / (1.0 + t)
    return i1, i2, w1, w2


def _outproj_kernel(*refs, has_ctx):
    if has_ctx:
        (o1_ref, o2_ref, h_ref, o1c_ref, o2c_ref, hc_ref, mod_ref, w1_ref, w2_ref, rw_ref, rb_ref,
         hn_ref, fx_ref, rt_ref, cnt_ref, run_ref) = refs
    else:
        (o1_ref, o2_ref, h_ref, mod_ref, w1_ref, w2_ref, rw_ref, rb_ref,
         hn_ref, fx_ref, rt_ref, cnt_ref, run_ref) = refs
    d = D_MODEL
    b, j = pl.program_id(0), pl.program_id(1)

    @pl.when((b == 0) & (j == 0))
    def _():
        run_ref[...] = jnp.zeros_like(run_ref)

    o1, o2, h = o1_ref[0], o2_ref[0], h_ref[0]
    if has_ctx:
        is_ctx = j == CTX_TILE
        o1 = jnp.where(is_ctx, o1c_ref[0], o1)
        o2 = jnp.where(is_ctx, o2c_ref[0], o2)
        h = jnp.where(is_ctx, hc_ref[0], h)
    acc = _dot(o1, w1_ref[...]) + _dot(o2, w2_ref[...])
    hn = h + mod_ref[0, :, 2 * d:3 * d] * acc
    hn_ref[0] = hn
    fx = _modulate(hn, mod_ref[0, :, 3 * d:4 * d], mod_ref[0, :, 4 * d:5 * d])
    fx_ref[0] = fx
    hi = fx.astype(BF16)
    lo = (fx - hi.astype(F32)).astype(BF16)
    part = _dot(jnp.concatenate([hi, lo], axis=0), rw_ref[...])
    logits = (part[:ROW_TILE, :LANES] + part[:ROW_TILE, LANES:]
              + part[ROW_TILE:, :LANES] + part[ROW_TILE:, LANES:] + rb_ref[...])
    lane = lax.broadcasted_iota(jnp.int32, (ROW_TILE, LANES), 1)
    lane_f = lane.astype(F32)
    i1, i2, w1, w2 = _route(logits, lane)

    hit1, hit2 = lane_f == i1, lane_f == i2
    multi_hot = jnp.where(hit1, 1.0, 0.0) + jnp.where(hit2, 1.0, 0.0)
    tri = (lax.broadcasted_iota(jnp.int32, (ROW_TILE, ROW_TILE), 0)
           > lax.broadcasted_iota(jnp.int32, (ROW_TILE, ROW_TILE), 1))
    before = _dot(jnp.where(tri, 1.0, 0.0).astype(BF16), multi_hot.astype(BF16)) + run_ref[...]
    r1 = jnp.sum(jnp.where(hit1, before, 0.0), axis=-1, keepdims=True)
    r2 = jnp.sum(jnp.where(hit2, before, 0.0), axis=-1, keepdims=True)
    run_ref[...] = run_ref[...] + jnp.sum(multi_hot, axis=0, keepdims=True)
    cnt_ref[...] = jnp.broadcast_to(run_ref[...], cnt_ref.shape)

    out = jnp.where(lane == 0, i1 - N_GROUPS, 0.0)
    out = jnp.where(lane == 1, i2 - N_GROUPS, out)
    out = jnp.where(lane == 2, w1, out)
    out = jnp.where(lane == 3, w2, out)
    out = jnp.where(lane == 4, r1, out)
    out = jnp.where(lane == 5, r2, out)
    rt_ref[0] = out


def _outproj_call(lat, ctx, mod, w1, w2, rw, rb, n_tiles):
    bsz = lat[2].shape[0]
    has_ctx = ctx is not None
    tile = lambda w: pl.BlockSpec((1, ROW_TILE, w), lambda b, j: (b, j, 0))
    in_specs = [_lat_tile(a.shape[-1]) for a in lat]
    args = list(lat)
    if has_ctx:
        in_specs += [_ctx_tile(a.shape[-1]) for a in ctx]
        args += list(ctx)
    in_specs += [
        pl.BlockSpec((1, 1, 6 * D_MODEL), lambda b, j: (_mod_row(b, j), 0, 0)),
        _resident(w1.shape), _resident(w2.shape), _resident(rw.shape), _resident(rb.shape),
    ]
    args += [mod, w1, w2, rw, rb]
    n_rows = n_tiles * ROW_TILE
    return pl.pallas_call(
        functools.partial(_outproj_kernel, has_ctx=has_ctx),
        grid=(bsz, n_tiles),
        in_specs=in_specs,
        out_specs=[tile(D_MODEL), tile(D_MODEL), tile(LANES),
                   pl.BlockSpec((8, LANES), lambda b, j: (0, 0))],
        out_shape=[jax.ShapeDtypeStruct((bsz, n_rows, D_MODEL), F32),
                   jax.ShapeDtypeStruct((bsz, n_rows, D_MODEL), F32),
                   jax.ShapeDtypeStruct((bsz, n_rows, LANES), F32),
                   jax.ShapeDtypeStruct((8, LANES), F32)],
        scratch_shapes=[pltpu.VMEM((1, LANES), F32)],
        compiler_params=pltpu.CompilerParams(
            dimension_semantics=("arbitrary", "arbitrary"), vmem_limit_bytes=VMEM_LIMIT),
        name="outproj_router",
    )(*args)


TOP_K = 2


def _dispatch_kernel(lo_ref, hi_ref, slot_ref, fx_ref, xs_hbm, zbuf, sem, zsem):
    b, j = pl.program_id(0), pl.program_id(1)

    def issue(r, carry):
        for k in range(TOP_K):
            pltpu.make_async_copy(fx_ref.at[0, pl.ds(r, 1)], xs_hbm.at[pl.ds(slot_ref[0, k, r], 1)], sem).start()
        return carry
    lax.fori_loop(0, ROW_TILE, issue, 0, unroll=8)

    @pl.when((b == pl.num_programs(0) - 1) & (j == pl.num_programs(1) - 1))
    def _():
        zbuf[...] = jnp.zeros_like(zbuf)

        def fill(s, carry):
            pltpu.make_async_copy(zbuf.at[pl.ds(0, 1)], xs_hbm.at[pl.ds(s, 1)], zsem).start()
            return carry

        def drain(s, carry):
            pltpu.make_async_copy(zbuf.at[pl.ds(0, 1)], xs_hbm.at[pl.ds(0, 1)], zsem).wait()
            return carry
        for e in range(N_EXPERTS):
            lax.fori_loop(lo_ref[e], hi_ref[e], fill, 0)
        for e in range(N_EXPERTS):
            lax.fori_loop(lo_ref[e], hi_ref[e], drain, 0)

    for k in range(TOP_K):
        pltpu.make_async_copy(fx_ref.at[0], xs_hbm.at[pl.ds(0, ROW_TILE)], sem).wait()


def _dispatch_call(fx, slots, fill_lo, fill_hi, n_blocks):
    bsz, n_rows, _ = fx.shape
    n_tiles = n_rows // ROW_TILE
    grid_spec = pltpu.PrefetchScalarGridSpec(
        num_scalar_prefetch=2,
        grid=(bsz, n_tiles),
        in_specs=[
            pl.BlockSpec((1, TOP_K, ROW_TILE), lambda b, j, lo, hi: (b * n_tiles + j, 0, 0),
                         memory_space=pltpu.SMEM),
            pl.BlockSpec((1, ROW_TILE, D_MODEL), lambda b, j, lo, hi: (b, j, 0)),
        ],
        out_specs=pl.BlockSpec(memory_space=pl.ANY),
        scratch_shapes=[pltpu.VMEM((8, D_MODEL), F32), pltpu.SemaphoreType.DMA(()), pltpu.SemaphoreType.DMA(())],
    )
    return pl.pallas_call(
        _dispatch_kernel,
        grid_spec=grid_spec,
        out_shape=jax.ShapeDtypeStruct((n_blocks * MOE_BLOCK, D_MODEL), F32),
        compiler_params=pltpu.CompilerParams(
            dimension_semantics=("arbitrary", "arbitrary"), vmem_limit_bytes=VMEM_LIMIT),
        name="moe_dispatch",
    )(fill_lo, fill_hi, slots, fx)


CAST_ROWS = 256


def _moe_kernel(be_ref, nu_ref, nx_ref, x_ref, wg_hbm, wu_hbm, wd_hbm, y_ref,
                sg, su, sd, wg, wu, wd, sems, *, layer):
    i = pl.program_id(0)
    e = be_ref[i]
    used = i < nu_ref[0]
    staged = ((wg_hbm, sg, wg, 0), (wu_hbm, su, wu, 1), (wd_hbm, sd, wd, 2))

    def fetch(expert):
        for hbm, stage, _, s in staged:
            pltpu.make_async_copy(hbm.at[layer, expert], stage, sems.at[s]).start(priority=1)

    @pl.when(i == 0)
    def _():
        fetch(e)

    first_of_expert = (i == 0) | (e != be_ref[jnp.maximum(i - 1, 0)])

    @pl.when(used & first_of_expert)
    def _():
        for hbm, stage, dst, s in staged:
            pltpu.make_async_copy(hbm.at[layer, 0], stage, sems.at[s]).wait()

            def cast(c, carry):
                rows = pl.ds(pl.multiple_of(c * CAST_ROWS, CAST_ROWS), CAST_ROWS)
                dst[rows, :] = stage[rows, :].astype(BF16)
                return carry
            lax.fori_loop(0, stage.shape[0] // CAST_ROWS, cast, 0)
        nxt = nx_ref[e]

        @pl.when(nxt >= 0)
        def _():
            fetch(nxt)

    @pl.when(used)
    def _():
        x = x_ref[...].astype(BF16)
        gt = _dot(x, wg[...])
        up = _dot(x, wu[...])
        u = (_silu(gt) * up).astype(BF16)
        y_ref[...] = _dot(u, wd[...])


def _moe_call(xs, block_expert, n_used, next_expert, wg, wu, wd, layer):
    n_blocks = block_expert.shape[0]
    row_blk = pl.BlockSpec((MOE_BLOCK, D_MODEL), lambda i, be, nu, nx: (jnp.minimum(i, nu[0] - 1), 0))
    hbm = pl.BlockSpec(memory_space=pl.ANY)
    grid_spec = pltpu.PrefetchScalarGridSpec(
        num_scalar_prefetch=3,
        grid=(n_blocks,),
        in_specs=[row_blk, hbm, hbm, hbm],
        out_specs=row_blk,
        scratch_shapes=[
            pltpu.VMEM((D_MODEL, D_EXPERT), F32), pltpu.VMEM((D_MODEL, D_EXPERT), F32),
            pltpu.VMEM((D_EXPERT, D_MODEL), F32),
            pltpu.VMEM((D_MODEL, D_EXPERT), BF16), pltpu.VMEM((D_MODEL, D_EXPERT), BF16),
            pltpu.VMEM((D_EXPERT, D_MODEL), BF16),
            pltpu.SemaphoreType.DMA((3,)),
        ],
    )
    return pl.pallas_call(
        functools.partial(_moe_kernel, layer=layer),
        grid_spec=grid_spec,
        out_shape=jax.ShapeDtypeStruct((n_blocks * MOE_BLOCK, D_MODEL), F32),
        compiler_params=pltpu.CompilerParams(
            dimension_semantics=("arbitrary",), vmem_limit_bytes=VMEM_LIMIT),
        name="moe_experts",
    )(block_expert, n_used, next_expert, xs, wg, wu, wd)


def _combine_kernel(slot_ref, h_ref, rt_ref, mod_ref, ys_hbm, o_ref, ybuf, sem):
    d = D_MODEL

    def issue(r, carry):
        for k in range(TOP_K):
            pltpu.make_async_copy(ys_hbm.at[pl.ds(slot_ref[0, k, r], 1)], ybuf.at[k, pl.ds(r, 1)], sem).start()
        return carry
    lax.fori_loop(0, ROW_TILE, issue, 0, unroll=8)
    for k in range(TOP_K):
        pltpu.make_async_copy(ys_hbm.at[pl.ds(0, ROW_TILE)], ybuf.at[k], sem).wait()
    rt = rt_ref[0]
    y = rt[:, 2:3] * ybuf[0] + rt[:, 3:4] * ybuf[1]
    o_ref[0] = h_ref[0] + mod_ref[0, :, 5 * d:6 * d] * y


def _combine_call(h, ys, slots, rt, mod, n_tiles):
    bsz = h.shape[0]
    tile = lambda w: pl.BlockSpec((1, ROW_TILE, w), lambda b, j: (b, j, 0))
    return pl.pallas_call(
        _combine_kernel,
        grid=(bsz, n_tiles),
        in_specs=[
            pl.BlockSpec((1, TOP_K, ROW_TILE), lambda b, j: (b * n_tiles + j, 0, 0), memory_space=pltpu.SMEM),
            tile(D_MODEL), tile(LANES),
            pl.BlockSpec((1, 1, 6 * D_MODEL), lambda b, j: (_mod_row(b, j), 0, 0)),
            pl.BlockSpec(memory_space=pl.ANY),
        ],
        out_specs=tile(D_MODEL),
        out_shape=jax.ShapeDtypeStruct((bsz, n_tiles * ROW_TILE, D_MODEL), F32),
        scratch_shapes=[pltpu.VMEM((TOP_K, ROW_TILE, D_MODEL), F32), pltpu.SemaphoreType.DMA(())],
        compiler_params=pltpu.CompilerParams(
            dimension_semantics=("arbitrary", "arbitrary"), vmem_limit_bytes=VMEM_LIMIT),
        name="moe_combine",
    )(slots, h, rt, mod, ys)


def _rope_tables():
    rows = SEQ // GRID_W
    row = jnp.repeat(jnp.arange(rows), GRID_W).astype(F32)
    col = jnp.tile(jnp.arange(GRID_W), rows).astype(F32)

    def tables(dim):
        n_freq = dim // 4
        inv = 1.0 / (ROPE_THETA ** (jnp.arange(n_freq, dtype=F32) / n_freq))
        ang_r = row[:, None] * inv
        ang_c = col[:, None] * inv
        ang = jnp.concatenate([ang_r, ang_r, ang_c, ang_c], axis=-1)
        sign = jnp.tile(jnp.concatenate([-jnp.ones((n_freq,), F32), jnp.ones((n_freq,), F32)]), 2)
        return jnp.cos(ang), jnp.sin(ang) * sign

    cos128, sin128 = tables(HEAD_DIM)
    cos64, sin64 = tables(B_ROPE)
    ones, zeros = jnp.ones((SEQ, 64), F32), jnp.zeros((SEQ, 64), F32)
    lat = jnp.concatenate([cos128, sin128,
                           cos64, cos64, sin64, sin64,
                           cos64, ones, sin64, zeros], axis=-1)
    ident = jnp.concatenate([jnp.ones((CTX_LEN, 128), F32), jnp.zeros((CTX_LEN, 128), F32)], axis=-1)
    return jnp.concatenate([lat, jnp.tile(ident, (1, 3))], axis=0)


def _pad_lanes(v, n):
    return jnp.pad(v, (0, n - v.shape[0]))


def _router_weights(wr_g, br_g, wr_e, br_e):
    w = jnp.pad(jnp.concatenate([wr_g, wr_e], axis=1), ((0, 0), (0, LANES - N_GROUPS - N_EXPERTS)))
    hi = w.astype(BF16)
    lo = (w - hi.astype(F32)).astype(BF16)
    b = _pad_lanes(jnp.concatenate([br_g, br_e]), LANES).reshape(1, LANES)
    return jnp.concatenate([hi, lo], axis=1), b


def _moe_layer(h, fx, rt, cnt, mod, wg, wu, wd, layer):
    bsz, n_rows, _ = fx.shape
    n_tiles = n_rows // ROW_TILE
    n_blocks = -(-(bsz * n_rows * TOP_K) // MOE_BLOCK) + N_EXPERTS
    experts = jnp.arange(N_EXPERTS, dtype=jnp.int32)
    counts = cnt[0, N_GROUPS:N_GROUPS + N_EXPERTS].astype(jnp.int32)
    padded = (counts + MOE_BLOCK - 1) // MOE_BLOCK * MOE_BLOCK
    pad_end = jnp.cumsum(padded)
    pad_start = pad_end - padded
    e_idx = rt[..., 0:TOP_K].astype(jnp.int32)
    rank = rt[..., 4:4 + TOP_K].astype(jnp.int32)
    slot = rank + jnp.sum(jnp.where(e_idx[..., None] == experts, pad_start, 0), axis=-1)
    slots = slot.reshape(bsz * n_tiles, ROW_TILE, TOP_K).transpose(0, 2, 1)
    block_start = jnp.arange(n_blocks, dtype=jnp.int32) * MOE_BLOCK
    block_expert = jnp.minimum(jnp.sum(pad_end[None, :] <= block_start[:, None], axis=1), N_EXPERTS - 1)
    n_used = pad_end[-1:] // MOE_BLOCK
    later = (experts[None, :] > experts[:, None]) & (counts[None, :] > 0)
    next_expert = jnp.where(jnp.any(later, axis=1), jnp.argmax(later, axis=1), -1)
    i32 = lambda a: a.astype(jnp.int32)
    xs = _dispatch_call(fx, i32(slots), i32(pad_start + counts), i32(pad_end), n_blocks)
    ys = _moe_call(xs, i32(block_expert), i32(n_used), i32(next_expert), wg, wu, wd, layer)
    return _combine_call(h, ys, i32(slots), rt, mod, n_tiles)


def kernel(x, c, ctx, c_ctx, mod_w, mod_b, even_w_in, even_w_out, a_q_norm, a_k_norm, b_cq_norm, b_w_uq,
           b_ckv_norm, b_w_ukv, b_q_norm, b_k_norm, odd_w_in, odd_w_out, c_q_norm, c_k_norm, c_lambda_q1,
           c_lambda_k1, c_lambda_q2, c_lambda_k2, c_subln, d_q_norm, d_k_norm, d_sink, moe_wr_group,
           moe_br_group, moe_wr_expert, moe_br_expert, moe_w_gate, moe_w_up, moe_w_down):
    bsz = x.shape[0]
    c_all = jnp.zeros((MOD_ROWS, D_MODEL), F32).at[:bsz].set(c).at[MOD_CTX_ROW].set(c_ctx)
    mod_all = _mod_call(c_all, mod_w, mod_b)
    tab = _rope_tables()

    i = 0
    mod = mod_all[0].reshape(MOD_ROWS, 1, 6 * D_MODEL)
    scale_a = HEAD_DIM ** -0.5 * LOG2_E
    scale_b = (B_NOPE + B_ROPE) ** -0.5 * LOG2_E
    win = jnp.pad(even_w_in[i], ((0, 0), (0, EVEN_IN_PAD - even_w_in.shape[-1]))).astype(BF16)
    wuq = b_w_uq[i].reshape(B_Q_LORA, B_HEADS, B_NOPE + B_ROPE)
    wuq = jnp.pad(wuq, ((0, 0), (0, 0), (0, B_QK_PAD - B_NOPE - B_ROPE))).reshape(B_Q_LORA, -1).astype(BF16)
    wukv = b_w_ukv[i].reshape(B_KV_LORA, B_HEADS, B_NOPE + B_V)
    wukv = jnp.concatenate([wukv[:, :, :B_NOPE].reshape(B_KV_LORA, -1),
                            wukv[:, :, B_NOPE:].reshape(B_KV_LORA, -1)], axis=1).astype(BF16)
    gains = jnp.stack([
        _pad_lanes(a_q_norm[i] * scale_a, 512), _pad_lanes(a_k_norm[i], 512),
        b_cq_norm[i], b_ckv_norm[i],
        _pad_lanes(b_q_norm[i] * scale_b, 512), _pad_lanes(b_k_norm[i], 512),
        jnp.zeros((512,), F32), jnp.zeros((512,), F32)])
    qa, ka, va, qb, kb, vb = _even_proj_call(x, ctx, mod, tab, gains, win, wuq, wukv)

    g_a = A_HEADS // A_KV_HEADS
    lat = dict(q_tile0=0, key_tile0=0, n_keys=TOK)
    oa = _attn_call(qa, ka, va, n_heads_kv=A_KV_HEADS, g=g_a, dk=HEAD_DIM, dv=HEAD_DIM, nkv=1,
                    tq=256, n_q_tiles=SEQ // 256, name="gqa_latent", **lat)
    ob = _attn_call(qb, kb, vb, n_heads_kv=B_HEADS, g=1, dk=B_QK_PAD, dv=B_V, nkv=2,
                    tq=512, n_q_tiles=SEQ // 512, name="mla_latent", **lat)
    cx = dict(tq=CTX_LEN, q_tile0=SEQ // CTX_LEN, n_q_tiles=1, key_tile0=SEQ // CTX_LEN, n_keys=CTX_LEN)
    oa_c = _attn_call(qa, ka, va, n_heads_kv=A_KV_HEADS, g=g_a, dk=HEAD_DIM, dv=HEAD_DIM, nkv=1,
                      name="gqa_context", **cx)
    ob_c = _attn_call(qb, kb, vb, n_heads_kv=B_HEADS, g=1, dk=B_QK_PAD, dv=B_V, nkv=2,
                      name="mla_context", **cx)

    w_out = even_w_out[i].astype(BF16)
    rw, rb = _router_weights(moe_wr_group[0], moe_br_group[0], moe_wr_expert[0], moe_br_expert[0])
    hn, fx, rt, cnt = _outproj_call((oa, ob, x), (oa_c, ob_c, ctx), mod,
                                    w_out[:A_HEADS * HEAD_DIM], w_out[A_HEADS * HEAD_DIM:],
                                    rw, rb, N_ROW_TILES)
    h = _moe_layer(hn, fx, rt, cnt, mod, moe_w_gate, moe_w_up, moe_w_down, 0)

    layer = 1
    mod = mod_all[1].reshape(MOD_ROWS, 1, 6 * D_MODEL)
    lambda_init = 0.8 - 0.6 * math.exp(-0.3 * layer)
    scale_c = C_HD ** -0.5 * LOG2_E
    scale_d = HEAD_DIM ** -0.5 * LOG2_E
    win = odd_w_in[i].astype(BF16)
    gains = jnp.stack([
        jnp.tile(c_q_norm[i] * scale_c, 2), jnp.tile(c_k_norm[i], 2),
        d_q_norm[i] * scale_d, d_k_norm[i],
        jnp.zeros((128,), F32), jnp.zeros((128,), F32), jnp.zeros((128,), F32), jnp.zeros((128,), F32)])
    qc, kc, vc, qd, kd, vd = _odd_proj_call(h, mod, tab, gains, win)

    aux = jnp.stack([
        _pad_lanes(c_lambda_q1[i], 128), _pad_lanes(c_lambda_k1[i], 128),
        _pad_lanes(c_lambda_q2[i], 128), _pad_lanes(c_lambda_k2[i], 128),
        c_subln[i], jnp.zeros((128,), F32), jnp.zeros((128,), F32), jnp.zeros((128,), F32)])
    oc = _attn_call(qc, kc, vc, n_heads_kv=C_HEADS, g=2, dk=LANES, dv=C_V, nkv=2,
                    tq=512, n_q_tiles=SEQ // 512, aux=aux, diff_lambda_init=lambda_init,
                    name="diff_latent", **lat)
    g_d = D_HEADS // D_KV_HEADS
    sink_col = jnp.repeat((d_sink[i] * LOG2_E).reshape(D_KV_HEADS, g_d), WIN_TQ, axis=1)
    sink_col = sink_col.reshape(D_KV_HEADS, g_d * WIN_TQ, 1)
    od = _window_call(qd, kd, vd, sink_col)

    w_out = odd_w_out[i].astype(BF16)
    rw, rb = _router_weights(moe_wr_group[1], moe_br_group[1], moe_wr_expert[1], moe_br_expert[1])
    n_lat_tiles = SEQ // ROW_TILE
    hn, fx, rt, cnt = _outproj_call((oc, od, h), None, mod, w_out[:C_HEADS * C_V], w_out[C_HEADS * C_V:],
                                    rw, rb, n_lat_tiles)
    return _moe_layer(hn, fx, rt, cnt, mod, moe_w_gate, moe_w_up, moe_w_down, 1)
```

```python
import functools
import math

import jax
import jax.numpy as jnp
from jax import lax
from jax.experimental import pallas as pl
from jax.experimental.pallas import tpu as pltpu

F32 = jnp.float32
BF16 = jnp.bfloat16

D_MODEL = 2048
BATCH = 8
SEQ = 2048
DEPTH = 2
GRID_W = 64
CTX_LEN = 256
TOK = SEQ + CTX_LEN
HEAD_DIM = 128
ROPE_THETA = 10000.0
NORM_EPS = 1e-6
A_HEADS = 8
A_KV_HEADS = 2
B_HEADS = 8
B_Q_LORA = 512
B_KV_LORA = 512
B_NOPE = 128
B_ROPE = 64
B_V = 128
B_QK_PAD = 256
C_HEADS = 8
C_HD = 64
C_V = 128
D_HEADS = 8
D_KV_HEADS = 2
WINDOW = 128
N_GROUPS = 4
EXPERTS_PER_GROUP = 8
N_EXPERTS = 32
D_EXPERT = 1024
MOE_BLOCK = 128
TOP_K = 2
LANES = 128
LOG2_E = math.log2(math.e)

ROW_TILE = 256
N_ROW_TILES = TOK // ROW_TILE
CTX_TILE = SEQ // ROW_TILE
MOD_ROWS = 16
MOD_CTX_ROW = BATCH
VMEM_LIMIT = 56 * 1024 * 1024


def _dot(a, b):
    return jnp.dot(a, b, preferred_element_type=F32)


def _dot_nt(a, b):
    return lax.dot_general(a, b, (((1,), (1,)), ((), ())), preferred_element_type=F32)


def _silu(x):
    return x / (1.0 + jnp.exp(-x))


def _rms(x, n):
    return x * lax.rsqrt(jnp.sum(x * x, axis=-1, keepdims=True) * (1.0 / n) + NORM_EPS)


def _modulate(x, shift, scale):
    return _rms(x, x.shape[-1]) * (1.0 + scale) + shift


def _rope(x, cos, sin_signed, quarter, lane):
    fwd = pltpu.roll(x, LANES - quarter, axis=1)
    bwd = pltpu.roll(x, quarter, axis=1)
    rot = jnp.where((lane // quarter) % 2 == 0, fwd, bwd)
    return x * cos + rot * sin_signed


def _start_row_gathers(slot_ref, src_hbm, dst_ref, sem):
    for r in range(ROW_TILE):
        for k in range(TOP_K):
            pltpu.make_async_copy(src_hbm.at[pl.ds(slot_ref[0, k, r], 1)], dst_ref.at[k, pl.ds(r, 1)], sem).start()


def _wait_row_gathers(src_hbm, dst_ref, sem):
    for k in range(TOP_K):
        pltpu.make_async_copy(src_hbm.at[pl.ds(0, ROW_TILE)], dst_ref.at[k], sem).wait()


MOD_TN = 1024


def _mod_kernel(c_ref, w_ref, b_ref, o_ref):
    a = _silu(c_ref[...]).astype(BF16)
    o_ref[0] = _dot(a, w_ref[0].astype(BF16)) + b_ref[0]


def _mod_call(c_all, mod_w, mod_b):
    d6 = 6 * D_MODEL
    return pl.pallas_call(
        _mod_kernel,
        grid=(DEPTH, d6 // MOD_TN),
        in_specs=[
            pl.BlockSpec((MOD_ROWS, D_MODEL), lambda l, n: (0, 0)),
            pl.BlockSpec((1, D_MODEL, MOD_TN), lambda l, n: (l, 0, n)),
            pl.BlockSpec((1, 1, MOD_TN), lambda l, n: (l, 0, n)),
        ],
        out_specs=pl.BlockSpec((1, MOD_ROWS, MOD_TN), lambda l, n: (l, 0, n)),
        out_shape=jax.ShapeDtypeStruct((DEPTH, MOD_ROWS, d6), F32),
        compiler_params=pltpu.CompilerParams(
            dimension_semantics=("parallel", "parallel"), vmem_limit_bytes=VMEM_LIMIT),
        name="mod_vectors",
    )(c_all, mod_w, mod_b.reshape(DEPTH, 1, d6))


def _mod_row(b, j):
    return jnp.where(j == CTX_TILE, MOD_CTX_ROW, b)


def _resident(shape):
    nd = len(shape)
    return pl.BlockSpec(shape, lambda *_: (0,) * nd, pipeline_mode=pl.Buffered(1))


EVEN_IN_PAD = 2688


def _even_proj_kernel(hx_ref, hc_ref, mod_ref, tab_ref, g_ref, win_ref, wuq_ref, wukv_ref,
                      qa_ref, ka_ref, va_ref, qb_ref, kb_ref, vb_ref):
    d = D_MODEL
    x = jnp.where(pl.program_id(1) == CTX_TILE, hc_ref[0], hx_ref[0])
    a = _modulate(x, mod_ref[0, :, 0:d], mod_ref[0, :, d:2 * d]).astype(BF16)
    lane = lax.broadcasted_iota(jnp.int32, (ROW_TILE, LANES), 1)
    cos128, sin128 = tab_ref[:, 0:128], tab_ref[:, 128:256]
    cos64p, sin64p = tab_ref[:, 512:640], tab_ref[:, 640:768]

    z = _dot(a, win_ref[:, 0:1024])
    g_q = g_ref[0:1, 0:128]
    for h in range(A_HEADS):
        blk = _rms(z[:, h * 128:(h + 1) * 128], HEAD_DIM) * g_q
        qa_ref[0, :, h * 128:(h + 1) * 128] = _rope(blk, cos128, sin128, 32, lane).astype(BF16)
    z = _dot(a, win_ref[:, 1024:1536])
    g_k = g_ref[1:2, 0:128]
    for h in range(A_KV_HEADS):
        blk = _rms(z[:, h * 128:(h + 1) * 128], HEAD_DIM) * g_k
        ka_ref[0, :, h * 128:(h + 1) * 128] = _rope(blk, cos128, sin128, 32, lane).astype(BF16)
    va_ref[0] = z[:, 256:512].astype(BF16)

    z = _dot(a, win_ref[:, 1536:2048])
    cq = (_rms(z, B_Q_LORA) * g_ref[2:3, :]).astype(BF16)
    zq = _dot(cq, wuq_ref[...])
    gq0, gq1 = g_ref[4:5, 0:128], g_ref[4:5, 128:256]
    n_qk = float(B_NOPE + B_ROPE)
    for h in range(B_HEADS):
        b0 = zq[:, h * 256:h * 256 + 128]
        b1 = zq[:, h * 256 + 128:(h + 1) * 256]
        ss = jnp.sum(b0 * b0, axis=-1, keepdims=True) + jnp.sum(b1 * b1, axis=-1, keepdims=True)
        r = lax.rsqrt(ss * (1.0 / n_qk) + NORM_EPS)
        qb_ref[0, :, h * 256:h * 256 + 128] = (b0 * r * gq0).astype(BF16)
        qb_ref[0, :, h * 256 + 128:(h + 1) * 256] = _rope(b1 * r * gq1, cos64p, sin64p, 16, lane).astype(BF16)

    z = _dot(a, win_ref[:, 2048:2688])
    ckv = (_rms(z[:, 0:512], B_KV_LORA) * g_ref[3:4, :]).astype(BF16)
    kr = z[:, 512:640]
    ss_kr = jnp.sum(kr * kr, axis=-1, keepdims=True)
    kr_rot = _rope(kr * g_ref[5:6, 128:256], cos64p, sin64p, 16, lane)
    zkv = _dot(ckv, wukv_ref[...])
    vb_ref[0] = zkv[:, 1024:2048].astype(BF16)
    gk0 = g_ref[5:6, 0:128]
    for h in range(B_HEADS):
        kn = zkv[:, h * 128:(h + 1) * 128]
        ss = jnp.sum(kn * kn, axis=-1, keepdims=True) + ss_kr
        r = lax.rsqrt(ss * (1.0 / n_qk) + NORM_EPS)
        kb_ref[0, :, h * 256:h * 256 + 128] = (kn * r * gk0).astype(BF16)
        kb_ref[0, :, h * 256 + 128:(h + 1) * 256] = (kr_rot * r).astype(BF16)


def _lat_tile(w):
    return pl.BlockSpec((1, ROW_TILE, w), lambda b, j: (b, jnp.minimum(j, CTX_TILE - 1), 0))


def _ctx_tile(w):
    return pl.BlockSpec((1, ROW_TILE, w), lambda b, j: (b, 0, 0))


def _even_proj_call(h_lat, h_ctx, mod, tab, gains, win, wuq, wukv):
    bsz = h_lat.shape[0]
    widths = (A_HEADS * HEAD_DIM, A_KV_HEADS * HEAD_DIM, A_KV_HEADS * HEAD_DIM,
              B_HEADS * B_QK_PAD, B_HEADS * B_QK_PAD, B_HEADS * B_V)
    tile = lambda w: pl.BlockSpec((1, ROW_TILE, w), lambda b, j: (b, j, 0))
    return pl.pallas_call(
        _even_proj_kernel,
        grid=(bsz, N_ROW_TILES),
        in_specs=[
            _lat_tile(D_MODEL), _ctx_tile(D_MODEL),
            pl.BlockSpec((1, 1, 6 * D_MODEL), lambda b, j: (_mod_row(b, j), 0, 0)),
            pl.BlockSpec((ROW_TILE, 768), lambda b, j: (j, 0)),
            _resident(gains.shape), _resident(win.shape), _resident(wuq.shape), _resident(wukv.shape),
        ],
        out_specs=[tile(w) for w in widths],
        out_shape=[jax.ShapeDtypeStruct((bsz, TOK, w), BF16) for w in widths],
        compiler_params=pltpu.CompilerParams(
            dimension_semantics=("parallel", "parallel"), vmem_limit_bytes=VMEM_LIMIT),
        name="even_proj",
    )(h_lat, h_ctx, mod, tab, gains, win, wuq, wukv)


def _rms_halves(x, lane):
    x2 = x * x
    s_lo = jnp.sum(jnp.where(lane < 64, x2, 0.0), axis=-1, keepdims=True)
    s_hi = jnp.sum(jnp.where(lane < 64, 0.0, x2), axis=-1, keepdims=True)
    r = jnp.where(lane < 64, lax.rsqrt(s_lo * (1.0 / C_HD) + NORM_EPS), lax.rsqrt(s_hi * (1.0 / C_HD) + NORM_EPS))
    return x * r


def _odd_proj_kernel(slot_ref, slot_next_ref, hn_ref, rt_ref, mod_prev_ref, mod_ref, tab_ref, g_ref, win_ref,
                     ys_hbm, h_ref, qc_ref, kc_ref, vc_ref, qd_ref, kd_ref, vd_ref, ybuf, sems):
    d = D_MODEL
    n_steps = pl.num_programs(0) * pl.num_programs(1)
    step = pl.program_id(0) * pl.num_programs(1) + pl.program_id(1)
    cur = step % 2

    @pl.when(step == 0)
    def _():
        _start_row_gathers(slot_ref, ys_hbm, ybuf.at[0], sems.at[0])

    _wait_row_gathers(ys_hbm, ybuf.at[cur], sems.at[cur])
    rt = rt_ref[0]
    y = rt[:, 2:3] * ybuf[cur, 0] + rt[:, 3:4] * ybuf[cur, 1]
    x = hn_ref[0] + mod_prev_ref[0, :, 5 * d:6 * d] * y
    h_ref[0] = x
    _start_row_gathers(slot_next_ref, ys_hbm, ybuf.at[1 - cur], sems.at[1 - cur])
    a = _modulate(x, mod_ref[0, :, 0:d], mod_ref[0, :, d:2 * d]).astype(BF16)
    lane = lax.broadcasted_iota(jnp.int32, (ROW_TILE, LANES), 1)
    cos128, sin128 = tab_ref[:, 0:128], tab_ref[:, 128:256]
    cos64, sin64 = tab_ref[:, 256:384], tab_ref[:, 384:512]

    z = _dot(a, win_ref[:, 0:1024])
    g_q = g_ref[0:1, :]
    for h in range(C_HEADS):
        blk = _rope(_rms_halves(z[:, h * 128:(h + 1) * 128], lane) * g_q, cos64, sin64, 16, lane)
        qc_ref[0, :, h * 256:h * 256 + 128] = jnp.where(lane < 64, blk, 0.0).astype(BF16)
        qc_ref[0, :, h * 256 + 128:(h + 1) * 256] = jnp.where(lane < 64, 0.0, blk).astype(BF16)
    z = _dot(a, win_ref[:, 1024:2048])
    g_k = g_ref[1:2, :]
    for h in range(C_HEADS):
        blk = _rope(_rms_halves(z[:, h * 128:(h + 1) * 128], lane) * g_k, cos64, sin64, 16, lane)
        kc_ref[0, :, h * 128:(h + 1) * 128] = blk.astype(BF16)
    vc_ref[0] = _dot(a, win_ref[:, 2048:3072]).astype(BF16)

    z = _dot(a, win_ref[:, 3072:4096])
    g_q = g_ref[2:3, :]
    for h in range(D_HEADS):
        blk = _rms(z[:, h * 128:(h + 1) * 128], HEAD_DIM) * g_q
        qd_ref[0, :, h * 128:(h + 1) * 128] = _rope(blk, cos128, sin128, 32, lane).astype(BF16)
    z = _dot(a, win_ref[:, 4096:4608])
    g_k = g_ref[3:4, :]
    for h in range(D_KV_HEADS):
        blk = _rms(z[:, h * 128:(h + 1) * 128], HEAD_DIM) * g_k
        kd_ref[0, :, h * 128:(h + 1) * 128] = _rope(blk, cos128, sin128, 32, lane).astype(BF16)
    vd_ref[0] = z[:, 256:512].astype(BF16)

    @pl.when(step == n_steps - 1)
    def _():
        _wait_row_gathers(ys_hbm, ybuf.at[1 - cur], sems.at[1 - cur])


def _odd_proj_call(hn, ys, slots, rt, mod_prev, mod, tab, gains, win):
    bsz = hn.shape[0]
    widths = (C_HEADS * 2 * LANES, C_HEADS * LANES, C_HEADS * C_V,
              D_HEADS * HEAD_DIM, D_KV_HEADS * HEAD_DIM, D_KV_HEADS * HEAD_DIM)
    n_steps = bsz * N_ROW_TILES
    tile = lambda w: pl.BlockSpec((1, ROW_TILE, w), lambda b, j: (b, j, 0))
    mod_spec = pl.BlockSpec((1, 1, 6 * D_MODEL), lambda b, j: (_mod_row(b, j), 0, 0))
    slot_spec = lambda ahead: pl.BlockSpec(
        (1, TOP_K, ROW_TILE), lambda b, j: (jnp.minimum(b * N_ROW_TILES + j + ahead, n_steps - 1), 0, 0),
        memory_space=pltpu.SMEM)
    return pl.pallas_call(
        _odd_proj_kernel,
        grid=(bsz, N_ROW_TILES),
        in_specs=[
            slot_spec(0), slot_spec(1), tile(D_MODEL), tile(LANES), mod_spec, mod_spec,
            pl.BlockSpec((ROW_TILE, 768), lambda b, j: (j, 0)),
            _resident(gains.shape), _resident(win.shape),
            pl.BlockSpec(memory_space=pl.ANY),
        ],
        out_specs=[tile(D_MODEL)] + [tile(w) for w in widths],
        out_shape=[jax.ShapeDtypeStruct((bsz, TOK, D_MODEL), F32)]
                  + [jax.ShapeDtypeStruct((bsz, TOK, w), BF16) for w in widths],
        scratch_shapes=[pltpu.VMEM((2, TOP_K, ROW_TILE, D_MODEL), F32), pltpu.SemaphoreType.DMA((2,))],
        compiler_params=pltpu.CompilerParams(
            dimension_semantics=("arbitrary", "arbitrary"), vmem_limit_bytes=VMEM_LIMIT),
        name="odd_proj",
    )(slots, slots, hn, rt, mod_prev, mod, tab, gains, win, ys)


KEY_CHUNK = 768


def _attn_kernel(*refs, nkv, g, dk, dv, tq, n_keys, diff_lambda_init):
    if diff_lambda_init is None:
        q_ref, k_ref, v_ref, o_ref = refs
    else:
        q_ref, k_ref, v_ref, aux_ref, o_ref = refs
    for kv in range(nkv):
        q = jnp.concatenate(
            [q_ref[0, :, (kv * g + gi) * dk:(kv * g + gi + 1) * dk] for gi in range(g)], axis=0)
        m = l = acc = None
        for c0 in range(0, n_keys, KEY_CHUNK):
            c1 = min(c0 + KEY_CHUNK, n_keys)
            s = _dot_nt(q, k_ref[0, c0:c1, kv * dk:(kv + 1) * dk])
            v = v_ref[0, c0:c1, kv * dv:(kv + 1) * dv]
            m_c = jnp.max(s, axis=-1, keepdims=True)
            if m is None:
                m = m_c
                p = jnp.exp2(s - m)
                l = jnp.sum(p, axis=-1, keepdims=True)
                acc = _dot(p.astype(BF16), v)
            else:
                m_new = jnp.maximum(m, m_c)
                alpha = jnp.exp2(m - m_new)
                p = jnp.exp2(s - m_new)
                l = alpha * l + jnp.sum(p, axis=-1, keepdims=True)
                acc = alpha * acc + _dot(p.astype(BF16), v)
                m = m_new
        o = acc / l
        if diff_lambda_init is None:
            for gi in range(g):
                o_ref[0, :, (kv * g + gi) * dv:(kv * g + gi + 1) * dv] = o[gi * tq:(gi + 1) * tq].astype(BF16)
        else:
            lam = (jnp.exp(jnp.sum(aux_ref[0:1, :] * aux_ref[1:2, :], axis=-1, keepdims=True))
                   - jnp.exp(jnp.sum(aux_ref[2:3, :] * aux_ref[3:4, :], axis=-1, keepdims=True))
                   + diff_lambda_init)
            od = o[0:tq] - lam * o[tq:2 * tq]
            od = _rms(od, dv) * aux_ref[4:5, :] * (1.0 - diff_lambda_init)
            o_ref[0, :, kv * dv:(kv + 1) * dv] = od.astype(BF16)


def _attn_call(q, k, v, *, n_heads_kv, g, dk, dv, nkv, tq, q_tile0, n_q_tiles, key_tile0, n_keys,
               aux=None, diff_lambda_init=None, name="attn"):
    bsz = q.shape[0]
    n_out_heads = n_heads_kv * (g if diff_lambda_init is None else 1)
    out_w = nkv * (g if diff_lambda_init is None else 1) * dv
    kern = functools.partial(_attn_kernel, nkv=nkv, g=g, dk=dk, dv=dv, tq=tq, n_keys=n_keys,
                             diff_lambda_init=diff_lambda_init)
    in_specs = [
        pl.BlockSpec((1, tq, nkv * g * dk), lambda b, hh, qi: (b, qi + q_tile0, hh)),
        pl.BlockSpec((1, n_keys, nkv * dk), lambda b, hh, qi: (b, key_tile0, hh)),
        pl.BlockSpec((1, n_keys, nkv * dv), lambda b, hh, qi: (b, key_tile0, hh)),
    ]
    args = [q, k, v]
    if aux is not None:
        in_specs.append(pl.BlockSpec(aux.shape, lambda b, hh, qi: (0, 0)))
        args.append(aux)
    return pl.pallas_call(
        kern,
        grid=(bsz, n_heads_kv // nkv, n_q_tiles),
        in_specs=in_specs,
        out_specs=pl.BlockSpec((1, tq, out_w), lambda b, hh, qi: (b, qi, hh)),
        out_shape=jax.ShapeDtypeStruct((bsz, n_q_tiles * tq, n_out_heads * dv), BF16),
        compiler_params=pltpu.CompilerParams(
            dimension_semantics=("parallel", "parallel", "parallel"), vmem_limit_bytes=VMEM_LIMIT),
        name=name,
    )(*args)


WIN_TQ = 256
WIN_BAND = WIN_TQ + 2 * WINDOW
NEG_BIG = -1e30


def _window_kernel(q_ref, k_ref, v_ref, sink_ref, o_ref):
    g = D_HEADS // D_KV_HEADS
    dk = HEAD_DIM
    qi = pl.program_id(2)
    q0 = qi * WIN_TQ
    start = pl.multiple_of(jnp.clip(q0 - WINDOW, 0, SEQ - WIN_BAND), WINDOW)
    q = jnp.concatenate([q_ref[0, :, gi * dk:(gi + 1) * dk] for gi in range(g)], axis=0)
    s_loc = _dot_nt(q, k_ref[0, pl.ds(start, WIN_BAND), :])
    s_ctx = _dot_nt(q, k_ref[0, SEQ:TOK, :])
    row = (lax.broadcasted_iota(jnp.int32, (g * WIN_TQ, WIN_BAND), 0) & (WIN_TQ - 1)) + q0
    col = lax.broadcasted_iota(jnp.int32, (g * WIN_TQ, WIN_BAND), 1) + start
    s_loc = jnp.where(jnp.abs(row - col) <= WINDOW, s_loc, NEG_BIG)
    sink = sink_ref[0]
    m = jnp.maximum(jnp.maximum(jnp.max(s_loc, axis=-1, keepdims=True),
                                jnp.max(s_ctx, axis=-1, keepdims=True)), sink)
    p_loc = jnp.exp2(s_loc - m)
    p_ctx = jnp.exp2(s_ctx - m)
    l = (jnp.sum(p_loc, axis=-1, keepdims=True) + jnp.sum(p_ctx, axis=-1, keepdims=True)
         + jnp.exp2(sink - m))
    acc = (_dot(p_ctx.astype(BF16), v_ref[0, SEQ:TOK, :])
           + _dot(p_loc.astype(BF16), v_ref[0, pl.ds(start, WIN_BAND), :]))
    o = acc / l
    for gi in range(g):
        o_ref[0, :, gi * dk:(gi + 1) * dk] = o[gi * WIN_TQ:(gi + 1) * WIN_TQ].astype(BF16)


def _window_call(q, k, v, sink_col):
    bsz = q.shape[0]
    g = D_HEADS // D_KV_HEADS
    return pl.pallas_call(
        _window_kernel,
        grid=(bsz, D_KV_HEADS, SEQ // WIN_TQ),
        in_specs=[
            pl.BlockSpec((1, WIN_TQ, g * HEAD_DIM), lambda b, hh, qi: (b, qi, hh)),
            pl.BlockSpec((1, TOK, HEAD_DIM), lambda b, hh, qi: (b, 0, hh)),
            pl.BlockSpec((1, TOK, HEAD_DIM), lambda b, hh, qi: (b, 0, hh)),
            pl.BlockSpec((1, g * WIN_TQ, 1), lambda b, hh, qi: (hh, 0, 0)),
        ],
        out_specs=pl.BlockSpec((1, WIN_TQ, g * HEAD_DIM), lambda b, hh, qi: (b, qi, hh)),
        out_shape=jax.ShapeDtypeStruct((bsz, SEQ, D_HEADS * HEAD_DIM), BF16),
        compiler_params=pltpu.CompilerParams(
            dimension_semantics=("parallel", "parallel", "parallel"), vmem_limit_bytes=VMEM_LIMIT),
        name="window_attn",
    )(q, k, v, sink_col)


def _route(logits, lane):
    lane_f = lane.astype(F32)
    lg = jnp.where(lane < N_GROUPS, logits, NEG_BIG)
    g_max = jnp.max(lg, axis=-1, keepdims=True)
    p_top = 1.0 / jnp.sum(jnp.exp(lg - g_max), axis=-1, keepdims=True)
    g_idx = jnp.min(jnp.where(lg == g_max, lane_f, float(LANES)), axis=-1, keepdims=True)
    e_lane = lane - N_GROUPS
    in_group = (e_lane >= 0) & (e_lane < N_EXPERTS) & ((e_lane // EXPERTS_PER_GROUP).astype(F32) == g_idx)
    le = jnp.where(in_group, logits, NEG_BIG)
    m1 = jnp.max(le, axis=-1, keepdims=True)
    i1 = jnp.min(jnp.where(le == m1, lane_f, float(LANES)), axis=-1, keepdims=True)
    le2 = jnp.where(lane_f == i1, NEG_BIG, le)
    m2 = jnp.max(le2, axis=-1, keepdims=True)
    i2 = jnp.min(jnp.where(le2 == m2, lane_f, float(LANES)), axis=-1, keepdims=True)
    t = jnp.exp(m2 - m1)
    w1 = p_top / (1.0 + t)
    w2 = p_top * t / (1.0 + t)
    return i1, i2, w1, w2


def _outproj_kernel(*refs, has_ctx):
    if has_ctx:
        (o1_ref, o2_ref, h_ref, o1c_ref, o2c_ref, hc_ref, mod_ref, w1_ref, w2_ref, rw_ref, rb_ref,
         hn_ref, fx_ref, rt_ref, cnt_ref, run_ref) = refs
    else:
        (o1_ref, o2_ref, h_ref, mod_ref, w1_ref, w2_ref, rw_ref, rb_ref,
         hn_ref, fx_ref, rt_ref, cnt_ref, run_ref) = refs
    d = D_MODEL
    b, j = pl.program_id(0), pl.program_id(1)

    @pl.when((b == 0) & (j == 0))
    def _():
        run_ref[...] = jnp.zeros_like(run_ref)

    o1, o2, h = o1_ref[0], o2_ref[0], h_ref[0]
    if has_ctx:
        is_ctx = j == CTX_TILE
        o1 = jnp.where(is_ctx, o1c_ref[0], o1)
        o2 = jnp.where(is_ctx, o2c_ref[0], o2)
        h = jnp.where(is_ctx, hc_ref[0], h)
    acc = _dot(o1, w1_ref[...]) + _dot(o2, w2_ref[...])
    hn = h + mod_ref[0, :, 2 * d:3 * d] * acc
    hn_ref[0] = hn
    fx = _modulate(hn, mod_ref[0, :, 3 * d:4 * d], mod_ref[0, :, 4 * d:5 * d])
    fx_ref[0] = fx
    hi = fx.astype(BF16)
    lo = (fx - hi.astype(F32)).astype(BF16)
    part = _dot(jnp.concatenate([hi, lo], axis=0), rw_ref[...])
    logits = (part[:ROW_TILE, :LANES] + part[:ROW_TILE, LANES:]
              + part[ROW_TILE:, :LANES] + part[ROW_TILE:, LANES:] + rb_ref[...])
    lane = lax.broadcasted_iota(jnp.int32, (ROW_TILE, LANES), 1)
    lane_f = lane.astype(F32)
    i1, i2, w1, w2 = _route(logits, lane)

    hit1, hit2 = lane_f == i1, lane_f == i2
    multi_hot = jnp.where(hit1, 1.0, 0.0) + jnp.where(hit2, 1.0, 0.0)
    tri = (lax.broadcasted_iota(jnp.int32, (ROW_TILE, ROW_TILE), 0)
           > lax.broadcasted_iota(jnp.int32, (ROW_TILE, ROW_TILE), 1))
    before = _dot(jnp.where(tri, 1.0, 0.0).astype(BF16), multi_hot.astype(BF16)) + run_ref[...]
    r1 = jnp.sum(jnp.where(hit1, before, 0.0), axis=-1, keepdims=True)
    r2 = jnp.sum(jnp.where(hit2, before, 0.0), axis=-1, keepdims=True)
    run_ref[...] = run_ref[...] + jnp.sum(multi_hot, axis=0, keepdims=True)
    cnt_ref[...] = jnp.broadcast_to(run_ref[...], cnt_ref.shape)

    out = jnp.where(lane == 0, i1 - N_GROUPS, 0.0)
    out = jnp.where(lane == 1, i2 - N_GROUPS, out)
    out = jnp.where(lane == 2, w1, out)
    out = jnp.where(lane == 3, w2, out)
    out = jnp.where(lane == 4, r1, out)
    out = jnp.where(lane == 5, r2, out)
    rt_ref[0] = out


def _outproj_call(lat, ctx, mod, w1, w2, rw, rb, n_tiles):
    bsz = lat[2].shape[0]
    has_ctx = ctx is not None
    tile = lambda w: pl.BlockSpec((1, ROW_TILE, w), lambda b, j: (b, j, 0))
    in_specs = [_lat_tile(a.shape[-1]) for a in lat]
    args = list(lat)
    if has_ctx:
        in_specs += [_ctx_tile(a.shape[-1]) for a in ctx]
        args += list(ctx)
    in_specs += [
        pl.BlockSpec((1, 1, 6 * D_MODEL), lambda b, j: (_mod_row(b, j), 0, 0)),
        _resident(w1.shape), _resident(w2.shape), _resident(rw.shape), _resident(rb.shape),
    ]
    args += [mod, w1, w2, rw, rb]
    n_rows = n_tiles * ROW_TILE
    return pl.pallas_call(
        functools.partial(_outproj_kernel, has_ctx=has_ctx),
        grid=(bsz, n_tiles),
        in_specs=in_specs,
        out_specs=[tile(D_MODEL), tile(D_MODEL), tile(LANES),
                   pl.BlockSpec((8, LANES), lambda b, j: (0, 0))],
        out_shape=[jax.ShapeDtypeStruct((bsz, n_rows, D_MODEL), F32),
                   jax.ShapeDtypeStruct((bsz, n_rows, D_MODEL), F32),
                   jax.ShapeDtypeStruct((bsz, n_rows, LANES), F32),
                   jax.ShapeDtypeStruct((8, LANES), F32)],
        scratch_shapes=[pltpu.VMEM((1, LANES), F32)],
        compiler_params=pltpu.CompilerParams(
            dimension_semantics=("arbitrary", "arbitrary"), vmem_limit_bytes=VMEM_LIMIT),
        name="outproj_router",
    )(*args)


def _dispatch_kernel(lo_ref, hi_ref, slot_ref, fx_ref, xs_hbm, zbuf, sem, zsem):
    b, j = pl.program_id(0), pl.program_id(1)

    for r in range(ROW_TILE):
        for k in range(TOP_K):
            pltpu.make_async_copy(fx_ref.at[0, pl.ds(r, 1)], xs_hbm.at[pl.ds(slot_ref[0, k, r], 1)], sem).start()

    @pl.when((b == pl.num_programs(0) - 1) & (j == pl.num_programs(1) - 1))
    def _():
        zbuf[...] = jnp.zeros_like(zbuf)

        def fill(s, carry):
            pltpu.make_async_copy(zbuf.at[pl.ds(0, 1)], xs_hbm.at[pl.ds(s, 1)], zsem).start()
            return carry

        def drain(s, carry):
            pltpu.make_async_copy(zbuf.at[pl.ds(0, 1)], xs_hbm.at[pl.ds(0, 1)], zsem).wait()
            return carry
        for e in range(N_EXPERTS):
            lax.fori_loop(lo_ref[e], hi_ref[e], fill, 0)
        for e in range(N_EXPERTS):
            lax.fori_loop(lo_ref[e], hi_ref[e], drain, 0)

    for k in range(TOP_K):
        pltpu.make_async_copy(fx_ref.at[0], xs_hbm.at[pl.ds(0, ROW_TILE)], sem).wait()


def _dispatch_call(fx, slots, fill_lo, fill_hi, n_blocks):
    bsz, n_rows, _ = fx.shape
    n_tiles = n_rows // ROW_TILE
    grid_spec = pltpu.PrefetchScalarGridSpec(
        num_scalar_prefetch=2,
        grid=(bsz, n_tiles),
        in_specs=[
            pl.BlockSpec((1, TOP_K, ROW_TILE), lambda b, j, lo, hi: (b * n_tiles + j, 0, 0),
                         memory_space=pltpu.SMEM),
            pl.BlockSpec((1, ROW_TILE, D_MODEL), lambda b, j, lo, hi: (b, j, 0)),
        ],
        out_specs=pl.BlockSpec(memory_space=pl.ANY),
        scratch_shapes=[pltpu.VMEM((8, D_MODEL), F32), pltpu.SemaphoreType.DMA(()), pltpu.SemaphoreType.DMA(())],
    )
    return pl.pallas_call(
        _dispatch_kernel,
        grid_spec=grid_spec,
        out_shape=jax.ShapeDtypeStruct((n_blocks * MOE_BLOCK, D_MODEL), F32),
        compiler_params=pltpu.CompilerParams(
            dimension_semantics=("arbitrary", "arbitrary"), vmem_limit_bytes=VMEM_LIMIT),
        name="moe_dispatch",
    )(fill_lo, fill_hi, slots, fx)


CAST_ROWS = 256


def _moe_kernel(be_ref, nu_ref, nx_ref, x_ref, wg_hbm, wu_hbm, wd_hbm, y_ref,
                sg, su, sd, wg, wu, wd, sems, *, layer):
    i = pl.program_id(0)
    e = be_ref[i]
    used = i < nu_ref[0]
    staged = ((wg_hbm, sg, wg, 0), (wu_hbm, su, wu, 1), (wd_hbm, sd, wd, 2))

    def fetch(expert):
        for hbm, stage, _, s in staged:
            pltpu.make_async_copy(hbm.at[layer, expert], stage, sems.at[s]).start(priority=1)

    @pl.when(i == 0)
    def _():
        fetch(e)

    first_of_expert = (i == 0) | (e != be_ref[jnp.maximum(i - 1, 0)])

    @pl.when(used & first_of_expert)
    def _():
        for hbm, stage, dst, s in staged:
            pltpu.make_async_copy(hbm.at[layer, 0], stage, sems.at[s]).wait()

            def cast(c, carry):
                rows = pl.ds(pl.multiple_of(c * CAST_ROWS, CAST_ROWS), CAST_ROWS)
                dst[rows, :] = stage[rows, :].astype(BF16)
                return carry
            lax.fori_loop(0, stage.shape[0] // CAST_ROWS, cast, 0)
        nxt = nx_ref[e]

        @pl.when(nxt >= 0)
        def _():
            fetch(nxt)

    @pl.when(used)
    def _():
        x = x_ref[...].astype(BF16)
        gt = _dot(x, wg[...])
        up = _dot(x, wu[...])
        u = (_silu(gt) * up).astype(BF16)
        y_ref[...] = _dot(u, wd[...])


def _moe_call(xs, block_expert, n_used, next_expert, wg, wu, wd, layer):
    n_blocks = block_expert.shape[0]
    row_blk = pl.BlockSpec((MOE_BLOCK, D_MODEL), lambda i, be, nu, nx: (jnp.minimum(i, nu[0] - 1), 0))
    hbm = pl.BlockSpec(memory_space=pl.ANY)
    grid_spec = pltpu.PrefetchScalarGridSpec(
        num_scalar_prefetch=3,
        grid=(n_blocks,),
        in_specs=[row_blk, hbm, hbm, hbm],
        out_specs=row_blk,
        scratch_shapes=[
            pltpu.VMEM((D_MODEL, D_EXPERT), F32), pltpu.VMEM((D_MODEL, D_EXPERT), F32),
            pltpu.VMEM((D_EXPERT, D_MODEL), F32),
            pltpu.VMEM((D_MODEL, D_EXPERT), BF16), pltpu.VMEM((D_MODEL, D_EXPERT), BF16),
            pltpu.VMEM((D_EXPERT, D_MODEL), BF16),
            pltpu.SemaphoreType.DMA((3,)),
        ],
    )
    return pl.pallas_call(
        functools.partial(_moe_kernel, layer=layer),
        grid_spec=grid_spec,
        out_shape=jax.ShapeDtypeStruct((n_blocks * MOE_BLOCK, D_MODEL), F32),
        compiler_params=pltpu.CompilerParams(
            dimension_semantics=("arbitrary",), vmem_limit_bytes=VMEM_LIMIT),
        name="moe_experts",
    )(block_expert, n_used, next_expert, xs, wg, wu, wd)


def _combine_kernel(slot_ref, h_ref, rt_ref, mod_ref, ys_hbm, o_ref, ybuf, sem):
    d = D_MODEL
    _start_row_gathers(slot_ref, ys_hbm, ybuf, sem)
    _wait_row_gathers(ys_hbm, ybuf, sem)
    rt = rt_ref[0]
    y = rt[:, 2:3] * ybuf[0] + rt[:, 3:4] * ybuf[1]
    o_ref[0] = h_ref[0] + mod_ref[0, :, 5 * d:6 * d] * y


def _combine_call(h, ys, slots, rt, mod, n_tiles):
    bsz = h.shape[0]
    tile = lambda w: pl.BlockSpec((1, ROW_TILE, w), lambda b, j: (b, j, 0))
    return pl.pallas_call(
        _combine_kernel,
        grid=(bsz, n_tiles),
        in_specs=[
            pl.BlockSpec((1, TOP_K, ROW_TILE), lambda b, j: (b * n_tiles + j, 0, 0), memory_space=pltpu.SMEM),
            tile(D_MODEL), tile(LANES),
            pl.BlockSpec((1, 1, 6 * D_MODEL), lambda b, j: (_mod_row(b, j), 0, 0)),
            pl.BlockSpec(memory_space=pl.ANY),
        ],
        out_specs=tile(D_MODEL),
        out_shape=jax.ShapeDtypeStruct((bsz, n_tiles * ROW_TILE, D_MODEL), F32),
        scratch_shapes=[pltpu.VMEM((TOP_K, ROW_TILE, D_MODEL), F32), pltpu.SemaphoreType.DMA(())],
        compiler_params=pltpu.CompilerParams(
            dimension_semantics=("arbitrary", "arbitrary"), vmem_limit_bytes=VMEM_LIMIT),
        name="moe_combine",
    )(slots, h, rt, mod, ys)


def _rope_tables():
    rows = SEQ // GRID_W
    row = jnp.repeat(jnp.arange(rows), GRID_W).astype(F32)
    col = jnp.tile(jnp.arange(GRID_W), rows).astype(F32)

    def tables(dim):
        n_freq = dim // 4
        inv = 1.0 / (ROPE_THETA ** (jnp.arange(n_freq, dtype=F32) / n_freq))
        ang_r = row[:, None] * inv
        ang_c = col[:, None] * inv
        ang = jnp.concatenate([ang_r, ang_r, ang_c, ang_c], axis=-1)
        sign = jnp.tile(jnp.concatenate([-jnp.ones((n_freq,), F32), jnp.ones((n_freq,), F32)]), 2)
        return jnp.cos(ang), jnp.sin(ang) * sign

    cos128, sin128 = tables(HEAD_DIM)
    cos64, sin64 = tables(B_ROPE)
    ones, zeros = jnp.ones((SEQ, 64), F32), jnp.zeros((SEQ, 64), F32)
    lat = jnp.concatenate([cos128, sin128,
                           cos64, cos64, sin64, sin64,
                           cos64, ones, sin64, zeros], axis=-1)
    ident = jnp.concatenate([jnp.ones((CTX_LEN, 128), F32), jnp.zeros((CTX_LEN, 128), F32)], axis=-1)
    return jnp.concatenate([lat, jnp.tile(ident, (1, 3))], axis=0)


def _pad_lanes(v, n):
    return jnp.pad(v, (0, n - v.shape[0]))


def _router_weights(wr_g, br_g, wr_e, br_e):
    w = jnp.pad(jnp.concatenate([wr_g, wr_e], axis=1), ((0, 0), (0, LANES - N_GROUPS - N_EXPERTS)))
    hi = w.astype(BF16)
    lo = (w - hi.astype(F32)).astype(BF16)
    b = _pad_lanes(jnp.concatenate([br_g, br_e]), LANES).reshape(1, LANES)
    return jnp.concatenate([hi, lo], axis=1), b


def _moe_experts(fx, rt, cnt, wg, wu, wd, layer):
    bsz, n_rows, _ = fx.shape
    n_tiles = n_rows // ROW_TILE
    n_blocks = -(-(bsz * n_rows * TOP_K) // MOE_BLOCK) + N_EXPERTS
    experts = jnp.arange(N_EXPERTS, dtype=jnp.int32)
    counts = cnt[0, N_GROUPS:N_GROUPS + N_EXPERTS].astype(jnp.int32)
    padded = (counts + MOE_BLOCK - 1) // MOE_BLOCK * MOE_BLOCK
    pad_end = jnp.cumsum(padded)
    pad_start = pad_end - padded
    e_idx = rt[..., 0:TOP_K].astype(jnp.int32)
    rank = rt[..., 4:4 + TOP_K].astype(jnp.int32)
    slot = rank + jnp.sum(jnp.where(e_idx[..., None] == experts, pad_start, 0), axis=-1)
    slots = slot.reshape(bsz * n_tiles, ROW_TILE, TOP_K).transpose(0, 2, 1)
    block_start = jnp.arange(n_blocks, dtype=jnp.int32) * MOE_BLOCK
    block_expert = jnp.minimum(jnp.sum(pad_end[None, :] <= block_start[:, None], axis=1), N_EXPERTS - 1)
    n_used = pad_end[-1:] // MOE_BLOCK
    later = (experts[None, :] > experts[:, None]) & (counts[None, :] > 0)
    next_expert = jnp.where(jnp.any(later, axis=1), jnp.argmax(later, axis=1), -1)
    i32 = lambda a: a.astype(jnp.int32)
    xs = _dispatch_call(fx, i32(slots), i32(pad_start + counts), i32(pad_end), n_blocks)
    ys = _moe_call(xs, i32(block_expert), i32(n_used), i32(next_expert), wg, wu, wd, layer)
    return ys, i32(slots)


def kernel(x, c, ctx, c_ctx, mod_w, mod_b, even_w_in, even_w_out, a_q_norm, a_k_norm, b_cq_norm, b_w_uq,
           b_ckv_norm, b_w_ukv, b_q_norm, b_k_norm, odd_w_in, odd_w_out, c_q_norm, c_k_norm, c_lambda_q1,
           c_lambda_k1, c_lambda_q2, c_lambda_k2, c_subln, d_q_norm, d_k_norm, d_sink, moe_wr_group,
           moe_br_group, moe_wr_expert, moe_br_expert, moe_w_gate, moe_w_up, moe_w_down):
    bsz = x.shape[0]
    c_all = jnp.zeros((MOD_ROWS, D_MODEL), F32).at[:bsz].set(c).at[MOD_CTX_ROW].set(c_ctx)
    mod_all = _mod_call(c_all, mod_w, mod_b)
    tab = _rope_tables()

    i = 0
    mod = mod_all[0].reshape(MOD_ROWS, 1, 6 * D_MODEL)
    scale_a = HEAD_DIM ** -0.5 * LOG2_E
    scale_b = (B_NOPE + B_ROPE) ** -0.5 * LOG2_E
    win = jnp.pad(even_w_in[i], ((0, 0), (0, EVEN_IN_PAD - even_w_in.shape[-1]))).astype(BF16)
    wuq = b_w_uq[i].reshape(B_Q_LORA, B_HEADS, B_NOPE + B_ROPE)
    wuq = jnp.pad(wuq, ((0, 0), (0, 0), (0, B_QK_PAD - B_NOPE - B_ROPE))).reshape(B_Q_LORA, -1).astype(BF16)
    wukv = b_w_ukv[i].reshape(B_KV_LORA, B_HEADS, B_NOPE + B_V)
    wukv = jnp.concatenate([wukv[:, :, :B_NOPE].reshape(B_KV_LORA, -1),
                            wukv[:, :, B_NOPE:].reshape(B_KV_LORA, -1)], axis=1).astype(BF16)
    gains = jnp.stack([
        _pad_lanes(a_q_norm[i] * scale_a, 512), _pad_lanes(a_k_norm[i], 512),
        b_cq_norm[i], b_ckv_norm[i],
        _pad_lanes(b_q_norm[i] * scale_b, 512), _pad_lanes(b_k_norm[i], 512),
        jnp.zeros((512,), F32), jnp.zeros((512,), F32)])
    qa, ka, va, qb, kb, vb = _even_proj_call(x, ctx, mod, tab, gains, win, wuq, wukv)

    g_a = A_HEADS // A_KV_HEADS
    lat = dict(q_tile0=0, key_tile0=0, n_keys=TOK)
    oa = _attn_call(qa, ka, va, n_heads_kv=A_KV_HEADS, g=g_a, dk=HEAD_DIM, dv=HEAD_DIM, nkv=1,
                    tq=256, n_q_tiles=SEQ // 256, name="gqa_latent", **lat)
    ob = _attn_call(qb, kb, vb, n_heads_kv=B_HEADS, g=1, dk=B_QK_PAD, dv=B_V, nkv=2,
                    tq=512, n_q_tiles=SEQ // 512, name="mla_latent", **lat)
    cx = dict(tq=CTX_LEN, q_tile0=SEQ // CTX_LEN, n_q_tiles=1, key_tile0=SEQ // CTX_LEN, n_keys=CTX_LEN)
    oa_c = _attn_call(qa, ka, va, n_heads_kv=A_KV_HEADS, g=g_a, dk=HEAD_DIM, dv=HEAD_DIM, nkv=1,
                      name="gqa_context", **cx)
    ob_c = _attn_call(qb, kb, vb, n_heads_kv=B_HEADS, g=1, dk=B_QK_PAD, dv=B_V, nkv=2,
                      name="mla_context", **cx)

    w_out = even_w_out[i].astype(BF16)
    rw, rb = _router_weights(moe_wr_group[0], moe_br_group[0], moe_wr_expert[0], moe_br_expert[0])
    hn, fx, rt, cnt = _outproj_call((oa, ob, x), (oa_c, ob_c, ctx), mod,
                                    w_out[:A_HEADS * HEAD_DIM], w_out[A_HEADS * HEAD_DIM:],
                                    rw, rb, N_ROW_TILES)
    ys, slots = _moe_experts(fx, rt, cnt, moe_w_gate, moe_w_up, moe_w_down, 0)
    hn0, rt0, mod0 = hn, rt, mod

    layer = 1
    mod = mod_all[1].reshape(MOD_ROWS, 1, 6 * D_MODEL)
    lambda_init = 0.8 - 0.6 * math.exp(-0.3 * layer)
    scale_c = C_HD ** -0.5 * LOG2_E
    scale_d = HEAD_DIM ** -0.5 * LOG2_E
    win = odd_w_in[i].astype(BF16)
    gains = jnp.stack([
        jnp.tile(c_q_norm[i] * scale_c, 2), jnp.tile(c_k_norm[i], 2),
        d_q_norm[i] * scale_d, d_k_norm[i],
        jnp.zeros((128,), F32), jnp.zeros((128,), F32), jnp.zeros((128,), F32), jnp.zeros((128,), F32)])
    h, qc, kc, vc, qd, kd, vd = _odd_proj_call(hn0, ys, slots, rt0, mod0, mod, tab, gains, win)

    aux = jnp.stack([
        _pad_lanes(c_lambda_q1[i], 128), _pad_lanes(c_lambda_k1[i], 128),
        _pad_lanes(c_lambda_q2[i], 128), _pad_lanes(c_lambda_k2[i], 128),
        c_subln[i], jnp.zeros((128,), F32), jnp.zeros((128,), F32), jnp.zeros((128,), F32)])
    oc = _attn_call(qc, kc, vc, n_heads_kv=C_HEADS, g=2, dk=LANES, dv=C_V, nkv=2,
                    tq=512, n_q_tiles=SEQ // 512, aux=aux, diff_lambda_init=lambda_init,
                    name="diff_latent", **lat)
    g_d = D_HEADS // D_KV_HEADS
    sink_col = jnp.repeat((d_sink[i] * LOG2_E).reshape(D_KV_HEADS, g_d), WIN_TQ, axis=1)
    sink_col = sink_col.reshape(D_KV_HEADS, g_d * WIN_TQ, 1)
    od = _window_call(qd, kd, vd, sink_col)

    w_out = odd_w_out[i].astype(BF16)
    rw, rb = _router_weights(moe_wr_group[1], moe_br_group[1], moe_wr_expert[1], moe_br_expert[1])
    n_lat_tiles = SEQ // ROW_TILE
    hn, fx, rt, cnt = _outproj_call((oc, od, h), None, mod, w_out[:C_HEADS * C_V], w_out[C_HEADS * C_V:],
                                    rw, rb, n_lat_tiles)
    ys, slots = _moe_experts(fx, rt, cnt, moe_w_gate, moe_w_up, moe_w_down, 1)
    return _combine_call(hn, ys, slots, rt, mod, n_lat_tiles)
```

```python
import functools
import math

import jax
import jax.numpy as jnp
from jax import lax
from jax.experimental import pallas as pl
from jax.experimental.pallas import tpu as pltpu

F32 = jnp.float32
BF16 = jnp.bfloat16

D_MODEL = 2048
BATCH = 8
SEQ = 2048
DEPTH = 2
GRID_W = 64
CTX_LEN = 256
TOK = SEQ + CTX_LEN
HEAD_DIM = 128
ROPE_THETA = 10000.0
NORM_EPS = 1e-6
A_HEADS = 8
A_KV_HEADS = 2
B_HEADS = 8
B_Q_LORA = 512
B_KV_LORA = 512
B_NOPE = 128
B_ROPE = 64
B_V = 128
B_QK_PAD = 256
C_HEADS = 8
C_HD = 64
C_V = 128
D_HEADS = 8
D_KV_HEADS = 2
WINDOW = 128
N_GROUPS = 4
EXPERTS_PER_GROUP = 8
N_EXPERTS = 32
D_EXPERT = 1024
MOE_BLOCK = 128
TOP_K = 2
LANES = 128
LOG2_E = math.log2(math.e)

ROW_TILE = 256
N_ROW_TILES = TOK // ROW_TILE
CTX_TILE = SEQ // ROW_TILE
MOD_ROWS = 16
MOD_CTX_ROW = BATCH
VMEM_LIMIT = 56 * 1024 * 1024


def _dot(a, b):
    return jnp.dot(a, b, preferred_element_type=F32)


def _dot_nt(a, b):
    return lax.dot_general(a, b, (((1,), (1,)), ((), ())), preferred_element_type=F32)


def _silu(x):
    return x / (1.0 + jnp.exp(-x))


def _rms(x, n):
    return x * lax.rsqrt(jnp.sum(x * x, axis=-1, keepdims=True) * (1.0 / n) + NORM_EPS)


def _modulate(x, shift, scale):
    return _rms(x, x.shape[-1]) * (1.0 + scale) + shift


def _rope(x, cos, sin_signed, quarter, lane):
    fwd = pltpu.roll(x, LANES - quarter, axis=1)
    bwd = pltpu.roll(x, quarter, axis=1)
    rot = jnp.where((lane // quarter) % 2 == 0, fwd, bwd)
    return x * cos + rot * sin_signed


PACKED_W = D_MODEL // 2
U32 = jnp.uint32


def _pack_bf16_pairs(x):
    n = x.shape[-1] // 2
    lo = lax.bitcast_convert_type(x[:, :n].astype(BF16).astype(F32), U32) >> 16
    hi = lax.bitcast_convert_type(x[:, n:].astype(BF16).astype(F32), U32) & U32(0xFFFF0000)
    return lo | hi


def _unpack_bf16_pairs(p):
    lo = lax.bitcast_convert_type(p << 16, F32)
    hi = lax.bitcast_convert_type(p & U32(0xFFFF0000), F32)
    return jnp.concatenate([lo, hi], axis=-1)


def _start_row_gathers(slot_ref, src_hbm, dst_ref, sem):
    for r in range(ROW_TILE):
        for k in range(TOP_K):
            pltpu.make_async_copy(src_hbm.at[pl.ds(slot_ref[0, k, r], 1)], dst_ref.at[k, pl.ds(r, 1)], sem).start()


def _wait_row_gathers(src_hbm, dst_ref, sem):
    for k in range(TOP_K):
        pltpu.make_async_copy(src_hbm.at[pl.ds(0, ROW_TILE)], dst_ref.at[k], sem).wait()


MOD_TN = 1024


def _mod_kernel(c_ref, w_ref, b_ref, o_ref):
    a = _silu(c_ref[...]).astype(BF16)
    o_ref[0] = _dot(a, w_ref[0].astype(BF16)) + b_ref[0]


def _mod_call(c_all, mod_w, mod_b):
    d6 = 6 * D_MODEL
    return pl.pallas_call(
        _mod_kernel,
        grid=(DEPTH, d6 // MOD_TN),
        in_specs=[
            pl.BlockSpec((MOD_ROWS, D_MODEL), lambda l, n: (0, 0)),
            pl.BlockSpec((1, D_MODEL, MOD_TN), lambda l, n: (l, 0, n)),
            pl.BlockSpec((1, 1, MOD_TN), lambda l, n: (l, 0, n)),
        ],
        out_specs=pl.BlockSpec((1, MOD_ROWS, MOD_TN), lambda l, n: (l, 0, n)),
        out_shape=jax.ShapeDtypeStruct((DEPTH, MOD_ROWS, d6), F32),
        compiler_params=pltpu.CompilerParams(
            dimension_semantics=("parallel", "parallel"), vmem_limit_bytes=VMEM_LIMIT),
        name="mod_vectors",
    )(c_all, mod_w, mod_b.reshape(DEPTH, 1, d6))


def _mod_row(b, j):
    return jnp.where(j == CTX_TILE, MOD_CTX_ROW, b)


def _resident(shape):
    nd = len(shape)
    return pl.BlockSpec(shape, lambda *_: (0,) * nd, pipeline_mode=pl.Buffered(1))


EVEN_IN_PAD = 2688


def _even_proj_kernel(hx_ref, hc_ref, mod_ref, tab_ref, g_ref, win_ref, wuq_ref, wukv_ref,
                      qa_ref, ka_ref, va_ref, qb_ref, kb_ref, vb_ref):
    d = D_MODEL
    x = jnp.where(pl.program_id(1) == CTX_TILE, hc_ref[0], hx_ref[0])
    a = _modulate(x, mod_ref[0, :, 0:d], mod_ref[0, :, d:2 * d]).astype(BF16)
    lane = lax.broadcasted_iota(jnp.int32, (ROW_TILE, LANES), 1)
    cos128, sin128 = tab_ref[:, 0:128], tab_ref[:, 128:256]
    cos64p, sin64p = tab_ref[:, 512:640], tab_ref[:, 640:768]

    z = _dot(a, win_ref[:, 0:1024])
    g_q = g_ref[0:1, 0:128]
    for h in range(A_HEADS):
        blk = _rms(z[:, h * 128:(h + 1) * 128], HEAD_DIM) * g_q
        qa_ref[0, :, h * 128:(h + 1) * 128] = _rope(blk, cos128, sin128, 32, lane).astype(BF16)
    z = _dot(a, win_ref[:, 1024:1536])
    g_k = g_ref[1:2, 0:128]
    for h in range(A_KV_HEADS):
        blk = _rms(z[:, h * 128:(h + 1) * 128], HEAD_DIM) * g_k
        ka_ref[0, :, h * 128:(h + 1) * 128] = _rope(blk, cos128, sin128, 32, lane).astype(BF16)
    va_ref[0] = z[:, 256:512].astype(BF16)

    z = _dot(a, win_ref[:, 1536:2048])
    cq = (_rms(z, B_Q_LORA) * g_ref[2:3, :]).astype(BF16)
    zq = _dot(cq, wuq_ref[...])
    gq0, gq1 = g_ref[4:5, 0:128], g_ref[4:5, 128:256]
    n_qk = float(B_NOPE + B_ROPE)
    for h in range(B_HEADS):
        b0 = zq[:, h * 256:h * 256 + 128]
        b1 = zq[:, h * 256 + 128:(h + 1) * 256]
        ss = jnp.sum(b0 * b0, axis=-1, keepdims=True) + jnp.sum(b1 * b1, axis=-1, keepdims=True)
        r = lax.rsqrt(ss * (1.0 / n_qk) + NORM_EPS)
        qb_ref[0, :, h * 256:h * 256 + 128] = (b0 * r * gq0).astype(BF16)
        qb_ref[0, :, h * 256 + 128:(h + 1) * 256] = _rope(b1 * r * gq1, cos64p, sin64p, 16, lane).astype(BF16)

    z = _dot(a, win_ref[:, 2048:2688])
    ckv = (_rms(z[:, 0:512], B_KV_LORA) * g_ref[3:4, :]).astype(BF16)
    kr = z[:, 512:640]
    ss_kr = jnp.sum(kr * kr, axis=-1, keepdims=True)
    kr_rot = _rope(kr * g_ref[5:6, 128:256], cos64p, sin64p, 16, lane)
    zkv = _dot(ckv, wukv_ref[...])
    vb_ref[0] = zkv[:, 1024:2048].astype(BF16)
    gk0 = g_ref[5:6, 0:128]
    for h in range(B_HEADS):
        kn = zkv[:, h * 128:(h + 1) * 128]
        ss = jnp.sum(kn * kn, axis=-1, keepdims=True) + ss_kr
        r = lax.rsqrt(ss * (1.0 / n_qk) + NORM_EPS)
        kb_ref[0, :, h * 256:h * 256 + 128] = (kn * r * gk0).astype(BF16)
        kb_ref[0, :, h * 256 + 128:(h + 1) * 256] = (kr_rot * r).astype(BF16)


def _lat_tile(w):
    return pl.BlockSpec((1, ROW_TILE, w), lambda b, j: (b, jnp.minimum(j, CTX_TILE - 1), 0))


def _ctx_tile(w):
    return pl.BlockSpec((1, ROW_TILE, w), lambda b, j: (b, 0, 0))


def _even_proj_call(h_lat, h_ctx, mod, tab, gains, win, wuq, wukv):
    bsz = h_lat.shape[0]
    widths = (A_HEADS * HEAD_DIM, A_KV_HEADS * HEAD_DIM, A_KV_HEADS * HEAD_DIM,
              B_HEADS * B_QK_PAD, B_HEADS * B_QK_PAD, B_HEADS * B_V)
    tile = lambda w: pl.BlockSpec((1, ROW_TILE, w), lambda b, j: (b, j, 0))
    return pl.pallas_call(
        _even_proj_kernel,
        grid=(bsz, N_ROW_TILES),
        in_specs=[
            _lat_tile(D_MODEL), _ctx_tile(D_MODEL),
            pl.BlockSpec((1, 1, 6 * D_MODEL), lambda b, j: (_mod_row(b, j), 0, 0)),
            pl.BlockSpec((ROW_TILE, 768), lambda b, j: (j, 0)),
            _resident(gains.shape), _resident(win.shape), _resident(wuq.shape), _resident(wukv.shape),
        ],
        out_specs=[tile(w) for w in widths],
        out_shape=[jax.ShapeDtypeStruct((bsz, TOK, w), BF16) for w in widths],
        compiler_params=pltpu.CompilerParams(
            dimension_semantics=("parallel", "parallel"), vmem_limit_bytes=VMEM_LIMIT),
        name="even_proj",
    )(h_lat, h_ctx, mod, tab, gains, win, wuq, wukv)


def _rms_halves(x, lane):
    x2 = x * x
    s_lo = jnp.sum(jnp.where(lane < 64, x2, 0.0), axis=-1, keepdims=True)
    s_hi = jnp.sum(jnp.where(lane < 64, 0.0, x2), axis=-1, keepdims=True)
    r = jnp.where(lane < 64, lax.rsqrt(s_lo * (1.0 / C_HD) + NORM_EPS), lax.rsqrt(s_hi * (1.0 / C_HD) + NORM_EPS))
    return x * r


def _odd_proj_kernel(slot_ref, slot_next_ref, hn_ref, rt_ref, mod_prev_ref, mod_ref, tab_ref, g_ref, win_ref,
                     ys_hbm, h_ref, qc_ref, kc_ref, vc_ref, qd_ref, kd_ref, vd_ref, ybuf, sems):
    d = D_MODEL
    n_steps = pl.num_programs(0) * pl.num_programs(1)
    step = pl.program_id(0) * pl.num_programs(1) + pl.program_id(1)
    cur = step % 2

    @pl.when(step == 0)
    def _():
        _start_row_gathers(slot_ref, ys_hbm, ybuf.at[0], sems.at[0])

    _wait_row_gathers(ys_hbm, ybuf.at[cur], sems.at[cur])
    rt = rt_ref[0]
    y = rt[:, 2:3] * _unpack_bf16_pairs(ybuf[cur, 0]) + rt[:, 3:4] * _unpack_bf16_pairs(ybuf[cur, 1])
    x = hn_ref[0] + mod_prev_ref[0, :, 5 * d:6 * d] * y
    h_ref[0] = x
    _start_row_gathers(slot_next_ref, ys_hbm, ybuf.at[1 - cur], sems.at[1 - cur])
    a = _modulate(x, mod_ref[0, :, 0:d], mod_ref[0, :, d:2 * d]).astype(BF16)
    lane = lax.broadcasted_iota(jnp.int32, (ROW_TILE, LANES), 1)
    cos128, sin128 = tab_ref[:, 0:128], tab_ref[:, 128:256]
    cos64, sin64 = tab_ref[:, 256:384], tab_ref[:, 384:512]

    z = _dot(a, win_ref[:, 0:1024])
    g_q = g_ref[0:1, :]
    for h in range(C_HEADS):
        blk = _rope(_rms_halves(z[:, h * 128:(h + 1) * 128], lane) * g_q, cos64, sin64, 16, lane)
        qc_ref[0, :, h * 256:h * 256 + 128] = jnp.where(lane < 64, blk, 0.0).astype(BF16)
        qc_ref[0, :, h * 256 + 128:(h + 1) * 256] = jnp.where(lane < 64, 0.0, blk).astype(BF16)
    z = _dot(a, win_ref[:, 1024:2048])
    g_k = g_ref[1:2, :]
    for h in range(C_HEADS):
        blk = _rope(_rms_halves(z[:, h * 128:(h + 1) * 128], lane) * g_k, cos64, sin64, 16, lane)
        kc_ref[0, :, h * 128:(h + 1) * 128] = blk.astype(BF16)
    vc_ref[0] = _dot(a, win_ref[:, 2048:3072]).astype(BF16)

    z = _dot(a, win_ref[:, 3072:4096])
    g_q = g_ref[2:3, :]
    for h in range(D_HEADS):
        blk = _rms(z[:, h * 128:(h + 1) * 128], HEAD_DIM) * g_q
        qd_ref[0, :, h * 128:(h + 1) * 128] = _rope(blk, cos128, sin128, 32, lane).astype(BF16)
    z = _dot(a, win_ref[:, 4096:4608])
    g_k = g_ref[3:4, :]
    for h in range(D_KV_HEADS):
        blk = _rms(z[:, h * 128:(h + 1) * 128], HEAD_DIM) * g_k
        kd_ref[0, :, h * 128:(h + 1) * 128] = _rope(blk, cos128, sin128, 32, lane).astype(BF16)
    vd_ref[0] = z[:, 256:512].astype(BF16)

    @pl.when(step == n_steps - 1)
    def _():
        _wait_row_gathers(ys_hbm, ybuf.at[1 - cur], sems.at[1 - cur])


def _odd_proj_call(hn, ys, slots, rt, mod_prev, mod, tab, gains, win):
    bsz = hn.shape[0]
    widths = (C_HEADS * 2 * LANES, C_HEADS * LANES, C_HEADS * C_V,
              D_HEADS * HEAD_DIM, D_KV_HEADS * HEAD_DIM, D_KV_HEADS * HEAD_DIM)
    n_steps = bsz * N_ROW_TILES
    tile = lambda w: pl.BlockSpec((1, ROW_TILE, w), lambda b, j: (b, j, 0))
    mod_spec = pl.BlockSpec((1, 1, 6 * D_MODEL), lambda b, j: (_mod_row(b, j), 0, 0))
    slot_spec = lambda ahead: pl.BlockSpec(
        (1, TOP_K, ROW_TILE), lambda b, j: (jnp.minimum(b * N_ROW_TILES + j + ahead, n_steps - 1), 0, 0),
        memory_space=pltpu.SMEM)
    return pl.pallas_call(
        _odd_proj_kernel,
        grid=(bsz, N_ROW_TILES),
        in_specs=[
            slot_spec(0), slot_spec(1), tile(D_MODEL), tile(LANES), mod_spec, mod_spec,
            pl.BlockSpec((ROW_TILE, 768), lambda b, j: (j, 0)),
            _resident(gains.shape), _resident(win.shape),
            pl.BlockSpec(memory_space=pl.ANY),
        ],
        out_specs=[tile(D_MODEL)] + [tile(w) for w in widths],
        out_shape=[jax.ShapeDtypeStruct((bsz, TOK, D_MODEL), F32)]
                  + [jax.ShapeDtypeStruct((bsz, TOK, w), BF16) for w in widths],
        scratch_shapes=[pltpu.VMEM((2, TOP_K, ROW_TILE, PACKED_W), U32), pltpu.SemaphoreType.DMA((2,))],
        compiler_params=pltpu.CompilerParams(
            dimension_semantics=("arbitrary", "arbitrary"), vmem_limit_bytes=VMEM_LIMIT),
        name="odd_proj",
    )(slots, slots, hn, rt, mod_prev, mod, tab, gains, win, ys)


KEY_CHUNK = 768


def _attn_kernel(*refs, nkv, g, dk, dv, tq, n_keys, diff_lambda_init):
    if diff_lambda_init is None:
        q_ref, k_ref, v_ref, o_ref = refs
    else:
        q_ref, k_ref, v_ref, aux_ref, o_ref = refs
    for kv in range(nkv):
        q = jnp.concatenate(
            [q_ref[0, :, (kv * g + gi) * dk:(kv * g + gi + 1) * dk] for gi in range(g)], axis=0)
        m = l = acc = None
        for c0 in range(0, n_keys, KEY_CHUNK):
            c1 = min(c0 + KEY_CHUNK, n_keys)
            s = _dot_nt(q, k_ref[0, c0:c1, kv * dk:(kv + 1) * dk])
            v = v_ref[0, c0:c1, kv * dv:(kv + 1) * dv]
            m_c = jnp.max(s, axis=-1, keepdims=True)
            if m is None:
                m = m_c
                p = jnp.exp2(s - m)
                l = jnp.sum(p, axis=-1, keepdims=True)
                acc = _dot(p.astype(BF16), v)
            else:
                m_new = jnp.maximum(m, m_c)
                alpha = jnp.exp2(m - m_new)
                p = jnp.exp2(s - m_new)
                l = alpha * l + jnp.sum(p, axis=-1, keepdims=True)
                acc = alpha * acc + _dot(p.astype(BF16), v)
                m = m_new
        o = acc / l
        if diff_lambda_init is None:
            for gi in range(g):
                o_ref[0, :, (kv * g + gi) * dv:(kv * g + gi + 1) * dv] = o[gi * tq:(gi + 1) * tq].astype(BF16)
        else:
            lam = (jnp.exp(jnp.sum(aux_ref[0:1, :] * aux_ref[1:2, :], axis=-1, keepdims=True))
                   - jnp.exp(jnp.sum(aux_ref[2:3, :] * aux_ref[3:4, :], axis=-1, keepdims=True))
                   + diff_lambda_init)
            od = o[0:tq] - lam * o[tq:2 * tq]
            od = _rms(od, dv) * aux_ref[4:5, :] * (1.0 - diff_lambda_init)
            o_ref[0, :, kv * dv:(kv + 1) * dv] = od.astype(BF16)


def _attn_call(q, k, v, *, n_heads_kv, g, dk, dv, nkv, tq, q_tile0, n_q_tiles, key_tile0, n_keys,
               aux=None, diff_lambda_init=None, name="attn"):
    bsz = q.shape[0]
    n_out_heads = n_heads_kv * (g if diff_lambda_init is None else 1)
    out_w = nkv * (g if diff_lambda_init is None else 1) * dv
    kern = functools.partial(_attn_kernel, nkv=nkv, g=g, dk=dk, dv=dv, tq=tq, n_keys=n_keys,
                             diff_lambda_init=diff_lambda_init)
    in_specs = [
        pl.BlockSpec((1, tq, nkv * g * dk), lambda b, hh, qi: (b, qi + q_tile0, hh)),
        pl.BlockSpec((1, n_keys, nkv * dk), lambda b, hh, qi: (b, key_tile0, hh)),
        pl.BlockSpec((1, n_keys, nkv * dv), lambda b, hh, qi: (b, key_tile0, hh)),
    ]
    args = [q, k, v]
    if aux is not None:
        in_specs.append(pl.BlockSpec(aux.shape, lambda b, hh, qi: (0, 0)))
        args.append(aux)
    return pl.pallas_call(
        kern,
        grid=(bsz, n_heads_kv // nkv, n_q_tiles),
        in_specs=in_specs,
        out_specs=pl.BlockSpec((1, tq, out_w), lambda b, hh, qi: (b, qi, hh)),
        out_shape=jax.ShapeDtypeStruct((bsz, n_q_tiles * tq, n_out_heads * dv), BF16),
        compiler_params=pltpu.CompilerParams(
            dimension_semantics=("parallel", "parallel", "parallel"), vmem_limit_bytes=VMEM_LIMIT),
        name=name,
    )(*args)


WIN_TQ = 256
WIN_BAND = WIN_TQ + 2 * WINDOW
NEG_BIG = -1e30


def _window_kernel(q_ref, k_ref, v_ref, sink_ref, o_ref):
    g = D_HEADS // D_KV_HEADS
    dk = HEAD_DIM
    qi = pl.program_id(1)
    q0 = qi * WIN_TQ
    start = pl.multiple_of(jnp.clip(q0 - WINDOW, 0, SEQ - WIN_BAND), WINDOW)
    row = (lax.broadcasted_iota(jnp.int32, (g * WIN_TQ, WIN_BAND), 0) & (WIN_TQ - 1)) + q0
    col = lax.broadcasted_iota(jnp.int32, (g * WIN_TQ, WIN_BAND), 1) + start
    near = jnp.abs(row - col) <= WINDOW
    for kv in range(D_KV_HEADS):
        kcols = slice(kv * dk, (kv + 1) * dk)
        q = jnp.concatenate([q_ref[0, :, (kv * g + gi) * dk:(kv * g + gi + 1) * dk] for gi in range(g)], axis=0)
        s_loc = jnp.where(near, _dot_nt(q, k_ref[0, pl.ds(start, WIN_BAND), kcols]), NEG_BIG)
        s_ctx = _dot_nt(q, k_ref[0, SEQ:TOK, kcols])
        sink = sink_ref[kv]
        m = jnp.maximum(jnp.maximum(jnp.max(s_loc, axis=-1, keepdims=True),
                                    jnp.max(s_ctx, axis=-1, keepdims=True)), sink)
        p_loc = jnp.exp2(s_loc - m)
        p_ctx = jnp.exp2(s_ctx - m)
        l = (jnp.sum(p_loc, axis=-1, keepdims=True) + jnp.sum(p_ctx, axis=-1, keepdims=True)
             + jnp.exp2(sink - m))
        acc = (_dot(p_ctx.astype(BF16), v_ref[0, SEQ:TOK, kcols])
               + _dot(p_loc.astype(BF16), v_ref[0, pl.ds(start, WIN_BAND), kcols]))
        o = acc / l
        for gi in range(g):
            o_ref[0, :, (kv * g + gi) * dk:(kv * g + gi + 1) * dk] = o[gi * WIN_TQ:(gi + 1) * WIN_TQ].astype(BF16)


def _window_call(q, k, v, sink_col):
    bsz = q.shape[0]
    g = D_HEADS // D_KV_HEADS
    return pl.pallas_call(
        _window_kernel,
        grid=(bsz, SEQ // WIN_TQ),
        in_specs=[
            pl.BlockSpec((1, WIN_TQ, D_HEADS * HEAD_DIM), lambda b, qi: (b, qi, 0)),
            pl.BlockSpec((1, TOK, D_KV_HEADS * HEAD_DIM), lambda b, qi: (b, 0, 0)),
            pl.BlockSpec((1, TOK, D_KV_HEADS * HEAD_DIM), lambda b, qi: (b, 0, 0)),
            pl.BlockSpec((D_KV_HEADS, g * WIN_TQ, 1), lambda b, qi: (0, 0, 0)),
        ],
        out_specs=pl.BlockSpec((1, WIN_TQ, D_HEADS * HEAD_DIM), lambda b, qi: (b, qi, 0)),
        out_shape=jax.ShapeDtypeStruct((bsz, SEQ, D_HEADS * HEAD_DIM), BF16),
        compiler_params=pltpu.CompilerParams(
            dimension_semantics=("parallel", "parallel"), vmem_limit_bytes=VMEM_LIMIT),
        name="window_attn",
    )(q, k, v, sink_col)


def _route(logits, lane):
    lane_f = lane.astype(F32)
    lg = jnp.where(lane < N_GROUPS, logits, NEG_BIG)
    g_max = jnp.max(lg, axis=-1, keepdims=True)
    p_top = 1.0 / jnp.sum(jnp.exp(lg - g_max), axis=-1, keepdims=True)
    g_idx = jnp.min(jnp.where(lg == g_max, lane_f, float(LANES)), axis=-1, keepdims=True)
    e_lane = lane - N_GROUPS
    in_group = (e_lane >= 0) & (e_lane < N_EXPERTS) & ((e_lane // EXPERTS_PER_GROUP).astype(F32) == g_idx)
    le = jnp.where(in_group, logits, NEG_BIG)
    m1 = jnp.max(le, axis=-1, keepdims=True)
    i1 = jnp.min(jnp.where(le == m1, lane_f, float(LANES)), axis=-1, keepdims=True)
    le2 = jnp.where(lane_f == i1, NEG_BIG, le)
    m2 = jnp.max(le2, axis=-1, keepdims=True)
    i2 = jnp.min(jnp.where(le2 == m2, lane_f, float(LANES)), axis=-1, keepdims=True)
    t = jnp.exp(m2 - m1)
    w1 = p_top / (1.0 + t)
    w2 = p_top * t / (1.0 + t)
    return i1, i2, w1, w2


def _outproj_kernel(*refs, has_ctx):
    if has_ctx:
        (o1_ref, o2_ref, h_ref, o1c_ref, o2c_ref, hc_ref, mod_ref, w1_ref, w2_ref, rw_ref, rb_ref,
         hn_ref, fx_ref, rt_ref, cnt_ref, run_ref) = refs
    else:
        (o1_ref, o2_ref, h_ref, mod_ref, w1_ref, w2_ref, rw_ref, rb_ref,
         hn_ref, fx_ref, rt_ref, cnt_ref, run_ref) = refs
    d = D_MODEL
    b, j = pl.program_id(0), pl.program_id(1)

    @pl.when((b == 0) & (j == 0))
    def _():
        run_ref[...] = jnp.zeros_like(run_ref)

    o1, o2, h = o1_ref[0], o2_ref[0], h_ref[0]
    if has_ctx:
        is_ctx = j == CTX_TILE
        o1 = jnp.where(is_ctx, o1c_ref[0], o1)
        o2 = jnp.where(is_ctx, o2c_ref[0], o2)
        h = jnp.where(is_ctx, hc_ref[0], h)
    acc = _dot(o1, w1_ref[...]) + _dot(o2, w2_ref[...])
    hn = h + mod_ref[0, :, 2 * d:3 * d] * acc
    hn_ref[0] = hn
    fx = _modulate(hn, mod_ref[0, :, 3 * d:4 * d], mod_ref[0, :, 4 * d:5 * d])
    fx_ref[0] = _pack_bf16_pairs(fx)
    hi = fx.astype(BF16)
    lo = (fx - hi.astype(F32)).astype(BF16)
    part = _dot(jnp.concatenate([hi, lo], axis=0), rw_ref[...])
    logits = (part[:ROW_TILE, :LANES] + part[:ROW_TILE, LANES:]
              + part[ROW_TILE:, :LANES] + part[ROW_TILE:, LANES:] + rb_ref[...])
    lane = lax.broadcasted_iota(jnp.int32, (ROW_TILE, LANES), 1)
    lane_f = lane.astype(F32)
    i1, i2, w1, w2 = _route(logits, lane)

    hit1, hit2 = lane_f == i1, lane_f == i2
    multi_hot = jnp.where(hit1, 1.0, 0.0) + jnp.where(hit2, 1.0, 0.0)
    tri = (lax.broadcasted_iota(jnp.int32, (ROW_TILE, ROW_TILE), 0)
           > lax.broadcasted_iota(jnp.int32, (ROW_TILE, ROW_TILE), 1))
    before = _dot(jnp.where(tri, 1.0, 0.0).astype(BF16), multi_hot.astype(BF16)) + run_ref[...]
    r1 = jnp.sum(jnp.where(hit1, before, 0.0), axis=-1, keepdims=True)
    r2 = jnp.sum(jnp.where(hit2, before, 0.0), axis=-1, keepdims=True)
    run_ref[...] = run_ref[...] + jnp.sum(multi_hot, axis=0, keepdims=True)
    cnt_ref[...] = jnp.broadcast_to(run_ref[...], cnt_ref.shape)

    out = jnp.where(lane == 0, i1 - N_GROUPS, 0.0)
    out = jnp.where(lane == 1, i2 - N_GROUPS, out)
    out = jnp.where(lane == 2, w1, out)
    out = jnp.where(lane == 3, w2, out)
    out = jnp.where(lane == 4, r1, out)
    out = jnp.where(lane == 5, r2, out)
    rt_ref[0] = out


def _outproj_call(lat, ctx, mod, w1, w2, rw, rb, n_tiles):
    bsz = lat[2].shape[0]
    has_ctx = ctx is not None
    tile = lambda w: pl.BlockSpec((1, ROW_TILE, w), lambda b, j: (b, j, 0))
    in_specs = [_lat_tile(a.shape[-1]) for a in lat]
    args = list(lat)
    if has_ctx:
        in_specs += [_ctx_tile(a.shape[-1]) for a in ctx]
        args += list(ctx)
    in_specs += [
        pl.BlockSpec((1, 1, 6 * D_MODEL), lambda b, j: (_mod_row(b, j), 0, 0)),
        _resident(w1.shape), _resident(w2.shape), _resident(rw.shape), _resident(rb.shape),
    ]
    args += [mod, w1, w2, rw, rb]
    n_rows = n_tiles * ROW_TILE
    return pl.pallas_call(
        functools.partial(_outproj_kernel, has_ctx=has_ctx),
        grid=(bsz, n_tiles),
        in_specs=in_specs,
        out_specs=[tile(D_MODEL), tile(PACKED_W), tile(LANES),
                   pl.BlockSpec((8, LANES), lambda b, j: (0, 0))],
        out_shape=[jax.ShapeDtypeStruct((bsz, n_rows, D_MODEL), F32),
                   jax.ShapeDtypeStruct((bsz, n_rows, PACKED_W), U32),
                   jax.ShapeDtypeStruct((bsz, n_rows, LANES), F32),
                   jax.ShapeDtypeStruct((8, LANES), F32)],
        scratch_shapes=[pltpu.VMEM((1, LANES), F32)],
        compiler_params=pltpu.CompilerParams(
            dimension_semantics=("arbitrary", "arbitrary"), vmem_limit_bytes=VMEM_LIMIT),
        name="outproj_router",
    )(*args)


def _dispatch_kernel(lo_ref, hi_ref, slot_ref, fx_ref, xs_hbm, zbuf, sem, zsem):
    b, j = pl.program_id(0), pl.program_id(1)

    for r in range(ROW_TILE):
        for k in range(TOP_K):
            pltpu.make_async_copy(fx_ref.at[0, pl.ds(r, 1)], xs_hbm.at[pl.ds(slot_ref[0, k, r], 1)], sem).start()

    @pl.when((b == pl.num_programs(0) - 1) & (j == pl.num_programs(1) - 1))
    def _():
        zbuf[...] = jnp.zeros_like(zbuf)

        def fill(s, carry):
            pltpu.make_async_copy(zbuf.at[pl.ds(0, 1)], xs_hbm.at[pl.ds(s, 1)], zsem).start()
            return carry

        def drain(s, carry):
            pltpu.make_async_copy(zbuf.at[pl.ds(0, 1)], xs_hbm.at[pl.ds(0, 1)], zsem).wait()
            return carry
        for e in range(N_EXPERTS):
            lax.fori_loop(lo_ref[e], hi_ref[e], fill, 0)
        for e in range(N_EXPERTS):
            lax.fori_loop(lo_ref[e], hi_ref[e], drain, 0)

    for k in range(TOP_K):
        pltpu.make_async_copy(fx_ref.at[0], xs_hbm.at[pl.ds(0, ROW_TILE)], sem).wait()


def _dispatch_call(fx, slots, fill_lo, fill_hi, n_blocks):
    bsz, n_rows, _ = fx.shape
    n_tiles = n_rows // ROW_TILE
    grid_spec = pltpu.PrefetchScalarGridSpec(
        num_scalar_prefetch=2,
        grid=(bsz, n_tiles),
        in_specs=[
            pl.BlockSpec((1, TOP_K, ROW_TILE), lambda b, j, lo, hi: (b * n_tiles + j, 0, 0),
                         memory_space=pltpu.SMEM),
            pl.BlockSpec((1, ROW_TILE, PACKED_W), lambda b, j, lo, hi: (b, j, 0)),
        ],
        out_specs=pl.BlockSpec(memory_space=pl.ANY),
        scratch_shapes=[pltpu.VMEM((8, PACKED_W), U32), pltpu.SemaphoreType.DMA(()), pltpu.SemaphoreType.DMA(())],
    )
    return pl.pallas_call(
        _dispatch_kernel,
        grid_spec=grid_spec,
        out_shape=jax.ShapeDtypeStruct((n_blocks * MOE_BLOCK, PACKED_W), U32),
        compiler_params=pltpu.CompilerParams(
            dimension_semantics=("arbitrary", "arbitrary"), vmem_limit_bytes=VMEM_LIMIT),
        name="moe_dispatch",
    )(fill_lo, fill_hi, slots, fx)


CAST_ROWS = 256


def _moe_kernel(be_ref, nu_ref, nx_ref, x_ref, wg_hbm, wu_hbm, wd_hbm, y_ref,
                sg, su, sd, wg, wu, wd, sems, *, layer):
    i = pl.program_id(0)
    e = be_ref[i]
    used = i < nu_ref[0]
    staged = ((wg_hbm, sg, wg, 0), (wu_hbm, su, wu, 1), (wd_hbm, sd, wd, 2))

    def fetch(expert):
        for hbm, stage, _, s in staged:
            pltpu.make_async_copy(hbm.at[layer, expert], stage, sems.at[s]).start(priority=1)

    @pl.when(i == 0)
    def _():
        fetch(e)

    first_of_expert = (i == 0) | (e != be_ref[jnp.maximum(i - 1, 0)])

    @pl.when(used & first_of_expert)
    def _():
        for hbm, stage, dst, s in staged:
            pltpu.make_async_copy(hbm.at[layer, 0], stage, sems.at[s]).wait()

            def cast(c, carry):
                rows = pl.ds(pl.multiple_of(c * CAST_ROWS, CAST_ROWS), CAST_ROWS)
                dst[rows, :] = stage[rows, :].astype(BF16)
                return carry
            lax.fori_loop(0, stage.shape[0] // CAST_ROWS, cast, 0)
        nxt = nx_ref[e]

        @pl.when(nxt >= 0)
        def _():
            fetch(nxt)

    @pl.when(used)
    def _():
        x = _unpack_bf16_pairs(x_ref[...]).astype(BF16)
        gt = _dot(x, wg[...])
        up = _dot(x, wu[...])
        u = (_silu(gt) * up).astype(BF16)
        y_ref[...] = _pack_bf16_pairs(_dot(u, wd[...]))


def _moe_call(xs, block_expert, n_used, next_expert, wg, wu, wd, layer):
    n_blocks = block_expert.shape[0]
    row_blk = pl.BlockSpec((MOE_BLOCK, PACKED_W), lambda i, be, nu, nx: (jnp.minimum(i, nu[0] - 1), 0))
    hbm = pl.BlockSpec(memory_space=pl.ANY)
    grid_spec = pltpu.PrefetchScalarGridSpec(
        num_scalar_prefetch=3,
        grid=(n_blocks,),
        in_specs=[row_blk, hbm, hbm, hbm],
        out_specs=row_blk,
        scratch_shapes=[
            pltpu.VMEM((D_MODEL, D_EXPERT), F32), pltpu.VMEM((D_MODEL, D_EXPERT), F32),
            pltpu.VMEM((D_EXPERT, D_MODEL), F32),
            pltpu.VMEM((D_MODEL, D_EXPERT), BF16), pltpu.VMEM((D_MODEL, D_EXPERT), BF16),
            pltpu.VMEM((D_EXPERT, D_MODEL), BF16),
            pltpu.SemaphoreType.DMA((3,)),
        ],
    )
    return pl.pallas_call(
        functools.partial(_moe_kernel, layer=layer),
        grid_spec=grid_spec,
        out_shape=jax.ShapeDtypeStruct((n_blocks * MOE_BLOCK, PACKED_W), U32),
        compiler_params=pltpu.CompilerParams(
            dimension_semantics=("arbitrary",), vmem_limit_bytes=VMEM_LIMIT),
        name="moe_experts",
    )(block_expert, n_used, next_expert, xs, wg, wu, wd)


def _combine_kernel(slot_ref, h_ref, rt_ref, mod_ref, ys_hbm, o_ref, ybuf, sem):
    d = D_MODEL
    _start_row_gathers(slot_ref, ys_hbm, ybuf, sem)
    _wait_row_gathers(ys_hbm, ybuf, sem)
    rt = rt_ref[0]
    y = rt[:, 2:3] * _unpack_bf16_pairs(ybuf[0]) + rt[:, 3:4] * _unpack_bf16_pairs(ybuf[1])
    o_ref[0] = h_ref[0] + mod_ref[0, :, 5 * d:6 * d] * y


def _combine_call(h, ys, slots, rt, mod, n_tiles):
    bsz = h.shape[0]
    tile = lambda w: pl.BlockSpec((1, ROW_TILE, w), lambda b, j: (b, j, 0))
    return pl.pallas_call(
        _combine_kernel,
        grid=(bsz, n_tiles),
        in_specs=[
            pl.BlockSpec((1, TOP_K, ROW_TILE), lambda b, j: (b * n_tiles + j, 0, 0), memory_space=pltpu.SMEM),
            tile(D_MODEL), tile(LANES),
            pl.BlockSpec((1, 1, 6 * D_MODEL), lambda b, j: (_mod_row(b, j), 0, 0)),
            pl.BlockSpec(memory_space=pl.ANY),
        ],
        out_specs=tile(D_MODEL),
        out_shape=jax.ShapeDtypeStruct((bsz, n_tiles * ROW_TILE, D_MODEL), F32),
        scratch_shapes=[pltpu.VMEM((TOP_K, ROW_TILE, PACKED_W), U32), pltpu.SemaphoreType.DMA(())],
        compiler_params=pltpu.CompilerParams(
            dimension_semantics=("arbitrary", "arbitrary"), vmem_limit_bytes=VMEM_LIMIT),
        name="moe_combine",
    )(slots, h, rt, mod, ys)


def _rope_tables():
    rows = SEQ // GRID_W
    row = jnp.repeat(jnp.arange(rows), GRID_W).astype(F32)
    col = jnp.tile(jnp.arange(GRID_W), rows).astype(F32)

    def tables(dim):
        n_freq = dim // 4
        inv = 1.0 / (ROPE_THETA ** (jnp.arange(n_freq, dtype=F32) / n_freq))
        ang_r = row[:, None] * inv
        ang_c = col[:, None] * inv
        ang = jnp.concatenate([ang_r, ang_r, ang_c, ang_c], axis=-1)
        sign = jnp.tile(jnp.concatenate([-jnp.ones((n_freq,), F32), jnp.ones((n_freq,), F32)]), 2)
        return jnp.cos(ang), jnp.sin(ang) * sign

    cos128, sin128 = tables(HEAD_DIM)
    cos64, sin64 = tables(B_ROPE)
    ones, zeros = jnp.ones((SEQ, 64), F32), jnp.zeros((SEQ, 64), F32)
    lat = jnp.concatenate([cos128, sin128,
                           cos64, cos64, sin64, sin64,
                           cos64, ones, sin64, zeros], axis=-1)
    ident = jnp.concatenate([jnp.ones((CTX_LEN, 128), F32), jnp.zeros((CTX_LEN, 128), F32)], axis=-1)
    return jnp.concatenate([lat, jnp.tile(ident, (1, 3))], axis=0)


def _pad_lanes(v, n):
    return jnp.pad(v, (0, n - v.shape[0]))


def _router_weights(wr_g, br_g, wr_e, br_e):
    w = jnp.pad(jnp.concatenate([wr_g, wr_e], axis=1), ((0, 0), (0, LANES - N_GROUPS - N_EXPERTS)))
    hi = w.astype(BF16)
    lo = (w - hi.astype(F32)).astype(BF16)
    b = _pad_lanes(jnp.concatenate([br_g, br_e]), LANES).reshape(1, LANES)
    return jnp.concatenate([hi, lo], axis=1), b


def _moe_experts(fx, rt, cnt, wg, wu, wd, layer):
    bsz, n_rows, _ = fx.shape
    n_tiles = n_rows // ROW_TILE
    n_blocks = -(-(bsz * n_rows * TOP_K) // MOE_BLOCK) + N_EXPERTS
    experts = jnp.arange(N_EXPERTS, dtype=jnp.int32)
    counts = cnt[0, N_GROUPS:N_GROUPS + N_EXPERTS].astype(jnp.int32)
    padded = (counts + MOE_BLOCK - 1) // MOE_BLOCK * MOE_BLOCK
    pad_end = jnp.cumsum(padded)
    pad_start = pad_end - padded
    e_idx = rt[..., 0:TOP_K].astype(jnp.int32)
    rank = rt[..., 4:4 + TOP_K].astype(jnp.int32)
    slot = rank + jnp.sum(jnp.where(e_idx[..., None] == experts, pad_start, 0), axis=-1)
    slots = slot.reshape(bsz * n_tiles, ROW_TILE, TOP_K).transpose(0, 2, 1)
    block_start = jnp.arange(n_blocks, dtype=jnp.int32) * MOE_BLOCK
    block_expert = jnp.minimum(jnp.sum(pad_end[None, :] <= block_start[:, None], axis=1), N_EXPERTS - 1)
    n_used = pad_end[-1:] // MOE_BLOCK
    later = (experts[None, :] > experts[:, None]) & (counts[None, :] > 0)
    next_expert = jnp.where(jnp.any(later, axis=1), jnp.argmax(later, axis=1), -1)
    i32 = lambda a: a.astype(jnp.int32)
    xs = _dispatch_call(fx, i32(slots), i32(pad_start + counts), i32(pad_end), n_blocks)
    ys = _moe_call(xs, i32(block_expert), i32(n_used), i32(next_expert), wg, wu, wd, layer)
    return ys, i32(slots)


def kernel(x, c, ctx, c_ctx, mod_w, mod_b, even_w_in, even_w_out, a_q_norm, a_k_norm, b_cq_norm, b_w_uq,
           b_ckv_norm, b_w_ukv, b_q_norm, b_k_norm, odd_w_in, odd_w_out, c_q_norm, c_k_norm, c_lambda_q1,
           c_lambda_k1, c_lambda_q2, c_lambda_k2, c_subln, d_q_norm, d_k_norm, d_sink, moe_wr_group,
           moe_br_group, moe_wr_expert, moe_br_expert, moe_w_gate, moe_w_up, moe_w_down):
    bsz = x.shape[0]
    c_all = jnp.zeros((MOD_ROWS, D_MODEL), F32).at[:bsz].set(c).at[MOD_CTX_ROW].set(c_ctx)
    mod_all = _mod_call(c_all, mod_w, mod_b)
    tab = _rope_tables()

    i = 0
    mod = mod_all[0].reshape(MOD_ROWS, 1, 6 * D_MODEL)
    scale_a = HEAD_DIM ** -0.5 * LOG2_E
    scale_b = (B_NOPE + B_ROPE) ** -0.5 * LOG2_E
    win = jnp.pad(even_w_in[i], ((0, 0), (0, EVEN_IN_PAD - even_w_in.shape[-1]))).astype(BF16)
    wuq = b_w_uq[i].reshape(B_Q_LORA, B_HEADS, B_NOPE + B_ROPE)
    wuq = jnp.pad(wuq, ((0, 0), (0, 0), (0, B_QK_PAD - B_NOPE - B_ROPE))).reshape(B_Q_LORA, -1).astype(BF16)
    wukv = b_w_ukv[i].reshape(B_KV_LORA, B_HEADS, B_NOPE + B_V)
    wukv = jnp.concatenate([wukv[:, :, :B_NOPE].reshape(B_KV_LORA, -1),
                            wukv[:, :, B_NOPE:].reshape(B_KV_LORA, -1)], axis=1).astype(BF16)
    gains = jnp.stack([
        _pad_lanes(a_q_norm[i] * scale_a, 512), _pad_lanes(a_k_norm[i], 512),
        b_cq_norm[i], b_ckv_norm[i],
        _pad_lanes(b_q_norm[i] * scale_b, 512), _pad_lanes(b_k_norm[i], 512),
        jnp.zeros((512,), F32), jnp.zeros((512,), F32)])
    qa, ka, va, qb, kb, vb = _even_proj_call(x, ctx, mod, tab, gains, win, wuq, wukv)

    g_a = A_HEADS // A_KV_HEADS
    lat = dict(q_tile0=0, key_tile0=0, n_keys=TOK)
    oa = _attn_call(qa, ka, va, n_heads_kv=A_KV_HEADS, g=g_a, dk=HEAD_DIM, dv=HEAD_DIM, nkv=2,
                    tq=256, n_q_tiles=SEQ // 256, name="gqa_latent", **lat)
    ob = _attn_call(qb, kb, vb, n_heads_kv=B_HEADS, g=1, dk=B_QK_PAD, dv=B_V, nkv=2,
                    tq=512, n_q_tiles=SEQ // 512, name="mla_latent", **lat)
    cx = dict(tq=CTX_LEN, q_tile0=SEQ // CTX_LEN, n_q_tiles=1, key_tile0=SEQ // CTX_LEN, n_keys=CTX_LEN)
    oa_c = _attn_call(qa, ka, va, n_heads_kv=A_KV_HEADS, g=g_a, dk=HEAD_DIM, dv=HEAD_DIM, nkv=2,
                      name="gqa_context", **cx)
    ob_c = _attn_call(qb, kb, vb, n_heads_kv=B_HEADS, g=1, dk=B_QK_PAD, dv=B_V, nkv=2,
                      name="mla_context", **cx)

    w_out = even_w_out[i].astype(BF16)
    rw, rb = _router_weights(moe_wr_group[0], moe_br_group[0], moe_wr_expert[0], moe_br_expert[0])
    hn, fx, rt, cnt = _outproj_call((oa, ob, x), (oa_c, ob_c, ctx), mod,
                                    w_out[:A_HEADS * HEAD_DIM], w_out[A_HEADS * HEAD_DIM:],
                                    rw, rb, N_ROW_TILES)
    ys, slots = _moe_experts(fx, rt, cnt, moe_w_gate, moe_w_up, moe_w_down, 0)
    hn0, rt0, mod0 = hn, rt, mod

    layer = 1
    mod = mod_all[1].reshape(MOD_ROWS, 1, 6 * D_MODEL)
    lambda_init = 0.8 - 0.6 * math.exp(-0.3 * layer)
    scale_c = C_HD ** -0.5 * LOG2_E
    scale_d = HEAD_DIM ** -0.5 * LOG2_E
    win = odd_w_in[i].astype(BF16)
    gains = jnp.stack([
        jnp.tile(c_q_norm[i] * scale_c, 2), jnp.tile(c_k_norm[i], 2),
        d_q_norm[i] * scale_d, d_k_norm[i],
        jnp.zeros((128,), F32), jnp.zeros((128,), F32), jnp.zeros((128,), F32), jnp.zeros((128,), F32)])
    h, qc, kc, vc, qd, kd, vd = _odd_proj_call(hn0, ys, slots, rt0, mod0, mod, tab, gains, win)

    aux = jnp.stack([
        _pad_lanes(c_lambda_q1[i], 128), _pad_lanes(c_lambda_k1[i], 128),
        _pad_lanes(c_lambda_q2[i], 128), _pad_lanes(c_lambda_k2[i], 128),
        c_subln[i], jnp.zeros((128,), F32), jnp.zeros((128,), F32), jnp.zeros((128,), F32)])
    oc = _attn_call(qc, kc, vc, n_heads_kv=C_HEADS, g=2, dk=LANES, dv=C_V, nkv=2,
                    tq=512, n_q_tiles=SEQ // 512, aux=aux, diff_lambda_init=lambda_init,
                    name="diff_latent", **lat)
    g_d = D_HEADS // D_KV_HEADS
    sink_col = jnp.repeat((d_sink[i] * LOG2_E).reshape(D_KV_HEADS, g_d), WIN_TQ, axis=1)
    sink_col = sink_col.reshape(D_KV_HEADS, g_d * WIN_TQ, 1)
    od = _window_call(qd, kd, vd, sink_col)

    w_out = odd_w_out[i].astype(BF16)
    rw, rb = _router_weights(moe_wr_group[1], moe_br_group[1], moe_wr_expert[1], moe_br_expert[1])
    n_lat_tiles = SEQ // ROW_TILE
    hn, fx, rt, cnt = _outproj_call((oc, od, h), None, mod, w_out[:C_HEADS * C_V], w_out[C_HEADS * C_V:],
                                    rw, rb, n_lat_tiles)
    ys, slots = _moe_experts(fx, rt, cnt, moe_w_gate, moe_w_up, moe_w_down, 1)
    return _combine_call(hn, ys, slots, rt, mod, n_lat_tiles)
```

```python
import functools
import math

import jax
import jax.numpy as jnp
from jax import lax
from jax.experimental import pallas as pl
from jax.experimental.pallas import tpu as pltpu

F32 = jnp.float32
BF16 = jnp.bfloat16

D_MODEL = 2048
BATCH = 8
SEQ = 2048
DEPTH = 2
GRID_W = 64
CTX_LEN = 256
TOK = SEQ + CTX_LEN
HEAD_DIM = 128
ROPE_THETA = 10000.0
NORM_EPS = 1e-6
A_HEADS = 8
A_KV_HEADS = 2
B_HEADS = 8
B_Q_LORA = 512
B_KV_LORA = 512
B_NOPE = 128
B_ROPE = 64
B_V = 128
B_QK_PAD = 256
C_HEADS = 8
C_HD = 64
C_V = 128
D_HEADS = 8
D_KV_HEADS = 2
WINDOW = 128
N_GROUPS = 4
EXPERTS_PER_GROUP = 8
N_EXPERTS = 32
D_EXPERT = 1024
MOE_BLOCK = 128
TOP_K = 2
LANES = 128
LOG2_E = math.log2(math.e)

ROW_TILE = 256
N_ROW_TILES = TOK // ROW_TILE
CTX_TILE = SEQ // ROW_TILE
MOD_ROWS = 16
MOD_CTX_ROW = BATCH
VMEM_LIMIT = 56 * 1024 * 1024


def _dot(a, b):
    return jnp.dot(a, b, preferred_element_type=F32)


def _dot_nt(a, b):
    return lax.dot_general(a, b, (((1,), (1,)), ((), ())), preferred_element_type=F32)


def _silu(x):
    return x / (1.0 + jnp.exp(-x))


def _rms(x, n):
    return x * lax.rsqrt(jnp.sum(x * x, axis=-1, keepdims=True) * (1.0 / n) + NORM_EPS)


def _modulate(x, shift, scale):
    return _rms(x, x.shape[-1]) * (1.0 + scale) + shift


def _rope(x, cos, sin_signed, quarter, lane):
    fwd = pltpu.roll(x, LANES - quarter, axis=1)
    bwd = pltpu.roll(x, quarter, axis=1)
    rot = jnp.where((lane // quarter) % 2 == 0, fwd, bwd)
    return x * cos + rot * sin_signed


PACKED_W = D_MODEL // 2
U32 = jnp.uint32


def _pack_bf16_pairs(x):
    n = x.shape[-1] // 2
    lo = lax.bitcast_convert_type(x[:, :n].astype(BF16).astype(F32), U32) >> 16
    hi = lax.bitcast_convert_type(x[:, n:].astype(BF16).astype(F32), U32) & U32(0xFFFF0000)
    return lo | hi


def _unpack_bf16_pairs(p):
    lo = lax.bitcast_convert_type(p << 16, F32)
    hi = lax.bitcast_convert_type(p & U32(0xFFFF0000), F32)
    return jnp.concatenate([lo, hi], axis=-1)


def _start_row_gathers(slot_ref, src_hbm, dst_ref, sem):
    for r in range(ROW_TILE):
        for k in range(TOP_K):
            pltpu.make_async_copy(src_hbm.at[pl.ds(slot_ref[0, k, r], 1)], dst_ref.at[k, pl.ds(r, 1)],
                                  sem).start(priority=k % 2)


def _wait_row_gathers(src_hbm, dst_ref, sem):
    for k in range(TOP_K):
        pltpu.make_async_copy(src_hbm.at[pl.ds(0, ROW_TILE)], dst_ref.at[k], sem).wait()


MOD_TN = 1024


def _mod_kernel(c_ref, w_ref, b_ref, o_ref):
    a = _silu(c_ref[...]).astype(BF16)
    o_ref[0] = _dot(a, w_ref[0].astype(BF16)) + b_ref[0]


def _mod_call(c_all, mod_w, mod_b):
    d6 = 6 * D_MODEL
    return pl.pallas_call(
        _mod_kernel,
        grid=(DEPTH, d6 // MOD_TN),
        in_specs=[
            pl.BlockSpec((MOD_ROWS, D_MODEL), lambda l, n: (0, 0)),
            pl.BlockSpec((1, D_MODEL, MOD_TN), lambda l, n: (l, 0, n)),
            pl.BlockSpec((1, 1, MOD_TN), lambda l, n: (l, 0, n)),
        ],
        out_specs=pl.BlockSpec((1, MOD_ROWS, MOD_TN), lambda l, n: (l, 0, n)),
        out_shape=jax.ShapeDtypeStruct((DEPTH, MOD_ROWS, d6), F32),
        compiler_params=pltpu.CompilerParams(
            dimension_semantics=("parallel", "parallel"), vmem_limit_bytes=VMEM_LIMIT),
        name="mod_vectors",
    )(c_all, mod_w, mod_b.reshape(DEPTH, 1, d6))


def _mod_row(b, j):
    return jnp.where(j == CTX_TILE, MOD_CTX_ROW, b)


def _resident(shape):
    nd = len(shape)
    return pl.BlockSpec(shape, lambda *_: (0,) * nd, pipeline_mode=pl.Buffered(1))


EVEN_IN_PAD = 2688


def _even_proj_kernel(hx_ref, hc_ref, mod_ref, tab_ref, g_ref, win_ref, wuq_ref, wukv_ref,
                      qa_ref, ka_ref, va_ref, qb_ref, kb_ref, vb_ref):
    d = D_MODEL
    x = jnp.where(pl.program_id(1) == CTX_TILE, hc_ref[0], hx_ref[0])
    a = _modulate(x, mod_ref[0, :, 0:d], mod_ref[0, :, d:2 * d]).astype(BF16)
    lane = lax.broadcasted_iota(jnp.int32, (ROW_TILE, LANES), 1)
    cos128, sin128 = tab_ref[:, 0:128], tab_ref[:, 128:256]
    cos64p, sin64p = tab_ref[:, 512:640], tab_ref[:, 640:768]

    z = _dot(a, win_ref[:, 0:1024])
    g_q = g_ref[0:1, 0:128]
    for h in range(A_HEADS):
        blk = _rms(z[:, h * 128:(h + 1) * 128], HEAD_DIM) * g_q
        qa_ref[0, :, h * 128:(h + 1) * 128] = _rope(blk, cos128, sin128, 32, lane).astype(BF16)
    z = _dot(a, win_ref[:, 1024:1536])
    g_k = g_ref[1:2, 0:128]
    for h in range(A_KV_HEADS):
        blk = _rms(z[:, h * 128:(h + 1) * 128], HEAD_DIM) * g_k
        ka_ref[0, :, h * 128:(h + 1) * 128] = _rope(blk, cos128, sin128, 32, lane).astype(BF16)
    va_ref[0] = z[:, 256:512].astype(BF16)

    z = _dot(a, win_ref[:, 1536:2048])
    cq = (_rms(z, B_Q_LORA) * g_ref[2:3, :]).astype(BF16)
    zq = _dot(cq, wuq_ref[...])
    gq0, gq1 = g_ref[4:5, 0:128], g_ref[4:5, 128:256]
    n_qk = float(B_NOPE + B_ROPE)
    for h in range(B_HEADS):
        b0 = zq[:, h * 256:h * 256 + 128]
        b1 = zq[:, h * 256 + 128:(h + 1) * 256]
        ss = jnp.sum(b0 * b0, axis=-1, keepdims=True) + jnp.sum(b1 * b1, axis=-1, keepdims=True)
        r = lax.rsqrt(ss * (1.0 / n_qk) + NORM_EPS)
        qb_ref[0, :, h * 256:h * 256 + 128] = (b0 * r * gq0).astype(BF16)
        qb_ref[0, :, h * 256 + 128:(h + 1) * 256] = _rope(b1 * r * gq1, cos64p, sin64p, 16, lane).astype(BF16)

    z = _dot(a, win_ref[:, 2048:2688])
    ckv = (_rms(z[:, 0:512], B_KV_LORA) * g_ref[3:4, :]).astype(BF16)
    kr = z[:, 512:640]
    ss_kr = jnp.sum(kr * kr, axis=-1, keepdims=True)
    kr_rot = _rope(kr * g_ref[5:6, 128:256], cos64p, sin64p, 16, lane)
    zkv = _dot(ckv, wukv_ref[...])
    vb_ref[0] = zkv[:, 1024:2048].astype(BF16)
    gk0 = g_ref[5:6, 0:128]
    for h in range(B_HEADS):
        kn = zkv[:, h * 128:(h + 1) * 128]
        ss = jnp.sum(kn * kn, axis=-1, keepdims=True) + ss_kr
        r = lax.rsqrt(ss * (1.0 / n_qk) + NORM_EPS)
        kb_ref[0, :, h * 256:h * 256 + 128] = (kn * r * gk0).astype(BF16)
        kb_ref[0, :, h * 256 + 128:(h + 1) * 256] = (kr_rot * r).astype(BF16)


def _lat_tile(w):
    return pl.BlockSpec((1, ROW_TILE, w), lambda b, j: (b, jnp.minimum(j, CTX_TILE - 1), 0))


def _ctx_tile(w):
    return pl.BlockSpec((1, ROW_TILE, w), lambda b, j: (b, 0, 0))


def _even_proj_call(h_lat, h_ctx, mod, tab, gains, win, wuq, wukv):
    bsz = h_lat.shape[0]
    widths = (A_HEADS * HEAD_DIM, A_KV_HEADS * HEAD_DIM, A_KV_HEADS * HEAD_DIM,
              B_HEADS * B_QK_PAD, B_HEADS * B_QK_PAD, B_HEADS * B_V)
    tile = lambda w: pl.BlockSpec((1, ROW_TILE, w), lambda b, j: (b, j, 0))
    return pl.pallas_call(
        _even_proj_kernel,
        grid=(bsz, N_ROW_TILES),
        in_specs=[
            _lat_tile(D_MODEL), _ctx_tile(D_MODEL),
            pl.BlockSpec((1, 1, 6 * D_MODEL), lambda b, j: (_mod_row(b, j), 0, 0)),
            pl.BlockSpec((ROW_TILE, 768), lambda b, j: (j, 0)),
            _resident(gains.shape), _resident(win.shape), _resident(wuq.shape), _resident(wukv.shape),
        ],
        out_specs=[tile(w) for w in widths],
        out_shape=[jax.ShapeDtypeStruct((bsz, TOK, w), BF16) for w in widths],
        compiler_params=pltpu.CompilerParams(
            dimension_semantics=("parallel", "parallel"), vmem_limit_bytes=VMEM_LIMIT),
        name="even_proj",
    )(h_lat, h_ctx, mod, tab, gains, win, wuq, wukv)


def _rms_halves(x, lane):
    x2 = x * x
    s_lo = jnp.sum(jnp.where(lane < 64, x2, 0.0), axis=-1, keepdims=True)
    s_hi = jnp.sum(jnp.where(lane < 64, 0.0, x2), axis=-1, keepdims=True)
    r = jnp.where(lane < 64, lax.rsqrt(s_lo * (1.0 / C_HD) + NORM_EPS), lax.rsqrt(s_hi * (1.0 / C_HD) + NORM_EPS))
    return x * r


def _odd_proj_kernel(slot_ref, slot_next_ref, hn_ref, rt_ref, mod_prev_ref, mod_ref, tab_ref, g_ref, win_ref,
                     ys_hbm, h_ref, qc_ref, kc_ref, vc_ref, qd_ref, kd_ref, vd_ref, ybuf, sems):
    d = D_MODEL
    n_steps = pl.num_programs(0) * pl.num_programs(1)
    step = pl.program_id(0) * pl.num_programs(1) + pl.program_id(1)
    cur = step % 2

    @pl.when(step == 0)
    def _():
        _start_row_gathers(slot_ref, ys_hbm, ybuf.at[0], sems.at[0])

    _wait_row_gathers(ys_hbm, ybuf.at[cur], sems.at[cur])
    rt = rt_ref[0]
    y = rt[:, 2:3] * _unpack_bf16_pairs(ybuf[cur, 0]) + rt[:, 3:4] * _unpack_bf16_pairs(ybuf[cur, 1])
    x = hn_ref[0] + mod_prev_ref[0, :, 5 * d:6 * d] * y
    h_ref[0] = x
    _start_row_gathers(slot_next_ref, ys_hbm, ybuf.at[1 - cur], sems.at[1 - cur])
    a = _modulate(x, mod_ref[0, :, 0:d], mod_ref[0, :, d:2 * d]).astype(BF16)
    lane = lax.broadcasted_iota(jnp.int32, (ROW_TILE, LANES), 1)
    cos128, sin128 = tab_ref[:, 0:128], tab_ref[:, 128:256]
    cos64, sin64 = tab_ref[:, 256:384], tab_ref[:, 384:512]

    z = _dot(a, win_ref[:, 0:1024])
    g_q = g_ref[0:1, :]
    for h in range(C_HEADS):
        blk = _rope(_rms_halves(z[:, h * 128:(h + 1) * 128], lane) * g_q, cos64, sin64, 16, lane)
        qc_ref[0, :, h * 256:h * 256 + 128] = jnp.where(lane < 64, blk, 0.0).astype(BF16)
        qc_ref[0, :, h * 256 + 128:(h + 1) * 256] = jnp.where(lane < 64, 0.0, blk).astype(BF16)
    z = _dot(a, win_ref[:, 1024:2048])
    g_k = g_ref[1:2, :]
    for h in range(C_HEADS):
        blk = _rope(_rms_halves(z[:, h * 128:(h + 1) * 128], lane) * g_k, cos64, sin64, 16, lane)
        kc_ref[0, :, h * 128:(h + 1) * 128] = blk.astype(BF16)
    vc_ref[0] = _dot(a, win_ref[:, 2048:3072]).astype(BF16)

    z = _dot(a, win_ref[:, 3072:4096])
    g_q = g_ref[2:3, :]
    for h in range(D_HEADS):
        blk = _rms(z[:, h * 128:(h + 1) * 128], HEAD_DIM) * g_q
        qd_ref[0, :, h * 128:(h + 1) * 128] = _rope(blk, cos128, sin128, 32, lane).astype(BF16)
    z = _dot(a, win_ref[:, 4096:4608])
    g_k = g_ref[3:4, :]
    for h in range(D_KV_HEADS):
        blk = _rms(z[:, h * 128:(h + 1) * 128], HEAD_DIM) * g_k
        kd_ref[0, :, h * 128:(h + 1) * 128] = _rope(blk, cos128, sin128, 32, lane).astype(BF16)
    vd_ref[0] = z[:, 256:512].astype(BF16)

    @pl.when(step == n_steps - 1)
    def _():
        _wait_row_gathers(ys_hbm, ybuf.at[1 - cur], sems.at[1 - cur])


def _odd_proj_call(hn, ys, slots, rt, mod_prev, mod, tab, gains, win):
    bsz = hn.shape[0]
    widths = (C_HEADS * 2 * LANES, C_HEADS * LANES, C_HEADS * C_V,
              D_HEADS * HEAD_DIM, D_KV_HEADS * HEAD_DIM, D_KV_HEADS * HEAD_DIM)
    n_steps = bsz * N_ROW_TILES
    tile = lambda w: pl.BlockSpec((1, ROW_TILE, w), lambda b, j: (b, j, 0))
    mod_spec = pl.BlockSpec((1, 1, 6 * D_MODEL), lambda b, j: (_mod_row(b, j), 0, 0))
    slot_spec = lambda ahead: pl.BlockSpec(
        (1, TOP_K, ROW_TILE), lambda b, j: (jnp.minimum(b * N_ROW_TILES + j + ahead, n_steps - 1), 0, 0),
        memory_space=pltpu.SMEM)
    return pl.pallas_call(
        _odd_proj_kernel,
        grid=(bsz, N_ROW_TILES),
        in_specs=[
            slot_spec(0), slot_spec(1), tile(D_MODEL), tile(LANES), mod_spec, mod_spec,
            pl.BlockSpec((ROW_TILE, 768), lambda b, j: (j, 0)),
            _resident(gains.shape), _resident(win.shape),
            pl.BlockSpec(memory_space=pl.ANY),
        ],
        out_specs=[tile(D_MODEL)] + [tile(w) for w in widths],
        out_shape=[jax.ShapeDtypeStruct((bsz, TOK, D_MODEL), F32)]
                  + [jax.ShapeDtypeStruct((bsz, TOK, w), BF16) for w in widths],
        scratch_shapes=[pltpu.VMEM((2, TOP_K, ROW_TILE, PACKED_W), U32), pltpu.SemaphoreType.DMA((2,))],
        compiler_params=pltpu.CompilerParams(
            dimension_semantics=("arbitrary", "arbitrary"), vmem_limit_bytes=VMEM_LIMIT),
        name="odd_proj",
    )(slots, slots, hn, rt, mod_prev, mod, tab, gains, win, ys)


KEY_CHUNK = 768


def _attn_kernel(*refs, nkv, g, dk, dv, tq, n_keys, diff_lambda_init):
    if diff_lambda_init is None:
        q_ref, k_ref, v_ref, o_ref = refs
    else:
        q_ref, k_ref, v_ref, aux_ref, o_ref = refs
    for kv in range(nkv):
        q = jnp.concatenate(
            [q_ref[0, :, (kv * g + gi) * dk:(kv * g + gi + 1) * dk] for gi in range(g)], axis=0)
        m = l = acc = None
        for c0 in range(0, n_keys, KEY_CHUNK):
            c1 = min(c0 + KEY_CHUNK, n_keys)
            s = _dot_nt(q, k_ref[0, c0:c1, kv * dk:(kv + 1) * dk])
            v = v_ref[0, c0:c1, kv * dv:(kv + 1) * dv]
            m_c = jnp.max(s, axis=-1, keepdims=True)
            if m is None:
                m = m_c
                p = jnp.exp2(s - m)
                l = jnp.sum(p, axis=-1, keepdims=True)
                acc = _dot(p.astype(BF16), v)
            else:
                m_new = jnp.maximum(m, m_c)
                alpha = jnp.exp2(m - m_new)
                p = jnp.exp2(s - m_new)
                l = alpha * l + jnp.sum(p, axis=-1, keepdims=True)
                acc = alpha * acc + _dot(p.astype(BF16), v)
                m = m_new
        o = acc / l
        if diff_lambda_init is None:
            for gi in range(g):
                o_ref[0, :, (kv * g + gi) * dv:(kv * g + gi + 1) * dv] = o[gi * tq:(gi + 1) * tq].astype(BF16)
        else:
            lam = (jnp.exp(jnp.sum(aux_ref[0:1, :] * aux_ref[1:2, :], axis=-1, keepdims=True))
                   - jnp.exp(jnp.sum(aux_ref[2:3, :] * aux_ref[3:4, :], axis=-1, keepdims=True))
                   + diff_lambda_init)
            od = o[0:tq] - lam * o[tq:2 * tq]
            od = _rms(od, dv) * aux_ref[4:5, :] * (1.0 - diff_lambda_init)
            o_ref[0, :, kv * dv:(kv + 1) * dv] = od.astype(BF16)


def _attn_call(q, k, v, *, n_heads_kv, g, dk, dv, nkv, tq, q_tile0, n_q_tiles, key_tile0, n_keys,
               aux=None, diff_lambda_init=None, name="attn"):
    bsz = q.shape[0]
    n_out_heads = n_heads_kv * (g if diff_lambda_init is None else 1)
    out_w = nkv * (g if diff_lambda_init is None else 1) * dv
    kern = functools.partial(_attn_kernel, nkv=nkv, g=g, dk=dk, dv=dv, tq=tq, n_keys=n_keys,
                             diff_lambda_init=diff_lambda_init)
    in_specs = [
        pl.BlockSpec((1, tq, nkv * g * dk), lambda b, hh, qi: (b, qi + q_tile0, hh)),
        pl.BlockSpec((1, n_keys, nkv * dk), lambda b, hh, qi: (b, key_tile0, hh)),
        pl.BlockSpec((1, n_keys, nkv * dv), lambda b, hh, qi: (b, key_tile0, hh)),
    ]
    args = [q, k, v]
    if aux is not None:
        in_specs.append(pl.BlockSpec(aux.shape, lambda b, hh, qi: (0, 0)))
        args.append(aux)
    return pl.pallas_call(
        kern,
        grid=(bsz, n_heads_kv // nkv, n_q_tiles),
        in_specs=in_specs,
        out_specs=pl.BlockSpec((1, tq, out_w), lambda b, hh, qi: (b, qi, hh)),
        out_shape=jax.ShapeDtypeStruct((bsz, n_q_tiles * tq, n_out_heads * dv), BF16),
        compiler_params=pltpu.CompilerParams(
            dimension_semantics=("parallel", "parallel", "parallel"), vmem_limit_bytes=VMEM_LIMIT),
        name=name,
    )(*args)


WIN_TQ = 256
WIN_BAND = WIN_TQ + 2 * WINDOW
NEG_BIG = -1e30


def _window_kernel(q_ref, k_ref, v_ref, sink_ref, o_ref):
    g = D_HEADS // D_KV_HEADS
    dk = HEAD_DIM
    qi = pl.program_id(2)
    q0 = qi * WIN_TQ
    start = pl.multiple_of(jnp.clip(q0 - WINDOW, 0, SEQ - WIN_BAND), WINDOW)
    q = jnp.concatenate([q_ref[0, :, gi * dk:(gi + 1) * dk] for gi in range(g)], axis=0)
    row = (lax.broadcasted_iota(jnp.int32, (g * WIN_TQ, WIN_BAND), 0) & (WIN_TQ - 1)) + q0
    col = lax.broadcasted_iota(jnp.int32, (g * WIN_TQ, WIN_BAND), 1) + start
    s_loc = jnp.where(jnp.abs(row - col) <= WINDOW, _dot_nt(q, k_ref[0, pl.ds(start, WIN_BAND), :]), NEG_BIG)
    s_ctx = _dot_nt(q, k_ref[0, SEQ:TOK, :])
    sink = sink_ref[0]
    m = jnp.maximum(jnp.maximum(jnp.max(s_loc, axis=-1, keepdims=True),
                                jnp.max(s_ctx, axis=-1, keepdims=True)), sink)
    p_loc = jnp.exp2(s_loc - m)
    p_ctx = jnp.exp2(s_ctx - m)
    l = (jnp.sum(p_loc, axis=-1, keepdims=True) + jnp.sum(p_ctx, axis=-1, keepdims=True)
         + jnp.exp2(sink - m))
    acc = (_dot(p_ctx.astype(BF16), v_ref[0, SEQ:TOK, :])
           + _dot(p_loc.astype(BF16), v_ref[0, pl.ds(start, WIN_BAND), :]))
    o = acc / l
    for gi in range(g):
        o_ref[0, :, gi * dk:(gi + 1) * dk] = o[gi * WIN_TQ:(gi + 1) * WIN_TQ].astype(BF16)


def _window_call(q, k, v, sink_col):
    bsz = q.shape[0]
    g = D_HEADS // D_KV_HEADS
    return pl.pallas_call(
        _window_kernel,
        grid=(bsz, D_KV_HEADS, SEQ // WIN_TQ),
        in_specs=[
            pl.BlockSpec((1, WIN_TQ, g * HEAD_DIM), lambda b, hh, qi: (b, qi, hh)),
            pl.BlockSpec((1, TOK, HEAD_DIM), lambda b, hh, qi: (b, 0, hh)),
            pl.BlockSpec((1, TOK, HEAD_DIM), lambda b, hh, qi: (b, 0, hh)),
            pl.BlockSpec((1, g * WIN_TQ, 1), lambda b, hh, qi: (hh, 0, 0)),
        ],
        out_specs=pl.BlockSpec((1, WIN_TQ, g * HEAD_DIM), lambda b, hh, qi: (b, qi, hh)),
        out_shape=jax.ShapeDtypeStruct((bsz, SEQ, D_HEADS * HEAD_DIM), BF16),
        compiler_params=pltpu.CompilerParams(
            dimension_semantics=("parallel", "parallel", "parallel"), vmem_limit_bytes=VMEM_LIMIT),
        name="window_attn",
    )(q, k, v, sink_col)


def _route(logits, lane):
    lane_f = lane.astype(F32)
    lg = jnp.where(lane < N_GROUPS, logits, NEG_BIG)
    g_max = jnp.max(lg, axis=-1, keepdims=True)
    p_top = 1.0 / jnp.sum(jnp.exp(lg - g_max), axis=-1, keepdims=True)
    g_idx = jnp.min(jnp.where(lg == g_max, lane_f, float(LANES)), axis=-1, keepdims=True)
    e_lane = lane - N_GROUPS
    in_group = (e_lane >= 0) & (e_lane < N_EXPERTS) & ((e_lane // EXPERTS_PER_GROUP).astype(F32) == g_idx)
    le = jnp.where(in_group, logits, NEG_BIG)
    m1 = jnp.max(le, axis=-1, keepdims=True)
    i1 = jnp.min(jnp.where(le == m1, lane_f, float(LANES)), axis=-1, keepdims=True)
    le2 = jnp.where(lane_f == i1, NEG_BIG, le)
    m2 = jnp.max(le2, axis=-1, keepdims=True)
    i2 = jnp.min(jnp.where(le2 == m2, lane_f, float(LANES)), axis=-1, keepdims=True)
    t = jnp.exp(m2 - m1)
    w1 = p_top / (1.0 + t)
    w2 = p_top * t / (1.0 + t)
    return i1, i2, w1, w2


def _outproj_kernel(*refs, has_ctx):
    if has_ctx:
        (o1_ref, o2_ref, h_ref, o1c_ref, o2c_ref, hc_ref, mod_ref, w1_ref, w2_ref, rw_ref, rb_ref,
         hn_ref, fx_ref, rt_ref, cnt_ref, run_ref) = refs
    else:
        (o1_ref, o2_ref, h_ref, mod_ref, w1_ref, w2_ref, rw_ref, rb_ref,
         hn_ref, fx_ref, rt_ref, cnt_ref, run_ref) = refs
    d = D_MODEL
    b, j = pl.program_id(0), pl.program_id(1)

    @pl.when((b == 0) & (j == 0))
    def _():
        run_ref[...] = jnp.zeros_like(run_ref)

    o1, o2, h = o1_ref[0], o2_ref[0], h_ref[0]
    if has_ctx:
        is_ctx = j == CTX_TILE
        o1 = jnp.where(is_ctx, o1c_ref[0], o1)
        o2 = jnp.where(is_ctx, o2c_ref[0], o2)
        h = jnp.where(is_ctx, hc_ref[0], h)
    acc = _dot(o1, w1_ref[...]) + _dot(o2, w2_ref[...])
    hn = h + mod_ref[0, :, 2 * d:3 * d] * acc
    hn_ref[0] = hn
    fx = _modulate(hn, mod_ref[0, :, 3 * d:4 * d], mod_ref[0, :, 4 * d:5 * d])
    fx_ref[0] = _pack_bf16_pairs(fx)
    hi = fx.astype(BF16)
    lo = (fx - hi.astype(F32)).astype(BF16)
    part = _dot(jnp.concatenate([hi, lo], axis=0), rw_ref[...])
    logits = (part[:ROW_TILE, :LANES] + part[:ROW_TILE, LANES:]
              + part[ROW_TILE:, :LANES] + part[ROW_TILE:, LANES:] + rb_ref[...])
    lane = lax.broadcasted_iota(jnp.int32, (ROW_TILE, LANES), 1)
    lane_f = lane.astype(F32)
    i1, i2, w1, w2 = _route(logits, lane)

    hit1, hit2 = lane_f == i1, lane_f == i2
    multi_hot = jnp.where(hit1, 1.0, 0.0) + jnp.where(hit2, 1.0, 0.0)
    tri = (lax.broadcasted_iota(jnp.int32, (ROW_TILE, ROW_TILE), 0)
           > lax.broadcasted_iota(jnp.int32, (ROW_TILE, ROW_TILE), 1))
    before = _dot(jnp.where(tri, 1.0, 0.0).astype(BF16), multi_hot.astype(BF16)) + run_ref[...]
    r1 = jnp.sum(jnp.where(hit1, before, 0.0), axis=-1, keepdims=True)
    r2 = jnp.sum(jnp.where(hit2, before, 0.0), axis=-1, keepdims=True)
    run_ref[...] = run_ref[...] + jnp.sum(multi_hot, axis=0, keepdims=True)
    cnt_ref[...] = jnp.broadcast_to(run_ref[...], cnt_ref.shape)

    out = jnp.where(lane == 0, i1 - N_GROUPS, 0.0)
    out = jnp.where(lane == 1, i2 - N_GROUPS, out)
    out = jnp.where(lane == 2, w1, out)
    out = jnp.where(lane == 3, w2, out)
    out = jnp.where(lane == 4, r1, out)
    out = jnp.where(lane == 5, r2, out)
    rt_ref[0] = out


def _outproj_call(lat, ctx, mod, w1, w2, rw, rb, n_tiles):
    bsz = lat[2].shape[0]
    has_ctx = ctx is not None
    tile = lambda w: pl.BlockSpec((1, ROW_TILE, w), lambda b, j: (b, j, 0))
    in_specs = [_lat_tile(a.shape[-1]) for a in lat]
    args = list(lat)
    if has_ctx:
        in_specs += [_ctx_tile(a.shape[-1]) for a in ctx]
        args += list(ctx)
    in_specs += [
        pl.BlockSpec((1, 1, 6 * D_MODEL), lambda b, j: (_mod_row(b, j), 0, 0)),
        _resident(w1.shape), _resident(w2.shape), _resident(rw.shape), _resident(rb.shape),
    ]
    args += [mod, w1, w2, rw, rb]
    n_rows = n_tiles * ROW_TILE
    return pl.pallas_call(
        functools.partial(_outproj_kernel, has_ctx=has_ctx),
        grid=(bsz, n_tiles),
        in_specs=in_specs,
        out_specs=[tile(D_MODEL), tile(PACKED_W), tile(LANES),
                   pl.BlockSpec((8, LANES), lambda b, j: (0, 0))],
        out_shape=[jax.ShapeDtypeStruct((bsz, n_rows, D_MODEL), F32),
                   jax.ShapeDtypeStruct((bsz, n_rows, PACKED_W), U32),
                   jax.ShapeDtypeStruct((bsz, n_rows, LANES), F32),
                   jax.ShapeDtypeStruct((8, LANES), F32)],
        scratch_shapes=[pltpu.VMEM((1, LANES), F32)],
        compiler_params=pltpu.CompilerParams(
            dimension_semantics=("arbitrary", "arbitrary"), vmem_limit_bytes=VMEM_LIMIT),
        name="outproj_router",
    )(*args)


def _dispatch_kernel(lo_ref, hi_ref, slot_ref, fx_ref, xs_hbm, zbuf, sem, zsem):
    b, j = pl.program_id(0), pl.program_id(1)

    for r in range(ROW_TILE):
        for k in range(TOP_K):
            pltpu.make_async_copy(fx_ref.at[0, pl.ds(r, 1)], xs_hbm.at[pl.ds(slot_ref[0, k, r], 1)],
                                  sem).start(priority=k % 2)

    @pl.when((b == pl.num_programs(0) - 1) & (j == pl.num_programs(1) - 1))
    def _():
        zbuf[...] = jnp.zeros_like(zbuf)

        def fill(s, carry):
            pltpu.make_async_copy(zbuf.at[pl.ds(0, 1)], xs_hbm.at[pl.ds(s, 1)], zsem).start()
            return carry

        def drain(s, carry):
            pltpu.make_async_copy(zbuf.at[pl.ds(0, 1)], xs_hbm.at[pl.ds(0, 1)], zsem).wait()
            return carry
        for e in range(N_EXPERTS):
            lax.fori_loop(lo_ref[e], hi_ref[e], fill, 0)
        for e in range(N_EXPERTS):
            lax.fori_loop(lo_ref[e], hi_ref[e], drain, 0)

    for k in range(TOP_K):
        pltpu.make_async_copy(fx_ref.at[0], xs_hbm.at[pl.ds(0, ROW_TILE)], sem).wait()


def _dispatch_call(fx, slots, fill_lo, fill_hi, n_blocks):
    bsz, n_rows, _ = fx.shape
    n_tiles = n_rows // ROW_TILE
    grid_spec = pltpu.PrefetchScalarGridSpec(
        num_scalar_prefetch=2,
        grid=(bsz, n_tiles),
        in_specs=[
            pl.BlockSpec((1, TOP_K, ROW_TILE), lambda b, j, lo, hi: (b * n_tiles + j, 0, 0),
                         memory_space=pltpu.SMEM),
            pl.BlockSpec((1, ROW_TILE, PACKED_W), lambda b, j, lo, hi: (b, j, 0)),
        ],
        out_specs=pl.BlockSpec(memory_space=pl.ANY),
        scratch_shapes=[pltpu.VMEM((8, PACKED_W), U32), pltpu.SemaphoreType.DMA(()), pltpu.SemaphoreType.DMA(())],
    )
    return pl.pallas_call(
        _dispatch_kernel,
        grid_spec=grid_spec,
        out_shape=jax.ShapeDtypeStruct((n_blocks * MOE_BLOCK, PACKED_W), U32),
        compiler_params=pltpu.CompilerParams(
            dimension_semantics=("arbitrary", "arbitrary"), vmem_limit_bytes=VMEM_LIMIT),
        name="moe_dispatch",
    )(fill_lo, fill_hi, slots, fx)


CAST_ROWS = 256


def _moe_kernel(be_ref, nu_ref, nx_ref, x_ref, wg_hbm, wu_hbm, wd_hbm, y_ref,
                sg, su, sd, wg, wu, wd, sems, *, layer):
    i = pl.program_id(0)
    e = be_ref[i]
    used = i < nu_ref[0]
    staged = ((wg_hbm, sg, wg, 0), (wu_hbm, su, wu, 1), (wd_hbm, sd, wd, 2))

    def fetch(expert):
        for hbm, stage, _, s in staged:
            pltpu.make_async_copy(hbm.at[layer, expert], stage, sems.at[s]).start(priority=1)

    @pl.when(i == 0)
    def _():
        fetch(e)

    first_of_expert = (i == 0) | (e != be_ref[jnp.maximum(i - 1, 0)])

    @pl.when(used & first_of_expert)
    def _():
        for hbm, stage, dst, s in staged:
            pltpu.make_async_copy(hbm.at[layer, 0], stage, sems.at[s]).wait()

            def cast(c, carry):
                rows = pl.ds(pl.multiple_of(c * CAST_ROWS, CAST_ROWS), CAST_ROWS)
                dst[rows, :] = stage[rows, :].astype(BF16)
                return carry
            lax.fori_loop(0, stage.shape[0] // CAST_ROWS, cast, 0)
        nxt = nx_ref[e]

        @pl.when(nxt >= 0)
        def _():
            fetch(nxt)

    @pl.when(used)
    def _():
        x = _unpack_bf16_pairs(x_ref[...]).astype(BF16)
        gt = _dot(x, wg[...])
        up = _dot(x, wu[...])
        u = (_silu(gt) * up).astype(BF16)
        y_ref[...] = _pack_bf16_pairs(_dot(u, wd[...]))


def _moe_call(xs, block_expert, n_used, next_expert, wg, wu, wd, layer):
    n_blocks = block_expert.shape[0]
    row_blk = pl.BlockSpec((MOE_BLOCK, PACKED_W), lambda i, be, nu, nx: (jnp.minimum(i, nu[0] - 1), 0))
    hbm = pl.BlockSpec(memory_space=pl.ANY)
    grid_spec = pltpu.PrefetchScalarGridSpec(
        num_scalar_prefetch=3,
        grid=(n_blocks,),
        in_specs=[row_blk, hbm, hbm, hbm],
        out_specs=row_blk,
        scratch_shapes=[
            pltpu.VMEM((D_MODEL, D_EXPERT), F32), pltpu.VMEM((D_MODEL, D_EXPERT), F32),
            pltpu.VMEM((D_EXPERT, D_MODEL), F32),
            pltpu.VMEM((D_MODEL, D_EXPERT), BF16), pltpu.VMEM((D_MODEL, D_EXPERT), BF16),
            pltpu.VMEM((D_EXPERT, D_MODEL), BF16),
            pltpu.SemaphoreType.DMA((3,)),
        ],
    )
    return pl.pallas_call(
        functools.partial(_moe_kernel, layer=layer),
        grid_spec=grid_spec,
        out_shape=jax.ShapeDtypeStruct((n_blocks * MOE_BLOCK, PACKED_W), U32),
        compiler_params=pltpu.CompilerParams(
            dimension_semantics=("arbitrary",), vmem_limit_bytes=VMEM_LIMIT),
        name="moe_experts",
    )(block_expert, n_used, next_expert, xs, wg, wu, wd)


def _combine_kernel(slot_ref, h_ref, rt_ref, mod_ref, ys_hbm, o_ref, ybuf, sem):
    d = D_MODEL
    _start_row_gathers(slot_ref, ys_hbm, ybuf, sem)
    _wait_row_gathers(ys_hbm, ybuf, sem)
    rt = rt_ref[0]
    y = rt[:, 2:3] * _unpack_bf16_pairs(ybuf[0]) + rt[:, 3:4] * _unpack_bf16_pairs(ybuf[1])
    o_ref[0] = h_ref[0] + mod_ref[0, :, 5 * d:6 * d] * y


def _combine_call(h, ys, slots, rt, mod, n_tiles):
    bsz = h.shape[0]
    tile = lambda w: pl.BlockSpec((1, ROW_TILE, w), lambda b, j: (b, j, 0))
    return pl.pallas_call(
        _combine_kernel,
        grid=(bsz, n_tiles),
        in_specs=[
            pl.BlockSpec((1, TOP_K, ROW_TILE), lambda b, j: (b * n_tiles + j, 0, 0), memory_space=pltpu.SMEM),
            tile(D_MODEL), tile(LANES),
            pl.BlockSpec((1, 1, 6 * D_MODEL), lambda b, j: (_mod_row(b, j), 0, 0)),
            pl.BlockSpec(memory_space=pl.ANY),
        ],
        out_specs=tile(D_MODEL),
        out_shape=jax.ShapeDtypeStruct((bsz, n_tiles * ROW_TILE, D_MODEL), F32),
        scratch_shapes=[pltpu.VMEM((TOP_K, ROW_TILE, PACKED_W), U32), pltpu.SemaphoreType.DMA(())],
        compiler_params=pltpu.CompilerParams(
            dimension_semantics=("arbitrary", "arbitrary"), vmem_limit_bytes=VMEM_LIMIT),
        name="moe_combine",
    )(slots, h, rt, mod, ys)


def _rope_tables():
    rows = SEQ // GRID_W
    row = jnp.repeat(jnp.arange(rows), GRID_W).astype(F32)
    col = jnp.tile(jnp.arange(GRID_W), rows).astype(F32)

    def tables(dim):
        n_freq = dim // 4
        inv = 1.0 / (ROPE_THETA ** (jnp.arange(n_freq, dtype=F32) / n_freq))
        ang_r = row[:, None] * inv
        ang_c = col[:, None] * inv
        ang = jnp.concatenate([ang_r, ang_r, ang_c, ang_c], axis=-1)
        sign = jnp.tile(jnp.concatenate([-jnp.ones((n_freq,), F32), jnp.ones((n_freq,), F32)]), 2)
        return jnp.cos(ang), jnp.sin(ang) * sign

    cos128, sin128 = tables(HEAD_DIM)
    cos64, sin64 = tables(B_ROPE)
    ones, zeros = jnp.ones((SEQ, 64), F32), jnp.zeros((SEQ, 64), F32)
    lat = jnp.concatenate([cos128, sin128,
                           cos64, cos64, sin64, sin64,
                           cos64, ones, sin64, zeros], axis=-1)
    ident = jnp.concatenate([jnp.ones((CTX_LEN, 128), F32), jnp.zeros((CTX_LEN, 128), F32)], axis=-1)
    return jnp.concatenate([lat, jnp.tile(ident, (1, 3))], axis=0)


def _pad_lanes(v, n):
    return jnp.pad(v, (0, n - v.shape[0]))


def _router_weights(wr_g, br_g, wr_e, br_e):
    w = jnp.pad(jnp.concatenate([wr_g, wr_e], axis=1), ((0, 0), (0, LANES - N_GROUPS - N_EXPERTS)))
    hi = w.astype(BF16)
    lo = (w - hi.astype(F32)).astype(BF16)
    b = _pad_lanes(jnp.concatenate([br_g, br_e]), LANES).reshape(1, LANES)
    return jnp.concatenate([hi, lo], axis=1), b


def _moe_experts(fx, rt, cnt, wg, wu, wd, layer):
    bsz, n_rows, _ = fx.shape
    n_tiles = n_rows // ROW_TILE
    n_blocks = -(-(bsz * n_rows * TOP_K) // MOE_BLOCK) + N_EXPERTS
    experts = jnp.arange(N_EXPERTS, dtype=jnp.int32)
    counts = cnt[0, N_GROUPS:N_GROUPS + N_EXPERTS].astype(jnp.int32)
    padded = (counts + MOE_BLOCK - 1) // MOE_BLOCK * MOE_BLOCK
    pad_end = jnp.cumsum(padded)
    pad_start = pad_end - padded
    e_idx = rt[..., 0:TOP_K].astype(jnp.int32)
    rank = rt[..., 4:4 + TOP_K].astype(jnp.int32)
    slot = rank + jnp.sum(jnp.where(e_idx[..., None] == experts, pad_start, 0), axis=-1)
    slots = slot.reshape(bsz * n_tiles, ROW_TILE, TOP_K).transpose(0, 2, 1)
    block_start = jnp.arange(n_blocks, dtype=jnp.int32) * MOE_BLOCK
    block_expert = jnp.minimum(jnp.sum(pad_end[None, :] <= block_start[:, None], axis=1), N_EXPERTS - 1)
    n_used = pad_end[-1:] // MOE_BLOCK
    later = (experts[None, :] > experts[:, None]) & (counts[None, :] > 0)
    next_expert = jnp.where(jnp.any(later, axis=1), jnp.argmax(later, axis=1), -1)
    i32 = lambda a: a.astype(jnp.int32)
    xs = _dispatch_call(fx, i32(slots), i32(pad_start + counts), i32(pad_end), n_blocks)
    ys = _moe_call(xs, i32(block_expert), i32(n_used), i32(next_expert), wg, wu, wd, layer)
    return ys, i32(slots)


def kernel(x, c, ctx, c_ctx, mod_w, mod_b, even_w_in, even_w_out, a_q_norm, a_k_norm, b_cq_norm, b_w_uq,
           b_ckv_norm, b_w_ukv, b_q_norm, b_k_norm, odd_w_in, odd_w_out, c_q_norm, c_k_norm, c_lambda_q1,
           c_lambda_k1, c_lambda_q2, c_lambda_k2, c_subln, d_q_norm, d_k_norm, d_sink, moe_wr_group,
           moe_br_group, moe_wr_expert, moe_br_expert, moe_w_gate, moe_w_up, moe_w_down):
    bsz = x.shape[0]
    c_all = jnp.zeros((MOD_ROWS, D_MODEL), F32).at[:bsz].set(c).at[MOD_CTX_ROW].set(c_ctx)
    mod_all = _mod_call(c_all, mod_w, mod_b)
    tab = _rope_tables()

    i = 0
    mod = mod_all[0].reshape(MOD_ROWS, 1, 6 * D_MODEL)
    scale_a = HEAD_DIM ** -0.5 * LOG2_E
    scale_b = (B_NOPE + B_ROPE) ** -0.5 * LOG2_E
    win = jnp.pad(even_w_in[i], ((0, 0), (0, EVEN_IN_PAD - even_w_in.shape[-1]))).astype(BF16)
    wuq = b_w_uq[i].reshape(B_Q_LORA, B_HEADS, B_NOPE + B_ROPE)
    wuq = jnp.pad(wuq, ((0, 0), (0, 0), (0, B_QK_PAD - B_NOPE - B_ROPE))).reshape(B_Q_LORA, -1).astype(BF16)
    wukv = b_w_ukv[i].reshape(B_KV_LORA, B_HEADS, B_NOPE + B_V)
    wukv = jnp.concatenate([wukv[:, :, :B_NOPE].reshape(B_KV_LORA, -1),
                            wukv[:, :, B_NOPE:].reshape(B_KV_LORA, -1)], axis=1).astype(BF16)
    gains = jnp.stack([
        _pad_lanes(a_q_norm[i] * scale_a, 512), _pad_lanes(a_k_norm[i], 512),
        b_cq_norm[i], b_ckv_norm[i],
        _pad_lanes(b_q_norm[i] * scale_b, 512), _pad_lanes(b_k_norm[i], 512),
        jnp.zeros((512,), F32), jnp.zeros((512,), F32)])
    qa, ka, va, qb, kb, vb = _even_proj_call(x, ctx, mod, tab, gains, win, wuq, wukv)

    g_a = A_HEADS // A_KV_HEADS
    lat = dict(q_tile0=0, key_tile0=0, n_keys=TOK)
    oa = _attn_call(qa, ka, va, n_heads_kv=A_KV_HEADS, g=g_a, dk=HEAD_DIM, dv=HEAD_DIM, nkv=2,
                    tq=256, n_q_tiles=SEQ // 256, name="gqa_latent", **lat)
    ob = _attn_call(qb, kb, vb, n_heads_kv=B_HEADS, g=1, dk=B_QK_PAD, dv=B_V, nkv=2,
                    tq=512, n_q_tiles=SEQ // 512, name="mla_latent", **lat)
    cx = dict(tq=CTX_LEN, q_tile0=SEQ // CTX_LEN, n_q_tiles=1, key_tile0=SEQ // CTX_LEN, n_keys=CTX_LEN)
    oa_c = _attn_call(qa, ka, va, n_heads_kv=A_KV_HEADS, g=g_a, dk=HEAD_DIM, dv=HEAD_DIM, nkv=2,
                      name="gqa_context", **cx)
    ob_c = _attn_call(qb, kb, vb, n_heads_kv=B_HEADS, g=1, dk=B_QK_PAD, dv=B_V, nkv=2,
                      name="mla_context", **cx)

    w_out = even_w_out[i].astype(BF16)
    rw, rb = _router_weights(moe_wr_group[0], moe_br_group[0], moe_wr_expert[0], moe_br_expert[0])
    hn, fx, rt, cnt = _outproj_call((oa, ob, x), (oa_c, ob_c, ctx), mod,
                                    w_out[:A_HEADS * HEAD_DIM], w_out[A_HEADS * HEAD_DIM:],
                                    rw, rb, N_ROW_TILES)
    ys, slots = _moe_experts(fx, rt, cnt, moe_w_gate, moe_w_up, moe_w_down, 0)
    hn0, rt0, mod0 = hn, rt, mod

    layer = 1
    mod = mod_all[1].reshape(MOD_ROWS, 1, 6 * D_MODEL)
    lambda_init = 0.8 - 0.6 * math.exp(-0.3 * layer)
    scale_c = C_HD ** -0.5 * LOG2_E
    scale_d = HEAD_DIM ** -0.5 * LOG2_E
    win = odd_w_in[i].astype(BF16)
    gains = jnp.stack([
        jnp.tile(c_q_norm[i] * scale_c, 2), jnp.tile(c_k_norm[i], 2),
        d_q_norm[i] * scale_d, d_k_norm[i],
        jnp.zeros((128,), F32), jnp.zeros((128,), F32), jnp.zeros((128,), F32), jnp.zeros((128,), F32)])
    h, qc, kc, vc, qd, kd, vd = _odd_proj_call(hn0, ys, slots, rt0, mod0, mod, tab, gains, win)

    aux = jnp.stack([
        _pad_lanes(c_lambda_q1[i], 128), _pad_lanes(c_lambda_k1[i], 128),
        _pad_lanes(c_lambda_q2[i], 128), _pad_lanes(c_lambda_k2[i], 128),
        c_subln[i], jnp.zeros((128,), F32), jnp.zeros((128,), F32), jnp.zeros((128,), F32)])
    oc = _attn_call(qc, kc, vc, n_heads_kv=C_HEADS, g=2, dk=LANES, dv=C_V, nkv=2,
                    tq=512, n_q_tiles=SEQ // 512, aux=aux, diff_lambda_init=lambda_init,
                    name="diff_latent", **lat)
    g_d = D_HEADS // D_KV_HEADS
    sink_col = jnp.repeat((d_sink[i] * LOG2_E).reshape(D_KV_HEADS, g_d), WIN_TQ, axis=1)
    sink_col = sink_col.reshape(D_KV_HEADS, g_d * WIN_TQ, 1)
    od = _window_call(qd, kd, vd, sink_col)

    w_out = odd_w_out[i].astype(BF16)
    rw, rb = _router_weights(moe_wr_group[1], moe_br_group[1], moe_wr_expert[1], moe_br_expert[1])
    n_lat_tiles = SEQ // ROW_TILE
    hn, fx, rt, cnt = _outproj_call((oc, od, h), None, mod, w_out[:C_HEADS * C_V], w_out[C_HEADS * C_V:],
                                    rw, rb, n_lat_tiles)
    ys, slots = _moe_experts(fx, rt, cnt, moe_w_gate, moe_w_up, moe_w_down, 1)
    return _combine_call(hn, ys, slots, rt, mod, n_lat_tiles)
```

```python
import functools
import math

import jax
import jax.numpy as jnp
from jax import lax
from jax.experimental import pallas as pl
from jax.experimental.pallas import tpu as pltpu

F32 = jnp.float32
BF16 = jnp.bfloat16

D_MODEL = 2048
BATCH = 8
SEQ = 2048
DEPTH = 2
GRID_W = 64
CTX_LEN = 256
TOK = SEQ + CTX_LEN
HEAD_DIM = 128
ROPE_THETA = 10000.0
NORM_EPS = 1e-6
A_HEADS = 8
A_KV_HEADS = 2
B_HEADS = 8
B_Q_LORA = 512
B_KV_LORA = 512
B_NOPE = 128
B_ROPE = 64
B_V = 128
B_QK_PAD = 256
C_HEADS = 8
C_HD = 64
C_V = 128
D_HEADS = 8
D_KV_HEADS = 2
WINDOW = 128
N_GROUPS = 4
EXPERTS_PER_GROUP = 8
N_EXPERTS = 32
D_EXPERT = 1024
MOE_BLOCK = 128
TOP_K = 2
LANES = 128
LOG2_E = math.log2(math.e)

ROW_TILE = 256
N_ROW_TILES = TOK // ROW_TILE
CTX_TILE = SEQ // ROW_TILE
MOD_ROWS = 16
MOD_CTX_ROW = BATCH
VMEM_LIMIT = 56 * 1024 * 1024


def _dot(a, b):
    return jnp.dot(a, b, preferred_element_type=F32)


def _dot_nt(a, b):
    return lax.dot_general(a, b, (((1,), (1,)), ((), ())), preferred_element_type=F32)


def _silu(x):
    return x / (1.0 + jnp.exp(-x))


def _rms(x, n):
    return x * lax.rsqrt(jnp.sum(x * x, axis=-1, keepdims=True) * (1.0 / n) + NORM_EPS)


def _modulate(x, shift, scale):
    return _rms(x, x.shape[-1]) * (1.0 + scale) + shift


def _rope(x, cos, sin_signed, quarter, lane):
    fwd = pltpu.roll(x, LANES - quarter, axis=1)
    bwd = pltpu.roll(x, quarter, axis=1)
    rot = jnp.where((lane // quarter) % 2 == 0, fwd, bwd)
    return x * cos + rot * sin_signed


PACKED_W = D_MODEL // 2
U32 = jnp.uint32


def _pack_bf16_pairs(x):
    n = x.shape[-1] // 2
    lo = lax.bitcast_convert_type(x[:, :n].astype(BF16).astype(F32), U32) >> 16
    hi = lax.bitcast_convert_type(x[:, n:].astype(BF16).astype(F32), U32) & U32(0xFFFF0000)
    return lo | hi


def _unpack_bf16_pairs(p):
    lo = lax.bitcast_convert_type(p << 16, F32)
    hi = lax.bitcast_convert_type(p & U32(0xFFFF0000), F32)
    return jnp.concatenate([lo, hi], axis=-1)


def _start_row_gathers(slot_ref, src_hbm, dst_ref, sem):
    for r in range(ROW_TILE):
        for k in range(TOP_K):
            pltpu.make_async_copy(src_hbm.at[pl.ds(slot_ref[0, k, r], 1)], dst_ref.at[k, pl.ds(r, 1)],
                                  sem).start(priority=k % 2)


def _wait_row_gathers(src_hbm, dst_ref, sem):
    for k in range(TOP_K):
        pltpu.make_async_copy(src_hbm.at[pl.ds(0, ROW_TILE)], dst_ref.at[k], sem).wait()


MOD_TN = 1024


def _mod_kernel(c_ref, w_ref, b_ref, o_ref):
    a = _silu(c_ref[...]).astype(BF16)
    o_ref[0] = _dot(a, w_ref[0].astype(BF16)) + b_ref[0]


def _mod_call(c_all, mod_w, mod_b):
    d6 = 6 * D_MODEL
    return pl.pallas_call(
        _mod_kernel,
        grid=(DEPTH, d6 // MOD_TN),
        in_specs=[
            pl.BlockSpec((MOD_ROWS, D_MODEL), lambda l, n: (0, 0)),
            pl.BlockSpec((1, D_MODEL, MOD_TN), lambda l, n: (l, 0, n)),
            pl.BlockSpec((1, 1, MOD_TN), lambda l, n: (l, 0, n)),
        ],
        out_specs=pl.BlockSpec((1, MOD_ROWS, MOD_TN), lambda l, n: (l, 0, n)),
        out_shape=jax.ShapeDtypeStruct((DEPTH, MOD_ROWS, d6), F32),
        compiler_params=pltpu.CompilerParams(
            dimension_semantics=("parallel", "parallel"), vmem_limit_bytes=VMEM_LIMIT),
        name="mod_vectors",
    )(c_all, mod_w, mod_b.reshape(DEPTH, 1, d6))


def _mod_row(b, j):
    return jnp.where(j == CTX_TILE, MOD_CTX_ROW, b)


def _resident(shape):
    nd = len(shape)
    return pl.BlockSpec(shape, lambda *_: (0,) * nd, pipeline_mode=pl.Buffered(1))


EVEN_IN_PAD = 2688


def _even_proj_kernel(hx_ref, hc_ref, mod_ref, tab_ref, g_ref, win_ref, wuq_ref, wukv_ref,
                      qa_ref, ka_ref, va_ref, qb_ref, kb_ref, vb_ref):
    d = D_MODEL
    x = jnp.where(pl.program_id(1) == CTX_TILE, hc_ref[0], hx_ref[0])
    a = _modulate(x, mod_ref[0, :, 0:d], mod_ref[0, :, d:2 * d]).astype(BF16)
    lane = lax.broadcasted_iota(jnp.int32, (ROW_TILE, LANES), 1)
    cos128, sin128 = tab_ref[:, 0:128], tab_ref[:, 128:256]
    cos64p, sin64p = tab_ref[:, 512:640], tab_ref[:, 640:768]

    z = _dot(a, win_ref[:, 0:1024])
    g_q = g_ref[0:1, 0:128]
    for h in range(A_HEADS):
        blk = _rms(z[:, h * 128:(h + 1) * 128], HEAD_DIM) * g_q
        qa_ref[0, :, h * 128:(h + 1) * 128] = _rope(blk, cos128, sin128, 32, lane).astype(BF16)
    z = _dot(a, win_ref[:, 1024:1536])
    g_k = g_ref[1:2, 0:128]
    for h in range(A_KV_HEADS):
        blk = _rms(z[:, h * 128:(h + 1) * 128], HEAD_DIM) * g_k
        ka_ref[0, :, h * 128:(h + 1) * 128] = _rope(blk, cos128, sin128, 32, lane).astype(BF16)
    va_ref[0] = z[:, 256:512].astype(BF16)

    z = _dot(a, win_ref[:, 1536:2048])
    cq = (_rms(z, B_Q_LORA) * g_ref[2:3, :]).astype(BF16)
    zq = _dot(cq, wuq_ref[...])
    gq0, gq1 = g_ref[4:5, 0:128], g_ref[4:5, 128:256]
    n_qk = float(B_NOPE + B_ROPE)
    for h in range(B_HEADS):
        b0 = zq[:, h * 256:h * 256 + 128]
        b1 = zq[:, h * 256 + 128:(h + 1) * 256]
        ss = jnp.sum(b0 * b0, axis=-1, keepdims=True) + jnp.sum(b1 * b1, axis=-1, keepdims=True)
        r = lax.rsqrt(ss * (1.0 / n_qk) + NORM_EPS)
        qb_ref[0, :, h * 256:h * 256 + 128] = (b0 * r * gq0).astype(BF16)
        qb_ref[0, :, h * 256 + 128:(h + 1) * 256] = _rope(b1 * r * gq1, cos64p, sin64p, 16, lane).astype(BF16)

    z = _dot(a, win_ref[:, 2048:2688])
    ckv = (_rms(z[:, 0:512], B_KV_LORA) * g_ref[3:4, :]).astype(BF16)
    kr = z[:, 512:640]
    ss_kr = jnp.sum(kr * kr, axis=-1, keepdims=True)
    kr_rot = _rope(kr * g_ref[5:6, 128:256], cos64p, sin64p, 16, lane)
    zkv = _dot(ckv, wukv_ref[...])
    vb_ref[0] = zkv[:, 1024:2048].astype(BF16)
    gk0 = g_ref[5:6, 0:128]
    for h in range(B_HEADS):
        kn = zkv[:, h * 128:(h + 1) * 128]
        ss = jnp.sum(kn * kn, axis=-1, keepdims=True) + ss_kr
        r = lax.rsqrt(ss * (1.0 / n_qk) + NORM_EPS)
        kb_ref[0, :, h * 256:h * 256 + 128] = (kn * r * gk0).astype(BF16)
        kb_ref[0, :, h * 256 + 128:(h + 1) * 256] = (kr_rot * r).astype(BF16)


def _lat_tile(w):
    return pl.BlockSpec((1, ROW_TILE, w), lambda b, j: (b, jnp.minimum(j, CTX_TILE - 1), 0))


def _ctx_tile(w):
    return pl.BlockSpec((1, ROW_TILE, w), lambda b, j: (b, 0, 0))


def _even_proj_call(h_lat, h_ctx, mod, tab, gains, win, wuq, wukv):
    bsz = h_lat.shape[0]
    widths = (A_HEADS * HEAD_DIM, A_KV_HEADS * HEAD_DIM, A_KV_HEADS * HEAD_DIM,
              B_HEADS * B_QK_PAD, B_HEADS * B_QK_PAD, B_HEADS * B_V)
    tile = lambda w: pl.BlockSpec((1, ROW_TILE, w), lambda b, j: (b, j, 0))
    return pl.pallas_call(
        _even_proj_kernel,
        grid=(bsz, N_ROW_TILES),
        in_specs=[
            _lat_tile(D_MODEL), _ctx_tile(D_MODEL),
            pl.BlockSpec((1, 1, 6 * D_MODEL), lambda b, j: (_mod_row(b, j), 0, 0)),
            pl.BlockSpec((ROW_TILE, 768), lambda b, j: (j, 0)),
            _resident(gains.shape), _resident(win.shape), _resident(wuq.shape), _resident(wukv.shape),
        ],
        out_specs=[tile(w) for w in widths],
        out_shape=[jax.ShapeDtypeStruct((bsz, TOK, w), BF16) for w in widths],
        compiler_params=pltpu.CompilerParams(
            dimension_semantics=("parallel", "parallel"), vmem_limit_bytes=VMEM_LIMIT),
        name="even_proj",
    )(h_lat, h_ctx, mod, tab, gains, win, wuq, wukv)


def _rms_halves(x, lane):
    x2 = x * x
    s_lo = jnp.sum(jnp.where(lane < 64, x2, 0.0), axis=-1, keepdims=True)
    s_hi = jnp.sum(jnp.where(lane < 64, 0.0, x2), axis=-1, keepdims=True)
    r = jnp.where(lane < 64, lax.rsqrt(s_lo * (1.0 / C_HD) + NORM_EPS), lax.rsqrt(s_hi * (1.0 / C_HD) + NORM_EPS))
    return x * r


def _odd_proj_kernel(slot_ref, slot_next_ref, hn_ref, rt_ref, mod_prev_ref, mod_ref, tab_ref, g_ref, win_ref,
                     ys_hbm, h_ref, qc_ref, kc_ref, vc_ref, qd_ref, kd_ref, vd_ref, ybuf, sems):
    d = D_MODEL
    n_steps = pl.num_programs(0) * pl.num_programs(1)
    step = pl.program_id(0) * pl.num_programs(1) + pl.program_id(1)
    cur = step % 2

    @pl.when(step == 0)
    def _():
        _start_row_gathers(slot_ref, ys_hbm, ybuf.at[0], sems.at[0])

    _wait_row_gathers(ys_hbm, ybuf.at[cur], sems.at[cur])
    rt = rt_ref[0]
    y = rt[:, 2:3] * _unpack_bf16_pairs(ybuf[cur, 0]) + rt[:, 3:4] * _unpack_bf16_pairs(ybuf[cur, 1])
    x = hn_ref[0] + mod_prev_ref[0, :, 5 * d:6 * d] * y
    h_ref[0] = x
    _start_row_gathers(slot_next_ref, ys_hbm, ybuf.at[1 - cur], sems.at[1 - cur])
    a = _modulate(x, mod_ref[0, :, 0:d], mod_ref[0, :, d:2 * d]).astype(BF16)
    lane = lax.broadcasted_iota(jnp.int32, (ROW_TILE, LANES), 1)
    cos128, sin128 = tab_ref[:, 0:128], tab_ref[:, 128:256]
    cos64, sin64 = tab_ref[:, 256:384], tab_ref[:, 384:512]

    z = _dot(a, win_ref[:, 0:1024])
    g_q = g_ref[0:1, :]
    for h in range(C_HEADS):
        blk = _rope(_rms_halves(z[:, h * 128:(h + 1) * 128], lane) * g_q, cos64, sin64, 16, lane)
        qc_ref[0, :, h * 256:h * 256 + 128] = jnp.where(lane < 64, blk, 0.0).astype(BF16)
        qc_ref[0, :, h * 256 + 128:(h + 1) * 256] = jnp.where(lane < 64, 0.0, blk).astype(BF16)
    z = _dot(a, win_ref[:, 1024:2048])
    g_k = g_ref[1:2, :]
    for h in range(C_HEADS):
        blk = _rope(_rms_halves(z[:, h * 128:(h + 1) * 128], lane) * g_k, cos64, sin64, 16, lane)
        kc_ref[0, :, h * 128:(h + 1) * 128] = blk.astype(BF16)
    vc_ref[0] = _dot(a, win_ref[:, 2048:3072]).astype(BF16)

    z = _dot(a, win_ref[:, 3072:4096])
    g_q = g_ref[2:3, :]
    for h in range(D_HEADS):
        blk = _rms(z[:, h * 128:(h + 1) * 128], HEAD_DIM) * g_q
        qd_ref[0, :, h * 128:(h + 1) * 128] = _rope(blk, cos128, sin128, 32, lane).astype(BF16)
    z = _dot(a, win_ref[:, 4096:4608])
    g_k = g_ref[3:4, :]
    for h in range(D_KV_HEADS):
        blk = _rms(z[:, h * 128:(h + 1) * 128], HEAD_DIM) * g_k
        kd_ref[0, :, h * 128:(h + 1) * 128] = _rope(blk, cos128, sin128, 32, lane).astype(BF16)
    vd_ref[0] = z[:, 256:512].astype(BF16)

    @pl.when(step == n_steps - 1)
    def _():
        _wait_row_gathers(ys_hbm, ybuf.at[1 - cur], sems.at[1 - cur])


def _odd_proj_call(hn, ys, slots, rt, mod_prev, mod, tab, gains, win):
    bsz = hn.shape[0]
    widths = (C_HEADS * 2 * LANES, C_HEADS * LANES, C_HEADS * C_V,
              D_HEADS * HEAD_DIM, D_KV_HEADS * HEAD_DIM, D_KV_HEADS * HEAD_DIM)
    n_steps = bsz * N_ROW_TILES
    tile = lambda w: pl.BlockSpec((1, ROW_TILE, w), lambda b, j: (b, j, 0))
    mod_spec = pl.BlockSpec((1, 1, 6 * D_MODEL), lambda b, j: (_mod_row(b, j), 0, 0))
    slot_spec = lambda ahead: pl.BlockSpec(
        (1, TOP_K, ROW_TILE), lambda b, j: (jnp.minimum(b * N_ROW_TILES + j + ahead, n_steps - 1), 0, 0),
        memory_space=pltpu.SMEM)
    return pl.pallas_call(
        _odd_proj_kernel,
        grid=(bsz, N_ROW_TILES),
        in_specs=[
            slot_spec(0), slot_spec(1), tile(D_MODEL), tile(LANES), mod_spec, mod_spec,
            pl.BlockSpec((ROW_TILE, 768), lambda b, j: (j, 0)),
            _resident(gains.shape), _resident(win.shape),
            pl.BlockSpec(memory_space=pl.ANY),
        ],
        out_specs=[tile(D_MODEL)] + [tile(w) for w in widths],
        out_shape=[jax.ShapeDtypeStruct((bsz, TOK, D_MODEL), F32)]
                  + [jax.ShapeDtypeStruct((bsz, TOK, w), BF16) for w in widths],
        scratch_shapes=[pltpu.VMEM((2, TOP_K, ROW_TILE, PACKED_W), U32), pltpu.SemaphoreType.DMA((2,))],
        compiler_params=pltpu.CompilerParams(
            dimension_semantics=("arbitrary", "arbitrary"), vmem_limit_bytes=VMEM_LIMIT),
        name="odd_proj",
    )(slots, slots, hn, rt, mod_prev, mod, tab, gains, win, ys)


KEY_CHUNK = 768


def _attn_kernel(*refs, nkv, g, dk, dv, tq, n_keys, diff_lambda_init):
    if diff_lambda_init is None:
        q_ref, k_ref, v_ref, o_ref = refs
    else:
        q_ref, k_ref, v_ref, aux_ref, o_ref = refs
    for kv in range(nkv):
        q = jnp.concatenate(
            [q_ref[0, :, (kv * g + gi) * dk:(kv * g + gi + 1) * dk] for gi in range(g)], axis=0)
        m = l = acc = None
        for c0 in range(0, n_keys, KEY_CHUNK):
            c1 = min(c0 + KEY_CHUNK, n_keys)
            s = _dot_nt(q, k_ref[0, c0:c1, kv * dk:(kv + 1) * dk])
            v = v_ref[0, c0:c1, kv * dv:(kv + 1) * dv]
            m_c = jnp.max(s, axis=-1, keepdims=True)
            if m is None:
                m = m_c
                p = jnp.exp2((s - m).astype(BF16))
                l = jnp.sum(p.astype(F32), axis=-1, keepdims=True)
                acc = _dot(p, v)
            else:
                m_new = jnp.maximum(m, m_c)
                alpha = jnp.exp2(m - m_new)
                p = jnp.exp2((s - m_new).astype(BF16))
                l = alpha * l + jnp.sum(p.astype(F32), axis=-1, keepdims=True)
                acc = alpha * acc + _dot(p, v)
                m = m_new
        o = acc / l
        if diff_lambda_init is None:
            for gi in range(g):
                o_ref[0, :, (kv * g + gi) * dv:(kv * g + gi + 1) * dv] = o[gi * tq:(gi + 1) * tq].astype(BF16)
        else:
            lam = (jnp.exp(jnp.sum(aux_ref[0:1, :] * aux_ref[1:2, :], axis=-1, keepdims=True))
                   - jnp.exp(jnp.sum(aux_ref[2:3, :] * aux_ref[3:4, :], axis=-1, keepdims=True))
                   + diff_lambda_init)
            od = o[0:tq] - lam * o[tq:2 * tq]
            od = _rms(od, dv) * aux_ref[4:5, :] * (1.0 - diff_lambda_init)
            o_ref[0, :, kv * dv:(kv + 1) * dv] = od.astype(BF16)


def _attn_call(q, k, v, *, n_heads_kv, g, dk, dv, nkv, tq, q_tile0, n_q_tiles, key_tile0, n_keys,
               aux=None, diff_lambda_init=None, name="attn"):
    bsz = q.shape[0]
    n_out_heads = n_heads_kv * (g if diff_lambda_init is None else 1)
    out_w = nkv * (g if diff_lambda_init is None else 1) * dv
    kern = functools.partial(_attn_kernel, nkv=nkv, g=g, dk=dk, dv=dv, tq=tq, n_keys=n_keys,
                             diff_lambda_init=diff_lambda_init)
    in_specs = [
        pl.BlockSpec((1, tq, nkv * g * dk), lambda b, hh, qi: (b, qi + q_tile0, hh)),
        pl.BlockSpec((1, n_keys, nkv * dk), lambda b, hh, qi: (b, key_tile0, hh)),
        pl.BlockSpec((1, n_keys, nkv * dv), lambda b, hh, qi: (b, key_tile0, hh)),
    ]
    args = [q, k, v]
    if aux is not None:
        in_specs.append(pl.BlockSpec(aux.shape, lambda b, hh, qi: (0, 0)))
        args.append(aux)
    return pl.pallas_call(
        kern,
        grid=(bsz, n_heads_kv // nkv, n_q_tiles),
        in_specs=in_specs,
        out_specs=pl.BlockSpec((1, tq, out_w), lambda b, hh, qi: (b, qi, hh)),
        out_shape=jax.ShapeDtypeStruct((bsz, n_q_tiles * tq, n_out_heads * dv), BF16),
        compiler_params=pltpu.CompilerParams(
            dimension_semantics=("parallel", "parallel", "parallel"), vmem_limit_bytes=VMEM_LIMIT),
        name=name,
    )(*args)


WIN_TQ = 256
WIN_BAND = WIN_TQ + 2 * WINDOW
NEG_BIG = -1e30


def _window_kernel(q_ref, k_ref, v_ref, sink_ref, o_ref):
    g = D_HEADS // D_KV_HEADS
    dk = HEAD_DIM
    qi = pl.program_id(2)
    q0 = qi * WIN_TQ
    start = pl.multiple_of(jnp.clip(q0 - WINDOW, 0, SEQ - WIN_BAND), WINDOW)
    q = jnp.concatenate([q_ref[0, :, gi * dk:(gi + 1) * dk] for gi in range(g)], axis=0)
    row = (lax.broadcasted_iota(jnp.int32, (g * WIN_TQ, WIN_BAND), 0) & (WIN_TQ - 1)) + q0
    col = lax.broadcasted_iota(jnp.int32, (g * WIN_TQ, WIN_BAND), 1) + start
    s_loc = jnp.where(jnp.abs(row - col) <= WINDOW, _dot_nt(q, k_ref[0, pl.ds(start, WIN_BAND), :]), NEG_BIG)
    s_ctx = _dot_nt(q, k_ref[0, SEQ:TOK, :])
    sink = sink_ref[0]
    m = jnp.maximum(jnp.maximum(jnp.max(s_loc, axis=-1, keepdims=True),
                                jnp.max(s_ctx, axis=-1, keepdims=True)), sink)
    p_loc = jnp.exp2(s_loc - m)
    p_ctx = jnp.exp2(s_ctx - m)
    l = (jnp.sum(p_loc, axis=-1, keepdims=True) + jnp.sum(p_ctx, axis=-1, keepdims=True)
         + jnp.exp2(sink - m))
    acc = (_dot(p_ctx.astype(BF16), v_ref[0, SEQ:TOK, :])
           + _dot(p_loc.astype(BF16), v_ref[0, pl.ds(start, WIN_BAND), :]))
    o = acc / l
    for gi in range(g):
        o_ref[0, :, gi * dk:(gi + 1) * dk] = o[gi * WIN_TQ:(gi + 1) * WIN_TQ].astype(BF16)


def _window_call(q, k, v, sink_col):
    bsz = q.shape[0]
    g = D_HEADS // D_KV_HEADS
    return pl.pallas_call(
        _window_kernel,
        grid=(bsz, D_KV_HEADS, SEQ // WIN_TQ),
        in_specs=[
            pl.BlockSpec((1, WIN_TQ, g * HEAD_DIM), lambda b, hh, qi: (b, qi, hh)),
            pl.BlockSpec((1, TOK, HEAD_DIM), lambda b, hh, qi: (b, 0, hh)),
            pl.BlockSpec((1, TOK, HEAD_DIM), lambda b, hh, qi: (b, 0, hh)),
            pl.BlockSpec((1, g * WIN_TQ, 1), lambda b, hh, qi: (hh, 0, 0)),
        ],
        out_specs=pl.BlockSpec((1, WIN_TQ, g * HEAD_DIM), lambda b, hh, qi: (b, qi, hh)),
        out_shape=jax.ShapeDtypeStruct((bsz, SEQ, D_HEADS * HEAD_DIM), BF16),
        compiler_params=pltpu.CompilerParams(
            dimension_semantics=("parallel", "parallel", "parallel"), vmem_limit_bytes=VMEM_LIMIT),
        name="window_attn",
    )(q, k, v, sink_col)


def _route(logits, lane):
    lane_f = lane.astype(F32)
    lg = jnp.where(lane < N_GROUPS, logits, NEG_BIG)
    g_max = jnp.max(lg, axis=-1, keepdims=True)
    p_top = 1.0 / jnp.sum(jnp.exp(lg - g_max), axis=-1, keepdims=True)
    g_idx = jnp.min(jnp.where(lg == g_max, lane_f, float(LANES)), axis=-1, keepdims=True)
    e_lane = lane - N_GROUPS
    in_group = (e_lane >= 0) & (e_lane < N_EXPERTS) & ((e_lane // EXPERTS_PER_GROUP).astype(F32) == g_idx)
    le = jnp.where(in_group, logits, NEG_BIG)
    m1 = jnp.max(le, axis=-1, keepdims=True)
    i1 = jnp.min(jnp.where(le == m1, lane_f, float(LANES)), axis=-1, keepdims=True)
    le2 = jnp.where(lane_f == i1, NEG_BIG, le)
    m2 = jnp.max(le2, axis=-1, keepdims=True)
    i2 = jnp.min(jnp.where(le2 == m2, lane_f, float(LANES)), axis=-1, keepdims=True)
    t = jnp.exp(m2 - m1)
    w1 = p_top / (1.0 + t)
    w2 = p_top * t / (1.0 + t)
    return i1, i2, w1, w2


def _outproj_kernel(*refs, has_ctx):
    if has_ctx:
        (o1_ref, o2_ref, h_ref, o1c_ref, o2c_ref, hc_ref, mod_ref, w1_ref, w2_ref, rw_ref, rb_ref,
         hn_ref, fx_ref, rt_ref, cnt_ref, run_ref) = refs
    else:
        (o1_ref, o2_ref, h_ref, mod_ref, w1_ref, w2_ref, rw_ref, rb_ref,
         hn_ref, fx_ref, rt_ref, cnt_ref, run_ref) = refs
    d = D_MODEL
    b, j = pl.program_id(0), pl.program_id(1)

    @pl.when((b == 0) & (j == 0))
    def _():
        run_ref[...] = jnp.zeros_like(run_ref)

    o1, o2, h = o1_ref[0], o2_ref[0], h_ref[0]
    if has_ctx:
        is_ctx = j == CTX_TILE
        o1 = jnp.where(is_ctx, o1c_ref[0], o1)
        o2 = jnp.where(is_ctx, o2c_ref[0], o2)
        h = jnp.where(is_ctx, hc_ref[0], h)
    acc = _dot(o1, w1_ref[...]) + _dot(o2, w2_ref[...])
    hn = h + mod_ref[0, :, 2 * d:3 * d] * acc
    hn_ref[0] = hn
    fx = _modulate(hn, mod_ref[0, :, 3 * d:4 * d], mod_ref[0, :, 4 * d:5 * d])
    fx_ref[0] = _pack_bf16_pairs(fx)
    hi = fx.astype(BF16)
    lo = (fx - hi.astype(F32)).astype(BF16)
    part = _dot(jnp.concatenate([hi, lo], axis=0), rw_ref[...])
    logits = (part[:ROW_TILE, :LANES] + part[:ROW_TILE, LANES:]
              + part[ROW_TILE:, :LANES] + part[ROW_TILE:, LANES:] + rb_ref[...])
    lane = lax.broadcasted_iota(jnp.int32, (ROW_TILE, LANES), 1)
    lane_f = lane.astype(F32)
    i1, i2, w1, w2 = _route(logits, lane)

    hit1, hit2 = lane_f == i1, lane_f == i2
    multi_hot = jnp.where(hit1, 1.0, 0.0) + jnp.where(hit2, 1.0, 0.0)
    tri = (lax.broadcasted_iota(jnp.int32, (ROW_TILE, ROW_TILE), 0)
           > lax.broadcasted_iota(jnp.int32, (ROW_TILE, ROW_TILE), 1))
    before = _dot(jnp.where(tri, 1.0, 0.0).astype(BF16), multi_hot.astype(BF16)) + run_ref[...]
    r1 = jnp.sum(jnp.where(hit1, before, 0.0), axis=-1, keepdims=True)
    r2 = jnp.sum(jnp.where(hit2, before, 0.0), axis=-1, keepdims=True)
    run_ref[...] = run_ref[...] + jnp.sum(multi_hot, axis=0, keepdims=True)
    cnt_ref[...] = jnp.broadcast_to(run_ref[...], cnt_ref.shape)

    out = jnp.where(lane == 0, i1 - N_GROUPS, 0.0)
    out = jnp.where(lane == 1, i2 - N_GROUPS, out)
    out = jnp.where(lane == 2, w1, out)
    out = jnp.where(lane == 3, w2, out)
    out = jnp.where(lane == 4, r1, out)
    out = jnp.where(lane == 5, r2, out)
    rt_ref[0] = out


def _outproj_call(lat, ctx, mod, w1, w2, rw, rb, n_tiles):
    bsz = lat[2].shape[0]
    has_ctx = ctx is not None
    tile = lambda w: pl.BlockSpec((1, ROW_TILE, w), lambda b, j: (b, j, 0))
    in_specs = [_lat_tile(a.shape[-1]) for a in lat]
    args = list(lat)
    if has_ctx:
        in_specs += [_ctx_tile(a.shape[-1]) for a in ctx]
        args += list(ctx)
    in_specs += [
        pl.BlockSpec((1, 1, 6 * D_MODEL), lambda b, j: (_mod_row(b, j), 0, 0)),
        _resident(w1.shape), _resident(w2.shape), _resident(rw.shape), _resident(rb.shape),
    ]
    args += [mod, w1, w2, rw, rb]
    n_rows = n_tiles * ROW_TILE
    return pl.pallas_call(
        functools.partial(_outproj_kernel, has_ctx=has_ctx),
        grid=(bsz, n_tiles),
        in_specs=in_specs,
        out_specs=[tile(D_MODEL), tile(PACKED_W), tile(LANES),
                   pl.BlockSpec((8, LANES), lambda b, j: (0, 0))],
        out_shape=[jax.ShapeDtypeStruct((bsz, n_rows, D_MODEL), F32),
                   jax.ShapeDtypeStruct((bsz, n_rows, PACKED_W), U32),
                   jax.ShapeDtypeStruct((bsz, n_rows, LANES), F32),
                   jax.ShapeDtypeStruct((8, LANES), F32)],
        scratch_shapes=[pltpu.VMEM((1, LANES), F32)],
        compiler_params=pltpu.CompilerParams(
            dimension_semantics=("arbitrary", "arbitrary"), vmem_limit_bytes=VMEM_LIMIT),
        name="outproj_router",
    )(*args)


def _dispatch_kernel(lo_ref, hi_ref, slot_ref, fx_ref, xs_hbm, zbuf, sem, zsem):
    b, j = pl.program_id(0), pl.program_id(1)

    for r in range(ROW_TILE):
        for k in range(TOP_K):
            pltpu.make_async_copy(fx_ref.at[0, pl.ds(r, 1)], xs_hbm.at[pl.ds(slot_ref[0, k, r], 1)],
                                  sem).start(priority=k % 2)

    @pl.when((b == pl.num_programs(0) - 1) & (j == pl.num_programs(1) - 1))
    def _():
        zbuf[...] = jnp.zeros_like(zbuf)

        def fill(s, carry):
            pltpu.make_async_copy(zbuf.at[pl.ds(0, 1)], xs_hbm.at[pl.ds(s, 1)], zsem).start()
            return carry

        def drain(s, carry):
            pltpu.make_async_copy(zbuf.at[pl.ds(0, 1)], xs_hbm.at[pl.ds(0, 1)], zsem).wait()
            return carry
        for e in range(N_EXPERTS):
            lax.fori_loop(lo_ref[e], hi_ref[e], fill, 0)
        for e in range(N_EXPERTS):
            lax.fori_loop(lo_ref[e], hi_ref[e], drain, 0)

    for k in range(TOP_K):
        pltpu.make_async_copy(fx_ref.at[0], xs_hbm.at[pl.ds(0, ROW_TILE)], sem).wait()


def _dispatch_call(fx, slots, fill_lo, fill_hi, n_blocks):
    bsz, n_rows, _ = fx.shape
    n_tiles = n_rows // ROW_TILE
    grid_spec = pltpu.PrefetchScalarGridSpec(
        num_scalar_prefetch=2,
        grid=(bsz, n_tiles),
        in_specs=[
            pl.BlockSpec((1, TOP_K, ROW_TILE), lambda b, j, lo, hi: (b * n_tiles + j, 0, 0),
                         memory_space=pltpu.SMEM),
            pl.BlockSpec((1, ROW_TILE, PACKED_W), lambda b, j, lo, hi: (b, j, 0)),
        ],
        out_specs=pl.BlockSpec(memory_space=pl.ANY),
        scratch_shapes=[pltpu.VMEM((8, PACKED_W), U32), pltpu.SemaphoreType.DMA(()), pltpu.SemaphoreType.DMA(())],
    )
    return pl.pallas_call(
        _dispatch_kernel,
        grid_spec=grid_spec,
        out_shape=jax.ShapeDtypeStruct((n_blocks * MOE_BLOCK, PACKED_W), U32),
        compiler_params=pltpu.CompilerParams(
            dimension_semantics=("arbitrary", "arbitrary"), vmem_limit_bytes=VMEM_LIMIT),
        name="moe_dispatch",
    )(fill_lo, fill_hi, slots, fx)


CAST_ROWS = 256


def _moe_kernel(be_ref, nu_ref, nx_ref, x_ref, wg_hbm, wu_hbm, wd_hbm, y_ref,
                sg, su, sd, wg, wu, wd, sems, *, layer):
    i = pl.program_id(0)
    e = be_ref[i]
    used = i < nu_ref[0]
    staged = ((wg_hbm, sg, wg, 0), (wu_hbm, su, wu, 1), (wd_hbm, sd, wd, 2))

    def fetch(expert):
        for hbm, stage, _, s in staged:
            pltpu.make_async_copy(hbm.at[layer, expert], stage, sems.at[s]).start(priority=1)

    @pl.when(i == 0)
    def _():
        fetch(e)

    first_of_expert = (i == 0) | (e != be_ref[jnp.maximum(i - 1, 0)])

    @pl.when(used & first_of_expert)
    def _():
        for hbm, stage, dst, s in staged:
            pltpu.make_async_copy(hbm.at[layer, 0], stage, sems.at[s]).wait()

            def cast(c, carry):
                rows = pl.ds(pl.multiple_of(c * CAST_ROWS, CAST_ROWS), CAST_ROWS)
                dst[rows, :] = stage[rows, :].astype(BF16)
                return carry
            lax.fori_loop(0, stage.shape[0] // CAST_ROWS, cast, 0)
        nxt = nx_ref[e]

        @pl.when(nxt >= 0)
        def _():
            fetch(nxt)

    @pl.when(used)
    def _():
        x = _unpack_bf16_pairs(x_ref[...]).astype(BF16)
        gt = _dot(x, wg[...])
        up = _dot(x, wu[...])
        u = (_silu(gt) * up).astype(BF16)
        y_ref[...] = _pack_bf16_pairs(_dot(u, wd[...]))


def _moe_call(xs, block_expert, n_used, next_expert, wg, wu, wd, layer):
    n_blocks = block_expert.shape[0]
    row_blk = pl.BlockSpec((MOE_BLOCK, PACKED_W), lambda i, be, nu, nx: (jnp.minimum(i, nu[0] - 1), 0))
    hbm = pl.BlockSpec(memory_space=pl.ANY)
    grid_spec = pltpu.PrefetchScalarGridSpec(
        num_scalar_prefetch=3,
        grid=(n_blocks,),
        in_specs=[row_blk, hbm, hbm, hbm],
        out_specs=row_blk,
        scratch_shapes=[
            pltpu.VMEM((D_MODEL, D_EXPERT), F32), pltpu.VMEM((D_MODEL, D_EXPERT), F32),
            pltpu.VMEM((D_EXPERT, D_MODEL), F32),
            pltpu.VMEM((D_MODEL, D_EXPERT), BF16), pltpu.VMEM((D_MODEL, D_EXPERT), BF16),
            pltpu.VMEM((D_EXPERT, D_MODEL), BF16),
            pltpu.SemaphoreType.DMA((3,)),
        ],
    )
    return pl.pallas_call(
        functools.partial(_moe_kernel, layer=layer),
        grid_spec=grid_spec,
        out_shape=jax.ShapeDtypeStruct((n_blocks * MOE_BLOCK, PACKED_W), U32),
        compiler_params=pltpu.CompilerParams(
            dimension_semantics=("arbitrary",), vmem_limit_bytes=VMEM_LIMIT),
        name="moe_experts",
    )(block_expert, n_used, next_expert, xs, wg, wu, wd)


def _combine_kernel(slot_ref, h_ref, rt_ref, mod_ref, ys_hbm, o_ref, ybuf, sem):
    d = D_MODEL
    _start_row_gathers(slot_ref, ys_hbm, ybuf, sem)
    _wait_row_gathers(ys_hbm, ybuf, sem)
    rt = rt_ref[0]
    y = rt[:, 2:3] * _unpack_bf16_pairs(ybuf[0]) + rt[:, 3:4] * _unpack_bf16_pairs(ybuf[1])
    o_ref[0] = h_ref[0] + mod_ref[0, :, 5 * d:6 * d] * y


def _combine_call(h, ys, slots, rt, mod, n_tiles):
    bsz = h.shape[0]
    tile = lambda w: pl.BlockSpec((1, ROW_TILE, w), lambda b, j: (b, j, 0))
    return pl.pallas_call(
        _combine_kernel,
        grid=(bsz, n_tiles),
        in_specs=[
            pl.BlockSpec((1, TOP_K, ROW_TILE), lambda b, j: (b * n_tiles + j, 0, 0), memory_space=pltpu.SMEM),
            tile(D_MODEL), tile(LANES),
            pl.BlockSpec((1, 1, 6 * D_MODEL), lambda b, j: (_mod_row(b, j), 0, 0)),
            pl.BlockSpec(memory_space=pl.ANY),
        ],
        out_specs=tile(D_MODEL),
        out_shape=jax.ShapeDtypeStruct((bsz, n_tiles * ROW_TILE, D_MODEL), F32),
        scratch_shapes=[pltpu.VMEM((TOP_K, ROW_TILE, PACKED_W), U32), pltpu.SemaphoreType.DMA(())],
        compiler_params=pltpu.CompilerParams(
            dimension_semantics=("arbitrary", "arbitrary"), vmem_limit_bytes=VMEM_LIMIT),
        name="moe_combine",
    )(slots, h, rt, mod, ys)


def _rope_tables():
    rows = SEQ // GRID_W
    row = jnp.repeat(jnp.arange(rows), GRID_W).astype(F32)
    col = jnp.tile(jnp.arange(GRID_W), rows).astype(F32)

    def tables(dim):
        n_freq = dim // 4
        inv = 1.0 / (ROPE_THETA ** (jnp.arange(n_freq, dtype=F32) / n_freq))
        ang_r = row[:, None] * inv
        ang_c = col[:, None] * inv
        ang = jnp.concatenate([ang_r, ang_r, ang_c, ang_c], axis=-1)
        sign = jnp.tile(jnp.concatenate([-jnp.ones((n_freq,), F32), jnp.ones((n_freq,), F32)]), 2)
        return jnp.cos(ang), jnp.sin(ang) * sign

    cos128, sin128 = tables(HEAD_DIM)
    cos64, sin64 = tables(B_ROPE)
    ones, zeros = jnp.ones((SEQ, 64), F32), jnp.zeros((SEQ, 64), F32)
    lat = jnp.concatenate([cos128, sin128,
                           cos64, cos64, sin64, sin64,
                           cos64, ones, sin64, zeros], axis=-1)
    ident = jnp.concatenate([jnp.ones((CTX_LEN, 128), F32), jnp.zeros((CTX_LEN, 128), F32)], axis=-1)
    return jnp.concatenate([lat, jnp.tile(ident, (1, 3))], axis=0)


def _pad_lanes(v, n):
    return jnp.pad(v, (0, n - v.shape[0]))


def _router_weights(wr_g, br_g, wr_e, br_e):
    w = jnp.pad(jnp.concatenate([wr_g, wr_e], axis=1), ((0, 0), (0, LANES - N_GROUPS - N_EXPERTS)))
    hi = w.astype(BF16)
    lo = (w - hi.astype(F32)).astype(BF16)
    b = _pad_lanes(jnp.concatenate([br_g, br_e]), LANES).reshape(1, LANES)
    return jnp.concatenate([hi, lo], axis=1), b


def _moe_experts(fx, rt, cnt, wg, wu, wd, layer):
    bsz, n_rows, _ = fx.shape
    n_tiles = n_rows // ROW_TILE
    n_blocks = -(-(bsz * n_rows * TOP_K) // MOE_BLOCK) + N_EXPERTS
    experts = jnp.arange(N_EXPERTS, dtype=jnp.int32)
    counts = cnt[0, N_GROUPS:N_GROUPS + N_EXPERTS].astype(jnp.int32)
    padded = (counts + MOE_BLOCK - 1) // MOE_BLOCK * MOE_BLOCK
    pad_end = jnp.cumsum(padded)
    pad_start = pad_end - padded
    e_idx = rt[..., 0:TOP_K].astype(jnp.int32)
    rank = rt[..., 4:4 + TOP_K].astype(jnp.int32)
    slot = rank + jnp.sum(jnp.where(e_idx[..., None] == experts, pad_start, 0), axis=-1)
    slots = slot.reshape(bsz * n_tiles, ROW_TILE, TOP_K).transpose(0, 2, 1)
    block_start = jnp.arange(n_blocks, dtype=jnp.int32) * MOE_BLOCK
    block_expert = jnp.minimum(jnp.sum(pad_end[None, :] <= block_start[:, None], axis=1), N_EXPERTS - 1)
    n_used = pad_end[-1:] // MOE_BLOCK
    later = (experts[None, :] > experts[:, None]) & (counts[None, :] > 0)
    next_expert = jnp.where(jnp.any(later, axis=1), jnp.argmax(later, axis=1), -1)
    i32 = lambda a: a.astype(jnp.int32)
    xs = _dispatch_call(fx, i32(slots), i32(pad_start + counts), i32(pad_end), n_blocks)
    ys = _moe_call(xs, i32(block_expert), i32(n_used), i32(next_expert), wg, wu, wd, layer)
    return ys, i32(slots)


def kernel(x, c, ctx, c_ctx, mod_w, mod_b, even_w_in, even_w_out, a_q_norm, a_k_norm, b_cq_norm, b_w_uq,
           b_ckv_norm, b_w_ukv, b_q_norm, b_k_norm, odd_w_in, odd_w_out, c_q_norm, c_k_norm, c_lambda_q1,
           c_lambda_k1, c_lambda_q2, c_lambda_k2, c_subln, d_q_norm, d_k_norm, d_sink, moe_wr_group,
           moe_br_group, moe_wr_expert, moe_br_expert, moe_w_gate, moe_w_up, moe_w_down):
    bsz = x.shape[0]
    c_all = jnp.zeros((MOD_ROWS, D_MODEL), F32).at[:bsz].set(c).at[MOD_CTX_ROW].set(c_ctx)
    mod_all = _mod_call(c_all, mod_w, mod_b)
    tab = _rope_tables()

    i = 0
    mod = mod_all[0].reshape(MOD_ROWS, 1, 6 * D_MODEL)
    scale_a = HEAD_DIM ** -0.5 * LOG2_E
    scale_b = (B_NOPE + B_ROPE) ** -0.5 * LOG2_E
    win = jnp.pad(even_w_in[i], ((0, 0), (0, EVEN_IN_PAD - even_w_in.shape[-1]))).astype(BF16)
    wuq = b_w_uq[i].reshape(B_Q_LORA, B_HEADS, B_NOPE + B_ROPE)
    wuq = jnp.pad(wuq, ((0, 0), (0, 0), (0, B_QK_PAD - B_NOPE - B_ROPE))).reshape(B_Q_LORA, -1).astype(BF16)
    wukv = b_w_ukv[i].reshape(B_KV_LORA, B_HEADS, B_NOPE + B_V)
    wukv = jnp.concatenate([wukv[:, :, :B_NOPE].reshape(B_KV_LORA, -1),
                            wukv[:, :, B_NOPE:].reshape(B_KV_LORA, -1)], axis=1).astype(BF16)
    gains = jnp.stack([
        _pad_lanes(a_q_norm[i] * scale_a, 512), _pad_lanes(a_k_norm[i], 512),
        b_cq_norm[i], b_ckv_norm[i],
        _pad_lanes(b_q_norm[i] * scale_b, 512), _pad_lanes(b_k_norm[i], 512),
        jnp.zeros((512,), F32), jnp.zeros((512,), F32)])
    qa, ka, va, qb, kb, vb = _even_proj_call(x, ctx, mod, tab, gains, win, wuq, wukv)

    g_a = A_HEADS // A_KV_HEADS
    lat = dict(q_tile0=0, key_tile0=0, n_keys=TOK)
    oa = _attn_call(qa, ka, va, n_heads_kv=A_KV_HEADS, g=g_a, dk=HEAD_DIM, dv=HEAD_DIM, nkv=2,
                    tq=256, n_q_tiles=SEQ // 256, name="gqa_latent", **lat)
    ob = _attn_call(qb, kb, vb, n_heads_kv=B_HEADS, g=1, dk=B_QK_PAD, dv=B_V, nkv=2,
                    tq=512, n_q_tiles=SEQ // 512, name="mla_latent", **lat)
    cx = dict(tq=CTX_LEN, q_tile0=SEQ // CTX_LEN, n_q_tiles=1, key_tile0=SEQ // CTX_LEN, n_keys=CTX_LEN)
    oa_c = _attn_call(qa, ka, va, n_heads_kv=A_KV_HEADS, g=g_a, dk=HEAD_DIM, dv=HEAD_DIM, nkv=2,
                      name="gqa_context", **cx)
    ob_c = _attn_call(qb, kb, vb, n_heads_kv=B_HEADS, g=1, dk=B_QK_PAD, dv=B_V, nkv=2,
                      name="mla_context", **cx)

    w_out = even_w_out[i].astype(BF16)
    rw, rb = _router_weights(moe_wr_group[0], moe_br_group[0], moe_wr_expert[0], moe_br_expert[0])
    hn, fx, rt, cnt = _outproj_call((oa, ob, x), (oa_c, ob_c, ctx), mod,
                                    w_out[:A_HEADS * HEAD_DIM], w_out[A_HEADS * HEAD_DIM:],
                                    rw, rb, N_ROW_TILES)
    ys, slots = _moe_experts(fx, rt, cnt, moe_w_gate, moe_w_up, moe_w_down, 0)
    hn0, rt0, mod0 = hn, rt, mod

    layer = 1
    mod = mod_all[1].reshape(MOD_ROWS, 1, 6 * D_MODEL)
    lambda_init = 0.8 - 0.6 * math.exp(-0.3 * layer)
    scale_c = C_HD ** -0.5 * LOG2_E
    scale_d = HEAD_DIM ** -0.5 * LOG2_E
    win = odd_w_in[i].astype(BF16)
    gains = jnp.stack([
        jnp.tile(c_q_norm[i] * scale_c, 2), jnp.tile(c_k_norm[i], 2),
        d_q_norm[i] * scale_d, d_k_norm[i],
        jnp.zeros((128,), F32), jnp.zeros((128,), F32), jnp.zeros((128,), F32), jnp.zeros((128,), F32)])
    h, qc, kc, vc, qd, kd, vd = _odd_proj_call(hn0, ys, slots, rt0, mod0, mod, tab, gains, win)

    aux = jnp.stack([
        _pad_lanes(c_lambda_q1[i], 128), _pad_lanes(c_lambda_k1[i], 128),
        _pad_lanes(c_lambda_q2[i], 128), _pad_lanes(c_lambda_k2[i], 128),
        c_subln[i], jnp.zeros((128,), F32), jnp.zeros((128,), F32), jnp.zeros((128,), F32)])
    oc = _attn_call(qc, kc, vc, n_heads_kv=C_HEADS, g=2, dk=LANES, dv=C_V, nkv=2,
                    tq=512, n_q_tiles=SEQ // 512, aux=aux, diff_lambda_init=lambda_init,
                    name="diff_latent", **lat)
    g_d = D_HEADS // D_KV_HEADS
    sink_col = jnp.repeat((d_sink[i] * LOG2_E).reshape(D_KV_HEADS, g_d), WIN_TQ, axis=1)
    sink_col = sink_col.reshape(D_KV_HEADS, g_d * WIN_TQ, 1)
    od = _window_call(qd, kd, vd, sink_col)

    w_out = odd_w_out[i].astype(BF16)
    rw, rb = _router_weights(moe_wr_group[1], moe_br_group[1], moe_wr_expert[1], moe_br_expert[1])
    n_lat_tiles = SEQ // ROW_TILE
    hn, fx, rt, cnt = _outproj_call((oc, od, h), None, mod, w_out[:C_HEADS * C_V], w_out[C_HEADS * C_V:],
                                    rw, rb, n_lat_tiles)
    ys, slots = _moe_experts(fx, rt, cnt, moe_w_gate, moe_w_up, moe_w_down, 1)
    return _combine_call(hn, ys, slots, rt, mod, n_lat_tiles)
```

```python
import functools
import math

import jax
import jax.numpy as jnp
from jax import lax
from jax.experimental import pallas as pl
from jax.experimental.pallas import tpu as pltpu

F32 = jnp.float32
BF16 = jnp.bfloat16

D_MODEL = 2048
BATCH = 8
SEQ = 2048
DEPTH = 2
GRID_W = 64
CTX_LEN = 256
TOK = SEQ + CTX_LEN
HEAD_DIM = 128
ROPE_THETA = 10000.0
NORM_EPS = 1e-6
A_HEADS = 8
A_KV_HEADS = 2
B_HEADS = 8
B_Q_LORA = 512
B_KV_LORA = 512
B_NOPE = 128
B_ROPE = 64
B_V = 128
B_QK_PAD = 256
C_HEADS = 8
C_HD = 64
C_V = 128
D_HEADS = 8
D_KV_HEADS = 2
WINDOW = 128
N_GROUPS = 4
EXPERTS_PER_GROUP = 8
N_EXPERTS = 32
D_EXPERT = 1024
MOE_BLOCK = 128
TOP_K = 2
LANES = 128
LOG2_E = math.log2(math.e)

ROW_TILE = 256
N_ROW_TILES = TOK // ROW_TILE
CTX_TILE = SEQ // ROW_TILE
MOD_ROWS = 16
MOD_CTX_ROW = BATCH
VMEM_LIMIT = 56 * 1024 * 1024


def _dot(a, b):
    return jnp.dot(a, b, preferred_element_type=F32)


def _dot_nt(a, b):
    return lax.dot_general(a, b, (((1,), (1,)), ((), ())), preferred_element_type=F32)


def _silu(x):
    return x / (1.0 + jnp.exp(-x))


def _rms(x, n):
    return x * lax.rsqrt(jnp.sum(x * x, axis=-1, keepdims=True) * (1.0 / n) + NORM_EPS)


def _modulate(x, shift, scale):
    return _rms(x, x.shape[-1]) * (1.0 + scale) + shift


def _rope(x, cos, sin_signed, quarter, lane):
    fwd = pltpu.roll(x, LANES - quarter, axis=1)
    bwd = pltpu.roll(x, quarter, axis=1)
    rot = jnp.where((lane // quarter) % 2 == 0, fwd, bwd)
    return x * cos + rot * sin_signed


PACKED_W = D_MODEL // 2
U32 = jnp.uint32


def _pack_bf16_pairs(x):
    n = x.shape[-1] // 2
    lo = lax.bitcast_convert_type(x[:, :n].astype(BF16).astype(F32), U32) >> 16
    hi = lax.bitcast_convert_type(x[:, n:].astype(BF16).astype(F32), U32) & U32(0xFFFF0000)
    return lo | hi


def _unpack_bf16_pairs(p):
    lo = lax.bitcast_convert_type(p << 16, F32)
    hi = lax.bitcast_convert_type(p & U32(0xFFFF0000), F32)
    return jnp.concatenate([lo, hi], axis=-1)


def _start_row_gathers(slot_ref, src_hbm, dst_ref, sem):
    for r in range(ROW_TILE):
        for k in range(TOP_K):
            pltpu.make_async_copy(src_hbm.at[pl.ds(slot_ref[0, k, r], 1)], dst_ref.at[k, pl.ds(r, 1)],
                                  sem).start(priority=k % 2)


def _wait_row_gathers(src_hbm, dst_ref, sem):
    for k in range(TOP_K):
        pltpu.make_async_copy(src_hbm.at[pl.ds(0, ROW_TILE)], dst_ref.at[k], sem).wait()


MOD_TN = 1024


def _mod_kernel(c_ref, w_ref, b_ref, o_ref):
    a = _silu(c_ref[...]).astype(BF16)
    o_ref[0] = _dot(a, w_ref[0].astype(BF16)) + b_ref[0]


def _mod_call(c_all, mod_w, mod_b):
    d6 = 6 * D_MODEL
    return pl.pallas_call(
        _mod_kernel,
        grid=(DEPTH, d6 // MOD_TN),
        in_specs=[
            pl.BlockSpec((MOD_ROWS, D_MODEL), lambda l, n: (0, 0)),
            pl.BlockSpec((1, D_MODEL, MOD_TN), lambda l, n: (l, 0, n)),
            pl.BlockSpec((1, 1, MOD_TN), lambda l, n: (l, 0, n)),
        ],
        out_specs=pl.BlockSpec((1, MOD_ROWS, MOD_TN), lambda l, n: (l, 0, n)),
        out_shape=jax.ShapeDtypeStruct((DEPTH, MOD_ROWS, d6), F32),
        compiler_params=pltpu.CompilerParams(
            dimension_semantics=("parallel", "parallel"), vmem_limit_bytes=VMEM_LIMIT),
        name="mod_vectors",
    )(c_all, mod_w, mod_b.reshape(DEPTH, 1, d6))


def _mod_row(b, j):
    return jnp.where(j == CTX_TILE, MOD_CTX_ROW, b)


def _resident(shape):
    nd = len(shape)
    return pl.BlockSpec(shape, lambda *_: (0,) * nd, pipeline_mode=pl.Buffered(1))


EVEN_IN_PAD = 2688


def _even_proj_kernel(hx_ref, hc_ref, mod_ref, tab_ref, g_ref, win_ref, wuq_ref, wukv_ref,
                      qa_ref, ka_ref, va_ref, qb_ref, kb_ref, vb_ref):
    d = D_MODEL
    x = jnp.where(pl.program_id(1) == CTX_TILE, hc_ref[0], hx_ref[0])
    a = _modulate(x, mod_ref[0, :, 0:d], mod_ref[0, :, d:2 * d]).astype(BF16)
    lane = lax.broadcasted_iota(jnp.int32, (ROW_TILE, LANES), 1)
    cos128, sin128 = tab_ref[:, 0:128], tab_ref[:, 128:256]
    cos64p, sin64p = tab_ref[:, 512:640], tab_ref[:, 640:768]

    z = _dot(a, win_ref[:, 0:1024])
    g_q = g_ref[0:1, 0:128]
    for h in range(A_HEADS):
        blk = _rms(z[:, h * 128:(h + 1) * 128], HEAD_DIM) * g_q
        qa_ref[0, :, h * 128:(h + 1) * 128] = _rope(blk, cos128, sin128, 32, lane).astype(BF16)
    z = _dot(a, win_ref[:, 1024:1536])
    g_k = g_ref[1:2, 0:128]
    for h in range(A_KV_HEADS):
        blk = _rms(z[:, h * 128:(h + 1) * 128], HEAD_DIM) * g_k
        ka_ref[0, :, h * 128:(h + 1) * 128] = _rope(blk, cos128, sin128, 32, lane).astype(BF16)
    va_ref[0] = z[:, 256:512].astype(BF16)

    z = _dot(a, win_ref[:, 1536:2048])
    cq = (_rms(z, B_Q_LORA) * g_ref[2:3, :]).astype(BF16)
    zq = _dot(cq, wuq_ref[...])
    gq0, gq1 = g_ref[4:5, 0:128], g_ref[4:5, 128:256]
    n_qk = float(B_NOPE + B_ROPE)
    for h in range(B_HEADS):
        b0 = zq[:, h * 256:h * 256 + 128]
        b1 = zq[:, h * 256 + 128:(h + 1) * 256]
        ss = jnp.sum(b0 * b0, axis=-1, keepdims=True) + jnp.sum(b1 * b1, axis=-1, keepdims=True)
        r = lax.rsqrt(ss * (1.0 / n_qk) + NORM_EPS)
        qb_ref[0, :, h * 256:h * 256 + 128] = (b0 * r * gq0).astype(BF16)
        qb_ref[0, :, h * 256 + 128:(h + 1) * 256] = _rope(b1 * r * gq1, cos64p, sin64p, 16, lane).astype(BF16)

    z = _dot(a, win_ref[:, 2048:2688])
    ckv = (_rms(z[:, 0:512], B_KV_LORA) * g_ref[3:4, :]).astype(BF16)
    kr = z[:, 512:640]
    ss_kr = jnp.sum(kr * kr, axis=-1, keepdims=True)
    kr_rot = _rope(kr * g_ref[5:6, 128:256], cos64p, sin64p, 16, lane)
    zkv = _dot(ckv, wukv_ref[...])
    vb_ref[0] = zkv[:, 1024:2048].astype(BF16)
    gk0 = g_ref[5:6, 0:128]
    for h in range(B_HEADS):
        kn = zkv[:, h * 128:(h + 1) * 128]
        ss = jnp.sum(kn * kn, axis=-1, keepdims=True) + ss_kr
        r = lax.rsqrt(ss * (1.0 / n_qk) + NORM_EPS)
        kb_ref[0, :, h * 256:h * 256 + 128] = (kn * r * gk0).astype(BF16)
        kb_ref[0, :, h * 256 + 128:(h + 1) * 256] = (kr_rot * r).astype(BF16)


def _lat_tile(w):
    return pl.BlockSpec((1, ROW_TILE, w), lambda b, j: (b, jnp.minimum(j, CTX_TILE - 1), 0))


def _ctx_tile(w):
    return pl.BlockSpec((1, ROW_TILE, w), lambda b, j: (b, 0, 0))


def _even_proj_call(h_lat, h_ctx, mod, tab, gains, win, wuq, wukv):
    bsz = h_lat.shape[0]
    widths = (A_HEADS * HEAD_DIM, A_KV_HEADS * HEAD_DIM, A_KV_HEADS * HEAD_DIM,
              B_HEADS * B_QK_PAD, B_HEADS * B_QK_PAD, B_HEADS * B_V)
    tile = lambda w: pl.BlockSpec((1, ROW_TILE, w), lambda b, j: (b, j, 0))
    return pl.pallas_call(
        _even_proj_kernel,
        grid=(bsz, N_ROW_TILES),
        in_specs=[
            _lat_tile(D_MODEL), _ctx_tile(D_MODEL),
            pl.BlockSpec((1, 1, 6 * D_MODEL), lambda b, j: (_mod_row(b, j), 0, 0)),
            pl.BlockSpec((ROW_TILE, 768), lambda b, j: (j, 0)),
            _resident(gains.shape), _resident(win.shape), _resident(wuq.shape), _resident(wukv.shape),
        ],
        out_specs=[tile(w) for w in widths],
        out_shape=[jax.ShapeDtypeStruct((bsz, TOK, w), BF16) for w in widths],
        compiler_params=pltpu.CompilerParams(
            dimension_semantics=("parallel", "parallel"), vmem_limit_bytes=VMEM_LIMIT),
        name="even_proj",
    )(h_lat, h_ctx, mod, tab, gains, win, wuq, wukv)


def _rms_halves(x, lane):
    x2 = x * x
    s_lo = jnp.sum(jnp.where(lane < 64, x2, 0.0), axis=-1, keepdims=True)
    s_hi = jnp.sum(jnp.where(lane < 64, 0.0, x2), axis=-1, keepdims=True)
    r = jnp.where(lane < 64, lax.rsqrt(s_lo * (1.0 / C_HD) + NORM_EPS), lax.rsqrt(s_hi * (1.0 / C_HD) + NORM_EPS))
    return x * r


def _odd_proj_kernel(slot_ref, slot_next_ref, hn_ref, rt_ref, mod_prev_ref, mod_ref, tab_ref, g_ref, win_ref,
                     ys_hbm, h_ref, qc_ref, kc_ref, vc_ref, qd_ref, kd_ref, vd_ref, ybuf, sems):
    d = D_MODEL
    n_steps = pl.num_programs(0) * pl.num_programs(1)
    step = pl.program_id(0) * pl.num_programs(1) + pl.program_id(1)
    cur = step % 2

    @pl.when(step == 0)
    def _():
        _start_row_gathers(slot_ref, ys_hbm, ybuf.at[0], sems.at[0])

    _wait_row_gathers(ys_hbm, ybuf.at[cur], sems.at[cur])
    rt = rt_ref[0]
    y = rt[:, 2:3] * _unpack_bf16_pairs(ybuf[cur, 0]) + rt[:, 3:4] * _unpack_bf16_pairs(ybuf[cur, 1])
    x = hn_ref[0] + mod_prev_ref[0, :, 5 * d:6 * d] * y
    h_ref[0] = x
    _start_row_gathers(slot_next_ref, ys_hbm, ybuf.at[1 - cur], sems.at[1 - cur])
    a = _modulate(x, mod_ref[0, :, 0:d], mod_ref[0, :, d:2 * d]).astype(BF16)
    lane = lax.broadcasted_iota(jnp.int32, (ROW_TILE, LANES), 1)
    cos128, sin128 = tab_ref[:, 0:128], tab_ref[:, 128:256]
    cos64, sin64 = tab_ref[:, 256:384], tab_ref[:, 384:512]

    z = _dot(a, win_ref[:, 0:1024])
    g_q = g_ref[0:1, :]
    for h in range(C_HEADS):
        blk = _rope(_rms_halves(z[:, h * 128:(h + 1) * 128], lane) * g_q, cos64, sin64, 16, lane)
        qc_ref[0, :, h * 256:h * 256 + 128] = jnp.where(lane < 64, blk, 0.0).astype(BF16)
        qc_ref[0, :, h * 256 + 128:(h + 1) * 256] = jnp.where(lane < 64, 0.0, blk).astype(BF16)
    z = _dot(a, win_ref[:, 1024:2048])
    g_k = g_ref[1:2, :]
    for h in range(C_HEADS):
        blk = _rope(_rms_halves(z[:, h * 128:(h + 1) * 128], lane) * g_k, cos64, sin64, 16, lane)
        kc_ref[0, :, h * 128:(h + 1) * 128] = blk.astype(BF16)
    vc_ref[0] = _dot(a, win_ref[:, 2048:3072]).astype(BF16)

    z = _dot(a, win_ref[:, 3072:4096])
    g_q = g_ref[2:3, :]
    for h in range(D_HEADS):
        blk = _rms(z[:, h * 128:(h + 1) * 128], HEAD_DIM) * g_q
        qd_ref[0, :, h * 128:(h + 1) * 128] = _rope(blk, cos128, sin128, 32, lane).astype(BF16)
    z = _dot(a, win_ref[:, 4096:4608])
    g_k = g_ref[3:4, :]
    for h in range(D_KV_HEADS):
        blk = _rms(z[:, h * 128:(h + 1) * 128], HEAD_DIM) * g_k
        kd_ref[0, :, h * 128:(h + 1) * 128] = _rope(blk, cos128, sin128, 32, lane).astype(BF16)
    vd_ref[0] = z[:, 256:512].astype(BF16)

    @pl.when(step == n_steps - 1)
    def _():
        _wait_row_gathers(ys_hbm, ybuf.at[1 - cur], sems.at[1 - cur])


def _odd_proj_call(hn, ys, slots, rt, mod_prev, mod, tab, gains, win):
    bsz = hn.shape[0]
    widths = (C_HEADS * 2 * LANES, C_HEADS * LANES, C_HEADS * C_V,
              D_HEADS * HEAD_DIM, D_KV_HEADS * HEAD_DIM, D_KV_HEADS * HEAD_DIM)
    n_steps = bsz * N_ROW_TILES
    tile = lambda w: pl.BlockSpec((1, ROW_TILE, w), lambda b, j: (b, j, 0))
    mod_spec = pl.BlockSpec((1, 1, 6 * D_MODEL), lambda b, j: (_mod_row(b, j), 0, 0))
    slot_spec = lambda ahead: pl.BlockSpec(
        (1, TOP_K, ROW_TILE), lambda b, j: (jnp.minimum(b * N_ROW_TILES + j + ahead, n_steps - 1), 0, 0),
        memory_space=pltpu.SMEM)
    return pl.pallas_call(
        _odd_proj_kernel,
        grid=(bsz, N_ROW_TILES),
        in_specs=[
            slot_spec(0), slot_spec(1), tile(D_MODEL), tile(LANES), mod_spec, mod_spec,
            pl.BlockSpec((ROW_TILE, 768), lambda b, j: (j, 0)),
            _resident(gains.shape), _resident(win.shape),
            pl.BlockSpec(memory_space=pl.ANY),
        ],
        out_specs=[tile(D_MODEL)] + [tile(w) for w in widths],
        out_shape=[jax.ShapeDtypeStruct((bsz, TOK, D_MODEL), F32)]
                  + [jax.ShapeDtypeStruct((bsz, TOK, w), BF16) for w in widths],
        scratch_shapes=[pltpu.VMEM((2, TOP_K, ROW_TILE, PACKED_W), U32), pltpu.SemaphoreType.DMA((2,))],
        compiler_params=pltpu.CompilerParams(
            dimension_semantics=("arbitrary", "arbitrary"), vmem_limit_bytes=VMEM_LIMIT),
        name="odd_proj",
    )(slots, slots, hn, rt, mod_prev, mod, tab, gains, win, ys)


KEY_CHUNK = 768


def _attn_kernel(*refs, nkv, g, dk, dv, tq, n_keys, diff_lambda_init):
    if diff_lambda_init is None:
        q_ref, k_ref, v_ref, o_ref = refs
    else:
        q_ref, k_ref, v_ref, aux_ref, o_ref = refs
    for kv in range(nkv):
        q = jnp.concatenate(
            [q_ref[0, :, (kv * g + gi) * dk:(kv * g + gi + 1) * dk] for gi in range(g)], axis=0)
        m = l = acc = None
        for c0 in range(0, n_keys, KEY_CHUNK):
            c1 = min(c0 + KEY_CHUNK, n_keys)
            s = _dot_nt(q, k_ref[0, c0:c1, kv * dk:(kv + 1) * dk])
            v = v_ref[0, c0:c1, kv * dv:(kv + 1) * dv]
            m_c = jnp.max(s, axis=-1, keepdims=True)
            if m is None:
                m = m_c
                p = jnp.exp2((s - m).astype(BF16))
                l = jnp.sum(p.astype(F32), axis=-1, keepdims=True)
                acc = _dot(p, v)
            else:
                m_new = jnp.maximum(m, m_c)
                alpha = jnp.exp2(m - m_new)
                p = jnp.exp2((s - m_new).astype(BF16))
                l = alpha * l + jnp.sum(p.astype(F32), axis=-1, keepdims=True)
                acc = alpha * acc + _dot(p, v)
                m = m_new
        o = acc / l
        if diff_lambda_init is None:
            for gi in range(g):
                o_ref[0, :, (kv * g + gi) * dv:(kv * g + gi + 1) * dv] = o[gi * tq:(gi + 1) * tq].astype(BF16)
        else:
            lam = (jnp.exp(jnp.sum(aux_ref[0:1, :] * aux_ref[1:2, :], axis=-1, keepdims=True))
                   - jnp.exp(jnp.sum(aux_ref[2:3, :] * aux_ref[3:4, :], axis=-1, keepdims=True))
                   + diff_lambda_init)
            od = o[0:tq] - lam * o[tq:2 * tq]
            od = _rms(od, dv) * aux_ref[4:5, :] * (1.0 - diff_lambda_init)
            o_ref[0, :, kv * dv:(kv + 1) * dv] = od.astype(BF16)


def _attn_call(q, k, v, *, n_heads_kv, g, dk, dv, nkv, tq, q_tile0, n_q_tiles, key_tile0, n_keys,
               aux=None, diff_lambda_init=None, name="attn"):
    bsz = q.shape[0]
    n_out_heads = n_heads_kv * (g if diff_lambda_init is None else 1)
    out_w = nkv * (g if diff_lambda_init is None else 1) * dv
    kern = functools.partial(_attn_kernel, nkv=nkv, g=g, dk=dk, dv=dv, tq=tq, n_keys=n_keys,
                             diff_lambda_init=diff_lambda_init)
    in_specs = [
        pl.BlockSpec((1, tq, nkv * g * dk), lambda b, hh, qi: (b, qi + q_tile0, hh)),
        pl.BlockSpec((1, n_keys, nkv * dk), lambda b, hh, qi: (b, key_tile0, hh)),
        pl.BlockSpec((1, n_keys, nkv * dv), lambda b, hh, qi: (b, key_tile0, hh)),
    ]
    args = [q, k, v]
    if aux is not None:
        in_specs.append(pl.BlockSpec(aux.shape, lambda b, hh, qi: (0, 0)))
        args.append(aux)
    return pl.pallas_call(
        kern,
        grid=(bsz, n_heads_kv // nkv, n_q_tiles),
        in_specs=in_specs,
        out_specs=pl.BlockSpec((1, tq, out_w), lambda b, hh, qi: (b, qi, hh)),
        out_shape=jax.ShapeDtypeStruct((bsz, n_q_tiles * tq, n_out_heads * dv), BF16),
        compiler_params=pltpu.CompilerParams(
            dimension_semantics=("parallel", "parallel", "parallel"), vmem_limit_bytes=VMEM_LIMIT),
        name=name,
    )(*args)


WIN_TQ = 256
WIN_BAND = WIN_TQ + 2 * WINDOW
NEG_BIG = -1e30


def _window_kernel(q_ref, k_ref, v_ref, sink_ref, o_ref):
    g = D_HEADS // D_KV_HEADS
    dk = HEAD_DIM
    qi = pl.program_id(2)
    q0 = qi * WIN_TQ
    start = pl.multiple_of(jnp.clip(q0 - WINDOW, 0, SEQ - WIN_BAND), WINDOW)
    q = jnp.concatenate([q_ref[0, :, gi * dk:(gi + 1) * dk] for gi in range(g)], axis=0)
    row = (lax.broadcasted_iota(jnp.int32, (g * WIN_TQ, WIN_BAND), 0) & (WIN_TQ - 1)) + q0
    col = lax.broadcasted_iota(jnp.int32, (g * WIN_TQ, WIN_BAND), 1) + start
    s_loc = jnp.where(jnp.abs(row - col) <= WINDOW, _dot_nt(q, k_ref[0, pl.ds(start, WIN_BAND), :]), NEG_BIG)
    s_ctx = _dot_nt(q, k_ref[0, SEQ:TOK, :])
    sink = sink_ref[0]
    m = jnp.maximum(jnp.maximum(jnp.max(s_loc, axis=-1, keepdims=True),
                                jnp.max(s_ctx, axis=-1, keepdims=True)), sink)
    p_loc = jnp.exp2(s_loc - m)
    p_ctx = jnp.exp2(s_ctx - m)
    l = (jnp.sum(p_loc, axis=-1, keepdims=True) + jnp.sum(p_ctx, axis=-1, keepdims=True)
         + jnp.exp2(sink - m))
    acc = (_dot(p_ctx.astype(BF16), v_ref[0, SEQ:TOK, :])
           + _dot(p_loc.astype(BF16), v_ref[0, pl.ds(start, WIN_BAND), :]))
    o = acc / l
    for gi in range(g):
        o_ref[0, :, gi * dk:(gi + 1) * dk] = o[gi * WIN_TQ:(gi + 1) * WIN_TQ].astype(BF16)


def _window_call(q, k, v, sink_col):
    bsz = q.shape[0]
    g = D_HEADS // D_KV_HEADS
    return pl.pallas_call(
        _window_kernel,
        grid=(bsz, D_KV_HEADS, SEQ // WIN_TQ),
        in_specs=[
            pl.BlockSpec((1, WIN_TQ, g * HEAD_DIM), lambda b, hh, qi: (b, qi, hh)),
            pl.BlockSpec((1, TOK, HEAD_DIM), lambda b, hh, qi: (b, 0, hh)),
            pl.BlockSpec((1, TOK, HEAD_DIM), lambda b, hh, qi: (b, 0, hh)),
            pl.BlockSpec((1, g * WIN_TQ, 1), lambda b, hh, qi: (hh, 0, 0)),
        ],
        out_specs=pl.BlockSpec((1, WIN_TQ, g * HEAD_DIM), lambda b, hh, qi: (b, qi, hh)),
        out_shape=jax.ShapeDtypeStruct((bsz, SEQ, D_HEADS * HEAD_DIM), BF16),
        compiler_params=pltpu.CompilerParams(
            dimension_semantics=("parallel", "parallel", "parallel"), vmem_limit_bytes=VMEM_LIMIT),
        name="window_attn",
    )(q, k, v, sink_col)


def _route(logits, lane):
    lane_f = lane.astype(F32)
    lg = jnp.where(lane < N_GROUPS, logits, NEG_BIG)
    g_max = jnp.max(lg, axis=-1, keepdims=True)
    p_top = 1.0 / jnp.sum(jnp.exp(lg - g_max), axis=-1, keepdims=True)
    g_idx = jnp.min(jnp.where(lg == g_max, lane_f, float(LANES)), axis=-1, keepdims=True)
    e_lane = lane - N_GROUPS
    in_group = (e_lane >= 0) & (e_lane < N_EXPERTS) & ((e_lane // EXPERTS_PER_GROUP).astype(F32) == g_idx)
    le = jnp.where(in_group, logits, NEG_BIG)
    m1 = jnp.max(le, axis=-1, keepdims=True)
    i1 = jnp.min(jnp.where(le == m1, lane_f, float(LANES)), axis=-1, keepdims=True)
    le2 = jnp.where(lane_f == i1, NEG_BIG, le)
    m2 = jnp.max(le2, axis=-1, keepdims=True)
    i2 = jnp.min(jnp.where(le2 == m2, lane_f, float(LANES)), axis=-1, keepdims=True)
    t = jnp.exp(m2 - m1)
    w1 = p_top / (1.0 + t)
    w2 = p_top * t / (1.0 + t)
    return i1, i2, w1, w2


def _outproj_kernel(*refs, has_ctx, n_tiles, n_total):
    if has_ctx:
        (o1_ref, o2_ref, h_ref, o1c_ref, o2c_ref, hc_ref, mod_ref, w1_ref, w2_ref, rw_ref, rb_ref,
         hn_ref, fx_ref, rt_ref, cnt_ref, run_ref, logit_ref) = refs
    else:
        (o1_ref, o2_ref, h_ref, mod_ref, w1_ref, w2_ref, rw_ref, rb_ref,
         hn_ref, fx_ref, rt_ref, cnt_ref, run_ref, logit_ref) = refs
    d = D_MODEL
    s = pl.program_id(0)

    @pl.when(s == 0)
    def _():
        run_ref[...] = jnp.zeros_like(run_ref)
        logit_ref[...] = jnp.zeros_like(logit_ref)

    logits = logit_ref[...]
    lane = lax.broadcasted_iota(jnp.int32, (ROW_TILE, LANES), 1)
    lane_f = lane.astype(F32)
    i1, i2, w1, w2 = _route(logits, lane)

    o1, o2, h = o1_ref[0], o2_ref[0], h_ref[0]
    if has_ctx:
        is_ctx = jnp.minimum(s, n_total - 1) % n_tiles == CTX_TILE
        o1 = jnp.where(is_ctx, o1c_ref[0], o1)
        o2 = jnp.where(is_ctx, o2c_ref[0], o2)
        h = jnp.where(is_ctx, hc_ref[0], h)
    acc = _dot(o1, w1_ref[...]) + _dot(o2, w2_ref[...])
    hn = h + mod_ref[0, :, 2 * d:3 * d] * acc
    hn_ref[0] = hn
    fx = _modulate(hn, mod_ref[0, :, 3 * d:4 * d], mod_ref[0, :, 4 * d:5 * d])
    fx_ref[0] = _pack_bf16_pairs(fx)
    hi = fx.astype(BF16)
    lo = (fx - hi.astype(F32)).astype(BF16)
    part = _dot(jnp.concatenate([hi, lo], axis=0), rw_ref[...])
    logit_ref[...] = (part[:ROW_TILE, :LANES] + part[:ROW_TILE, LANES:]
                      + part[ROW_TILE:, :LANES] + part[ROW_TILE:, LANES:] + rb_ref[...])

    hit1, hit2 = lane_f == i1, lane_f == i2
    multi_hot = jnp.where(s > 0, jnp.where(hit1, 1.0, 0.0) + jnp.where(hit2, 1.0, 0.0), 0.0)
    tri = (lax.broadcasted_iota(jnp.int32, (ROW_TILE, ROW_TILE), 0)
           > lax.broadcasted_iota(jnp.int32, (ROW_TILE, ROW_TILE), 1))
    before = _dot(jnp.where(tri, 1.0, 0.0).astype(BF16), multi_hot.astype(BF16)) + run_ref[...]
    r1 = jnp.sum(jnp.where(hit1, before, 0.0), axis=-1, keepdims=True)
    r2 = jnp.sum(jnp.where(hit2, before, 0.0), axis=-1, keepdims=True)
    run_ref[...] = run_ref[...] + jnp.sum(multi_hot, axis=0, keepdims=True)
    cnt_ref[...] = jnp.broadcast_to(run_ref[...], cnt_ref.shape)

    out = jnp.where(lane == 0, i1 - N_GROUPS, 0.0)
    out = jnp.where(lane == 1, i2 - N_GROUPS, out)
    out = jnp.where(lane == 2, w1, out)
    out = jnp.where(lane == 3, w2, out)
    out = jnp.where(lane == 4, r1, out)
    out = jnp.where(lane == 5, r2, out)
    rt_ref[0] = out


def _outproj_call(lat, ctx, mod, w1, w2, rw, rb, n_tiles):
    bsz = lat[2].shape[0]
    has_ctx = ctx is not None
    n_total = bsz * n_tiles
    stage_a = lambda s: jnp.minimum(s, n_total - 1)
    stage_b = lambda s: jnp.maximum(s - 1, 0)
    bj = lambda t: (t // n_tiles, t % n_tiles)

    def a_tile(w, kind):
        def index(s):
            b, j = bj(stage_a(s))
            return (b, {"all": j, "lat": jnp.minimum(j, CTX_TILE - 1), "ctx": 0}[kind], 0)
        return pl.BlockSpec((1, ROW_TILE, w), index)

    def b_tile(w):
        return pl.BlockSpec((1, ROW_TILE, w), lambda s: (*bj(stage_b(s)), 0))

    in_specs = [a_tile(a.shape[-1], "lat") for a in lat]
    args = list(lat)
    if has_ctx:
        in_specs += [a_tile(a.shape[-1], "ctx") for a in ctx]
        args += list(ctx)
    in_specs += [
        pl.BlockSpec((1, 1, 6 * D_MODEL), lambda s: (_mod_row(*bj(stage_a(s))), 0, 0)),
        _resident(w1.shape), _resident(w2.shape), _resident(rw.shape), _resident(rb.shape),
    ]
    args += [mod, w1, w2, rw, rb]
    n_rows = n_tiles * ROW_TILE
    return pl.pallas_call(
        functools.partial(_outproj_kernel, has_ctx=has_ctx, n_tiles=n_tiles, n_total=n_total),
        grid=(n_total + 1,),
        in_specs=in_specs,
        out_specs=[a_tile(D_MODEL, "all"), a_tile(PACKED_W, "all"), b_tile(LANES),
                   pl.BlockSpec((8, LANES), lambda s: (0, 0))],
        out_shape=[jax.ShapeDtypeStruct((bsz, n_rows, D_MODEL), F32),
                   jax.ShapeDtypeStruct((bsz, n_rows, PACKED_W), U32),
                   jax.ShapeDtypeStruct((bsz, n_rows, LANES), F32),
                   jax.ShapeDtypeStruct((8, LANES), F32)],
        scratch_shapes=[pltpu.VMEM((1, LANES), F32), pltpu.VMEM((ROW_TILE, LANES), F32)],
        compiler_params=pltpu.CompilerParams(
            dimension_semantics=("arbitrary",), vmem_limit_bytes=VMEM_LIMIT),
        name="outproj_router",
    )(*args)


def _dispatch_kernel(lo_ref, hi_ref, slot_ref, fx_ref, xs_hbm, zbuf, sem, zsem):
    b, j = pl.program_id(0), pl.program_id(1)

    for r in range(ROW_TILE):
        for k in range(TOP_K):
            pltpu.make_async_copy(fx_ref.at[0, pl.ds(r, 1)], xs_hbm.at[pl.ds(slot_ref[0, k, r], 1)],
                                  sem).start(priority=k % 2)

    @pl.when((b == pl.num_programs(0) - 1) & (j == pl.num_programs(1) - 1))
    def _():
        zbuf[...] = jnp.zeros_like(zbuf)

        def fill(s, carry):
            pltpu.make_async_copy(zbuf.at[pl.ds(0, 1)], xs_hbm.at[pl.ds(s, 1)], zsem).start()
            return carry

        def drain(s, carry):
            pltpu.make_async_copy(zbuf.at[pl.ds(0, 1)], xs_hbm.at[pl.ds(0, 1)], zsem).wait()
            return carry
        for e in range(N_EXPERTS):
            lax.fori_loop(lo_ref[e], hi_ref[e], fill, 0)
        for e in range(N_EXPERTS):
            lax.fori_loop(lo_ref[e], hi_ref[e], drain, 0)

    for k in range(TOP_K):
        pltpu.make_async_copy(fx_ref.at[0], xs_hbm.at[pl.ds(0, ROW_TILE)], sem).wait()


def _dispatch_call(fx, slots, fill_lo, fill_hi, n_blocks):
    bsz, n_rows, _ = fx.shape
    n_tiles = n_rows // ROW_TILE
    grid_spec = pltpu.PrefetchScalarGridSpec(
        num_scalar_prefetch=2,
        grid=(bsz, n_tiles),
        in_specs=[
            pl.BlockSpec((1, TOP_K, ROW_TILE), lambda b, j, lo, hi: (b * n_tiles + j, 0, 0),
                         memory_space=pltpu.SMEM),
            pl.BlockSpec((1, ROW_TILE, PACKED_W), lambda b, j, lo, hi: (b, j, 0)),
        ],
        out_specs=pl.BlockSpec(memory_space=pl.ANY),
        scratch_shapes=[pltpu.VMEM((8, PACKED_W), U32), pltpu.SemaphoreType.DMA(()), pltpu.SemaphoreType.DMA(())],
    )
    return pl.pallas_call(
        _dispatch_kernel,
        grid_spec=grid_spec,
        out_shape=jax.ShapeDtypeStruct((n_blocks * MOE_BLOCK, PACKED_W), U32),
        compiler_params=pltpu.CompilerParams(
            dimension_semantics=("arbitrary", "arbitrary"), vmem_limit_bytes=VMEM_LIMIT),
        name="moe_dispatch",
    )(fill_lo, fill_hi, slots, fx)


CAST_ROWS = 256


def _moe_kernel(be_ref, nu_ref, nx_ref, x_ref, wg_hbm, wu_hbm, wd_hbm, y_ref,
                sg, su, sd, wg, wu, wd, sems, *, layer):
    i = pl.program_id(0)
    e = be_ref[i]
    used = i < nu_ref[0]
    staged = ((wg_hbm, sg, wg, 0), (wu_hbm, su, wu, 1), (wd_hbm, sd, wd, 2))

    def fetch(expert):
        for hbm, stage, _, s in staged:
            pltpu.make_async_copy(hbm.at[layer, expert], stage, sems.at[s]).start(priority=1)

    @pl.when(i == 0)
    def _():
        fetch(e)

    first_of_expert = (i == 0) | (e != be_ref[jnp.maximum(i - 1, 0)])

    @pl.when(used & first_of_expert)
    def _():
        for hbm, stage, dst, s in staged:
            pltpu.make_async_copy(hbm.at[layer, 0], stage, sems.at[s]).wait()

            def cast(c, carry):
                rows = pl.ds(pl.multiple_of(c * CAST_ROWS, CAST_ROWS), CAST_ROWS)
                dst[rows, :] = stage[rows, :].astype(BF16)
                return carry
            lax.fori_loop(0, stage.shape[0] // CAST_ROWS, cast, 0)
        nxt = nx_ref[e]

        @pl.when(nxt >= 0)
        def _():
            fetch(nxt)

    @pl.when(used)
    def _():
        x = _unpack_bf16_pairs(x_ref[...]).astype(BF16)
        gt = _dot(x, wg[...])
        up = _dot(x, wu[...])
        u = (_silu(gt) * up).astype(BF16)
        y_ref[...] = _pack_bf16_pairs(_dot(u, wd[...]))


def _moe_call(xs, block_expert, n_used, next_expert, wg, wu, wd, layer):
    n_blocks = block_expert.shape[0]
    row_blk = pl.BlockSpec((MOE_BLOCK, PACKED_W), lambda i, be, nu, nx: (jnp.minimum(i, nu[0] - 1), 0))
    hbm = pl.BlockSpec(memory_space=pl.ANY)
    grid_spec = pltpu.PrefetchScalarGridSpec(
        num_scalar_prefetch=3,
        grid=(n_blocks,),
        in_specs=[row_blk, hbm, hbm, hbm],
        out_specs=row_blk,
        scratch_shapes=[
            pltpu.VMEM((D_MODEL, D_EXPERT), F32), pltpu.VMEM((D_MODEL, D_EXPERT), F32),
            pltpu.VMEM((D_EXPERT, D_MODEL), F32),
            pltpu.VMEM((D_MODEL, D_EXPERT), BF16), pltpu.VMEM((D_MODEL, D_EXPERT), BF16),
            pltpu.VMEM((D_EXPERT, D_MODEL), BF16),
            pltpu.SemaphoreType.DMA((3,)),
        ],
    )
    return pl.pallas_call(
        functools.partial(_moe_kernel, layer=layer),
        grid_spec=grid_spec,
        out_shape=jax.ShapeDtypeStruct((n_blocks * MOE_BLOCK, PACKED_W), U32),
        compiler_params=pltpu.CompilerParams(
            dimension_semantics=("arbitrary",), vmem_limit_bytes=VMEM_LIMIT),
        name="moe_experts",
    )(block_expert, n_used, next_expert, xs, wg, wu, wd)


def _combine_kernel(slot_ref, h_ref, rt_ref, mod_ref, ys_hbm, o_ref, ybuf, sem):
    d = D_MODEL
    _start_row_gathers(slot_ref, ys_hbm, ybuf, sem)
    _wait_row_gathers(ys_hbm, ybuf, sem)
    rt = rt_ref[0]
    y = rt[:, 2:3] * _unpack_bf16_pairs(ybuf[0]) + rt[:, 3:4] * _unpack_bf16_pairs(ybuf[1])
    o_ref[0] = h_ref[0] + mod_ref[0, :, 5 * d:6 * d] * y


def _combine_call(h, ys, slots, rt, mod, n_tiles):
    bsz = h.shape[0]
    tile = lambda w: pl.BlockSpec((1, ROW_TILE, w), lambda b, j: (b, j, 0))
    return pl.pallas_call(
        _combine_kernel,
        grid=(bsz, n_tiles),
        in_specs=[
            pl.BlockSpec((1, TOP_K, ROW_TILE), lambda b, j: (b * n_tiles + j, 0, 0), memory_space=pltpu.SMEM),
            tile(D_MODEL), tile(LANES),
            pl.BlockSpec((1, 1, 6 * D_MODEL), lambda b, j: (_mod_row(b, j), 0, 0)),
            pl.BlockSpec(memory_space=pl.ANY),
        ],
        out_specs=tile(D_MODEL),
        out_shape=jax.ShapeDtypeStruct((bsz, n_tiles * ROW_TILE, D_MODEL), F32),
        scratch_shapes=[pltpu.VMEM((TOP_K, ROW_TILE, PACKED_W), U32), pltpu.SemaphoreType.DMA(())],
        compiler_params=pltpu.CompilerParams(
            dimension_semantics=("arbitrary", "arbitrary"), vmem_limit_bytes=VMEM_LIMIT),
        name="moe_combine",
    )(slots, h, rt, mod, ys)


def _rope_tables():
    rows = SEQ // GRID_W
    row = jnp.repeat(jnp.arange(rows), GRID_W).astype(F32)
    col = jnp.tile(jnp.arange(GRID_W), rows).astype(F32)

    def tables(dim):
        n_freq = dim // 4
        inv = 1.0 / (ROPE_THETA ** (jnp.arange(n_freq, dtype=F32) / n_freq))
        ang_r = row[:, None] * inv
        ang_c = col[:, None] * inv
        ang = jnp.concatenate([ang_r, ang_r, ang_c, ang_c], axis=-1)
        sign = jnp.tile(jnp.concatenate([-jnp.ones((n_freq,), F32), jnp.ones((n_freq,), F32)]), 2)
        return jnp.cos(ang), jnp.sin(ang) * sign

    cos128, sin128 = tables(HEAD_DIM)
    cos64, sin64 = tables(B_ROPE)
    ones, zeros = jnp.ones((SEQ, 64), F32), jnp.zeros((SEQ, 64), F32)
    lat = jnp.concatenate([cos128, sin128,
                           cos64, cos64, sin64, sin64,
                           cos64, ones, sin64, zeros], axis=-1)
    ident = jnp.concatenate([jnp.ones((CTX_LEN, 128), F32), jnp.zeros((CTX_LEN, 128), F32)], axis=-1)
    return jnp.concatenate([lat, jnp.tile(ident, (1, 3))], axis=0)


def _pad_lanes(v, n):
    return jnp.pad(v, (0, n - v.shape[0]))


def _router_weights(wr_g, br_g, wr_e, br_e):
    w = jnp.pad(jnp.concatenate([wr_g, wr_e], axis=1), ((0, 0), (0, LANES - N_GROUPS - N_EXPERTS)))
    hi = w.astype(BF16)
    lo = (w - hi.astype(F32)).astype(BF16)
    b = _pad_lanes(jnp.concatenate([br_g, br_e]), LANES).reshape(1, LANES)
    return jnp.concatenate([hi, lo], axis=1), b


def _moe_experts(fx, rt, cnt, wg, wu, wd, layer):
    bsz, n_rows, _ = fx.shape
    n_tiles = n_rows // ROW_TILE
    n_blocks = -(-(bsz * n_rows * TOP_K) // MOE_BLOCK) + N_EXPERTS
    experts = jnp.arange(N_EXPERTS, dtype=jnp.int32)
    counts = cnt[0, N_GROUPS:N_GROUPS + N_EXPERTS].astype(jnp.int32)
    padded = (counts + MOE_BLOCK - 1) // MOE_BLOCK * MOE_BLOCK
    pad_end = jnp.cumsum(padded)
    pad_start = pad_end - padded
    e_idx = rt[..., 0:TOP_K].astype(jnp.int32)
    rank = rt[..., 4:4 + TOP_K].astype(jnp.int32)
    slot = rank + jnp.sum(jnp.where(e_idx[..., None] == experts, pad_start, 0), axis=-1)
    slots = slot.reshape(bsz * n_tiles, ROW_TILE, TOP_K).transpose(0, 2, 1)
    block_start = jnp.arange(n_blocks, dtype=jnp.int32) * MOE_BLOCK
    block_expert = jnp.minimum(jnp.sum(pad_end[None, :] <= block_start[:, None], axis=1), N_EXPERTS - 1)
    n_used = pad_end[-1:] // MOE_BLOCK
    later = (experts[None, :] > experts[:, None]) & (counts[None, :] > 0)
    next_expert = jnp.where(jnp.any(later, axis=1), jnp.argmax(later, axis=1), -1)
    i32 = lambda a: a.astype(jnp.int32)
    xs = _dispatch_call(fx, i32(slots), i32(pad_start + counts), i32(pad_end), n_blocks)
    ys = _moe_call(xs, i32(block_expert), i32(n_used), i32(next_expert), wg, wu, wd, layer)
    return ys, i32(slots)


def kernel(x, c, ctx, c_ctx, mod_w, mod_b, even_w_in, even_w_out, a_q_norm, a_k_norm, b_cq_norm, b_w_uq,
           b_ckv_norm, b_w_ukv, b_q_norm, b_k_norm, odd_w_in, odd_w_out, c_q_norm, c_k_norm, c_lambda_q1,
           c_lambda_k1, c_lambda_q2, c_lambda_k2, c_subln, d_q_norm, d_k_norm, d_sink, moe_wr_group,
           moe_br_group, moe_wr_expert, moe_br_expert, moe_w_gate, moe_w_up, moe_w_down):
    bsz = x.shape[0]
    c_all = jnp.zeros((MOD_ROWS, D_MODEL), F32).at[:bsz].set(c).at[MOD_CTX_ROW].set(c_ctx)
    mod_all = _mod_call(c_all, mod_w, mod_b)
    tab = _rope_tables()

    i = 0
    mod = mod_all[0].reshape(MOD_ROWS, 1, 6 * D_MODEL)
    scale_a = HEAD_DIM ** -0.5 * LOG2_E
    scale_b = (B_NOPE + B_ROPE) ** -0.5 * LOG2_E
    win = jnp.pad(even_w_in[i], ((0, 0), (0, EVEN_IN_PAD - even_w_in.shape[-1]))).astype(BF16)
    wuq = b_w_uq[i].reshape(B_Q_LORA, B_HEADS, B_NOPE + B_ROPE)
    wuq = jnp.pad(wuq, ((0, 0), (0, 0), (0, B_QK_PAD - B_NOPE - B_ROPE))).reshape(B_Q_LORA, -1).astype(BF16)
    wukv = b_w_ukv[i].reshape(B_KV_LORA, B_HEADS, B_NOPE + B_V)
    wukv = jnp.concatenate([wukv[:, :, :B_NOPE].reshape(B_KV_LORA, -1),
                            wukv[:, :, B_NOPE:].reshape(B_KV_LORA, -1)], axis=1).astype(BF16)
    gains = jnp.stack([
        _pad_lanes(a_q_norm[i] * scale_a, 512), _pad_lanes(a_k_norm[i], 512),
        b_cq_norm[i], b_ckv_norm[i],
        _pad_lanes(b_q_norm[i] * scale_b, 512), _pad_lanes(b_k_norm[i], 512),
        jnp.zeros((512,), F32), jnp.zeros((512,), F32)])
    qa, ka, va, qb, kb, vb = _even_proj_call(x, ctx, mod, tab, gains, win, wuq, wukv)

    g_a = A_HEADS // A_KV_HEADS
    lat = dict(q_tile0=0, key_tile0=0, n_keys=TOK)
    oa = _attn_call(qa, ka, va, n_heads_kv=A_KV_HEADS, g=g_a, dk=HEAD_DIM, dv=HEAD_DIM, nkv=2,
                    tq=256, n_q_tiles=SEQ // 256, name="gqa_latent", **lat)
    ob = _attn_call(qb, kb, vb, n_heads_kv=B_HEADS, g=1, dk=B_QK_PAD, dv=B_V, nkv=2,
                    tq=512, n_q_tiles=SEQ // 512, name="mla_latent", **lat)
    cx = dict(tq=CTX_LEN, q_tile0=SEQ // CTX_LEN, n_q_tiles=1, key_tile0=SEQ // CTX_LEN, n_keys=CTX_LEN)
    oa_c = _attn_call(qa, ka, va, n_heads_kv=A_KV_HEADS, g=g_a, dk=HEAD_DIM, dv=HEAD_DIM, nkv=2,
                      name="gqa_context", **cx)
    ob_c = _attn_call(qb, kb, vb, n_heads_kv=B_HEADS, g=1, dk=B_QK_PAD, dv=B_V, nkv=2,
                      name="mla_context", **cx)

    w_out = even_w_out[i].astype(BF16)
    rw, rb = _router_weights(moe_wr_group[0], moe_br_group[0], moe_wr_expert[0], moe_br_expert[0])
    hn, fx, rt, cnt = _outproj_call((oa, ob, x), (oa_c, ob_c, ctx), mod,
                                    w_out[:A_HEADS * HEAD_DIM], w_out[A_HEADS * HEAD_DIM:],
                                    rw, rb, N_ROW_TILES)
    ys, slots = _moe_experts(fx, rt, cnt, moe_w_gate, moe_w_up, moe_w_down, 0)
    hn0, rt0, mod0 = hn, rt, mod

    layer = 1
    mod = mod_all[1].reshape(MOD_ROWS, 1, 6 * D_MODEL)
    lambda_init = 0.8 - 0.6 * math.exp(-0.3 * layer)
    scale_c = C_HD ** -0.5 * LOG2_E
    scale_d = HEAD_DIM ** -0.5 * LOG2_E
    win = odd_w_in[i].astype(BF16)
    gains = jnp.stack([
        jnp.tile(c_q_norm[i] * scale_c, 2), jnp.tile(c_k_norm[i], 2),
        d_q_norm[i] * scale_d, d_k_norm[i],
        jnp.zeros((128,), F32), jnp.zeros((128,), F32), jnp.zeros((128,), F32), jnp.zeros((128,), F32)])
    h, qc, kc, vc, qd, kd, vd = _odd_proj_call(hn0, ys, slots, rt0, mod0, mod, tab, gains, win)

    aux = jnp.stack([
        _pad_lanes(c_lambda_q1[i], 128), _pad_lanes(c_lambda_k1[i], 128),
        _pad_lanes(c_lambda_q2[i], 128), _pad_lanes(c_lambda_k2[i], 128),
        c_subln[i], jnp.zeros((128,), F32), jnp.zeros((128,), F32), jnp.zeros((128,), F32)])
    oc = _attn_call(qc, kc, vc, n_heads_kv=C_HEADS, g=2, dk=LANES, dv=C_V, nkv=2,
                    tq=512, n_q_tiles=SEQ // 512, aux=aux, diff_lambda_init=lambda_init,
                    name="diff_latent", **lat)
    g_d = D_HEADS // D_KV_HEADS
    sink_col = jnp.repeat((d_sink[i] * LOG2_E).reshape(D_KV_HEADS, g_d), WIN_TQ, axis=1)
    sink_col = sink_col.reshape(D_KV_HEADS, g_d * WIN_TQ, 1)
    od = _window_call(qd, kd, vd, sink_col)

    w_out = odd_w_out[i].astype(BF16)
    rw, rb = _router_weights(moe_wr_group[1], moe_br_group[1], moe_wr_expert[1], moe_br_expert[1])
    n_lat_tiles = SEQ // ROW_TILE
    hn, fx, rt, cnt = _outproj_call((oc, od, h), None, mod, w_out[:C_HEADS * C_V], w_out[C_HEADS * C_V:],
                                    rw, rb, n_lat_tiles)
    ys, slots = _moe_experts(fx, rt, cnt, moe_w_gate, moe_w_up, moe_w_down, 1)
    return _combine_call(hn, ys, slots, rt, mod, n_lat_tiles)
```

```python
import functools
import math

import jax
import jax.numpy as jnp
from jax import lax
from jax.experimental import pallas as pl
from jax.experimental.pallas import tpu as pltpu

F32 = jnp.float32
BF16 = jnp.bfloat16

D_MODEL = 2048
BATCH = 8
SEQ = 2048
DEPTH = 2
GRID_W = 64
CTX_LEN = 256
TOK = SEQ + CTX_LEN
HEAD_DIM = 128
ROPE_THETA = 10000.0
NORM_EPS = 1e-6
A_HEADS = 8
A_KV_HEADS = 2
B_HEADS = 8
B_Q_LORA = 512
B_KV_LORA = 512
B_NOPE = 128
B_ROPE = 64
B_V = 128
B_QK_PAD = 256
C_HEADS = 8
C_HD = 64
C_V = 128
D_HEADS = 8
D_KV_HEADS = 2
WINDOW = 128
N_GROUPS = 4
EXPERTS_PER_GROUP = 8
N_EXPERTS = 32
D_EXPERT = 1024
MOE_BLOCK = 128
TOP_K = 2
LANES = 128
LOG2_E = math.log2(math.e)

ROW_TILE = 256
N_ROW_TILES = TOK // ROW_TILE
CTX_TILE = SEQ // ROW_TILE
MOD_ROWS = 16
MOD_CTX_ROW = BATCH
VMEM_LIMIT = 56 * 1024 * 1024


def _dot(a, b):
    return jnp.dot(a, b, preferred_element_type=F32)


def _dot_nt(a, b):
    return lax.dot_general(a, b, (((1,), (1,)), ((), ())), preferred_element_type=F32)


def _silu(x):
    return x / (1.0 + jnp.exp(-x))


def _rms(x, n):
    return x * lax.rsqrt(jnp.sum(x * x, axis=-1, keepdims=True) * (1.0 / n) + NORM_EPS)


def _modulate(x, shift, scale):
    return _rms(x, x.shape[-1]) * (1.0 + scale) + shift


def _rope(x, cos, sin_signed, quarter, lane):
    fwd = pltpu.roll(x, LANES - quarter, axis=1)
    bwd = pltpu.roll(x, quarter, axis=1)
    rot = jnp.where((lane // quarter) % 2 == 0, fwd, bwd)
    return x * cos + rot * sin_signed


PACKED_W = D_MODEL // 2
U32 = jnp.uint32


def _pack_bf16_pairs(x):
    n = x.shape[-1] // 2
    lo = lax.bitcast_convert_type(x[:, :n].astype(BF16).astype(F32), U32) >> 16
    hi = lax.bitcast_convert_type(x[:, n:].astype(BF16).astype(F32), U32) & U32(0xFFFF0000)
    return lo | hi


def _unpack_bf16_pairs(p):
    lo = lax.bitcast_convert_type(p << 16, F32)
    hi = lax.bitcast_convert_type(p & U32(0xFFFF0000), F32)
    return jnp.concatenate([lo, hi], axis=-1)


def _start_row_gathers(slot_ref, src_hbm, dst_ref, sem):
    for r in range(ROW_TILE):
        for k in range(TOP_K):
            pltpu.make_async_copy(src_hbm.at[pl.ds(slot_ref[0, k, r], 1)], dst_ref.at[k, pl.ds(r, 1)],
                                  sem).start(priority=k % 2)


def _wait_row_gathers(src_hbm, dst_ref, sem):
    for k in range(TOP_K):
        pltpu.make_async_copy(src_hbm.at[pl.ds(0, ROW_TILE)], dst_ref.at[k], sem).wait()


MOD_TN = 1024


def _mod_kernel(c_ref, w_ref, b_ref, o_ref):
    a = _silu(c_ref[...]).astype(BF16)
    o_ref[0] = _dot(a, w_ref[0].astype(BF16)) + b_ref[0]


def _mod_call(c_all, mod_w, mod_b):
    d6 = 6 * D_MODEL
    return pl.pallas_call(
        _mod_kernel,
        grid=(DEPTH, d6 // MOD_TN),
        in_specs=[
            pl.BlockSpec((MOD_ROWS, D_MODEL), lambda l, n: (0, 0)),
            pl.BlockSpec((1, D_MODEL, MOD_TN), lambda l, n: (l, 0, n)),
            pl.BlockSpec((1, 1, MOD_TN), lambda l, n: (l, 0, n)),
        ],
        out_specs=pl.BlockSpec((1, MOD_ROWS, MOD_TN), lambda l, n: (l, 0, n)),
        out_shape=jax.ShapeDtypeStruct((DEPTH, MOD_ROWS, d6), F32),
        compiler_params=pltpu.CompilerParams(
            dimension_semantics=("parallel", "parallel"), vmem_limit_bytes=VMEM_LIMIT),
        name="mod_vectors",
    )(c_all, mod_w, mod_b.reshape(DEPTH, 1, d6))


def _mod_row(b, j):
    return jnp.where(j == CTX_TILE, MOD_CTX_ROW, b)


def _resident(shape):
    nd = len(shape)
    return pl.BlockSpec(shape, lambda *_: (0,) * nd, pipeline_mode=pl.Buffered(1))


EVEN_IN_PAD = 2688


def _even_proj_kernel(hx_ref, hc_ref, mod_ref, tab_ref, g_ref, win_ref, wuq_ref, wukv_ref,
                      qa_ref, ka_ref, va_ref, qb_ref, kb_ref, vb_ref):
    d = D_MODEL
    x = jnp.where(pl.program_id(1) == CTX_TILE, hc_ref[0], hx_ref[0])
    a = _modulate(x, mod_ref[0, :, 0:d], mod_ref[0, :, d:2 * d]).astype(BF16)
    lane = lax.broadcasted_iota(jnp.int32, (ROW_TILE, LANES), 1)
    cos128, sin128 = tab_ref[:, 0:128], tab_ref[:, 128:256]
    cos64p, sin64p = tab_ref[:, 512:640], tab_ref[:, 640:768]

    z = _dot(a, win_ref[:, 0:1024])
    g_q = g_ref[0:1, 0:128]
    for h in range(A_HEADS):
        blk = _rms(z[:, h * 128:(h + 1) * 128], HEAD_DIM) * g_q
        qa_ref[0, :, h * 128:(h + 1) * 128] = _rope(blk, cos128, sin128, 32, lane).astype(BF16)
    z = _dot(a, win_ref[:, 1024:1536])
    g_k = g_ref[1:2, 0:128]
    for h in range(A_KV_HEADS):
        blk = _rms(z[:, h * 128:(h + 1) * 128], HEAD_DIM) * g_k
        ka_ref[0, :, h * 128:(h + 1) * 128] = _rope(blk, cos128, sin128, 32, lane).astype(BF16)
    va_ref[0] = z[:, 256:512].astype(BF16)

    z = _dot(a, win_ref[:, 1536:2048])
    cq = (_rms(z, B_Q_LORA) * g_ref[2:3, :]).astype(BF16)
    zq = _dot(cq, wuq_ref[...])
    gq0, gq1 = g_ref[4:5, 0:128], g_ref[4:5, 128:256]
    n_qk = float(B_NOPE + B_ROPE)
    for h in range(B_HEADS):
        b0 = zq[:, h * 256:h * 256 + 128]
        b1 = zq[:, h * 256 + 128:(h + 1) * 256]
        ss = jnp.sum(b0 * b0, axis=-1, keepdims=True) + jnp.sum(b1 * b1, axis=-1, keepdims=True)
        r = lax.rsqrt(ss * (1.0 / n_qk) + NORM_EPS)
        qb_ref[0, :, h * 256:h * 256 + 128] = (b0 * r * gq0).astype(BF16)
        qb_ref[0, :, h * 256 + 128:(h + 1) * 256] = _rope(b1 * r * gq1, cos64p, sin64p, 16, lane).astype(BF16)

    z = _dot(a, win_ref[:, 2048:2688])
    ckv = (_rms(z[:, 0:512], B_KV_LORA) * g_ref[3:4, :]).astype(BF16)
    kr = z[:, 512:640]
    ss_kr = jnp.sum(kr * kr, axis=-1, keepdims=True)
    kr_rot = _rope(kr * g_ref[5:6, 128:256], cos64p, sin64p, 16, lane)
    zkv = _dot(ckv, wukv_ref[...])
    vb_ref[0] = zkv[:, 1024:2048].astype(BF16)
    gk0 = g_ref[5:6, 0:128]
    for h in range(B_HEADS):
        kn = zkv[:, h * 128:(h + 1) * 128]
        ss = jnp.sum(kn * kn, axis=-1, keepdims=True) + ss_kr
        r = lax.rsqrt(ss * (1.0 / n_qk) + NORM_EPS)
        kb_ref[0, :, h * 256:h * 256 + 128] = (kn * r * gk0).astype(BF16)
        kb_ref[0, :, h * 256 + 128:(h + 1) * 256] = (kr_rot * r).astype(BF16)


def _lat_tile(w):
    return pl.BlockSpec((1, ROW_TILE, w), lambda b, j: (b, jnp.minimum(j, CTX_TILE - 1), 0))


def _ctx_tile(w):
    return pl.BlockSpec((1, ROW_TILE, w), lambda b, j: (b, 0, 0))


def _even_proj_call(h_lat, h_ctx, mod, tab, gains, win, wuq, wukv):
    bsz = h_lat.shape[0]
    widths = (A_HEADS * HEAD_DIM, A_KV_HEADS * HEAD_DIM, A_KV_HEADS * HEAD_DIM,
              B_HEADS * B_QK_PAD, B_HEADS * B_QK_PAD, B_HEADS * B_V)
    tile = lambda w: pl.BlockSpec((1, ROW_TILE, w), lambda b, j: (b, j, 0))
    return pl.pallas_call(
        _even_proj_kernel,
        grid=(bsz, N_ROW_TILES),
        in_specs=[
            _lat_tile(D_MODEL), _ctx_tile(D_MODEL),
            pl.BlockSpec((1, 1, 6 * D_MODEL), lambda b, j: (_mod_row(b, j), 0, 0)),
            pl.BlockSpec((ROW_TILE, 768), lambda b, j: (j, 0)),
            _resident(gains.shape), _resident(win.shape), _resident(wuq.shape), _resident(wukv.shape),
        ],
        out_specs=[tile(w) for w in widths],
        out_shape=[jax.ShapeDtypeStruct((bsz, TOK, w), BF16) for w in widths],
        compiler_params=pltpu.CompilerParams(
            dimension_semantics=("parallel", "parallel"), vmem_limit_bytes=VMEM_LIMIT),
        name="even_proj",
    )(h_lat, h_ctx, mod, tab, gains, win, wuq, wukv)


def _rms_halves(x, lane):
    x2 = x * x
    s_lo = jnp.sum(jnp.where(lane < 64, x2, 0.0), axis=-1, keepdims=True)
    s_hi = jnp.sum(jnp.where(lane < 64, 0.0, x2), axis=-1, keepdims=True)
    r = jnp.where(lane < 64, lax.rsqrt(s_lo * (1.0 / C_HD) + NORM_EPS), lax.rsqrt(s_hi * (1.0 / C_HD) + NORM_EPS))
    return x * r


def _odd_proj_kernel(slot_ref, slot_next_ref, hn_ref, rt_ref, mod_prev_ref, mod_ref, tab_ref, g_ref, win_ref,
                     ys_hbm, h_ref, qc_ref, kc_ref, vc_ref, qd_ref, kd_ref, vd_ref, ybuf, sems):
    d = D_MODEL
    n_steps = pl.num_programs(0) * pl.num_programs(1)
    step = pl.program_id(0) * pl.num_programs(1) + pl.program_id(1)
    cur = step % 2

    @pl.when(step == 0)
    def _():
        _start_row_gathers(slot_ref, ys_hbm, ybuf.at[0], sems.at[0])

    _wait_row_gathers(ys_hbm, ybuf.at[cur], sems.at[cur])
    rt = rt_ref[0]
    y = rt[:, 2:3] * _unpack_bf16_pairs(ybuf[cur, 0]) + rt[:, 3:4] * _unpack_bf16_pairs(ybuf[cur, 1])
    x = hn_ref[0] + mod_prev_ref[0, :, 5 * d:6 * d] * y
    h_ref[0] = x
    _start_row_gathers(slot_next_ref, ys_hbm, ybuf.at[1 - cur], sems.at[1 - cur])
    a = _modulate(x, mod_ref[0, :, 0:d], mod_ref[0, :, d:2 * d]).astype(BF16)
    lane = lax.broadcasted_iota(jnp.int32, (ROW_TILE, LANES), 1)
    cos128, sin128 = tab_ref[:, 0:128], tab_ref[:, 128:256]
    cos64, sin64 = tab_ref[:, 256:384], tab_ref[:, 384:512]

    z = _dot(a, win_ref[:, 0:1024])
    g_q = g_ref[0:1, :]
    for h in range(C_HEADS):
        blk = _rope(_rms_halves(z[:, h * 128:(h + 1) * 128], lane) * g_q, cos64, sin64, 16, lane)
        qc_ref[0, :, h * 256:h * 256 + 128] = jnp.where(lane < 64, blk, 0.0).astype(BF16)
        qc_ref[0, :, h * 256 + 128:(h + 1) * 256] = jnp.where(lane < 64, 0.0, blk).astype(BF16)
    z = _dot(a, win_ref[:, 1024:2048])
    g_k = g_ref[1:2, :]
    for h in range(C_HEADS):
        blk = _rope(_rms_halves(z[:, h * 128:(h + 1) * 128], lane) * g_k, cos64, sin64, 16, lane)
        kc_ref[0, :, h * 128:(h + 1) * 128] = blk.astype(BF16)
    vc_ref[0] = _dot(a, win_ref[:, 2048:3072]).astype(BF16)

    z = _dot(a, win_ref[:, 3072:4096])
    g_q = g_ref[2:3, :]
    for h in range(D_HEADS):
        blk = _rms(z[:, h * 128:(h + 1) * 128], HEAD_DIM) * g_q
        qd_ref[0, :, h * 128:(h + 1) * 128] = _rope(blk, cos128, sin128, 32, lane).astype(BF16)
    z = _dot(a, win_ref[:, 4096:4608])
    g_k = g_ref[3:4, :]
    for h in range(D_KV_HEADS):
        blk = _rms(z[:, h * 128:(h + 1) * 128], HEAD_DIM) * g_k
        kd_ref[0, :, h * 128:(h + 1) * 128] = _rope(blk, cos128, sin128, 32, lane).astype(BF16)
    vd_ref[0] = z[:, 256:512].astype(BF16)

    @pl.when(step == n_steps - 1)
    def _():
        _wait_row_gathers(ys_hbm, ybuf.at[1 - cur], sems.at[1 - cur])


def _odd_proj_call(hn, ys, slots, rt, mod_prev, mod, tab, gains, win):
    bsz = hn.shape[0]
    widths = (C_HEADS * 2 * LANES, C_HEADS * LANES, C_HEADS * C_V,
              D_HEADS * HEAD_DIM, D_KV_HEADS * HEAD_DIM, D_KV_HEADS * HEAD_DIM)
    n_steps = bsz * N_ROW_TILES
    tile = lambda w: pl.BlockSpec((1, ROW_TILE, w), lambda b, j: (b, j, 0))
    mod_spec = pl.BlockSpec((1, 1, 6 * D_MODEL), lambda b, j: (_mod_row(b, j), 0, 0))
    slot_spec = lambda ahead: pl.BlockSpec(
        (1, TOP_K, ROW_TILE), lambda b, j: (jnp.minimum(b * N_ROW_TILES + j + ahead, n_steps - 1), 0, 0),
        memory_space=pltpu.SMEM)
    return pl.pallas_call(
        _odd_proj_kernel,
        grid=(bsz, N_ROW_TILES),
        in_specs=[
            slot_spec(0), slot_spec(1), tile(D_MODEL), tile(LANES), mod_spec, mod_spec,
            pl.BlockSpec((ROW_TILE, 768), lambda b, j: (j, 0)),
            _resident(gains.shape), _resident(win.shape),
            pl.BlockSpec(memory_space=pl.ANY),
        ],
        out_specs=[tile(D_MODEL)] + [tile(w) for w in widths],
        out_shape=[jax.ShapeDtypeStruct((bsz, TOK, D_MODEL), F32)]
                  + [jax.ShapeDtypeStruct((bsz, TOK, w), BF16) for w in widths],
        scratch_shapes=[pltpu.VMEM((2, TOP_K, ROW_TILE, PACKED_W), U32), pltpu.SemaphoreType.DMA((2,))],
        compiler_params=pltpu.CompilerParams(
            dimension_semantics=("arbitrary", "arbitrary"), vmem_limit_bytes=VMEM_LIMIT),
        name="odd_proj",
    )(slots, slots, hn, rt, mod_prev, mod, tab, gains, win, ys)


KEY_CHUNK = 768


def _attn_kernel(*refs, nkv, g, dk, dv, tq, n_keys, diff_lambda_init):
    if diff_lambda_init is None:
        q_ref, k_ref, v_ref, o_ref = refs
    else:
        q_ref, k_ref, v_ref, aux_ref, o_ref = refs
    for kv in range(nkv):
        q = jnp.concatenate(
            [q_ref[0, :, (kv * g + gi) * dk:(kv * g + gi + 1) * dk] for gi in range(g)], axis=0)
        m = l = acc = None
        for c0 in range(0, n_keys, KEY_CHUNK):
            c1 = min(c0 + KEY_CHUNK, n_keys)
            s = _dot_nt(q, k_ref[0, c0:c1, kv * dk:(kv + 1) * dk])
            v = v_ref[0, c0:c1, kv * dv:(kv + 1) * dv]
            m_c = jnp.max(s, axis=-1, keepdims=True)
            if m is None:
                m = m_c
                p = jnp.exp2((s - m).astype(BF16))
                l = jnp.sum(p.astype(F32), axis=-1, keepdims=True)
                acc = _dot(p, v)
            else:
                m_new = jnp.maximum(m, m_c)
                alpha = jnp.exp2(m - m_new)
                p = jnp.exp2((s - m_new).astype(BF16))
                l = alpha * l + jnp.sum(p.astype(F32), axis=-1, keepdims=True)
                acc = alpha * acc + _dot(p, v)
                m = m_new
        o = acc / l
        if diff_lambda_init is None:
            for gi in range(g):
                o_ref[0, :, (kv * g + gi) * dv:(kv * g + gi + 1) * dv] = o[gi * tq:(gi + 1) * tq].astype(BF16)
        else:
            lam = (jnp.exp(jnp.sum(aux_ref[0:1, :] * aux_ref[1:2, :], axis=-1, keepdims=True))
                   - jnp.exp(jnp.sum(aux_ref[2:3, :] * aux_ref[3:4, :], axis=-1, keepdims=True))
                   + diff_lambda_init)
            od = o[0:tq] - lam * o[tq:2 * tq]
            od = _rms(od, dv) * aux_ref[4:5, :] * (1.0 - diff_lambda_init)
            o_ref[0, :, kv * dv:(kv + 1) * dv] = od.astype(BF16)


def _attn_call(q, k, v, *, n_heads_kv, g, dk, dv, nkv, tq, q_tile0, n_q_tiles, key_tile0, n_keys,
               aux=None, diff_lambda_init=None, name="attn"):
    bsz = q.shape[0]
    n_out_heads = n_heads_kv * (g if diff_lambda_init is None else 1)
    out_w = nkv * (g if diff_lambda_init is None else 1) * dv
    kern = functools.partial(_attn_kernel, nkv=nkv, g=g, dk=dk, dv=dv, tq=tq, n_keys=n_keys,
                             diff_lambda_init=diff_lambda_init)
    in_specs = [
        pl.BlockSpec((1, tq, nkv * g * dk), lambda b, hh, qi: (b, qi + q_tile0, hh)),
        pl.BlockSpec((1, n_keys, nkv * dk), lambda b, hh, qi: (b, key_tile0, hh)),
        pl.BlockSpec((1, n_keys, nkv * dv), lambda b, hh, qi: (b, key_tile0, hh)),
    ]
    args = [q, k, v]
    if aux is not None:
        in_specs.append(pl.BlockSpec(aux.shape, lambda b, hh, qi: (0, 0)))
        args.append(aux)
    return pl.pallas_call(
        kern,
        grid=(bsz, n_heads_kv // nkv, n_q_tiles),
        in_specs=in_specs,
        out_specs=pl.BlockSpec((1, tq, out_w), lambda b, hh, qi: (b, qi, hh)),
        out_shape=jax.ShapeDtypeStruct((bsz, n_q_tiles * tq, n_out_heads * dv), BF16),
        compiler_params=pltpu.CompilerParams(
            dimension_semantics=("parallel", "parallel", "parallel"), vmem_limit_bytes=VMEM_LIMIT),
        name=name,
    )(*args)


WIN_TQ = 256
WIN_BAND = WIN_TQ + 2 * WINDOW
NEG_BIG = -1e30


WIN_Q_TILES = SEQ // WIN_TQ


def _window_band(tile):
    q0 = (tile % WIN_Q_TILES) * WIN_TQ
    return q0, pl.multiple_of(jnp.clip(q0 - WINDOW, 0, SEQ - WIN_BAND), WINDOW)


def _window_kernel(q_ref, k_ref, v_ref, sink_ref, o_ref, sloc0, sctx0, sloc1, sctx1, *, n_total):
    g = D_HEADS // D_KV_HEADS
    dk = HEAD_DIM
    s = pl.program_id(0)

    @pl.when(s == 0)
    def _():
        sloc1[...] = jnp.zeros_like(sloc1)
        sctx1[...] = jnp.zeros_like(sctx1)

    def step(write, read):
        sloc_w, sctx_w = write
        sloc_r, sctx_r = read
        q0, start = _window_band(jnp.minimum(s, n_total - 1))
        q = jnp.concatenate([q_ref[0, :, gi * dk:(gi + 1) * dk] for gi in range(g)], axis=0)
        row = (lax.broadcasted_iota(jnp.int32, (g * WIN_TQ, WIN_BAND), 0) & (WIN_TQ - 1)) + q0
        col = lax.broadcasted_iota(jnp.int32, (g * WIN_TQ, WIN_BAND), 1) + start
        sloc_w[...] = jnp.where(jnp.abs(row - col) <= WINDOW,
                                _dot_nt(q, k_ref[0, pl.ds(start, WIN_BAND), :]), NEG_BIG)
        sctx_w[...] = _dot_nt(q, k_ref[0, SEQ:TOK, :])

        _, start = _window_band(jnp.maximum(s - 1, 0))
        s_loc = sloc_r[...]
        s_ctx = sctx_r[...]
        sink = sink_ref[0]
        m = jnp.maximum(jnp.maximum(jnp.max(s_loc, axis=-1, keepdims=True),
                                    jnp.max(s_ctx, axis=-1, keepdims=True)), sink)
        p_loc = jnp.exp2(s_loc - m)
        p_ctx = jnp.exp2(s_ctx - m)
        l = (jnp.sum(p_loc, axis=-1, keepdims=True) + jnp.sum(p_ctx, axis=-1, keepdims=True)
             + jnp.exp2(sink - m))
        acc = (_dot(p_ctx.astype(BF16), v_ref[0, SEQ:TOK, :])
               + _dot(p_loc.astype(BF16), v_ref[0, pl.ds(start, WIN_BAND), :]))
        o = acc / l
        for gi in range(g):
            o_ref[0, :, gi * dk:(gi + 1) * dk] = o[gi * WIN_TQ:(gi + 1) * WIN_TQ].astype(BF16)

    @pl.when(s % 2 == 0)
    def _():
        step((sloc0, sctx0), (sloc1, sctx1))

    @pl.when(s % 2 == 1)
    def _():
        step((sloc1, sctx1), (sloc0, sctx0))


def _window_call(q, k, v, sink_col):
    bsz = q.shape[0]
    g = D_HEADS // D_KV_HEADS
    n_total = bsz * D_KV_HEADS * WIN_Q_TILES
    split = lambda t: (t // (D_KV_HEADS * WIN_Q_TILES), (t // WIN_Q_TILES) % D_KV_HEADS, t % WIN_Q_TILES)
    stage_a = lambda s: split(jnp.minimum(s, n_total - 1))
    stage_b = lambda s: split(jnp.maximum(s - 1, 0))

    def q_index(s):
        b, hh, qi = stage_a(s)
        return (b, qi, hh)

    def k_index(s):
        b, hh, _ = stage_a(s)
        return (b, 0, hh)

    def v_index(s):
        b, hh, _ = stage_b(s)
        return (b, 0, hh)

    def o_index(s):
        b, hh, qi = stage_b(s)
        return (b, qi, hh)

    m_rows = g * WIN_TQ
    return pl.pallas_call(
        functools.partial(_window_kernel, n_total=n_total),
        grid=(n_total + 1,),
        in_specs=[
            pl.BlockSpec((1, WIN_TQ, g * HEAD_DIM), q_index),
            pl.BlockSpec((1, TOK, HEAD_DIM), k_index),
            pl.BlockSpec((1, TOK, HEAD_DIM), v_index),
            pl.BlockSpec((1, m_rows, 1), lambda s: (stage_b(s)[1], 0, 0)),
        ],
        out_specs=pl.BlockSpec((1, WIN_TQ, g * HEAD_DIM), o_index),
        out_shape=jax.ShapeDtypeStruct((bsz, SEQ, D_HEADS * HEAD_DIM), BF16),
        scratch_shapes=[pltpu.VMEM((m_rows, WIN_BAND), F32), pltpu.VMEM((m_rows, CTX_LEN), F32),
                        pltpu.VMEM((m_rows, WIN_BAND), F32), pltpu.VMEM((m_rows, CTX_LEN), F32)],
        compiler_params=pltpu.CompilerParams(
            dimension_semantics=("arbitrary",), vmem_limit_bytes=VMEM_LIMIT),
        name="window_attn",
    )(q, k, v, sink_col)


def _route(logits, lane):
    lane_f = lane.astype(F32)
    lg = jnp.where(lane < N_GROUPS, logits, NEG_BIG)
    g_max = jnp.max(lg, axis=-1, keepdims=True)
    p_top = 1.0 / jnp.sum(jnp.exp(lg - g_max), axis=-1, keepdims=True)
    g_idx = jnp.min(jnp.where(lg == g_max, lane_f, float(LANES)), axis=-1, keepdims=True)
    e_lane = lane - N_GROUPS
    in_group = (e_lane >= 0) & (e_lane < N_EXPERTS) & ((e_lane // EXPERTS_PER_GROUP).astype(F32) == g_idx)
    le = jnp.where(in_group, logits, NEG_BIG)
    m1 = jnp.max(le, axis=-1, keepdims=True)
    i1 = jnp.min(jnp.where(le == m1, lane_f, float(LANES)), axis=-1, keepdims=True)
    le2 = jnp.where(lane_f == i1, NEG_BIG, le)
    m2 = jnp.max(le2, axis=-1, keepdims=True)
    i2 = jnp.min(jnp.where(le2 == m2, lane_f, float(LANES)), axis=-1, keepdims=True)
    t = jnp.exp(m2 - m1)
    w1 = p_top / (1.0 + t)
    w2 = p_top * t / (1.0 + t)
    return i1, i2, w1, w2


def _outproj_kernel(*refs, has_ctx, n_tiles, n_total):
    if has_ctx:
        (o1_ref, o2_ref, h_ref, o1c_ref, o2c_ref, hc_ref, mod_ref, w1_ref, w2_ref, rw_ref, rb_ref,
         hn_ref, fx_ref, rt_ref, cnt_ref, run_ref, logit_ref) = refs
    else:
        (o1_ref, o2_ref, h_ref, mod_ref, w1_ref, w2_ref, rw_ref, rb_ref,
         hn_ref, fx_ref, rt_ref, cnt_ref, run_ref, logit_ref) = refs
    d = D_MODEL
    s = pl.program_id(0)

    @pl.when(s == 0)
    def _():
        run_ref[...] = jnp.zeros_like(run_ref)
        logit_ref[...] = jnp.zeros_like(logit_ref)

    logits = logit_ref[...]
    lane = lax.broadcasted_iota(jnp.int32, (ROW_TILE, LANES), 1)
    lane_f = lane.astype(F32)
    i1, i2, w1, w2 = _route(logits, lane)

    o1, o2, h = o1_ref[0], o2_ref[0], h_ref[0]
    if has_ctx:
        is_ctx = jnp.minimum(s, n_total - 1) % n_tiles == CTX_TILE
        o1 = jnp.where(is_ctx, o1c_ref[0], o1)
        o2 = jnp.where(is_ctx, o2c_ref[0], o2)
        h = jnp.where(is_ctx, hc_ref[0], h)
    acc = _dot(o1, w1_ref[...]) + _dot(o2, w2_ref[...])
    hn = h + mod_ref[0, :, 2 * d:3 * d] * acc
    hn_ref[0] = hn
    fx = _modulate(hn, mod_ref[0, :, 3 * d:4 * d], mod_ref[0, :, 4 * d:5 * d])
    fx_ref[0] = _pack_bf16_pairs(fx)
    hi = fx.astype(BF16)
    lo = (fx - hi.astype(F32)).astype(BF16)
    part = _dot(jnp.concatenate([hi, lo], axis=0), rw_ref[...])
    logit_ref[...] = (part[:ROW_TILE, :LANES] + part[:ROW_TILE, LANES:]
                      + part[ROW_TILE:, :LANES] + part[ROW_TILE:, LANES:] + rb_ref[...])

    hit1, hit2 = lane_f == i1, lane_f == i2
    multi_hot = jnp.where(s > 0, jnp.where(hit1, 1.0, 0.0) + jnp.where(hit2, 1.0, 0.0), 0.0)
    tri = (lax.broadcasted_iota(jnp.int32, (ROW_TILE, ROW_TILE), 0)
           > lax.broadcasted_iota(jnp.int32, (ROW_TILE, ROW_TILE), 1))
    before = _dot(jnp.where(tri, 1.0, 0.0).astype(BF16), multi_hot.astype(BF16)) + run_ref[...]
    r1 = jnp.sum(jnp.where(hit1, before, 0.0), axis=-1, keepdims=True)
    r2 = jnp.sum(jnp.where(hit2, before, 0.0), axis=-1, keepdims=True)
    run_ref[...] = run_ref[...] + jnp.sum(multi_hot, axis=0, keepdims=True)
    cnt_ref[...] = jnp.broadcast_to(run_ref[...], cnt_ref.shape)

    out = jnp.where(lane == 0, i1 - N_GROUPS, 0.0)
    out = jnp.where(lane == 1, i2 - N_GROUPS, out)
    out = jnp.where(lane == 2, w1, out)
    out = jnp.where(lane == 3, w2, out)
    out = jnp.where(lane == 4, r1, out)
    out = jnp.where(lane == 5, r2, out)
    rt_ref[0] = out


def _outproj_call(lat, ctx, mod, w1, w2, rw, rb, n_tiles):
    bsz = lat[2].shape[0]
    has_ctx = ctx is not None
    n_total = bsz * n_tiles
    stage_a = lambda s: jnp.minimum(s, n_total - 1)
    stage_b = lambda s: jnp.maximum(s - 1, 0)
    bj = lambda t: (t // n_tiles, t % n_tiles)

    def a_tile(w, kind):
        def index(s):
            b, j = bj(stage_a(s))
            return (b, {"all": j, "lat": jnp.minimum(j, CTX_TILE - 1), "ctx": 0}[kind], 0)
        return pl.BlockSpec((1, ROW_TILE, w), index)

    def b_tile(w):
        return pl.BlockSpec((1, ROW_TILE, w), lambda s: (*bj(stage_b(s)), 0))

    in_specs = [a_tile(a.shape[-1], "lat") for a in lat]
    args = list(lat)
    if has_ctx:
        in_specs += [a_tile(a.shape[-1], "ctx") for a in ctx]
        args += list(ctx)
    in_specs += [
        pl.BlockSpec((1, 1, 6 * D_MODEL), lambda s: (_mod_row(*bj(stage_a(s))), 0, 0)),
        _resident(w1.shape), _resident(w2.shape), _resident(rw.shape), _resident(rb.shape),
    ]
    args += [mod, w1, w2, rw, rb]
    n_rows = n_tiles * ROW_TILE
    return pl.pallas_call(
        functools.partial(_outproj_kernel, has_ctx=has_ctx, n_tiles=n_tiles, n_total=n_total),
        grid=(n_total + 1,),
        in_specs=in_specs,
        out_specs=[a_tile(D_MODEL, "all"), a_tile(PACKED_W, "all"), b_tile(LANES),
                   pl.BlockSpec((8, LANES), lambda s: (0, 0))],
        out_shape=[jax.ShapeDtypeStruct((bsz, n_rows, D_MODEL), F32),
                   jax.ShapeDtypeStruct((bsz, n_rows, PACKED_W), U32),
                   jax.ShapeDtypeStruct((bsz, n_rows, LANES), F32),
                   jax.ShapeDtypeStruct((8, LANES), F32)],
        scratch_shapes=[pltpu.VMEM((1, LANES), F32), pltpu.VMEM((ROW_TILE, LANES), F32)],
        compiler_params=pltpu.CompilerParams(
            dimension_semantics=("arbitrary",), vmem_limit_bytes=VMEM_LIMIT),
        name="outproj_router",
    )(*args)


def _dispatch_kernel(lo_ref, hi_ref, slot_ref, fx_ref, xs_hbm, zbuf, sem, zsem):
    b, j = pl.program_id(0), pl.program_id(1)

    for r in range(ROW_TILE):
        for k in range(TOP_K):
            pltpu.make_async_copy(fx_ref.at[0, pl.ds(r, 1)], xs_hbm.at[pl.ds(slot_ref[0, k, r], 1)],
                                  sem).start(priority=k % 2)

    @pl.when((b == pl.num_programs(0) - 1) & (j == pl.num_programs(1) - 1))
    def _():
        zbuf[...] = jnp.zeros_like(zbuf)

        def fill(s, carry):
            pltpu.make_async_copy(zbuf.at[pl.ds(0, 1)], xs_hbm.at[pl.ds(s, 1)], zsem).start()
            return carry

        def drain(s, carry):
            pltpu.make_async_copy(zbuf.at[pl.ds(0, 1)], xs_hbm.at[pl.ds(0, 1)], zsem).wait()
            return carry
        for e in range(N_EXPERTS):
            lax.fori_loop(lo_ref[e], hi_ref[e], fill, 0)
        for e in range(N_EXPERTS):
            lax.fori_loop(lo_ref[e], hi_ref[e], drain, 0)

    for k in range(TOP_K):
        pltpu.make_async_copy(fx_ref.at[0], xs_hbm.at[pl.ds(0, ROW_TILE)], sem).wait()


def _dispatch_call(fx, slots, fill_lo, fill_hi, n_blocks):
    bsz, n_rows, _ = fx.shape
    n_tiles = n_rows // ROW_TILE
    grid_spec = pltpu.PrefetchScalarGridSpec(
        num_scalar_prefetch=2,
        grid=(bsz, n_tiles),
        in_specs=[
            pl.BlockSpec((1, TOP_K, ROW_TILE), lambda b, j, lo, hi: (b * n_tiles + j, 0, 0),
                         memory_space=pltpu.SMEM),
            pl.BlockSpec((1, ROW_TILE, PACKED_W), lambda b, j, lo, hi: (b, j, 0)),
        ],
        out_specs=pl.BlockSpec(memory_space=pl.ANY),
        scratch_shapes=[pltpu.VMEM((8, PACKED_W), U32), pltpu.SemaphoreType.DMA(()), pltpu.SemaphoreType.DMA(())],
    )
    return pl.pallas_call(
        _dispatch_kernel,
        grid_spec=grid_spec,
        out_shape=jax.ShapeDtypeStruct((n_blocks * MOE_BLOCK, PACKED_W), U32),
        compiler_params=pltpu.CompilerParams(
            dimension_semantics=("arbitrary", "arbitrary"), vmem_limit_bytes=VMEM_LIMIT),
        name="moe_dispatch",
    )(fill_lo, fill_hi, slots, fx)


CAST_ROWS = 256


def _moe_kernel(be_ref, nu_ref, nx_ref, x_ref, wg_hbm, wu_hbm, wd_hbm, y_ref,
                sg, su, sd, wg, wu, wd, sems, *, layer):
    i = pl.program_id(0)
    e = be_ref[i]
    used = i < nu_ref[0]
    staged = ((wg_hbm, sg, wg, 0), (wu_hbm, su, wu, 1), (wd_hbm, sd, wd, 2))

    def fetch(expert):
        for hbm, stage, _, s in staged:
            pltpu.make_async_copy(hbm.at[layer, expert], stage, sems.at[s]).start(priority=1)

    @pl.when(i == 0)
    def _():
        fetch(e)

    first_of_expert = (i == 0) | (e != be_ref[jnp.maximum(i - 1, 0)])

    @pl.when(used & first_of_expert)
    def _():
        for hbm, stage, dst, s in staged:
            pltpu.make_async_copy(hbm.at[layer, 0], stage, sems.at[s]).wait()

            def cast(c, carry):
                rows = pl.ds(pl.multiple_of(c * CAST_ROWS, CAST_ROWS), CAST_ROWS)
                dst[rows, :] = stage[rows, :].astype(BF16)
                return carry
            lax.fori_loop(0, stage.shape[0] // CAST_ROWS, cast, 0)
        nxt = nx_ref[e]

        @pl.when(nxt >= 0)
        def _():
            fetch(nxt)

    @pl.when(used)
    def _():
        x = _unpack_bf16_pairs(x_ref[...]).astype(BF16)
        gt = _dot(x, wg[...])
        up = _dot(x, wu[...])
        u = (_silu(gt) * up).astype(BF16)
        y_ref[...] = _pack_bf16_pairs(_dot(u, wd[...]))


def _moe_call(xs, block_expert, n_used, next_expert, wg, wu, wd, layer):
    n_blocks = block_expert.shape[0]
    row_blk = pl.BlockSpec((MOE_BLOCK, PACKED_W), lambda i, be, nu, nx: (jnp.minimum(i, nu[0] - 1), 0))
    hbm = pl.BlockSpec(memory_space=pl.ANY)
    grid_spec = pltpu.PrefetchScalarGridSpec(
        num_scalar_prefetch=3,
        grid=(n_blocks,),
        in_specs=[row_blk, hbm, hbm, hbm],
        out_specs=row_blk,
        scratch_shapes=[
            pltpu.VMEM((D_MODEL, D_EXPERT), F32), pltpu.VMEM((D_MODEL, D_EXPERT), F32),
            pltpu.VMEM((D_EXPERT, D_MODEL), F32),
            pltpu.VMEM((D_MODEL, D_EXPERT), BF16), pltpu.VMEM((D_MODEL, D_EXPERT), BF16),
            pltpu.VMEM((D_EXPERT, D_MODEL), BF16),
            pltpu.SemaphoreType.DMA((3,)),
        ],
    )
    return pl.pallas_call(
        functools.partial(_moe_kernel, layer=layer),
        grid_spec=grid_spec,
        out_shape=jax.ShapeDtypeStruct((n_blocks * MOE_BLOCK, PACKED_W), U32),
        compiler_params=pltpu.CompilerParams(
            dimension_semantics=("arbitrary",), vmem_limit_bytes=VMEM_LIMIT),
        name="moe_experts",
    )(block_expert, n_used, next_expert, xs, wg, wu, wd)


def _combine_kernel(slot_ref, h_ref, rt_ref, mod_ref, ys_hbm, o_ref, ybuf, sem):
    d = D_MODEL
    _start_row_gathers(slot_ref, ys_hbm, ybuf, sem)
    _wait_row_gathers(ys_hbm, ybuf, sem)
    rt = rt_ref[0]
    y = rt[:, 2:3] * _unpack_bf16_pairs(ybuf[0]) + rt[:, 3:4] * _unpack_bf16_pairs(ybuf[1])
    o_ref[0] = h_ref[0] + mod_ref[0, :, 5 * d:6 * d] * y


def _combine_call(h, ys, slots, rt, mod, n_tiles):
    bsz = h.shape[0]
    tile = lambda w: pl.BlockSpec((1, ROW_TILE, w), lambda b, j: (b, j, 0))
    return pl.pallas_call(
        _combine_kernel,
        grid=(bsz, n_tiles),
        in_specs=[
            pl.BlockSpec((1, TOP_K, ROW_TILE), lambda b, j: (b * n_tiles + j, 0, 0), memory_space=pltpu.SMEM),
            tile(D_MODEL), tile(LANES),
            pl.BlockSpec((1, 1, 6 * D_MODEL), lambda b, j: (_mod_row(b, j), 0, 0)),
            pl.BlockSpec(memory_space=pl.ANY),
        ],
        out_specs=tile(D_MODEL),
        out_shape=jax.ShapeDtypeStruct((bsz, n_tiles * ROW_TILE, D_MODEL), F32),
        scratch_shapes=[pltpu.VMEM((TOP_K, ROW_TILE, PACKED_W), U32), pltpu.SemaphoreType.DMA(())],
        compiler_params=pltpu.CompilerParams(
            dimension_semantics=("arbitrary", "arbitrary"), vmem_limit_bytes=VMEM_LIMIT),
        name="moe_combine",
    )(slots, h, rt, mod, ys)


def _rope_tables():
    rows = SEQ // GRID_W
    row = jnp.repeat(jnp.arange(rows), GRID_W).astype(F32)
    col = jnp.tile(jnp.arange(GRID_W), rows).astype(F32)

    def tables(dim):
        n_freq = dim // 4
        inv = 1.0 / (ROPE_THETA ** (jnp.arange(n_freq, dtype=F32) / n_freq))
        ang_r = row[:, None] * inv
        ang_c = col[:, None] * inv
        ang = jnp.concatenate([ang_r, ang_r, ang_c, ang_c], axis=-1)
        sign = jnp.tile(jnp.concatenate([-jnp.ones((n_freq,), F32), jnp.ones((n_freq,), F32)]), 2)
        return jnp.cos(ang), jnp.sin(ang) * sign

    cos128, sin128 = tables(HEAD_DIM)
    cos64, sin64 = tables(B_ROPE)
    ones, zeros = jnp.ones((SEQ, 64), F32), jnp.zeros((SEQ, 64), F32)
    lat = jnp.concatenate([cos128, sin128,
                           cos64, cos64, sin64, sin64,
                           cos64, ones, sin64, zeros], axis=-1)
    ident = jnp.concatenate([jnp.ones((CTX_LEN, 128), F32), jnp.zeros((CTX_LEN, 128), F32)], axis=-1)
    return jnp.concatenate([lat, jnp.tile(ident, (1, 3))], axis=0)


def _pad_lanes(v, n):
    return jnp.pad(v, (0, n - v.shape[0]))


def _router_weights(wr_g, br_g, wr_e, br_e):
    w = jnp.pad(jnp.concatenate([wr_g, wr_e], axis=1), ((0, 0), (0, LANES - N_GROUPS - N_EXPERTS)))
    hi = w.astype(BF16)
    lo = (w - hi.astype(F32)).astype(BF16)
    b = _pad_lanes(jnp.concatenate([br_g, br_e]), LANES).reshape(1, LANES)
    return jnp.concatenate([hi, lo], axis=1), b


def _moe_experts(fx, rt, cnt, wg, wu, wd, layer):
    bsz, n_rows, _ = fx.shape
    n_tiles = n_rows // ROW_TILE
    n_blocks = -(-(bsz * n_rows * TOP_K) // MOE_BLOCK) + N_EXPERTS
    experts = jnp.arange(N_EXPERTS, dtype=jnp.int32)
    counts = cnt[0, N_GROUPS:N_GROUPS + N_EXPERTS].astype(jnp.int32)
    padded = (counts + MOE_BLOCK - 1) // MOE_BLOCK * MOE_BLOCK
    pad_end = jnp.cumsum(padded)
    pad_start = pad_end - padded
    e_idx = rt[..., 0:TOP_K].astype(jnp.int32)
    rank = rt[..., 4:4 + TOP_K].astype(jnp.int32)
    slot = rank + jnp.sum(jnp.where(e_idx[..., None] == experts, pad_start, 0), axis=-1)
    slots = slot.reshape(bsz * n_tiles, ROW_TILE, TOP_K).transpose(0, 2, 1)
    block_start = jnp.arange(n_blocks, dtype=jnp.int32) * MOE_BLOCK
    block_expert = jnp.minimum(jnp.sum(pad_end[None, :] <= block_start[:, None], axis=1), N_EXPERTS - 1)
    n_used = pad_end[-1:] // MOE_BLOCK
    later = (experts[None, :] > experts[:, None]) & (counts[None, :] > 0)
    next_expert = jnp.where(jnp.any(later, axis=1), jnp.argmax(later, axis=1), -1)
    i32 = lambda a: a.astype(jnp.int32)
    xs = _dispatch_call(fx, i32(slots), i32(pad_start + counts), i32(pad_end), n_blocks)
    ys = _moe_call(xs, i32(block_expert), i32(n_used), i32(next_expert), wg, wu, wd, layer)
    return ys, i32(slots)


def kernel(x, c, ctx, c_ctx, mod_w, mod_b, even_w_in, even_w_out, a_q_norm, a_k_norm, b_cq_norm, b_w_uq,
           b_ckv_norm, b_w_ukv, b_q_norm, b_k_norm, odd_w_in, odd_w_out, c_q_norm, c_k_norm, c_lambda_q1,
           c_lambda_k1, c_lambda_q2, c_lambda_k2, c_subln, d_q_norm, d_k_norm, d_sink, moe_wr_group,
           moe_br_group, moe_wr_expert, moe_br_expert, moe_w_gate, moe_w_up, moe_w_down):
    bsz = x.shape[0]
    c_all = jnp.zeros((MOD_ROWS, D_MODEL), F32).at[:bsz].set(c).at[MOD_CTX_ROW].set(c_ctx)
    mod_all = _mod_call(c_all, mod_w, mod_b)
    tab = _rope_tables()

    i = 0
    mod = mod_all[0].reshape(MOD_ROWS, 1, 6 * D_MODEL)
    scale_a = HEAD_DIM ** -0.5 * LOG2_E
    scale_b = (B_NOPE + B_ROPE) ** -0.5 * LOG2_E
    win = jnp.pad(even_w_in[i], ((0, 0), (0, EVEN_IN_PAD - even_w_in.shape[-1]))).astype(BF16)
    wuq = b_w_uq[i].reshape(B_Q_LORA, B_HEADS, B_NOPE + B_ROPE)
    wuq = jnp.pad(wuq, ((0, 0), (0, 0), (0, B_QK_PAD - B_NOPE - B_ROPE))).reshape(B_Q_LORA, -1).astype(BF16)
    wukv = b_w_ukv[i].reshape(B_KV_LORA, B_HEADS, B_NOPE + B_V)
    wukv = jnp.concatenate([wukv[:, :, :B_NOPE].reshape(B_KV_LORA, -1),
                            wukv[:, :, B_NOPE:].reshape(B_KV_LORA, -1)], axis=1).astype(BF16)
    gains = jnp.stack([
        _pad_lanes(a_q_norm[i] * scale_a, 512), _pad_lanes(a_k_norm[i], 512),
        b_cq_norm[i], b_ckv_norm[i],
        _pad_lanes(b_q_norm[i] * scale_b, 512), _pad_lanes(b_k_norm[i], 512),
        jnp.zeros((512,), F32), jnp.zeros((512,), F32)])
    qa, ka, va, qb, kb, vb = _even_proj_call(x, ctx, mod, tab, gains, win, wuq, wukv)

    g_a = A_HEADS // A_KV_HEADS
    lat = dict(q_tile0=0, key_tile0=0, n_keys=TOK)
    oa = _attn_call(qa, ka, va, n_heads_kv=A_KV_HEADS, g=g_a, dk=HEAD_DIM, dv=HEAD_DIM, nkv=2,
                    tq=256, n_q_tiles=SEQ // 256, name="gqa_latent", **lat)
    ob = _attn_call(qb, kb, vb, n_heads_kv=B_HEADS, g=1, dk=B_QK_PAD, dv=B_V, nkv=2,
                    tq=512, n_q_tiles=SEQ // 512, name="mla_latent", **lat)
    cx = dict(tq=CTX_LEN, q_tile0=SEQ // CTX_LEN, n_q_tiles=1, key_tile0=SEQ // CTX_LEN, n_keys=CTX_LEN)
    oa_c = _attn_call(qa, ka, va, n_heads_kv=A_KV_HEADS, g=g_a, dk=HEAD_DIM, dv=HEAD_DIM, nkv=2,
                      name="gqa_context", **cx)
    ob_c = _attn_call(qb, kb, vb, n_heads_kv=B_HEADS, g=1, dk=B_QK_PAD, dv=B_V, nkv=2,
                      name="mla_context", **cx)

    w_out = even_w_out[i].astype(BF16)
    rw, rb = _router_weights(moe_wr_group[0], moe_br_group[0], moe_wr_expert[0], moe_br_expert[0])
    hn, fx, rt, cnt = _outproj_call((oa, ob, x), (oa_c, ob_c, ctx), mod,
                                    w_out[:A_HEADS * HEAD_DIM], w_out[A_HEADS * HEAD_DIM:],
                                    rw, rb, N_ROW_TILES)
    ys, slots = _moe_experts(fx, rt, cnt, moe_w_gate, moe_w_up, moe_w_down, 0)
    hn0, rt0, mod0 = hn, rt, mod

    layer = 1
    mod = mod_all[1].reshape(MOD_ROWS, 1, 6 * D_MODEL)
    lambda_init = 0.8 - 0.6 * math.exp(-0.3 * layer)
    scale_c = C_HD ** -0.5 * LOG2_E
    scale_d = HEAD_DIM ** -0.5 * LOG2_E
    win = odd_w_in[i].astype(BF16)
    gains = jnp.stack([
        jnp.tile(c_q_norm[i] * scale_c, 2), jnp.tile(c_k_norm[i], 2),
        d_q_norm[i] * scale_d, d_k_norm[i],
        jnp.zeros((128,), F32), jnp.zeros((128,), F32), jnp.zeros((128,), F32), jnp.zeros((128,), F32)])
    h, qc, kc, vc, qd, kd, vd = _odd_proj_call(hn0, ys, slots, rt0, mod0, mod, tab, gains, win)

    aux = jnp.stack([
        _pad_lanes(c_lambda_q1[i], 128), _pad_lanes(c_lambda_k1[i], 128),
        _pad_lanes(c_lambda_q2[i], 128), _pad_lanes(c_lambda_k2[i], 128),
        c_subln[i], jnp.zeros((128,), F32), jnp.zeros((128,), F32), jnp.zeros((128,), F32)])
    oc = _attn_call(qc, kc, vc, n_heads_kv=C_HEADS, g=2, dk=LANES, dv=C_V, nkv=2,
                    tq=512, n_q_tiles=SEQ // 512, aux=aux, diff_lambda_init=lambda_init,
                    name="diff_latent", **lat)
    g_d = D_HEADS // D_KV_HEADS
    sink_col = jnp.repeat((d_sink[i] * LOG2_E).reshape(D_KV_HEADS, g_d), WIN_TQ, axis=1)
    sink_col = sink_col.reshape(D_KV_HEADS, g_d * WIN_TQ, 1)
    od = _window_call(qd, kd, vd, sink_col)

    w_out = odd_w_out[i].astype(BF16)
    rw, rb = _router_weights(moe_wr_group[1], moe_br_group[1], moe_wr_expert[1], moe_br_expert[1])
    n_lat_tiles = SEQ // ROW_TILE
    hn, fx, rt, cnt = _outproj_call((oc, od, h), None, mod, w_out[:C_HEADS * C_V], w_out[C_HEADS * C_V:],
                                    rw, rb, n_lat_tiles)
    ys, slots = _moe_experts(fx, rt, cnt, moe_w_gate, moe_w_up, moe_w_down, 1)
    return _combine_call(hn, ys, slots, rt, mod, n_lat_tiles)
```

```python
import functools
import math

import jax
import jax.numpy as jnp
from jax import lax
from jax.experimental import pallas as pl
from jax.experimental.pallas import tpu as pltpu

F32 = jnp.float32
BF16 = jnp.bfloat16

D_MODEL = 2048
BATCH = 8
SEQ = 2048
DEPTH = 2
GRID_W = 64
CTX_LEN = 256
TOK = SEQ + CTX_LEN
HEAD_DIM = 128
ROPE_THETA = 10000.0
NORM_EPS = 1e-6
A_HEADS = 8
A_KV_HEADS = 2
B_HEADS = 8
B_Q_LORA = 512
B_KV_LORA = 512
B_NOPE = 128
B_ROPE = 64
B_V = 128
B_QK_PAD = 256
C_HEADS = 8
C_HD = 64
C_V = 128
D_HEADS = 8
D_KV_HEADS = 2
WINDOW = 128
N_GROUPS = 4
EXPERTS_PER_GROUP = 8
N_EXPERTS = 32
D_EXPERT = 1024
MOE_BLOCK = 256
TOP_K = 2
LANES = 128
LOG2_E = math.log2(math.e)

ROW_TILE = 256
N_ROW_TILES = TOK // ROW_TILE
CTX_TILE = SEQ // ROW_TILE
MOD_ROWS = 16
MOD_CTX_ROW = BATCH
VMEM_LIMIT = 56 * 1024 * 1024


def _dot(a, b):
    return jnp.dot(a, b, preferred_element_type=F32)


def _dot_nt(a, b):
    return lax.dot_general(a, b, (((1,), (1,)), ((), ())), preferred_element_type=F32)


def _silu(x):
    return x / (1.0 + jnp.exp(-x))


def _rms(x, n):
    return x * lax.rsqrt(jnp.sum(x * x, axis=-1, keepdims=True) * (1.0 / n) + NORM_EPS)


def _modulate(x, shift, scale):
    return _rms(x, x.shape[-1]) * (1.0 + scale) + shift


def _rope(x, cos, sin_signed, quarter, lane):
    fwd = pltpu.roll(x, LANES - quarter, axis=1)
    bwd = pltpu.roll(x, quarter, axis=1)
    rot = jnp.where((lane // quarter) % 2 == 0, fwd, bwd)
    return x * cos + rot * sin_signed


PACKED_W = D_MODEL // 2
U32 = jnp.uint32


def _pack_bf16_pairs(x):
    n = x.shape[-1] // 2
    lo = lax.bitcast_convert_type(x[:, :n].astype(BF16).astype(F32), U32) >> 16
    hi = lax.bitcast_convert_type(x[:, n:].astype(BF16).astype(F32), U32) & U32(0xFFFF0000)
    return lo | hi


def _unpack_bf16_pairs(p):
    lo = lax.bitcast_convert_type(p << 16, F32)
    hi = lax.bitcast_convert_type(p & U32(0xFFFF0000), F32)
    return jnp.concatenate([lo, hi], axis=-1)


def _start_row_gathers(slot_ref, src_hbm, dst_ref, sem):
    for r in range(ROW_TILE):
        for k in range(TOP_K):
            pltpu.make_async_copy(src_hbm.at[pl.ds(slot_ref[0, k, r], 1)], dst_ref.at[k, pl.ds(r, 1)],
                                  sem).start(priority=k % 2)


def _wait_row_gathers(src_hbm, dst_ref, sem):
    for k in range(TOP_K):
        pltpu.make_async_copy(src_hbm.at[pl.ds(0, ROW_TILE)], dst_ref.at[k], sem).wait()


MOD_TN = 1024


def _mod_kernel(c_ref, w_ref, b_ref, o_ref):
    a = _silu(c_ref[...]).astype(BF16)
    o_ref[0] = _dot(a, w_ref[0].astype(BF16)) + b_ref[0]


def _mod_call(c_all, mod_w, mod_b):
    d6 = 6 * D_MODEL
    return pl.pallas_call(
        _mod_kernel,
        grid=(DEPTH, d6 // MOD_TN),
        in_specs=[
            pl.BlockSpec((MOD_ROWS, D_MODEL), lambda l, n: (0, 0)),
            pl.BlockSpec((1, D_MODEL, MOD_TN), lambda l, n: (l, 0, n)),
            pl.BlockSpec((1, 1, MOD_TN), lambda l, n: (l, 0, n)),
        ],
        out_specs=pl.BlockSpec((1, MOD_ROWS, MOD_TN), lambda l, n: (l, 0, n)),
        out_shape=jax.ShapeDtypeStruct((DEPTH, MOD_ROWS, d6), F32),
        compiler_params=pltpu.CompilerParams(
            dimension_semantics=("parallel", "parallel"), vmem_limit_bytes=VMEM_LIMIT),
        name="mod_vectors",
    )(c_all, mod_w, mod_b.reshape(DEPTH, 1, d6))


def _mod_row(b, j):
    return jnp.where(j == CTX_TILE, MOD_CTX_ROW, b)


def _resident(shape):
    nd = len(shape)
    return pl.BlockSpec(shape, lambda *_: (0,) * nd, pipeline_mode=pl.Buffered(1))


EVEN_IN_PAD = 2688


def _even_proj_kernel(hx_ref, hc_ref, mod_ref, tab_ref, g_ref, win_ref, wuq_ref, wukv_ref,
                      qa_ref, ka_ref, va_ref, qb_ref, kb_ref, vb_ref):
    d = D_MODEL
    x = jnp.where(pl.program_id(1) == CTX_TILE, hc_ref[0], hx_ref[0])
    a = _modulate(x, mod_ref[0, :, 0:d], mod_ref[0, :, d:2 * d]).astype(BF16)
    lane = lax.broadcasted_iota(jnp.int32, (ROW_TILE, LANES), 1)
    cos128, sin128 = tab_ref[:, 0:128], tab_ref[:, 128:256]
    cos64p, sin64p = tab_ref[:, 512:640], tab_ref[:, 640:768]

    z = _dot(a, win_ref[:, 0:1024])
    g_q = g_ref[0:1, 0:128]
    for h in range(A_HEADS):
        blk = _rms(z[:, h * 128:(h + 1) * 128], HEAD_DIM) * g_q
        qa_ref[0, :, h * 128:(h + 1) * 128] = _rope(blk, cos128, sin128, 32, lane).astype(BF16)
    z = _dot(a, win_ref[:, 1024:1536])
    g_k = g_ref[1:2, 0:128]
    for h in range(A_KV_HEADS):
        blk = _rms(z[:, h * 128:(h + 1) * 128], HEAD_DIM) * g_k
        ka_ref[0, :, h * 128:(h + 1) * 128] = _rope(blk, cos128, sin128, 32, lane).astype(BF16)
    va_ref[0] = z[:, 256:512].astype(BF16)

    z = _dot(a, win_ref[:, 1536:2048])
    cq = (_rms(z, B_Q_LORA) * g_ref[2:3, :]).astype(BF16)
    zq = _dot(cq, wuq_ref[...])
    gq0, gq1 = g_ref[4:5, 0:128], g_ref[4:5, 128:256]
    n_qk = float(B_NOPE + B_ROPE)
    for h in range(B_HEADS):
        b0 = zq[:, h * 256:h * 256 + 128]
        b1 = zq[:, h * 256 + 128:(h + 1) * 256]
        ss = jnp.sum(b0 * b0, axis=-1, keepdims=True) + jnp.sum(b1 * b1, axis=-1, keepdims=True)
        r = lax.rsqrt(ss * (1.0 / n_qk) + NORM_EPS)
        qb_ref[0, :, h * 256:h * 256 + 128] = (b0 * r * gq0).astype(BF16)
        qb_ref[0, :, h * 256 + 128:(h + 1) * 256] = _rope(b1 * r * gq1, cos64p, sin64p, 16, lane).astype(BF16)

    z = _dot(a, win_ref[:, 2048:2688])
    ckv = (_rms(z[:, 0:512], B_KV_LORA) * g_ref[3:4, :]).astype(BF16)
    kr = z[:, 512:640]
    ss_kr = jnp.sum(kr * kr, axis=-1, keepdims=True)
    kr_rot = _rope(kr * g_ref[5:6, 128:256], cos64p, sin64p, 16, lane)
    zkv = _dot(ckv, wukv_ref[...])
    vb_ref[0] = zkv[:, 1024:2048].astype(BF16)
    gk0 = g_ref[5:6, 0:128]
    for h in range(B_HEADS):
        kn = zkv[:, h * 128:(h + 1) * 128]
        ss = jnp.sum(kn * kn, axis=-1, keepdims=True) + ss_kr
        r = lax.rsqrt(ss * (1.0 / n_qk) + NORM_EPS)
        kb_ref[0, :, h * 256:h * 256 + 128] = (kn * r * gk0).astype(BF16)
        kb_ref[0, :, h * 256 + 128:(h + 1) * 256] = (kr_rot * r).astype(BF16)


def _lat_tile(w):
    return pl.BlockSpec((1, ROW_TILE, w), lambda b, j: (b, jnp.minimum(j, CTX_TILE - 1), 0))


def _ctx_tile(w):
    return pl.BlockSpec((1, ROW_TILE, w), lambda b, j: (b, 0, 0))


def _even_proj_call(h_lat, h_ctx, mod, tab, gains, win, wuq, wukv):
    bsz = h_lat.shape[0]
    widths = (A_HEADS * HEAD_DIM, A_KV_HEADS * HEAD_DIM, A_KV_HEADS * HEAD_DIM,
              B_HEADS * B_QK_PAD, B_HEADS * B_QK_PAD, B_HEADS * B_V)
    tile = lambda w: pl.BlockSpec((1, ROW_TILE, w), lambda b, j: (b, j, 0))
    return pl.pallas_call(
        _even_proj_kernel,
        grid=(bsz, N_ROW_TILES),
        in_specs=[
            _lat_tile(D_MODEL), _ctx_tile(D_MODEL),
            pl.BlockSpec((1, 1, 6 * D_MODEL), lambda b, j: (_mod_row(b, j), 0, 0)),
            pl.BlockSpec((ROW_TILE, 768), lambda b, j: (j, 0)),
            _resident(gains.shape), _resident(win.shape), _resident(wuq.shape), _resident(wukv.shape),
        ],
        out_specs=[tile(w) for w in widths],
        out_shape=[jax.ShapeDtypeStruct((bsz, TOK, w), BF16) for w in widths],
        compiler_params=pltpu.CompilerParams(
            dimension_semantics=("parallel", "parallel"), vmem_limit_bytes=VMEM_LIMIT),
        name="even_proj",
    )(h_lat, h_ctx, mod, tab, gains, win, wuq, wukv)


def _rms_halves(x, lane):
    x2 = x * x
    s_lo = jnp.sum(jnp.where(lane < 64, x2, 0.0), axis=-1, keepdims=True)
    s_hi = jnp.sum(jnp.where(lane < 64, 0.0, x2), axis=-1, keepdims=True)
    r = jnp.where(lane < 64, lax.rsqrt(s_lo * (1.0 / C_HD) + NORM_EPS), lax.rsqrt(s_hi * (1.0 / C_HD) + NORM_EPS))
    return x * r


def _odd_proj_kernel(slot_ref, slot_next_ref, hn_ref, rt_ref, mod_prev_ref, mod_ref, tab_ref, g_ref, win_ref,
                     ys_hbm, h_ref, qc_ref, kc_ref, vc_ref, qd_ref, kd_ref, vd_ref, ybuf, sems):
    d = D_MODEL
    n_steps = pl.num_programs(0) * pl.num_programs(1)
    step = pl.program_id(0) * pl.num_programs(1) + pl.program_id(1)
    cur = step % 2

    @pl.when(step == 0)
    def _():
        _start_row_gathers(slot_ref, ys_hbm, ybuf.at[0], sems.at[0])

    _wait_row_gathers(ys_hbm, ybuf.at[cur], sems.at[cur])
    rt = rt_ref[0]
    y = rt[:, 2:3] * _unpack_bf16_pairs(ybuf[cur, 0]) + rt[:, 3:4] * _unpack_bf16_pairs(ybuf[cur, 1])
    x = hn_ref[0] + mod_prev_ref[0, :, 5 * d:6 * d] * y
    h_ref[0] = x
    _start_row_gathers(slot_next_ref, ys_hbm, ybuf.at[1 - cur], sems.at[1 - cur])
    a = _modulate(x, mod_ref[0, :, 0:d], mod_ref[0, :, d:2 * d]).astype(BF16)
    lane = lax.broadcasted_iota(jnp.int32, (ROW_TILE, LANES), 1)
    cos128, sin128 = tab_ref[:, 0:128], tab_ref[:, 128:256]
    cos64, sin64 = tab_ref[:, 256:384], tab_ref[:, 384:512]

    z = _dot(a, win_ref[:, 0:1024])
    g_q = g_ref[0:1, :]
    for h in range(C_HEADS):
        blk = _rope(_rms_halves(z[:, h * 128:(h + 1) * 128], lane) * g_q, cos64, sin64, 16, lane)
        qc_ref[0, :, h * 256:h * 256 + 128] = jnp.where(lane < 64, blk, 0.0).astype(BF16)
        qc_ref[0, :, h * 256 + 128:(h + 1) * 256] = jnp.where(lane < 64, 0.0, blk).astype(BF16)
    z = _dot(a, win_ref[:, 1024:2048])
    g_k = g_ref[1:2, :]
    for h in range(C_HEADS):
        blk = _rope(_rms_halves(z[:, h * 128:(h + 1) * 128], lane) * g_k, cos64, sin64, 16, lane)
        kc_ref[0, :, h * 128:(h + 1) * 128] = blk.astype(BF16)
    vc_ref[0] = _dot(a, win_ref[:, 2048:3072]).astype(BF16)

    z = _dot(a, win_ref[:, 3072:4096])
    g_q = g_ref[2:3, :]
    for h in range(D_HEADS):
        blk = _rms(z[:, h * 128:(h + 1) * 128], HEAD_DIM) * g_q
        qd_ref[0, :, h * 128:(h + 1) * 128] = _rope(blk, cos128, sin128, 32, lane).astype(BF16)
    z = _dot(a, win_ref[:, 4096:4608])
    g_k = g_ref[3:4, :]
    for h in range(D_KV_HEADS):
        blk = _rms(z[:, h * 128:(h + 1) * 128], HEAD_DIM) * g_k
        kd_ref[0, :, h * 128:(h + 1) * 128] = _rope(blk, cos128, sin128, 32, lane).astype(BF16)
    vd_ref[0] = z[:, 256:512].astype(BF16)

    @pl.when(step == n_steps - 1)
    def _():
        _wait_row_gathers(ys_hbm, ybuf.at[1 - cur], sems.at[1 - cur])


def _odd_proj_call(hn, ys, slots, rt, mod_prev, mod, tab, gains, win):
    bsz = hn.shape[0]
    widths = (C_HEADS * 2 * LANES, C_HEADS * LANES, C_HEADS * C_V,
              D_HEADS * HEAD_DIM, D_KV_HEADS * HEAD_DIM, D_KV_HEADS * HEAD_DIM)
    n_steps = bsz * N_ROW_TILES
    tile = lambda w: pl.BlockSpec((1, ROW_TILE, w), lambda b, j: (b, j, 0))
    mod_spec = pl.BlockSpec((1, 1, 6 * D_MODEL), lambda b, j: (_mod_row(b, j), 0, 0))
    slot_spec = lambda ahead: pl.BlockSpec(
        (1, TOP_K, ROW_TILE), lambda b, j: (jnp.minimum(b * N_ROW_TILES + j + ahead, n_steps - 1), 0, 0),
        memory_space=pltpu.SMEM)
    return pl.pallas_call(
        _odd_proj_kernel,
        grid=(bsz, N_ROW_TILES),
        in_specs=[
            slot_spec(0), slot_spec(1), tile(D_MODEL), tile(LANES), mod_spec, mod_spec,
            pl.BlockSpec((ROW_TILE, 768), lambda b, j: (j, 0)),
            _resident(gains.shape), _resident(win.shape),
            pl.BlockSpec(memory_space=pl.ANY),
        ],
        out_specs=[tile(D_MODEL)] + [tile(w) for w in widths],
        out_shape=[jax.ShapeDtypeStruct((bsz, TOK, D_MODEL), F32)]
                  + [jax.ShapeDtypeStruct((bsz, TOK, w), BF16) for w in widths],
        scratch_shapes=[pltpu.VMEM((2, TOP_K, ROW_TILE, PACKED_W), U32), pltpu.SemaphoreType.DMA((2,))],
        compiler_params=pltpu.CompilerParams(
            dimension_semantics=("arbitrary", "arbitrary"), vmem_limit_bytes=VMEM_LIMIT),
        name="odd_proj",
    )(slots, slots, hn, rt, mod_prev, mod, tab, gains, win, ys)


KEY_CHUNK = 768


def _attn_kernel(*refs, nkv, g, dk, dv, tq, n_keys, diff_lambda_init):
    if diff_lambda_init is None:
        q_ref, k_ref, v_ref, o_ref = refs
    else:
        q_ref, k_ref, v_ref, aux_ref, o_ref = refs
    for kv in range(nkv):
        q = jnp.concatenate(
            [q_ref[0, :, (kv * g + gi) * dk:(kv * g + gi + 1) * dk] for gi in range(g)], axis=0)
        m = l = acc = None
        for c0 in range(0, n_keys, KEY_CHUNK):
            c1 = min(c0 + KEY_CHUNK, n_keys)
            s = _dot_nt(q, k_ref[0, c0:c1, kv * dk:(kv + 1) * dk])
            v = v_ref[0, c0:c1, kv * dv:(kv + 1) * dv]
            m_c = jnp.max(s, axis=-1, keepdims=True)
            if m is None:
                m = m_c
                p = jnp.exp2((s - m).astype(BF16))
                l = jnp.sum(p.astype(F32), axis=-1, keepdims=True)
                acc = _dot(p, v)
            else:
                m_new = jnp.maximum(m, m_c)
                alpha = jnp.exp2(m - m_new)
                p = jnp.exp2((s - m_new).astype(BF16))
                l = alpha * l + jnp.sum(p.astype(F32), axis=-1, keepdims=True)
                acc = alpha * acc + _dot(p, v)
                m = m_new
        o = acc / l
        if diff_lambda_init is None:
            for gi in range(g):
                o_ref[0, :, (kv * g + gi) * dv:(kv * g + gi + 1) * dv] = o[gi * tq:(gi + 1) * tq].astype(BF16)
        else:
            lam = (jnp.exp(jnp.sum(aux_ref[0:1, :] * aux_ref[1:2, :], axis=-1, keepdims=True))
                   - jnp.exp(jnp.sum(aux_ref[2:3, :] * aux_ref[3:4, :], axis=-1, keepdims=True))
                   + diff_lambda_init)
            od = o[0:tq] - lam * o[tq:2 * tq]
            od = _rms(od, dv) * aux_ref[4:5, :] * (1.0 - diff_lambda_init)
            o_ref[0, :, kv * dv:(kv + 1) * dv] = od.astype(BF16)


def _attn_call(q, k, v, *, n_heads_kv, g, dk, dv, nkv, tq, q_tile0, n_q_tiles, key_tile0, n_keys,
               aux=None, diff_lambda_init=None, name="attn"):
    bsz = q.shape[0]
    n_out_heads = n_heads_kv * (g if diff_lambda_init is None else 1)
    out_w = nkv * (g if diff_lambda_init is None else 1) * dv
    kern = functools.partial(_attn_kernel, nkv=nkv, g=g, dk=dk, dv=dv, tq=tq, n_keys=n_keys,
                             diff_lambda_init=diff_lambda_init)
    in_specs = [
        pl.BlockSpec((1, tq, nkv * g * dk), lambda b, hh, qi: (b, qi + q_tile0, hh)),
        pl.BlockSpec((1, n_keys, nkv * dk), lambda b, hh, qi: (b, key_tile0, hh)),
        pl.BlockSpec((1, n_keys, nkv * dv), lambda b, hh, qi: (b, key_tile0, hh)),
    ]
    args = [q, k, v]
    if aux is not None:
        in_specs.append(pl.BlockSpec(aux.shape, lambda b, hh, qi: (0, 0)))
        args.append(aux)
    return pl.pallas_call(
        kern,
        grid=(bsz, n_heads_kv // nkv, n_q_tiles),
        in_specs=in_specs,
        out_specs=pl.BlockSpec((1, tq, out_w), lambda b, hh, qi: (b, qi, hh)),
        out_shape=jax.ShapeDtypeStruct((bsz, n_q_tiles * tq, n_out_heads * dv), BF16),
        compiler_params=pltpu.CompilerParams(
            dimension_semantics=("parallel", "parallel", "parallel"), vmem_limit_bytes=VMEM_LIMIT),
        name=name,
    )(*args)


WIN_TQ = 256
WIN_BAND = WIN_TQ + 2 * WINDOW
NEG_BIG = -1e30


WIN_Q_TILES = SEQ // WIN_TQ


def _window_band(tile):
    q0 = (tile % WIN_Q_TILES) * WIN_TQ
    return q0, pl.multiple_of(jnp.clip(q0 - WINDOW, 0, SEQ - WIN_BAND), WINDOW)


def _window_kernel(q_ref, k_ref, v_ref, sink_ref, o_ref, sloc0, sctx0, sloc1, sctx1, *, n_total):
    g = D_HEADS // D_KV_HEADS
    dk = HEAD_DIM
    s = pl.program_id(0)

    @pl.when(s == 0)
    def _():
        sloc1[...] = jnp.zeros_like(sloc1)
        sctx1[...] = jnp.zeros_like(sctx1)

    def step(write, read):
        sloc_w, sctx_w = write
        sloc_r, sctx_r = read
        q0, start = _window_band(jnp.minimum(s, n_total - 1))
        q = jnp.concatenate([q_ref[0, :, gi * dk:(gi + 1) * dk] for gi in range(g)], axis=0)
        row = (lax.broadcasted_iota(jnp.int32, (g * WIN_TQ, WIN_BAND), 0) & (WIN_TQ - 1)) + q0
        col = lax.broadcasted_iota(jnp.int32, (g * WIN_TQ, WIN_BAND), 1) + start
        sloc_w[...] = jnp.where(jnp.abs(row - col) <= WINDOW,
                                _dot_nt(q, k_ref[0, pl.ds(start, WIN_BAND), :]), NEG_BIG)
        sctx_w[...] = _dot_nt(q, k_ref[0, SEQ:TOK, :])

        _, start = _window_band(jnp.maximum(s - 1, 0))
        s_loc = sloc_r[...]
        s_ctx = sctx_r[...]
        sink = sink_ref[0]
        m = jnp.maximum(jnp.maximum(jnp.max(s_loc, axis=-1, keepdims=True),
                                    jnp.max(s_ctx, axis=-1, keepdims=True)), sink)
        p_loc = jnp.exp2(s_loc - m)
        p_ctx = jnp.exp2(s_ctx - m)
        l = (jnp.sum(p_loc, axis=-1, keepdims=True) + jnp.sum(p_ctx, axis=-1, keepdims=True)
             + jnp.exp2(sink - m))
        acc = (_dot(p_ctx.astype(BF16), v_ref[0, SEQ:TOK, :])
               + _dot(p_loc.astype(BF16), v_ref[0, pl.ds(start, WIN_BAND), :]))
        o = acc / l
        for gi in range(g):
            o_ref[0, :, gi * dk:(gi + 1) * dk] = o[gi * WIN_TQ:(gi + 1) * WIN_TQ].astype(BF16)

    @pl.when(s % 2 == 0)
    def _():
        step((sloc0, sctx0), (sloc1, sctx1))

    @pl.when(s % 2 == 1)
    def _():
        step((sloc1, sctx1), (sloc0, sctx0))


def _window_call(q, k, v, sink_col):
    bsz = q.shape[0]
    g = D_HEADS // D_KV_HEADS
    n_total = bsz * D_KV_HEADS * WIN_Q_TILES
    split = lambda t: (t // (D_KV_HEADS * WIN_Q_TILES), (t // WIN_Q_TILES) % D_KV_HEADS, t % WIN_Q_TILES)
    stage_a = lambda s: split(jnp.minimum(s, n_total - 1))
    stage_b = lambda s: split(jnp.maximum(s - 1, 0))

    def q_index(s):
        b, hh, qi = stage_a(s)
        return (b, qi, hh)

    def k_index(s):
        b, hh, _ = stage_a(s)
        return (b, 0, hh)

    def v_index(s):
        b, hh, _ = stage_b(s)
        return (b, 0, hh)

    def o_index(s):
        b, hh, qi = stage_b(s)
        return (b, qi, hh)

    m_rows = g * WIN_TQ
    return pl.pallas_call(
        functools.partial(_window_kernel, n_total=n_total),
        grid=(n_total + 1,),
        in_specs=[
            pl.BlockSpec((1, WIN_TQ, g * HEAD_DIM), q_index),
            pl.BlockSpec((1, TOK, HEAD_DIM), k_index),
            pl.BlockSpec((1, TOK, HEAD_DIM), v_index),
            pl.BlockSpec((1, m_rows, 1), lambda s: (stage_b(s)[1], 0, 0)),
        ],
        out_specs=pl.BlockSpec((1, WIN_TQ, g * HEAD_DIM), o_index),
        out_shape=jax.ShapeDtypeStruct((bsz, SEQ, D_HEADS * HEAD_DIM), BF16),
        scratch_shapes=[pltpu.VMEM((m_rows, WIN_BAND), F32), pltpu.VMEM((m_rows, CTX_LEN), F32),
                        pltpu.VMEM((m_rows, WIN_BAND), F32), pltpu.VMEM((m_rows, CTX_LEN), F32)],
        compiler_params=pltpu.CompilerParams(
            dimension_semantics=("arbitrary",), vmem_limit_bytes=VMEM_LIMIT),
        name="window_attn",
    )(q, k, v, sink_col)


def _route(logits, lane):
    lane_f = lane.astype(F32)
    lg = jnp.where(lane < N_GROUPS, logits, NEG_BIG)
    g_max = jnp.max(lg, axis=-1, keepdims=True)
    p_top = 1.0 / jnp.sum(jnp.exp(lg - g_max), axis=-1, keepdims=True)
    g_idx = jnp.min(jnp.where(lg == g_max, lane_f, float(LANES)), axis=-1, keepdims=True)
    e_lane = lane - N_GROUPS
    in_group = (e_lane >= 0) & (e_lane < N_EXPERTS) & ((e_lane // EXPERTS_PER_GROUP).astype(F32) == g_idx)
    le = jnp.where(in_group, logits, NEG_BIG)
    m1 = jnp.max(le, axis=-1, keepdims=True)
    i1 = jnp.min(jnp.where(le == m1, lane_f, float(LANES)), axis=-1, keepdims=True)
    le2 = jnp.where(lane_f == i1, NEG_BIG, le)
    m2 = jnp.max(le2, axis=-1, keepdims=True)
    i2 = jnp.min(jnp.where(le2 == m2, lane_f, float(LANES)), axis=-1, keepdims=True)
    t = jnp.exp(m2 - m1)
    w1 = p_top / (1.0 + t)
    w2 = p_top * t / (1.0 + t)
    return i1, i2, w1, w2


def _outproj_kernel(*refs, has_ctx, n_tiles, n_total):
    if has_ctx:
        (o1_ref, o2_ref, h_ref, o1c_ref, o2c_ref, hc_ref, mod_ref, w1_ref, w2_ref, rw_ref, rb_ref,
         hn_ref, fx_ref, rt_ref, cnt_ref, run_ref, logit_ref) = refs
    else:
        (o1_ref, o2_ref, h_ref, mod_ref, w1_ref, w2_ref, rw_ref, rb_ref,
         hn_ref, fx_ref, rt_ref, cnt_ref, run_ref, logit_ref) = refs
    d = D_MODEL
    s = pl.program_id(0)

    @pl.when(s == 0)
    def _():
        run_ref[...] = jnp.zeros_like(run_ref)
        logit_ref[...] = jnp.zeros_like(logit_ref)

    logits = logit_ref[...]
    lane = lax.broadcasted_iota(jnp.int32, (ROW_TILE, LANES), 1)
    lane_f = lane.astype(F32)
    i1, i2, w1, w2 = _route(logits, lane)

    o1, o2, h = o1_ref[0], o2_ref[0], h_ref[0]
    if has_ctx:
        is_ctx = jnp.minimum(s, n_total - 1) % n_tiles == CTX_TILE
        o1 = jnp.where(is_ctx, o1c_ref[0], o1)
        o2 = jnp.where(is_ctx, o2c_ref[0], o2)
        h = jnp.where(is_ctx, hc_ref[0], h)
    acc = _dot(o1, w1_ref[...]) + _dot(o2, w2_ref[...])
    hn = h + mod_ref[0, :, 2 * d:3 * d] * acc
    hn_ref[0] = hn
    fx = _modulate(hn, mod_ref[0, :, 3 * d:4 * d], mod_ref[0, :, 4 * d:5 * d])
    fx_ref[0] = _pack_bf16_pairs(fx)
    hi = fx.astype(BF16)
    lo = (fx - hi.astype(F32)).astype(BF16)
    part = _dot(jnp.concatenate([hi, lo], axis=0), rw_ref[...])
    logit_ref[...] = (part[:ROW_TILE, :LANES] + part[:ROW_TILE, LANES:]
                      + part[ROW_TILE:, :LANES] + part[ROW_TILE:, LANES:] + rb_ref[...])

    hit1, hit2 = lane_f == i1, lane_f == i2
    multi_hot = jnp.where(s > 0, jnp.where(hit1, 1.0, 0.0) + jnp.where(hit2, 1.0, 0.0), 0.0)
    tri = (lax.broadcasted_iota(jnp.int32, (ROW_TILE, ROW_TILE), 0)
           > lax.broadcasted_iota(jnp.int32, (ROW_TILE, ROW_TILE), 1))
    before = _dot(jnp.where(tri, 1.0, 0.0).astype(BF16), multi_hot.astype(BF16)) + run_ref[...]
    r1 = jnp.sum(jnp.where(hit1, before, 0.0), axis=-1, keepdims=True)
    r2 = jnp.sum(jnp.where(hit2, before, 0.0), axis=-1, keepdims=True)
    run_ref[...] = run_ref[...] + jnp.sum(multi_hot, axis=0, keepdims=True)
    cnt_ref[...] = jnp.broadcast_to(run_ref[...], cnt_ref.shape)

    out = jnp.where(lane == 0, i1 - N_GROUPS, 0.0)
    out = jnp.where(lane == 1, i2 - N_GROUPS, out)
    out = jnp.where(lane == 2, w1, out)
    out = jnp.where(lane == 3, w2, out)
    out = jnp.where(lane == 4, r1, out)
    out = jnp.where(lane == 5, r2, out)
    rt_ref[0] = out


def _outproj_call(lat, ctx, mod, w1, w2, rw, rb, n_tiles):
    bsz = lat[2].shape[0]
    has_ctx = ctx is not None
    n_total = bsz * n_tiles
    stage_a = lambda s: jnp.minimum(s, n_total - 1)
    stage_b = lambda s: jnp.maximum(s - 1, 0)
    bj = lambda t: (t // n_tiles, t % n_tiles)

    def a_tile(w, kind):
        def index(s):
            b, j = bj(stage_a(s))
            return (b, {"all": j, "lat": jnp.minimum(j, CTX_TILE - 1), "ctx": 0}[kind], 0)
        return pl.BlockSpec((1, ROW_TILE, w), index)

    def b_tile(w):
        return pl.BlockSpec((1, ROW_TILE, w), lambda s: (*bj(stage_b(s)), 0))

    in_specs = [a_tile(a.shape[-1], "lat") for a in lat]
    args = list(lat)
    if has_ctx:
        in_specs += [a_tile(a.shape[-1], "ctx") for a in ctx]
        args += list(ctx)
    in_specs += [
        pl.BlockSpec((1, 1, 6 * D_MODEL), lambda s: (_mod_row(*bj(stage_a(s))), 0, 0)),
        _resident(w1.shape), _resident(w2.shape), _resident(rw.shape), _resident(rb.shape),
    ]
    args += [mod, w1, w2, rw, rb]
    n_rows = n_tiles * ROW_TILE
    return pl.pallas_call(
        functools.partial(_outproj_kernel, has_ctx=has_ctx, n_tiles=n_tiles, n_total=n_total),
        grid=(n_total + 1,),
        in_specs=in_specs,
        out_specs=[a_tile(D_MODEL, "all"), a_tile(PACKED_W, "all"), b_tile(LANES),
                   pl.BlockSpec((8, LANES), lambda s: (0, 0))],
        out_shape=[jax.ShapeDtypeStruct((bsz, n_rows, D_MODEL), F32),
                   jax.ShapeDtypeStruct((bsz, n_rows, PACKED_W), U32),
                   jax.ShapeDtypeStruct((bsz, n_rows, LANES), F32),
                   jax.ShapeDtypeStruct((8, LANES), F32)],
        scratch_shapes=[pltpu.VMEM((1, LANES), F32), pltpu.VMEM((ROW_TILE, LANES), F32)],
        compiler_params=pltpu.CompilerParams(
            dimension_semantics=("arbitrary",), vmem_limit_bytes=VMEM_LIMIT),
        name="outproj_router",
    )(*args)


def _dispatch_kernel(lo_ref, hi_ref, slot_ref, fx_ref, xs_hbm, zbuf, sem, zsem):
    b, j = pl.program_id(0), pl.program_id(1)

    for r in range(ROW_TILE):
        for k in range(TOP_K):
            pltpu.make_async_copy(fx_ref.at[0, pl.ds(r, 1)], xs_hbm.at[pl.ds(slot_ref[0, k, r], 1)],
                                  sem).start(priority=k % 2)

    @pl.when((b == pl.num_programs(0) - 1) & (j == pl.num_programs(1) - 1))
    def _():
        zbuf[...] = jnp.zeros_like(zbuf)

        def fill(s, carry):
            pltpu.make_async_copy(zbuf.at[pl.ds(0, 1)], xs_hbm.at[pl.ds(s, 1)], zsem).start()
            return carry

        def drain(s, carry):
            pltpu.make_async_copy(zbuf.at[pl.ds(0, 1)], xs_hbm.at[pl.ds(0, 1)], zsem).wait()
            return carry
        for e in range(N_EXPERTS):
            lax.fori_loop(lo_ref[e], hi_ref[e], fill, 0)
        for e in range(N_EXPERTS):
            lax.fori_loop(lo_ref[e], hi_ref[e], drain, 0)

    for k in range(TOP_K):
        pltpu.make_async_copy(fx_ref.at[0], xs_hbm.at[pl.ds(0, ROW_TILE)], sem).wait()


def _dispatch_call(fx, slots, fill_lo, fill_hi, n_blocks):
    bsz, n_rows, _ = fx.shape
    n_tiles = n_rows // ROW_TILE
    grid_spec = pltpu.PrefetchScalarGridSpec(
        num_scalar_prefetch=2,
        grid=(bsz, n_tiles),
        in_specs=[
            pl.BlockSpec((1, TOP_K, ROW_TILE), lambda b, j, lo, hi: (b * n_tiles + j, 0, 0),
                         memory_space=pltpu.SMEM),
            pl.BlockSpec((1, ROW_TILE, PACKED_W), lambda b, j, lo, hi: (b, j, 0)),
        ],
        out_specs=pl.BlockSpec(memory_space=pl.ANY),
        scratch_shapes=[pltpu.VMEM((8, PACKED_W), U32), pltpu.SemaphoreType.DMA(()), pltpu.SemaphoreType.DMA(())],
    )
    return pl.pallas_call(
        _dispatch_kernel,
        grid_spec=grid_spec,
        out_shape=jax.ShapeDtypeStruct((n_blocks * MOE_BLOCK, PACKED_W), U32),
        compiler_params=pltpu.CompilerParams(
            dimension_semantics=("arbitrary", "arbitrary"), vmem_limit_bytes=VMEM_LIMIT),
        name="moe_dispatch",
    )(fill_lo, fill_hi, slots, fx)


CAST_ROWS = 256


def _moe_kernel(be_ref, nu_ref, nx_ref, x_ref, wg_hbm, wu_hbm, wd_hbm, y_ref,
                sg, su, sd, wg, wu, wd, sems, *, layer):
    i = pl.program_id(0)
    e = be_ref[i]
    used = i < nu_ref[0]
    staged = ((wg_hbm, sg, wg, 0), (wu_hbm, su, wu, 1), (wd_hbm, sd, wd, 2))

    def fetch(expert):
        for hbm, stage, _, s in staged:
            pltpu.make_async_copy(hbm.at[layer, expert], stage, sems.at[s]).start(priority=1)

    @pl.when(i == 0)
    def _():
        fetch(e)

    first_of_expert = (i == 0) | (e != be_ref[jnp.maximum(i - 1, 0)])

    @pl.when(used & first_of_expert)
    def _():
        for hbm, stage, dst, s in staged:
            pltpu.make_async_copy(hbm.at[layer, 0], stage, sems.at[s]).wait()

            def cast(c, carry):
                rows = pl.ds(pl.multiple_of(c * CAST_ROWS, CAST_ROWS), CAST_ROWS)
                dst[rows, :] = stage[rows, :].astype(BF16)
                return carry
            lax.fori_loop(0, stage.shape[0] // CAST_ROWS, cast, 0)
        nxt = nx_ref[e]

        @pl.when(nxt >= 0)
        def _():
            fetch(nxt)

    @pl.when(used)
    def _():
        x = _unpack_bf16_pairs(x_ref[...]).astype(BF16)
        gt = _dot(x, wg[...])
        up = _dot(x, wu[...])
        u = (_silu(gt) * up).astype(BF16)
        y_ref[...] = _pack_bf16_pairs(_dot(u, wd[...]))


def _moe_call(xs, block_expert, n_used, next_expert, wg, wu, wd, layer):
    n_blocks = block_expert.shape[0]
    row_blk = pl.BlockSpec((MOE_BLOCK, PACKED_W), lambda i, be, nu, nx: (jnp.minimum(i, nu[0] - 1), 0))
    hbm = pl.BlockSpec(memory_space=pl.ANY)
    grid_spec = pltpu.PrefetchScalarGridSpec(
        num_scalar_prefetch=3,
        grid=(n_blocks,),
        in_specs=[row_blk, hbm, hbm, hbm],
        out_specs=row_blk,
        scratch_shapes=[
            pltpu.VMEM((D_MODEL, D_EXPERT), F32), pltpu.VMEM((D_MODEL, D_EXPERT), F32),
            pltpu.VMEM((D_EXPERT, D_MODEL), F32),
            pltpu.VMEM((D_MODEL, D_EXPERT), BF16), pltpu.VMEM((D_MODEL, D_EXPERT), BF16),
            pltpu.VMEM((D_EXPERT, D_MODEL), BF16),
            pltpu.SemaphoreType.DMA((3,)),
        ],
    )
    return pl.pallas_call(
        functools.partial(_moe_kernel, layer=layer),
        grid_spec=grid_spec,
        out_shape=jax.ShapeDtypeStruct((n_blocks * MOE_BLOCK, PACKED_W), U32),
        compiler_params=pltpu.CompilerParams(
            dimension_semantics=("arbitrary",), vmem_limit_bytes=VMEM_LIMIT),
        name="moe_experts",
    )(block_expert, n_used, next_expert, xs, wg, wu, wd)


def _combine_kernel(slot_ref, h_ref, rt_ref, mod_ref, ys_hbm, o_ref, ybuf, sem):
    d = D_MODEL
    _start_row_gathers(slot_ref, ys_hbm, ybuf, sem)
    _wait_row_gathers(ys_hbm, ybuf, sem)
    rt = rt_ref[0]
    y = rt[:, 2:3] * _unpack_bf16_pairs(ybuf[0]) + rt[:, 3:4] * _unpack_bf16_pairs(ybuf[1])
    o_ref[0] = h_ref[0] + mod_ref[0, :, 5 * d:6 * d] * y


def _combine_call(h, ys, slots, rt, mod, n_tiles):
    bsz = h.shape[0]
    tile = lambda w: pl.BlockSpec((1, ROW_TILE, w), lambda b, j: (b, j, 0))
    return pl.pallas_call(
        _combine_kernel,
        grid=(bsz, n_tiles),
        in_specs=[
            pl.BlockSpec((1, TOP_K, ROW_TILE), lambda b, j: (b * n_tiles + j, 0, 0), memory_space=pltpu.SMEM),
            tile(D_MODEL), tile(LANES),
            pl.BlockSpec((1, 1, 6 * D_MODEL), lambda b, j: (_mod_row(b, j), 0, 0)),
            pl.BlockSpec(memory_space=pl.ANY),
        ],
        out_specs=tile(D_MODEL),
        out_shape=jax.ShapeDtypeStruct((bsz, n_tiles * ROW_TILE, D_MODEL), F32),
        scratch_shapes=[pltpu.VMEM((TOP_K, ROW_TILE, PACKED_W), U32), pltpu.SemaphoreType.DMA(())],
        compiler_params=pltpu.CompilerParams(
            dimension_semantics=("arbitrary", "arbitrary"), vmem_limit_bytes=VMEM_LIMIT),
        name="moe_combine",
    )(slots, h, rt, mod, ys)


def _rope_tables():
    rows = SEQ // GRID_W
    row = jnp.repeat(jnp.arange(rows), GRID_W).astype(F32)
    col = jnp.tile(jnp.arange(GRID_W), rows).astype(F32)

    def tables(dim):
        n_freq = dim // 4
        inv = 1.0 / (ROPE_THETA ** (jnp.arange(n_freq, dtype=F32) / n_freq))
        ang_r = row[:, None] * inv
        ang_c = col[:, None] * inv
        ang = jnp.concatenate([ang_r, ang_r, ang_c, ang_c], axis=-1)
        sign = jnp.tile(jnp.concatenate([-jnp.ones((n_freq,), F32), jnp.ones((n_freq,), F32)]), 2)
        return jnp.cos(ang), jnp.sin(ang) * sign

    cos128, sin128 = tables(HEAD_DIM)
    cos64, sin64 = tables(B_ROPE)
    ones, zeros = jnp.ones((SEQ, 64), F32), jnp.zeros((SEQ, 64), F32)
    lat = jnp.concatenate([cos128, sin128,
                           cos64, cos64, sin64, sin64,
                           cos64, ones, sin64, zeros], axis=-1)
    ident = jnp.concatenate([jnp.ones((CTX_LEN, 128), F32), jnp.zeros((CTX_LEN, 128), F32)], axis=-1)
    return jnp.concatenate([lat, jnp.tile(ident, (1, 3))], axis=0)


def _pad_lanes(v, n):
    return jnp.pad(v, (0, n - v.shape[0]))


def _router_weights(wr_g, br_g, wr_e, br_e):
    w = jnp.pad(jnp.concatenate([wr_g, wr_e], axis=1), ((0, 0), (0, LANES - N_GROUPS - N_EXPERTS)))
    hi = w.astype(BF16)
    lo = (w - hi.astype(F32)).astype(BF16)
    b = _pad_lanes(jnp.concatenate([br_g, br_e]), LANES).reshape(1, LANES)
    return jnp.concatenate([hi, lo], axis=1), b


def _moe_experts(fx, rt, cnt, wg, wu, wd, layer):
    bsz, n_rows, _ = fx.shape
    n_tiles = n_rows // ROW_TILE
    n_blocks = -(-(bsz * n_rows * TOP_K) // MOE_BLOCK) + N_EXPERTS
    experts = jnp.arange(N_EXPERTS, dtype=jnp.int32)
    counts = cnt[0, N_GROUPS:N_GROUPS + N_EXPERTS].astype(jnp.int32)
    padded = (counts + MOE_BLOCK - 1) // MOE_BLOCK * MOE_BLOCK
    pad_end = jnp.cumsum(padded)
    pad_start = pad_end - padded
    e_idx = rt[..., 0:TOP_K].astype(jnp.int32)
    rank = rt[..., 4:4 + TOP_K].astype(jnp.int32)
    slot = rank + jnp.sum(jnp.where(e_idx[..., None] == experts, pad_start, 0), axis=-1)
    slots = slot.reshape(bsz * n_tiles, ROW_TILE, TOP_K).transpose(0, 2, 1)
    block_start = jnp.arange(n_blocks, dtype=jnp.int32) * MOE_BLOCK
    block_expert = jnp.minimum(jnp.sum(pad_end[None, :] <= block_start[:, None], axis=1), N_EXPERTS - 1)
    n_used = pad_end[-1:] // MOE_BLOCK
    later = (experts[None, :] > experts[:, None]) & (counts[None, :] > 0)
    next_expert = jnp.where(jnp.any(later, axis=1), jnp.argmax(later, axis=1), -1)
    i32 = lambda a: a.astype(jnp.int32)
    xs = _dispatch_call(fx, i32(slots), i32(pad_start + counts), i32(pad_end), n_blocks)
    ys = _moe_call(xs, i32(block_expert), i32(n_used), i32(next_expert), wg, wu, wd, layer)
    return ys, i32(slots)


def kernel(x, c, ctx, c_ctx, mod_w, mod_b, even_w_in, even_w_out, a_q_norm, a_k_norm, b_cq_norm, b_w_uq,
           b_ckv_norm, b_w_ukv, b_q_norm, b_k_norm, odd_w_in, odd_w_out, c_q_norm, c_k_norm, c_lambda_q1,
           c_lambda_k1, c_lambda_q2, c_lambda_k2, c_subln, d_q_norm, d_k_norm, d_sink, moe_wr_group,
           moe_br_group, moe_wr_expert, moe_br_expert, moe_w_gate, moe_w_up, moe_w_down):
    bsz = x.shape[0]
    c_all = jnp.zeros((MOD_ROWS, D_MODEL), F32).at[:bsz].set(c).at[MOD_CTX_ROW].set(c_ctx)
    mod_all = _mod_call(c_all, mod_w, mod_b)
    tab = _rope_tables()

    i = 0
    mod = mod_all[0].reshape(MOD_ROWS, 1, 6 * D_MODEL)
    scale_a = HEAD_DIM ** -0.5 * LOG2_E
    scale_b = (B_NOPE + B_ROPE) ** -0.5 * LOG2_E
    win = jnp.pad(even_w_in[i], ((0, 0), (0, EVEN_IN_PAD - even_w_in.shape[-1]))).astype(BF16)
    wuq = b_w_uq[i].reshape(B_Q_LORA, B_HEADS, B_NOPE + B_ROPE)
    wuq = jnp.pad(wuq, ((0, 0), (0, 0), (0, B_QK_PAD - B_NOPE - B_ROPE))).reshape(B_Q_LORA, -1).astype(BF16)
    wukv = b_w_ukv[i].reshape(B_KV_LORA, B_HEADS, B_NOPE + B_V)
    wukv = jnp.concatenate([wukv[:, :, :B_NOPE].reshape(B_KV_LORA, -1),
                            wukv[:, :, B_NOPE:].reshape(B_KV_LORA, -1)], axis=1).astype(BF16)
    gains = jnp.stack([
        _pad_lanes(a_q_norm[i] * scale_a, 512), _pad_lanes(a_k_norm[i], 512),
        b_cq_norm[i], b_ckv_norm[i],
        _pad_lanes(b_q_norm[i] * scale_b, 512), _pad_lanes(b_k_norm[i], 512),
        jnp.zeros((512,), F32), jnp.zeros((512,), F32)])
    qa, ka, va, qb, kb, vb = _even_proj_call(x, ctx, mod, tab, gains, win, wuq, wukv)

    g_a = A_HEADS // A_KV_HEADS
    lat = dict(q_tile0=0, key_tile0=0, n_keys=TOK)
    oa = _attn_call(qa, ka, va, n_heads_kv=A_KV_HEADS, g=g_a, dk=HEAD_DIM, dv=HEAD_DIM, nkv=2,
                    tq=256, n_q_tiles=SEQ // 256, name="gqa_latent", **lat)
    ob = _attn_call(qb, kb, vb, n_heads_kv=B_HEADS, g=1, dk=B_QK_PAD, dv=B_V, nkv=2,
                    tq=512, n_q_tiles=SEQ // 512, name="mla_latent", **lat)
    cx = dict(tq=CTX_LEN, q_tile0=SEQ // CTX_LEN, n_q_tiles=1, key_tile0=SEQ // CTX_LEN, n_keys=CTX_LEN)
    oa_c = _attn_call(qa, ka, va, n_heads_kv=A_KV_HEADS, g=g_a, dk=HEAD_DIM, dv=HEAD_DIM, nkv=2,
                      name="gqa_context", **cx)
    ob_c = _attn_call(qb, kb, vb, n_heads_kv=B_HEADS, g=1, dk=B_QK_PAD, dv=B_V, nkv=2,
                      name="mla_context", **cx)

    w_out = even_w_out[i].astype(BF16)
    rw, rb = _router_weights(moe_wr_group[0], moe_br_group[0], moe_wr_expert[0], moe_br_expert[0])
    hn, fx, rt, cnt = _outproj_call((oa, ob, x), (oa_c, ob_c, ctx), mod,
                                    w_out[:A_HEADS * HEAD_DIM], w_out[A_HEADS * HEAD_DIM:],
                                    rw, rb, N_ROW_TILES)
    ys, slots = _moe_experts(fx, rt, cnt, moe_w_gate, moe_w_up, moe_w_down, 0)
    hn0, rt0, mod0 = hn, rt, mod

    layer = 1
    mod = mod_all[1].reshape(MOD_ROWS, 1, 6 * D_MODEL)
    lambda_init = 0.8 - 0.6 * math.exp(-0.3 * layer)
    scale_c = C_HD ** -0.5 * LOG2_E
    scale_d = HEAD_DIM ** -0.5 * LOG2_E
    win = odd_w_in[i].astype(BF16)
    gains = jnp.stack([
        jnp.tile(c_q_norm[i] * scale_c, 2), jnp.tile(c_k_norm[i], 2),
        d_q_norm[i] * scale_d, d_k_norm[i],
        jnp.zeros((128,), F32), jnp.zeros((128,), F32), jnp.zeros((128,), F32), jnp.zeros((128,), F32)])
    h, qc, kc, vc, qd, kd, vd = _odd_proj_call(hn0, ys, slots, rt0, mod0, mod, tab, gains, win)

    aux = jnp.stack([
        _pad_lanes(c_lambda_q1[i], 128), _pad_lanes(c_lambda_k1[i], 128),
        _pad_lanes(c_lambda_q2[i], 128), _pad_lanes(c_lambda_k2[i], 128),
        c_subln[i], jnp.zeros((128,), F32), jnp.zeros((128,), F32), jnp.zeros((128,), F32)])
    oc = _attn_call(qc, kc, vc, n_heads_kv=C_HEADS, g=2, dk=LANES, dv=C_V, nkv=2,
                    tq=512, n_q_tiles=SEQ // 512, aux=aux, diff_lambda_init=lambda_init,
                    name="diff_latent", **lat)
    g_d = D_HEADS // D_KV_HEADS
    sink_col = jnp.repeat((d_sink[i] * LOG2_E).reshape(D_KV_HEADS, g_d), WIN_TQ, axis=1)
    sink_col = sink_col.reshape(D_KV_HEADS, g_d * WIN_TQ, 1)
    od = _window_call(qd, kd, vd, sink_col)

    w_out = odd_w_out[i].astype(BF16)
    rw, rb = _router_weights(moe_wr_group[1], moe_br_group[1], moe_wr_expert[1], moe_br_expert[1])
    n_lat_tiles = SEQ // ROW_TILE
    hn, fx, rt, cnt = _outproj_call((oc, od, h), None, mod, w_out[:C_HEADS * C_V], w_out[C_HEADS * C_V:],
                                    rw, rb, n_lat_tiles)
    ys, slots = _moe_experts(fx, rt, cnt, moe_w_gate, moe_w_up, moe_w_down, 1)
    return _combine_call(hn, ys, slots, rt, mod, n_lat_tiles)
```

```python
import functools
import math

import jax
import jax.numpy as jnp
from jax import lax
from jax.experimental import pallas as pl
from jax.experimental.pallas import tpu as pltpu

F32 = jnp.float32
BF16 = jnp.bfloat16

D_MODEL = 2048
BATCH = 8
SEQ = 2048
DEPTH = 2
GRID_W = 64
CTX_LEN = 256
TOK = SEQ + CTX_LEN
HEAD_DIM = 128
ROPE_THETA = 10000.0
NORM_EPS = 1e-6
A_HEADS = 8
A_KV_HEADS = 2
B_HEADS = 8
B_Q_LORA = 512
B_KV_LORA = 512
B_NOPE = 128
B_ROPE = 64
B_V = 128
B_QK_PAD = 256
C_HEADS = 8
C_HD = 64
C_V = 128
D_HEADS = 8
D_KV_HEADS = 2
WINDOW = 128
N_GROUPS = 4
EXPERTS_PER_GROUP = 8
N_EXPERTS = 32
D_EXPERT = 1024
MOE_BLOCK = 256
TOP_K = 2
LANES = 128
LOG2_E = math.log2(math.e)

ROW_TILE = 256
N_ROW_TILES = TOK // ROW_TILE
CTX_TILE = SEQ // ROW_TILE
MOD_ROWS = 16
MOD_CTX_ROW = BATCH
VMEM_LIMIT = 56 * 1024 * 1024


def _dot(a, b):
    return jnp.dot(a, b, preferred_element_type=F32)


def _dot_nt(a, b):
    return lax.dot_general(a, b, (((1,), (1,)), ((), ())), preferred_element_type=F32)


def _silu(x):
    return x / (1.0 + jnp.exp(-x))


def _rms(x, n):
    return x * lax.rsqrt(jnp.sum(x * x, axis=-1, keepdims=True) * (1.0 / n) + NORM_EPS)


def _modulate(x, shift, scale):
    return _rms(x, x.shape[-1]) * (1.0 + scale) + shift


def _rope(x, cos, sin_signed, quarter, lane):
    fwd = pltpu.roll(x, LANES - quarter, axis=1)
    bwd = pltpu.roll(x, quarter, axis=1)
    rot = jnp.where((lane // quarter) % 2 == 0, fwd, bwd)
    return x * cos + rot * sin_signed


PACKED_W = D_MODEL // 2
U32 = jnp.uint32


def _pack_bf16_pairs(x):
    n = x.shape[-1] // 2
    lo = lax.bitcast_convert_type(x[:, :n].astype(BF16).astype(F32), U32) >> 16
    hi = lax.bitcast_convert_type(x[:, n:].astype(BF16).astype(F32), U32) & U32(0xFFFF0000)
    return lo | hi


def _unpack_bf16_pairs(p):
    lo = lax.bitcast_convert_type(p << 16, F32)
    hi = lax.bitcast_convert_type(p & U32(0xFFFF0000), F32)
    return jnp.concatenate([lo, hi], axis=-1)


def _start_row_gathers(slot_ref, src_hbm, dst_ref, sem):
    for r in range(ROW_TILE):
        for k in range(TOP_K):
            pltpu.make_async_copy(src_hbm.at[pl.ds(slot_ref[0, k, r], 1)], dst_ref.at[k, pl.ds(r, 1)],
                                  sem).start(priority=k % 2)


def _wait_row_gathers(src_hbm, dst_ref, sem):
    for k in range(TOP_K):
        pltpu.make_async_copy(src_hbm.at[pl.ds(0, ROW_TILE)], dst_ref.at[k], sem).wait()


MOD_TN = 1024


def _mod_kernel(c_ref, w_ref, b_ref, o_ref):
    a = _silu(c_ref[...]).astype(BF16)
    o_ref[0] = _dot(a, w_ref[0].astype(BF16)) + b_ref[0]


def _mod_call(c_all, mod_w, mod_b):
    d6 = 6 * D_MODEL
    return pl.pallas_call(
        _mod_kernel,
        grid=(DEPTH, d6 // MOD_TN),
        in_specs=[
            pl.BlockSpec((MOD_ROWS, D_MODEL), lambda l, n: (0, 0)),
            pl.BlockSpec((1, D_MODEL, MOD_TN), lambda l, n: (l, 0, n)),
            pl.BlockSpec((1, 1, MOD_TN), lambda l, n: (l, 0, n)),
        ],
        out_specs=pl.BlockSpec((1, MOD_ROWS, MOD_TN), lambda l, n: (l, 0, n)),
        out_shape=jax.ShapeDtypeStruct((DEPTH, MOD_ROWS, d6), F32),
        compiler_params=pltpu.CompilerParams(
            dimension_semantics=("parallel", "parallel"), vmem_limit_bytes=VMEM_LIMIT),
        name="mod_vectors",
    )(c_all, mod_w, mod_b.reshape(DEPTH, 1, d6))


def _mod_row(b, j):
    return jnp.where(j == CTX_TILE, MOD_CTX_ROW, b)


def _resident(shape):
    nd = len(shape)
    return pl.BlockSpec(shape, lambda *_: (0,) * nd, pipeline_mode=pl.Buffered(1))


EVEN_IN_PAD = 2688


def _even_proj_kernel(hx_ref, hc_ref, mod_ref, tab_ref, g_ref, win_ref, wuq_ref, wukv_ref,
                      qa_ref, ka_ref, va_ref, qb_ref, kb_ref, vb_ref):
    d = D_MODEL
    x = jnp.where(pl.program_id(1) == CTX_TILE, hc_ref[0], hx_ref[0])
    a = _modulate(x, mod_ref[0, :, 0:d], mod_ref[0, :, d:2 * d]).astype(BF16)
    lane = lax.broadcasted_iota(jnp.int32, (ROW_TILE, LANES), 1)
    cos128, sin128 = tab_ref[:, 0:128], tab_ref[:, 128:256]
    cos64p, sin64p = tab_ref[:, 512:640], tab_ref[:, 640:768]

    z = _dot(a, win_ref[:, 0:1024])
    g_q = g_ref[0:1, 0:128]
    for h in range(A_HEADS):
        blk = _rms(z[:, h * 128:(h + 1) * 128], HEAD_DIM) * g_q
        qa_ref[0, :, h * 128:(h + 1) * 128] = _rope(blk, cos128, sin128, 32, lane).astype(BF16)
    z = _dot(a, win_ref[:, 1024:1536])
    g_k = g_ref[1:2, 0:128]
    for h in range(A_KV_HEADS):
        blk = _rms(z[:, h * 128:(h + 1) * 128], HEAD_DIM) * g_k
        ka_ref[0, :, h * 128:(h + 1) * 128] = _rope(blk, cos128, sin128, 32, lane).astype(BF16)
    va_ref[0] = z[:, 256:512].astype(BF16)

    z = _dot(a, win_ref[:, 1536:2048])
    cq = (_rms(z, B_Q_LORA) * g_ref[2:3, :]).astype(BF16)
    zq = _dot(cq, wuq_ref[...])
    gq0, gq1 = g_ref[4:5, 0:128], g_ref[4:5, 128:256]
    n_qk = float(B_NOPE + B_ROPE)
    for h in range(B_HEADS):
        b0 = zq[:, h * 256:h * 256 + 128]
        b1 = zq[:, h * 256 + 128:(h + 1) * 256]
        ss = jnp.sum(b0 * b0, axis=-1, keepdims=True) + jnp.sum(b1 * b1, axis=-1, keepdims=True)
        r = lax.rsqrt(ss * (1.0 / n_qk) + NORM_EPS)
        qb_ref[0, :, h * 256:h * 256 + 128] = (b0 * r * gq0).astype(BF16)
        qb_ref[0, :, h * 256 + 128:(h + 1) * 256] = _rope(b1 * r * gq1, cos64p, sin64p, 16, lane).astype(BF16)

    z = _dot(a, win_ref[:, 2048:2688])
    ckv = (_rms(z[:, 0:512], B_KV_LORA) * g_ref[3:4, :]).astype(BF16)
    kr = z[:, 512:640]
    ss_kr = jnp.sum(kr * kr, axis=-1, keepdims=True)
    kr_rot = _rope(kr * g_ref[5:6, 128:256], cos64p, sin64p, 16, lane)
    zkv = _dot(ckv, wukv_ref[...])
    vb_ref[0] = zkv[:, 1024:2048].astype(BF16)
    gk0 = g_ref[5:6, 0:128]
    for h in range(B_HEADS):
        kn = zkv[:, h * 128:(h + 1) * 128]
        ss = jnp.sum(kn * kn, axis=-1, keepdims=True) + ss_kr
        r = lax.rsqrt(ss * (1.0 / n_qk) + NORM_EPS)
        kb_ref[0, :, h * 256:h * 256 + 128] = (kn * r * gk0).astype(BF16)
        kb_ref[0, :, h * 256 + 128:(h + 1) * 256] = (kr_rot * r).astype(BF16)


def _even_proj_call(h_lat, h_ctx, mod, tab, gains, win, wuq, wukv):
    bsz = h_lat.shape[0]
    widths = (A_HEADS * HEAD_DIM, A_KV_HEADS * HEAD_DIM, A_KV_HEADS * HEAD_DIM,
              B_HEADS * B_QK_PAD, B_HEADS * B_QK_PAD, B_HEADS * B_V)
    tile = lambda w: pl.BlockSpec((1, ROW_TILE, w), lambda b, j: (b, j, 0))
    return pl.pallas_call(
        _even_proj_kernel,
        grid=(bsz, N_ROW_TILES),
        in_specs=[
            pl.BlockSpec((1, ROW_TILE, D_MODEL), lambda b, j: (b, jnp.minimum(j, CTX_TILE - 1), 0)),
            pl.BlockSpec((1, ROW_TILE, D_MODEL), lambda b, j: (b, 0, 0)),
            pl.BlockSpec((1, 1, 6 * D_MODEL), lambda b, j: (_mod_row(b, j), 0, 0)),
            pl.BlockSpec((ROW_TILE, 768), lambda b, j: (j, 0)),
            _resident(gains.shape), _resident(win.shape), _resident(wuq.shape), _resident(wukv.shape),
        ],
        out_specs=[tile(w) for w in widths],
        out_shape=[jax.ShapeDtypeStruct((bsz, TOK, w), BF16) for w in widths],
        compiler_params=pltpu.CompilerParams(
            dimension_semantics=("parallel", "parallel"), vmem_limit_bytes=VMEM_LIMIT),
        name="even_proj",
    )(h_lat, h_ctx, mod, tab, gains, win, wuq, wukv)


def _rms_halves(x, lane):
    x2 = x * x
    s_lo = jnp.sum(jnp.where(lane < 64, x2, 0.0), axis=-1, keepdims=True)
    s_hi = jnp.sum(jnp.where(lane < 64, 0.0, x2), axis=-1, keepdims=True)
    r = jnp.where(lane < 64, lax.rsqrt(s_lo * (1.0 / C_HD) + NORM_EPS), lax.rsqrt(s_hi * (1.0 / C_HD) + NORM_EPS))
    return x * r


def _odd_proj_kernel(slot_ref, slot_next_ref, hn_ref, rt_ref, mod_prev_ref, mod_ref, tab_ref, g_ref, win_ref,
                     ys_hbm, h_ref, qc_ref, kc_ref, vc_ref, qd_ref, kd_ref, vd_ref, ybuf, sems):
    d = D_MODEL
    n_steps = pl.num_programs(0) * pl.num_programs(1)
    step = pl.program_id(0) * pl.num_programs(1) + pl.program_id(1)
    cur = step % 2

    @pl.when(step == 0)
    def _():
        _start_row_gathers(slot_ref, ys_hbm, ybuf.at[0], sems.at[0])

    _wait_row_gathers(ys_hbm, ybuf.at[cur], sems.at[cur])
    rt = rt_ref[0]
    y = rt[:, 2:3] * _unpack_bf16_pairs(ybuf[cur, 0]) + rt[:, 3:4] * _unpack_bf16_pairs(ybuf[cur, 1])
    x = hn_ref[0] + mod_prev_ref[0, :, 5 * d:6 * d] * y
    h_ref[0] = x
    _start_row_gathers(slot_next_ref, ys_hbm, ybuf.at[1 - cur], sems.at[1 - cur])
    a = _modulate(x, mod_ref[0, :, 0:d], mod_ref[0, :, d:2 * d]).astype(BF16)
    lane = lax.broadcasted_iota(jnp.int32, (ROW_TILE, LANES), 1)
    cos128, sin128 = tab_ref[:, 0:128], tab_ref[:, 128:256]
    cos64, sin64 = tab_ref[:, 256:384], tab_ref[:, 384:512]

    z = _dot(a, win_ref[:, 0:1024])
    g_q = g_ref[0:1, :]
    for h in range(C_HEADS):
        blk = _rope(_rms_halves(z[:, h * 128:(h + 1) * 128], lane) * g_q, cos64, sin64, 16, lane)
        qc_ref[0, :, h * 256:h * 256 + 128] = jnp.where(lane < 64, blk, 0.0).astype(BF16)
        qc_ref[0, :, h * 256 + 128:(h + 1) * 256] = jnp.where(lane < 64, 0.0, blk).astype(BF16)
    z = _dot(a, win_ref[:, 1024:2048])
    g_k = g_ref[1:2, :]
    for h in range(C_HEADS):
        blk = _rope(_rms_halves(z[:, h * 128:(h + 1) * 128], lane) * g_k, cos64, sin64, 16, lane)
        kc_ref[0, :, h * 128:(h + 1) * 128] = blk.astype(BF16)
    vc_ref[0] = _dot(a, win_ref[:, 2048:3072]).astype(BF16)

    z = _dot(a, win_ref[:, 3072:4096])
    g_q = g_ref[2:3, :]
    for h in range(D_HEADS):
        blk = _rms(z[:, h * 128:(h + 1) * 128], HEAD_DIM) * g_q
        qd_ref[0, :, h * 128:(h + 1) * 128] = _rope(blk, cos128, sin128, 32, lane).astype(BF16)
    z = _dot(a, win_ref[:, 4096:4608])
    g_k = g_ref[3:4, :]
    for h in range(D_KV_HEADS):
        blk = _rms(z[:, h * 128:(h + 1) * 128], HEAD_DIM) * g_k
        kd_ref[0, :, h * 128:(h + 1) * 128] = _rope(blk, cos128, sin128, 32, lane).astype(BF16)
    vd_ref[0] = z[:, 256:512].astype(BF16)

    @pl.when(step == n_steps - 1)
    def _():
        _wait_row_gathers(ys_hbm, ybuf.at[1 - cur], sems.at[1 - cur])


def _odd_proj_call(hn, ys, slots, rt, mod_prev, mod, tab, gains, win):
    bsz = hn.shape[0]
    widths = (C_HEADS * 2 * LANES, C_HEADS * LANES, C_HEADS * C_V,
              D_HEADS * HEAD_DIM, D_KV_HEADS * HEAD_DIM, D_KV_HEADS * HEAD_DIM)
    n_steps = bsz * N_ROW_TILES
    tile = lambda w: pl.BlockSpec((1, ROW_TILE, w), lambda b, j: (b, j, 0))
    mod_spec = pl.BlockSpec((1, 1, 6 * D_MODEL), lambda b, j: (_mod_row(b, j), 0, 0))
    slot_spec = lambda ahead: pl.BlockSpec(
        (1, TOP_K, ROW_TILE), lambda b, j: (jnp.minimum(b * N_ROW_TILES + j + ahead, n_steps - 1), 0, 0),
        memory_space=pltpu.SMEM)
    return pl.pallas_call(
        _odd_proj_kernel,
        grid=(bsz, N_ROW_TILES),
        in_specs=[
            slot_spec(0), slot_spec(1), tile(D_MODEL), tile(LANES), mod_spec, mod_spec,
            pl.BlockSpec((ROW_TILE, 768), lambda b, j: (j, 0)),
            _resident(gains.shape), _resident(win.shape),
            pl.BlockSpec(memory_space=pl.ANY),
        ],
        out_specs=[tile(D_MODEL)] + [tile(w) for w in widths],
        out_shape=[jax.ShapeDtypeStruct((bsz, TOK, D_MODEL), F32)]
                  + [jax.ShapeDtypeStruct((bsz, TOK, w), BF16) for w in widths],
        scratch_shapes=[pltpu.VMEM((2, TOP_K, ROW_TILE, PACKED_W), U32), pltpu.SemaphoreType.DMA((2,))],
        compiler_params=pltpu.CompilerParams(
            dimension_semantics=("arbitrary", "arbitrary"), vmem_limit_bytes=VMEM_LIMIT),
        name="odd_proj",
    )(slots, slots, hn, rt, mod_prev, mod, tab, gains, win, ys)


KEY_CHUNK = 768


def _attn_kernel(*refs, nkv, g, dk, dv, tq, n_keys, diff_lambda_init):
    if diff_lambda_init is None:
        q_ref, k_ref, v_ref, o_ref = refs
    else:
        q_ref, k_ref, v_ref, aux_ref, o_ref = refs
    for kv in range(nkv):
        q = jnp.concatenate(
            [q_ref[0, :, (kv * g + gi) * dk:(kv * g + gi + 1) * dk] for gi in range(g)], axis=0)
        m = l = acc = None
        for c0 in range(0, n_keys, KEY_CHUNK):
            c1 = min(c0 + KEY_CHUNK, n_keys)
            s = _dot_nt(q, k_ref[0, c0:c1, kv * dk:(kv + 1) * dk])
            v = v_ref[0, c0:c1, kv * dv:(kv + 1) * dv]
            m_c = jnp.max(s, axis=-1, keepdims=True)
            if m is None:
                m = m_c
                p = jnp.exp2((s - m).astype(BF16))
                l = jnp.sum(p.astype(F32), axis=-1, keepdims=True)
                acc = _dot(p, v)
            else:
                m_new = jnp.maximum(m, m_c)
                alpha = jnp.exp2(m - m_new)
                p = jnp.exp2((s - m_new).astype(BF16))
                l = alpha * l + jnp.sum(p.astype(F32), axis=-1, keepdims=True)
                acc = alpha * acc + _dot(p, v)
                m = m_new
        o = acc / l
        if diff_lambda_init is None:
            for gi in range(g):
                o_ref[0, :, (kv * g + gi) * dv:(kv * g + gi + 1) * dv] = o[gi * tq:(gi + 1) * tq].astype(BF16)
        else:
            lam = (jnp.exp(jnp.sum(aux_ref[0:1, :] * aux_ref[1:2, :], axis=-1, keepdims=True))
                   - jnp.exp(jnp.sum(aux_ref[2:3, :] * aux_ref[3:4, :], axis=-1, keepdims=True))
                   + diff_lambda_init)
            od = o[0:tq] - lam * o[tq:2 * tq]
            od = _rms(od, dv) * aux_ref[4:5, :] * (1.0 - diff_lambda_init)
            o_ref[0, :, kv * dv:(kv + 1) * dv] = od.astype(BF16)


def _attn_call(q, k, v, *, n_heads_kv, g, dk, dv, nkv, tq, q_tile0, n_q_tiles, key_tile0, n_keys,
               aux=None, diff_lambda_init=None, name="attn"):
    bsz = q.shape[0]
    n_out_heads = n_heads_kv * (g if diff_lambda_init is None else 1)
    out_w = nkv * (g if diff_lambda_init is None else 1) * dv
    kern = functools.partial(_attn_kernel, nkv=nkv, g=g, dk=dk, dv=dv, tq=tq, n_keys=n_keys,
                             diff_lambda_init=diff_lambda_init)
    in_specs = [
        pl.BlockSpec((1, tq, nkv * g * dk), lambda b, hh, qi: (b, qi + q_tile0, hh)),
        pl.BlockSpec((1, n_keys, nkv * dk), lambda b, hh, qi: (b, key_tile0, hh)),
        pl.BlockSpec((1, n_keys, nkv * dv), lambda b, hh, qi: (b, key_tile0, hh)),
    ]
    args = [q, k, v]
    if aux is not None:
        in_specs.append(pl.BlockSpec(aux.shape, lambda b, hh, qi: (0, 0)))
        args.append(aux)
    return pl.pallas_call(
        kern,
        grid=(bsz, n_heads_kv // nkv, n_q_tiles),
        in_specs=in_specs,
        out_specs=pl.BlockSpec((1, tq, out_w), lambda b, hh, qi: (b, qi, hh)),
        out_shape=jax.ShapeDtypeStruct((bsz, n_q_tiles * tq, n_out_heads * dv), BF16),
        compiler_params=pltpu.CompilerParams(
            dimension_semantics=("parallel", "parallel", "parallel"), vmem_limit_bytes=VMEM_LIMIT),
        name=name,
    )(*args)


WIN_TQ = 256
WIN_BAND = WIN_TQ + 2 * WINDOW
NEG_BIG = -1e30


WIN_Q_TILES = SEQ // WIN_TQ


def _window_band(tile):
    q0 = (tile % WIN_Q_TILES) * WIN_TQ
    return q0, pl.multiple_of(jnp.clip(q0 - WINDOW, 0, SEQ - WIN_BAND), WINDOW)


def _window_kernel(q_ref, k_ref, v_ref, sink_ref, o_ref, sloc0, sctx0, sloc1, sctx1, *, n_total):
    g = D_HEADS // D_KV_HEADS
    dk = HEAD_DIM
    s = pl.program_id(0)

    @pl.when(s == 0)
    def _():
        sloc1[...] = jnp.zeros_like(sloc1)
        sctx1[...] = jnp.zeros_like(sctx1)

    def step(write, read):
        sloc_w, sctx_w = write
        sloc_r, sctx_r = read
        q0, start = _window_band(jnp.minimum(s, n_total - 1))
        q = jnp.concatenate([q_ref[0, :, gi * dk:(gi + 1) * dk] for gi in range(g)], axis=0)
        row = (lax.broadcasted_iota(jnp.int32, (g * WIN_TQ, WIN_BAND), 0) & (WIN_TQ - 1)) + q0
        col = lax.broadcasted_iota(jnp.int32, (g * WIN_TQ, WIN_BAND), 1) + start
        sloc_w[...] = jnp.where(jnp.abs(row - col) <= WINDOW,
                                _dot_nt(q, k_ref[0, pl.ds(start, WIN_BAND), :]), NEG_BIG)
        sctx_w[...] = _dot_nt(q, k_ref[0, SEQ:TOK, :])

        _, start = _window_band(jnp.maximum(s - 1, 0))
        s_loc = sloc_r[...]
        s_ctx = sctx_r[...]
        sink = sink_ref[0]
        m = jnp.maximum(jnp.maximum(jnp.max(s_loc, axis=-1, keepdims=True),
                                    jnp.max(s_ctx, axis=-1, keepdims=True)), sink)
        p_loc = jnp.exp2(s_loc - m)
        p_ctx = jnp.exp2(s_ctx - m)
        l = (jnp.sum(p_loc, axis=-1, keepdims=True) + jnp.sum(p_ctx, axis=-1, keepdims=True)
             + jnp.exp2(sink - m))
        acc = (_dot(p_ctx.astype(BF16), v_ref[0, SEQ:TOK, :])
               + _dot(p_loc.astype(BF16), v_ref[0, pl.ds(start, WIN_BAND), :]))
        o = acc / l
        for gi in range(g):
            o_ref[0, :, gi * dk:(gi + 1) * dk] = o[gi * WIN_TQ:(gi + 1) * WIN_TQ].astype(BF16)

    @pl.when(s % 2 == 0)
    def _():
        step((sloc0, sctx0), (sloc1, sctx1))

    @pl.when(s % 2 == 1)
    def _():
        step((sloc1, sctx1), (sloc0, sctx0))


def _window_call(q, k, v, sink_col):
    bsz = q.shape[0]
    g = D_HEADS // D_KV_HEADS
    n_total = bsz * D_KV_HEADS * WIN_Q_TILES
    split = lambda t: (t // (D_KV_HEADS * WIN_Q_TILES), (t // WIN_Q_TILES) % D_KV_HEADS, t % WIN_Q_TILES)
    stage_a = lambda s: split(jnp.minimum(s, n_total - 1))
    stage_b = lambda s: split(jnp.maximum(s - 1, 0))

    def q_index(s):
        b, hh, qi = stage_a(s)
        return (b, qi, hh)

    def k_index(s):
        b, hh, _ = stage_a(s)
        return (b, 0, hh)

    def v_index(s):
        b, hh, _ = stage_b(s)
        return (b, 0, hh)

    def o_index(s):
        b, hh, qi = stage_b(s)
        return (b, qi, hh)

    m_rows = g * WIN_TQ
    return pl.pallas_call(
        functools.partial(_window_kernel, n_total=n_total),
        grid=(n_total + 1,),
        in_specs=[
            pl.BlockSpec((1, WIN_TQ, g * HEAD_DIM), q_index),
            pl.BlockSpec((1, TOK, HEAD_DIM), k_index),
            pl.BlockSpec((1, TOK, HEAD_DIM), v_index),
            pl.BlockSpec((1, m_rows, 1), lambda s: (stage_b(s)[1], 0, 0)),
        ],
        out_specs=pl.BlockSpec((1, WIN_TQ, g * HEAD_DIM), o_index),
        out_shape=jax.ShapeDtypeStruct((bsz, SEQ, D_HEADS * HEAD_DIM), BF16),
        scratch_shapes=[pltpu.VMEM((m_rows, WIN_BAND), F32), pltpu.VMEM((m_rows, CTX_LEN), F32),
                        pltpu.VMEM((m_rows, WIN_BAND), F32), pltpu.VMEM((m_rows, CTX_LEN), F32)],
        compiler_params=pltpu.CompilerParams(
            dimension_semantics=("arbitrary",), vmem_limit_bytes=VMEM_LIMIT),
        name="window_attn",
    )(q, k, v, sink_col)


def _route(logits, lane):
    lane_f = lane.astype(F32)
    lg = jnp.where(lane < N_GROUPS, logits, NEG_BIG)
    g_max = jnp.max(lg, axis=-1, keepdims=True)
    p_top = 1.0 / jnp.sum(jnp.exp(lg - g_max), axis=-1, keepdims=True)
    g_idx = jnp.min(jnp.where(lg == g_max, lane_f, float(LANES)), axis=-1, keepdims=True)
    e_lane = lane - N_GROUPS
    in_group = (e_lane >= 0) & (e_lane < N_EXPERTS) & ((e_lane // EXPERTS_PER_GROUP).astype(F32) == g_idx)
    le = jnp.where(in_group, logits, NEG_BIG)
    m1 = jnp.max(le, axis=-1, keepdims=True)
    i1 = jnp.min(jnp.where(le == m1, lane_f, float(LANES)), axis=-1, keepdims=True)
    le2 = jnp.where(lane_f == i1, NEG_BIG, le)
    m2 = jnp.max(le2, axis=-1, keepdims=True)
    i2 = jnp.min(jnp.where(le2 == m2, lane_f, float(LANES)), axis=-1, keepdims=True)
    t = jnp.exp(m2 - m1)
    w1 = p_top / (1.0 + t)
    w2 = p_top * t / (1.0 + t)
    return i1, i2, w1, w2


def _outproj_kernel(*refs, has_ctx, n_tiles, n_total):
    if has_ctx:
        (o1_ref, o2_ref, h_ref, o1c_ref, o2c_ref, hc_ref, mod_ref, w1_ref, w2_ref, rw_ref, rb_ref,
         hn_ref, fx_ref, rt_ref, cnt_ref, run_ref, logit_ref) = refs
    else:
        (o1_ref, o2_ref, h_ref, mod_ref, w1_ref, w2_ref, rw_ref, rb_ref,
         hn_ref, fx_ref, rt_ref, cnt_ref, run_ref, logit_ref) = refs
    d = D_MODEL
    s = pl.program_id(0)

    @pl.when(s == 0)
    def _():
        run_ref[...] = jnp.zeros_like(run_ref)
        logit_ref[...] = jnp.zeros_like(logit_ref)

    logits = logit_ref[...]
    lane = lax.broadcasted_iota(jnp.int32, (ROW_TILE, LANES), 1)
    lane_f = lane.astype(F32)
    i1, i2, w1, w2 = _route(logits, lane)

    o1, o2, h = o1_ref[0], o2_ref[0], h_ref[0]
    if has_ctx:
        is_ctx = jnp.minimum(s, n_total - 1) % n_tiles == CTX_TILE
        o1 = jnp.where(is_ctx, o1c_ref[0], o1)
        o2 = jnp.where(is_ctx, o2c_ref[0], o2)
        h = jnp.where(is_ctx, hc_ref[0], h)
    acc = _dot(o1, w1_ref[...]) + _dot(o2, w2_ref[...])
    hn = h + mod_ref[0, :, 2 * d:3 * d] * acc
    hn_ref[0] = hn
    fx = _modulate(hn, mod_ref[0, :, 3 * d:4 * d], mod_ref[0, :, 4 * d:5 * d])
    fx_ref[0] = _pack_bf16_pairs(fx)
    hi = fx.astype(BF16)
    lo = (fx - hi.astype(F32)).astype(BF16)
    part = _dot(jnp.concatenate([hi, lo], axis=0), rw_ref[...])
    logit_ref[...] = (part[:ROW_TILE, :LANES] + part[:ROW_TILE, LANES:]
                      + part[ROW_TILE:, :LANES] + part[ROW_TILE:, LANES:] + rb_ref[...])

    hit1, hit2 = lane_f == i1, lane_f == i2
    multi_hot = jnp.where(s > 0, jnp.where(hit1, 1.0, 0.0) + jnp.where(hit2, 1.0, 0.0), 0.0)
    tri = (lax.broadcasted_iota(jnp.int32, (ROW_TILE, ROW_TILE), 0)
           > lax.broadcasted_iota(jnp.int32, (ROW_TILE, ROW_TILE), 1))
    before = _dot(jnp.where(tri, 1.0, 0.0).astype(BF16), multi_hot.astype(BF16)) + run_ref[...]
    r1 = jnp.sum(jnp.where(hit1, before, 0.0), axis=-1, keepdims=True)
    r2 = jnp.sum(jnp.where(hit2, before, 0.0), axis=-1, keepdims=True)
    run_ref[...] = run_ref[...] + jnp.sum(multi_hot, axis=0, keepdims=True)
    cnt_ref[...] = jnp.broadcast_to(run_ref[...], cnt_ref.shape)

    out = jnp.where(lane == 0, i1 - N_GROUPS, 0.0)
    out = jnp.where(lane == 1, i2 - N_GROUPS, out)
    out = jnp.where(lane == 2, w1, out)
    out = jnp.where(lane == 3, w2, out)
    out = jnp.where(lane == 4, r1, out)
    out = jnp.where(lane == 5, r2, out)
    rt_ref[0] = out


def _outproj_call(lat, ctx, mod, w1, w2, rw, rb, n_tiles):
    bsz = lat[2].shape[0]
    has_ctx = ctx is not None
    n_total = bsz * n_tiles
    stage_a = lambda s: jnp.minimum(s, n_total - 1)
    stage_b = lambda s: jnp.maximum(s - 1, 0)
    bj = lambda t: (t // n_tiles, t % n_tiles)

    def a_tile(w, kind):
        def index(s):
            b, j = bj(stage_a(s))
            return (b, {"all": j, "lat": jnp.minimum(j, CTX_TILE - 1), "ctx": 0}[kind], 0)
        return pl.BlockSpec((1, ROW_TILE, w), index)

    def b_tile(w):
        return pl.BlockSpec((1, ROW_TILE, w), lambda s: (*bj(stage_b(s)), 0))

    in_specs = [a_tile(a.shape[-1], "lat") for a in lat]
    args = list(lat)
    if has_ctx:
        in_specs += [a_tile(a.shape[-1], "ctx") for a in ctx]
        args += list(ctx)
    in_specs += [
        pl.BlockSpec((1, 1, 6 * D_MODEL), lambda s: (_mod_row(*bj(stage_a(s))), 0, 0)),
        _resident(w1.shape), _resident(w2.shape), _resident(rw.shape), _resident(rb.shape),
    ]
    args += [mod, w1, w2, rw, rb]
    n_rows = n_tiles * ROW_TILE
    return pl.pallas_call(
        functools.partial(_outproj_kernel, has_ctx=has_ctx, n_tiles=n_tiles, n_total=n_total),
        grid=(n_total + 1,),
        in_specs=in_specs,
        out_specs=[a_tile(D_MODEL, "all"), a_tile(PACKED_W, "all"), b_tile(LANES),
                   pl.BlockSpec((8, LANES), lambda s: (0, 0))],
        out_shape=[jax.ShapeDtypeStruct((bsz, n_rows, D_MODEL), F32),
                   jax.ShapeDtypeStruct((bsz, n_rows, PACKED_W), U32),
                   jax.ShapeDtypeStruct((bsz, n_rows, LANES), F32),
                   jax.ShapeDtypeStruct((8, LANES), F32)],
        scratch_shapes=[pltpu.VMEM((1, LANES), F32), pltpu.VMEM((ROW_TILE, LANES), F32)],
        compiler_params=pltpu.CompilerParams(
            dimension_semantics=("arbitrary",), vmem_limit_bytes=VMEM_LIMIT),
        name="outproj_router",
    )(*args)


def _dispatch_kernel(lo_ref, hi_ref, slot_ref, fx_ref, xs_hbm, zbuf, sem, zsem):
    b, j = pl.program_id(0), pl.program_id(1)

    for r in range(ROW_TILE):
        for k in range(TOP_K):
            pltpu.make_async_copy(fx_ref.at[0, pl.ds(r, 1)], xs_hbm.at[pl.ds(slot_ref[0, k, r], 1)],
                                  sem).start(priority=k % 2)

    @pl.when((b == pl.num_programs(0) - 1) & (j == pl.num_programs(1) - 1))
    def _():
        zbuf[...] = jnp.zeros_like(zbuf)

        def fill(s, carry):
            pltpu.make_async_copy(zbuf.at[pl.ds(0, 1)], xs_hbm.at[pl.ds(s, 1)], zsem).start()
            return carry

        def drain(s, carry):
            pltpu.make_async_copy(zbuf.at[pl.ds(0, 1)], xs_hbm.at[pl.ds(0, 1)], zsem).wait()
            return carry
        for e in range(N_EXPERTS):
            lax.fori_loop(lo_ref[e], hi_ref[e], fill, 0)
        for e in range(N_EXPERTS):
            lax.fori_loop(lo_ref[e], hi_ref[e], drain, 0)

    for k in range(TOP_K):
        pltpu.make_async_copy(fx_ref.at[0], xs_hbm.at[pl.ds(0, ROW_TILE)], sem).wait()


def _dispatch_call(fx, slots, fill_lo, fill_hi, n_blocks):
    bsz, n_rows, _ = fx.shape
    n_tiles = n_rows // ROW_TILE
    grid_spec = pltpu.PrefetchScalarGridSpec(
        num_scalar_prefetch=2,
        grid=(bsz, n_tiles),
        in_specs=[
            pl.BlockSpec((1, TOP_K, ROW_TILE), lambda b, j, lo, hi: (b * n_tiles + j, 0, 0),
                         memory_space=pltpu.SMEM),
            pl.BlockSpec((1, ROW_TILE, PACKED_W), lambda b, j, lo, hi: (b, j, 0)),
        ],
        out_specs=pl.BlockSpec(memory_space=pl.ANY),
        scratch_shapes=[pltpu.VMEM((8, PACKED_W), U32), pltpu.SemaphoreType.DMA(()), pltpu.SemaphoreType.DMA(())],
    )
    return pl.pallas_call(
        _dispatch_kernel,
        grid_spec=grid_spec,
        out_shape=jax.ShapeDtypeStruct((n_blocks * MOE_BLOCK, PACKED_W), U32),
        compiler_params=pltpu.CompilerParams(
            dimension_semantics=("arbitrary", "arbitrary"), vmem_limit_bytes=VMEM_LIMIT),
        name="moe_dispatch",
    )(fill_lo, fill_hi, slots, fx)


CAST_ROWS = 256


def _moe_kernel(be_ref, nu_ref, nx_ref, x_ref, wg_hbm, wu_hbm, wd_hbm, y_ref,
                sg, su, sd, wg, wu, wd, sems, *, layer):
    i = pl.program_id(0)
    e = be_ref[i]
    used = i < nu_ref[0]
    staged = ((wg_hbm, sg, wg, 0), (wu_hbm, su, wu, 1), (wd_hbm, sd, wd, 2))

    def fetch(expert):
        for hbm, stage, _, s in staged:
            pltpu.make_async_copy(hbm.at[layer, expert], stage, sems.at[s]).start(priority=1)

    @pl.when(i == 0)
    def _():
        fetch(e)

    first_of_expert = (i == 0) | (e != be_ref[jnp.maximum(i - 1, 0)])

    @pl.when(used & first_of_expert)
    def _():
        for hbm, stage, dst, s in staged:
            pltpu.make_async_copy(hbm.at[layer, 0], stage, sems.at[s]).wait()

            def cast(c, carry):
                rows = pl.ds(pl.multiple_of(c * CAST_ROWS, CAST_ROWS), CAST_ROWS)
                dst[rows, :] = stage[rows, :].astype(BF16)
                return carry
            lax.fori_loop(0, stage.shape[0] // CAST_ROWS, cast, 0)
        nxt = nx_ref[e]

        @pl.when(nxt >= 0)
        def _():
            fetch(nxt)

    @pl.when(used)
    def _():
        x = _unpack_bf16_pairs(x_ref[...]).astype(BF16)
        gt = _dot(x, wg[...])
        up = _dot(x, wu[...])
        u = (_silu(gt) * up).astype(BF16)
        y_ref[...] = _pack_bf16_pairs(_dot(u, wd[...]))


def _moe_call(xs, block_expert, n_used, next_expert, wg, wu, wd, layer):
    n_blocks = block_expert.shape[0]
    row_blk = pl.BlockSpec((MOE_BLOCK, PACKED_W), lambda i, be, nu, nx: (jnp.minimum(i, nu[0] - 1), 0))
    hbm = pl.BlockSpec(memory_space=pl.ANY)
    grid_spec = pltpu.PrefetchScalarGridSpec(
        num_scalar_prefetch=3,
        grid=(n_blocks,),
        in_specs=[row_blk, hbm, hbm, hbm],
        out_specs=row_blk,
        scratch_shapes=[
            pltpu.VMEM((D_MODEL, D_EXPERT), F32), pltpu.VMEM((D_MODEL, D_EXPERT), F32),
            pltpu.VMEM((D_EXPERT, D_MODEL), F32),
            pltpu.VMEM((D_MODEL, D_EXPERT), BF16), pltpu.VMEM((D_MODEL, D_EXPERT), BF16),
            pltpu.VMEM((D_EXPERT, D_MODEL), BF16),
            pltpu.SemaphoreType.DMA((3,)),
        ],
    )
    return pl.pallas_call(
        functools.partial(_moe_kernel, layer=layer),
        grid_spec=grid_spec,
        out_shape=jax.ShapeDtypeStruct((n_blocks * MOE_BLOCK, PACKED_W), U32),
        compiler_params=pltpu.CompilerParams(
            dimension_semantics=("arbitrary",), vmem_limit_bytes=VMEM_LIMIT),
        name="moe_experts",
    )(block_expert, n_used, next_expert, xs, wg, wu, wd)


def _combine_kernel(slot_ref, slot_next_ref, h_ref, rt_ref, mod_ref, ys_hbm, o_ref, ybuf, sems):
    d = D_MODEL
    n_steps = pl.num_programs(0) * pl.num_programs(1)
    step = pl.program_id(0) * pl.num_programs(1) + pl.program_id(1)
    cur = step % 2

    @pl.when(step == 0)
    def _():
        _start_row_gathers(slot_ref, ys_hbm, ybuf.at[0], sems.at[0])

    _start_row_gathers(slot_next_ref, ys_hbm, ybuf.at[1 - cur], sems.at[1 - cur])
    _wait_row_gathers(ys_hbm, ybuf.at[cur], sems.at[cur])
    rt = rt_ref[0]
    y = rt[:, 2:3] * _unpack_bf16_pairs(ybuf[cur, 0]) + rt[:, 3:4] * _unpack_bf16_pairs(ybuf[cur, 1])
    o_ref[0] = h_ref[0] + mod_ref[0, :, 5 * d:6 * d] * y

    @pl.when(step == n_steps - 1)
    def _():
        _wait_row_gathers(ys_hbm, ybuf.at[1 - cur], sems.at[1 - cur])


def _combine_call(h, ys, slots, rt, mod, n_tiles):
    bsz = h.shape[0]
    n_steps = bsz * n_tiles
    tile = lambda w: pl.BlockSpec((1, ROW_TILE, w), lambda b, j: (b, j, 0))
    slot_spec = lambda ahead: pl.BlockSpec(
        (1, TOP_K, ROW_TILE), lambda b, j: (jnp.minimum(b * n_tiles + j + ahead, n_steps - 1), 0, 0),
        memory_space=pltpu.SMEM)
    return pl.pallas_call(
        _combine_kernel,
        grid=(bsz, n_tiles),
        in_specs=[
            slot_spec(0), slot_spec(1),
            tile(D_MODEL), tile(LANES),
            pl.BlockSpec((1, 1, 6 * D_MODEL), lambda b, j: (_mod_row(b, j), 0, 0)),
            pl.BlockSpec(memory_space=pl.ANY),
        ],
        out_specs=tile(D_MODEL),
        out_shape=jax.ShapeDtypeStruct((bsz, n_tiles * ROW_TILE, D_MODEL), F32),
        scratch_shapes=[pltpu.VMEM((2, TOP_K, ROW_TILE, PACKED_W), U32), pltpu.SemaphoreType.DMA((2,))],
        compiler_params=pltpu.CompilerParams(
            dimension_semantics=("arbitrary", "arbitrary"), vmem_limit_bytes=VMEM_LIMIT),
        name="moe_combine",
    )(slots, slots, h, rt, mod, ys)


def _rope_tables():
    rows = SEQ // GRID_W
    row = jnp.repeat(jnp.arange(rows), GRID_W).astype(F32)
    col = jnp.tile(jnp.arange(GRID_W), rows).astype(F32)

    def tables(dim):
        n_freq = dim // 4
        inv = 1.0 / (ROPE_THETA ** (jnp.arange(n_freq, dtype=F32) / n_freq))
        ang_r = row[:, None] * inv
        ang_c = col[:, None] * inv
        ang = jnp.concatenate([ang_r, ang_r, ang_c, ang_c], axis=-1)
        sign = jnp.tile(jnp.concatenate([-jnp.ones((n_freq,), F32), jnp.ones((n_freq,), F32)]), 2)
        return jnp.cos(ang), jnp.sin(ang) * sign

    cos128, sin128 = tables(HEAD_DIM)
    cos64, sin64 = tables(B_ROPE)
    ones, zeros = jnp.ones((SEQ, 64), F32), jnp.zeros((SEQ, 64), F32)
    lat = jnp.concatenate([cos128, sin128,
                           cos64, cos64, sin64, sin64,
                           cos64, ones, sin64, zeros], axis=-1)
    ident = jnp.concatenate([jnp.ones((CTX_LEN, 128), F32), jnp.zeros((CTX_LEN, 128), F32)], axis=-1)
    return jnp.concatenate([lat, jnp.tile(ident, (1, 3))], axis=0)


def _pad_lanes(v, n):
    return jnp.pad(v, (0, n - v.shape[0]))


def _router_weights(wr_g, br_g, wr_e, br_e):
    w = jnp.pad(jnp.concatenate([wr_g, wr_e], axis=1), ((0, 0), (0, LANES - N_GROUPS - N_EXPERTS)))
    hi = w.astype(BF16)
    lo = (w - hi.astype(F32)).astype(BF16)
    b = _pad_lanes(jnp.concatenate([br_g, br_e]), LANES).reshape(1, LANES)
    return jnp.concatenate([hi, lo], axis=1), b


def _moe_experts(fx, rt, cnt, wg, wu, wd, layer):
    bsz, n_rows, _ = fx.shape
    n_tiles = n_rows // ROW_TILE
    n_blocks = -(-(bsz * n_rows * TOP_K) // MOE_BLOCK) + N_EXPERTS
    experts = jnp.arange(N_EXPERTS, dtype=jnp.int32)
    counts = cnt[0, N_GROUPS:N_GROUPS + N_EXPERTS].astype(jnp.int32)
    padded = (counts + MOE_BLOCK - 1) // MOE_BLOCK * MOE_BLOCK
    pad_end = jnp.cumsum(padded)
    pad_start = pad_end - padded
    e_idx = rt[..., 0:TOP_K].astype(jnp.int32)
    rank = rt[..., 4:4 + TOP_K].astype(jnp.int32)
    slot = rank + jnp.sum(jnp.where(e_idx[..., None] == experts, pad_start, 0), axis=-1)
    slots = slot.reshape(bsz * n_tiles, ROW_TILE, TOP_K).transpose(0, 2, 1)
    block_start = jnp.arange(n_blocks, dtype=jnp.int32) * MOE_BLOCK
    block_expert = jnp.minimum(jnp.sum(pad_end[None, :] <= block_start[:, None], axis=1), N_EXPERTS - 1)
    n_used = pad_end[-1:] // MOE_BLOCK
    later = (experts[None, :] > experts[:, None]) & (counts[None, :] > 0)
    next_expert = jnp.where(jnp.any(later, axis=1), jnp.argmax(later, axis=1), -1)
    i32 = lambda a: a.astype(jnp.int32)
    xs = _dispatch_call(fx, i32(slots), i32(pad_start + counts), i32(pad_end), n_blocks)
    ys = _moe_call(xs, i32(block_expert), i32(n_used), i32(next_expert), wg, wu, wd, layer)
    return ys, i32(slots)


def kernel(x, c, ctx, c_ctx, mod_w, mod_b, even_w_in, even_w_out, a_q_norm, a_k_norm, b_cq_norm, b_w_uq,
           b_ckv_norm, b_w_ukv, b_q_norm, b_k_norm, odd_w_in, odd_w_out, c_q_norm, c_k_norm, c_lambda_q1,
           c_lambda_k1, c_lambda_q2, c_lambda_k2, c_subln, d_q_norm, d_k_norm, d_sink, moe_wr_group,
           moe_br_group, moe_wr_expert, moe_br_expert, moe_w_gate, moe_w_up, moe_w_down):
    bsz = x.shape[0]
    c_all = jnp.zeros((MOD_ROWS, D_MODEL), F32).at[:bsz].set(c).at[MOD_CTX_ROW].set(c_ctx)
    mod_all = _mod_call(c_all, mod_w, mod_b)
    tab = _rope_tables()

    i = 0
    mod = mod_all[0].reshape(MOD_ROWS, 1, 6 * D_MODEL)
    scale_a = HEAD_DIM ** -0.5 * LOG2_E
    scale_b = (B_NOPE + B_ROPE) ** -0.5 * LOG2_E
    win = jnp.pad(even_w_in[i], ((0, 0), (0, EVEN_IN_PAD - even_w_in.shape[-1]))).astype(BF16)
    wuq = b_w_uq[i].reshape(B_Q_LORA, B_HEADS, B_NOPE + B_ROPE)
    wuq = jnp.pad(wuq, ((0, 0), (0, 0), (0, B_QK_PAD - B_NOPE - B_ROPE))).reshape(B_Q_LORA, -1).astype(BF16)
    wukv = b_w_ukv[i].reshape(B_KV_LORA, B_HEADS, B_NOPE + B_V)
    wukv = jnp.concatenate([wukv[:, :, :B_NOPE].reshape(B_KV_LORA, -1),
                            wukv[:, :, B_NOPE:].reshape(B_KV_LORA, -1)], axis=1).astype(BF16)
    gains = jnp.stack([
        _pad_lanes(a_q_norm[i] * scale_a, 512), _pad_lanes(a_k_norm[i], 512),
        b_cq_norm[i], b_ckv_norm[i],
        _pad_lanes(b_q_norm[i] * scale_b, 512), _pad_lanes(b_k_norm[i], 512),
        jnp.zeros((512,), F32), jnp.zeros((512,), F32)])
    qa, ka, va, qb, kb, vb = _even_proj_call(x, ctx, mod, tab, gains, win, wuq, wukv)

    g_a = A_HEADS // A_KV_HEADS
    lat = dict(q_tile0=0, key_tile0=0, n_keys=TOK)
    oa = _attn_call(qa, ka, va, n_heads_kv=A_KV_HEADS, g=g_a, dk=HEAD_DIM, dv=HEAD_DIM, nkv=2,
                    tq=256, n_q_tiles=SEQ // 256, name="gqa_latent", **lat)
    ob = _attn_call(qb, kb, vb, n_heads_kv=B_HEADS, g=1, dk=B_QK_PAD, dv=B_V, nkv=2,
                    tq=512, n_q_tiles=SEQ // 512, name="mla_latent", **lat)
    cx = dict(tq=CTX_LEN, q_tile0=SEQ // CTX_LEN, n_q_tiles=1, key_tile0=SEQ // CTX_LEN, n_keys=CTX_LEN)
    oa_c = _attn_call(qa, ka, va, n_heads_kv=A_KV_HEADS, g=g_a, dk=HEAD_DIM, dv=HEAD_DIM, nkv=2,
                      name="gqa_context", **cx)
    ob_c = _attn_call(qb, kb, vb, n_heads_kv=B_HEADS, g=1, dk=B_QK_PAD, dv=B_V, nkv=2,
                      name="mla_context", **cx)

    w_out = even_w_out[i].astype(BF16)
    rw, rb = _router_weights(moe_wr_group[0], moe_br_group[0], moe_wr_expert[0], moe_br_expert[0])
    hn, fx, rt, cnt = _outproj_call((oa, ob, x), (oa_c, ob_c, ctx), mod,
                                    w_out[:A_HEADS * HEAD_DIM], w_out[A_HEADS * HEAD_DIM:],
                                    rw, rb, N_ROW_TILES)
    ys, slots = _moe_experts(fx, rt, cnt, moe_w_gate, moe_w_up, moe_w_down, 0)
    hn0, rt0, mod0 = hn, rt, mod

    layer = 1
    mod = mod_all[1].reshape(MOD_ROWS, 1, 6 * D_MODEL)
    lambda_init = 0.8 - 0.6 * math.exp(-0.3 * layer)
    scale_c = C_HD ** -0.5 * LOG2_E
    scale_d = HEAD_DIM ** -0.5 * LOG2_E
    win = odd_w_in[i].astype(BF16)
    gains = jnp.stack([
        jnp.tile(c_q_norm[i] * scale_c, 2), jnp.tile(c_k_norm[i], 2),
        d_q_norm[i] * scale_d, d_k_norm[i],
        jnp.zeros((128,), F32), jnp.zeros((128,), F32), jnp.zeros((128,), F32), jnp.zeros((128,), F32)])
    h, qc, kc, vc, qd, kd, vd = _odd_proj_call(hn0, ys, slots, rt0, mod0, mod, tab, gains, win)

    aux = jnp.stack([
        _pad_lanes(c_lambda_q1[i], 128), _pad_lanes(c_lambda_k1[i], 128),
        _pad_lanes(c_lambda_q2[i], 128), _pad_lanes(c_lambda_k2[i], 128),
        c_subln[i], jnp.zeros((128,), F32), jnp.zeros((128,), F32), jnp.zeros((128,), F32)])
    oc = _attn_call(qc, kc, vc, n_heads_kv=C_HEADS, g=2, dk=LANES, dv=C_V, nkv=2,
                    tq=512, n_q_tiles=SEQ // 512, aux=aux, diff_lambda_init=lambda_init,
                    name="diff_latent", **lat)
    g_d = D_HEADS // D_KV_HEADS
    sink_col = jnp.repeat((d_sink[i] * LOG2_E).reshape(D_KV_HEADS, g_d), WIN_TQ, axis=1)
    sink_col = sink_col.reshape(D_KV_HEADS, g_d * WIN_TQ, 1)
    od = _window_call(qd, kd, vd, sink_col)

    w_out = odd_w_out[i].astype(BF16)
    rw, rb = _router_weights(moe_wr_group[1], moe_br_group[1], moe_wr_expert[1], moe_br_expert[1])
    n_lat_tiles = SEQ // ROW_TILE
    hn, fx, rt, cnt = _outproj_call((oc, od, h), None, mod, w_out[:C_HEADS * C_V], w_out[C_HEADS * C_V:],
                                    rw, rb, n_lat_tiles)
    ys, slots = _moe_experts(fx, rt, cnt, moe_w_gate, moe_w_up, moe_w_down, 1)
    return _combine_call(hn, ys, slots, rt, mod, n_lat_tiles)
```

```python
import functools
import math

import jax
import jax.numpy as jnp
from jax import lax
from jax.experimental import pallas as pl
from jax.experimental.pallas import tpu as pltpu

F32 = jnp.float32
BF16 = jnp.bfloat16

D_MODEL = 2048
BATCH = 8
SEQ = 2048
DEPTH = 2
GRID_W = 64
CTX_LEN = 256
TOK = SEQ + CTX_LEN
HEAD_DIM = 128
ROPE_THETA = 10000.0
NORM_EPS = 1e-6
A_HEADS = 8
A_KV_HEADS = 2
B_HEADS = 8
B_Q_LORA = 512
B_KV_LORA = 512
B_NOPE = 128
B_ROPE = 64
B_V = 128
B_QK_PAD = 256
C_HEADS = 8
C_HD = 64
C_V = 128
D_HEADS = 8
D_KV_HEADS = 2
WINDOW = 128
N_GROUPS = 4
EXPERTS_PER_GROUP = 8
N_EXPERTS = 32
D_EXPERT = 1024
MOE_BLOCK = 256
TOP_K = 2
LANES = 128
LOG2_E = math.log2(math.e)

ROW_TILE = 256
N_ROW_TILES = TOK // ROW_TILE
CTX_TILE = SEQ // ROW_TILE
MOD_ROWS = 16
MOD_CTX_ROW = BATCH
VMEM_LIMIT = 56 * 1024 * 1024


def _dot(a, b):
    return jnp.dot(a, b, preferred_element_type=F32)


def _dot_nt(a, b):
    return lax.dot_general(a, b, (((1,), (1,)), ((), ())), preferred_element_type=F32)


def _silu(x):
    return x / (1.0 + jnp.exp(-x))


def _rms(x, n):
    return x * lax.rsqrt(jnp.sum(x * x, axis=-1, keepdims=True) * (1.0 / n) + NORM_EPS)


def _modulate(x, shift, scale):
    return _rms(x, x.shape[-1]) * (1.0 + scale) + shift


def _rope(x, cos, sin_signed, quarter, lane):
    fwd = pltpu.roll(x, LANES - quarter, axis=1)
    bwd = pltpu.roll(x, quarter, axis=1)
    rot = jnp.where((lane // quarter) % 2 == 0, fwd, bwd)
    return x * cos + rot * sin_signed


PACKED_W = D_MODEL // 2
U32 = jnp.uint32


def _pack_bf16_pairs(x):
    n = x.shape[-1] // 2
    lo = lax.bitcast_convert_type(x[:, :n].astype(BF16).astype(F32), U32) >> 16
    hi = lax.bitcast_convert_type(x[:, n:].astype(BF16).astype(F32), U32) & U32(0xFFFF0000)
    return lo | hi


def _unpack_bf16_pairs(p):
    lo = lax.bitcast_convert_type(p << 16, F32)
    hi = lax.bitcast_convert_type(p & U32(0xFFFF0000), F32)
    return jnp.concatenate([lo, hi], axis=-1)


def _start_row_gathers(slot_ref, src_hbm, dst_ref, sem):
    for r in range(ROW_TILE):
        for k in range(TOP_K):
            pltpu.make_async_copy(src_hbm.at[pl.ds(slot_ref[0, k, r], 1)], dst_ref.at[k, pl.ds(r, 1)],
                                  sem).start(priority=k % 2)


def _wait_row_gathers(src_hbm, dst_ref, sem):
    for k in range(TOP_K):
        pltpu.make_async_copy(src_hbm.at[pl.ds(0, ROW_TILE)], dst_ref.at[k], sem).wait()


MOD_TN = 1024


def _mod_kernel(c_ref, w_ref, b_ref, o_ref):
    a = _silu(c_ref[...]).astype(BF16)
    o_ref[0] = _dot(a, w_ref[0].astype(BF16)) + b_ref[0]


def _mod_call(c_all, mod_w, mod_b):
    d6 = 6 * D_MODEL
    return pl.pallas_call(
        _mod_kernel,
        grid=(DEPTH, d6 // MOD_TN),
        in_specs=[
            pl.BlockSpec((MOD_ROWS, D_MODEL), lambda l, n: (0, 0)),
            pl.BlockSpec((1, D_MODEL, MOD_TN), lambda l, n: (l, 0, n)),
            pl.BlockSpec((1, 1, MOD_TN), lambda l, n: (l, 0, n)),
        ],
        out_specs=pl.BlockSpec((1, MOD_ROWS, MOD_TN), lambda l, n: (l, 0, n)),
        out_shape=jax.ShapeDtypeStruct((DEPTH, MOD_ROWS, d6), F32),
        compiler_params=pltpu.CompilerParams(
            dimension_semantics=("parallel", "parallel"), vmem_limit_bytes=VMEM_LIMIT),
        name="mod_vectors",
    )(c_all, mod_w, mod_b.reshape(DEPTH, 1, d6))


def _mod_row(b, j):
    return jnp.where(j == CTX_TILE, MOD_CTX_ROW, b)


def _resident(shape):
    nd = len(shape)
    return pl.BlockSpec(shape, lambda *_: (0,) * nd, pipeline_mode=pl.Buffered(1))


EVEN_IN_PAD = 2688


def _even_proj_kernel(hx_ref, hc_ref, mod_ref, tab_ref, g_ref, win_ref, wuq_ref, wukv_ref,
                      qa_ref, ka_ref, va_ref, qb_ref, kb_ref, vb_ref):
    d = D_MODEL
    x = jnp.where(pl.program_id(1) == CTX_TILE, hc_ref[0], hx_ref[0])
    a = _modulate(x, mod_ref[0, :, 0:d], mod_ref[0, :, d:2 * d]).astype(BF16)
    lane = lax.broadcasted_iota(jnp.int32, (ROW_TILE, LANES), 1)
    cos128, sin128 = tab_ref[:, 0:128], tab_ref[:, 128:256]
    cos64p, sin64p = tab_ref[:, 512:640], tab_ref[:, 640:768]

    z = _dot(a, win_ref[:, 0:1024])
    g_q = g_ref[0:1, 0:128]
    for h in range(A_HEADS):
        blk = _rms(z[:, h * 128:(h + 1) * 128], HEAD_DIM) * g_q
        qa_ref[0, :, h * 128:(h + 1) * 128] = _rope(blk, cos128, sin128, 32, lane).astype(BF16)
    z = _dot(a, win_ref[:, 1024:1536])
    g_k = g_ref[1:2, 0:128]
    for h in range(A_KV_HEADS):
        blk = _rms(z[:, h * 128:(h + 1) * 128], HEAD_DIM) * g_k
        ka_ref[0, :, h * 128:(h + 1) * 128] = _rope(blk, cos128, sin128, 32, lane).astype(BF16)
    va_ref[0] = z[:, 256:512].astype(BF16)

    z = _dot(a, win_ref[:, 1536:2048])
    cq = (_rms(z, B_Q_LORA) * g_ref[2:3, :]).astype(BF16)
    zq = _dot(cq, wuq_ref[...])
    gq0, gq1 = g_ref[4:5, 0:128], g_ref[4:5, 128:256]
    n_qk = float(B_NOPE + B_ROPE)
    for h in range(B_HEADS):
        b0 = zq[:, h * 256:h * 256 + 128]
        b1 = zq[:, h * 256 + 128:(h + 1) * 256]
        ss = jnp.sum(b0 * b0, axis=-1, keepdims=True) + jnp.sum(b1 * b1, axis=-1, keepdims=True)
        r = lax.rsqrt(ss * (1.0 / n_qk) + NORM_EPS)
        qb_ref[0, :, h * 256:h * 256 + 128] = (b0 * r * gq0).astype(BF16)
        qb_ref[0, :, h * 256 + 128:(h + 1) * 256] = _rope(b1 * r * gq1, cos64p, sin64p, 16, lane).astype(BF16)

    z = _dot(a, win_ref[:, 2048:2688])
    ckv = (_rms(z[:, 0:512], B_KV_LORA) * g_ref[3:4, :]).astype(BF16)
    kr = z[:, 512:640]
    ss_kr = jnp.sum(kr * kr, axis=-1, keepdims=True)
    kr_rot = _rope(kr * g_ref[5:6, 128:256], cos64p, sin64p, 16, lane)
    zkv = _dot(ckv, wukv_ref[...])
    vb_ref[0] = zkv[:, 1024:2048].astype(BF16)
    gk0 = g_ref[5:6, 0:128]
    for h in range(B_HEADS):
        kn = zkv[:, h * 128:(h + 1) * 128]
        ss = jnp.sum(kn * kn, axis=-1, keepdims=True) + ss_kr
        r = lax.rsqrt(ss * (1.0 / n_qk) + NORM_EPS)
        kb_ref[0, :, h * 256:h * 256 + 128] = (kn * r * gk0).astype(BF16)
        kb_ref[0, :, h * 256 + 128:(h + 1) * 256] = (kr_rot * r).astype(BF16)


def _even_proj_call(h_lat, h_ctx, mod, tab, gains, win, wuq, wukv):
    bsz = h_lat.shape[0]
    widths = (A_HEADS * HEAD_DIM, A_KV_HEADS * HEAD_DIM, A_KV_HEADS * HEAD_DIM,
              B_HEADS * B_QK_PAD, B_HEADS * B_QK_PAD, B_HEADS * B_V)
    tile = lambda w: pl.BlockSpec((1, ROW_TILE, w), lambda b, j: (b, j, 0))
    return pl.pallas_call(
        _even_proj_kernel,
        grid=(bsz, N_ROW_TILES),
        in_specs=[
            pl.BlockSpec((1, ROW_TILE, D_MODEL), lambda b, j: (b, jnp.minimum(j, CTX_TILE - 1), 0)),
            pl.BlockSpec((1, ROW_TILE, D_MODEL), lambda b, j: (b, 0, 0)),
            pl.BlockSpec((1, 1, 6 * D_MODEL), lambda b, j: (_mod_row(b, j), 0, 0)),
            pl.BlockSpec((ROW_TILE, 768), lambda b, j: (j, 0)),
            _resident(gains.shape), _resident(win.shape), _resident(wuq.shape), _resident(wukv.shape),
        ],
        out_specs=[tile(w) for w in widths],
        out_shape=[jax.ShapeDtypeStruct((bsz, TOK, w), BF16) for w in widths],
        compiler_params=pltpu.CompilerParams(
            dimension_semantics=("parallel", "parallel"), vmem_limit_bytes=VMEM_LIMIT),
        name="even_proj",
    )(h_lat, h_ctx, mod, tab, gains, win, wuq, wukv)


def _rms_halves(x, lane):
    x2 = x * x
    s_lo = jnp.sum(jnp.where(lane < 64, x2, 0.0), axis=-1, keepdims=True)
    s_hi = jnp.sum(jnp.where(lane < 64, 0.0, x2), axis=-1, keepdims=True)
    r = jnp.where(lane < 64, lax.rsqrt(s_lo * (1.0 / C_HD) + NORM_EPS), lax.rsqrt(s_hi * (1.0 / C_HD) + NORM_EPS))
    return x * r


def _odd_proj_kernel(slot_ref, slot_next_ref, hn_ref, rt_ref, mod_prev_ref, mod_ref, tab_ref, g_ref, win_ref,
                     ys_hbm, h_ref, qc_ref, kc_ref, vc_ref, qd_ref, kd_ref, vd_ref, ybuf, sems):
    d = D_MODEL
    n_steps = pl.num_programs(0) * pl.num_programs(1)
    step = pl.program_id(0) * pl.num_programs(1) + pl.program_id(1)
    cur = step % 2

    @pl.when(step == 0)
    def _():
        _start_row_gathers(slot_ref, ys_hbm, ybuf.at[0], sems.at[0])

    @pl.when(step < n_steps)
    def _():
        _start_row_gathers(slot_next_ref, ys_hbm, ybuf.at[1 - cur], sems.at[1 - cur])

    _wait_row_gathers(ys_hbm, ybuf.at[cur], sems.at[cur])
    rt = rt_ref[0]
    y = rt[:, 2:3] * _unpack_bf16_pairs(ybuf[cur, 0]) + rt[:, 3:4] * _unpack_bf16_pairs(ybuf[cur, 1])
    x = hn_ref[0] + mod_prev_ref[0, :, 5 * d:6 * d] * y
    h_ref[0] = x
    a = _modulate(x, mod_ref[0, :, 0:d], mod_ref[0, :, d:2 * d]).astype(BF16)
    lane = lax.broadcasted_iota(jnp.int32, (ROW_TILE, LANES), 1)
    cos128, sin128 = tab_ref[:, 0:128], tab_ref[:, 128:256]
    cos64, sin64 = tab_ref[:, 256:384], tab_ref[:, 384:512]

    z = _dot(a, win_ref[:, 0:1024])
    g_q = g_ref[0:1, :]
    for h in range(C_HEADS):
        blk = _rope(_rms_halves(z[:, h * 128:(h + 1) * 128], lane) * g_q, cos64, sin64, 16, lane)
        qc_ref[0, :, h * 256:h * 256 + 128] = jnp.where(lane < 64, blk, 0.0).astype(BF16)
        qc_ref[0, :, h * 256 + 128:(h + 1) * 256] = jnp.where(lane < 64, 0.0, blk).astype(BF16)
    z = _dot(a, win_ref[:, 1024:2048])
    g_k = g_ref[1:2, :]
    for h in range(C_HEADS):
        blk = _rope(_rms_halves(z[:, h * 128:(h + 1) * 128], lane) * g_k, cos64, sin64, 16, lane)
        kc_ref[0, :, h * 128:(h + 1) * 128] = blk.astype(BF16)
    vc_ref[0] = _dot(a, win_ref[:, 2048:3072]).astype(BF16)

    z = _dot(a, win_ref[:, 3072:4096])
    g_q = g_ref[2:3, :]
    for h in range(D_HEADS):
        blk = _rms(z[:, h * 128:(h + 1) * 128], HEAD_DIM) * g_q
        qd_ref[0, :, h * 128:(h + 1) * 128] = _rope(blk, cos128, sin128, 32, lane).astype(BF16)
    z = _dot(a, win_ref[:, 4096:4608])
    g_k = g_ref[3:4, :]
    for h in range(D_KV_HEADS):
        blk = _rms(z[:, h * 128:(h + 1) * 128], HEAD_DIM) * g_k
        kd_ref[0, :, h * 128:(h + 1) * 128] = _rope(blk, cos128, sin128, 32, lane).astype(BF16)
    vd_ref[0] = z[:, 256:512].astype(BF16)

    @pl.when(step == n_steps - 1)
    def _():
        _wait_row_gathers(ys_hbm, ybuf.at[1 - cur], sems.at[1 - cur])


def _odd_proj_call(hn, ys, slots, rt, mod_prev, mod, tab, gains, win):
    bsz = hn.shape[0]
    widths = (C_HEADS * 2 * LANES, C_HEADS * LANES, C_HEADS * C_V,
              D_HEADS * HEAD_DIM, D_KV_HEADS * HEAD_DIM, D_KV_HEADS * HEAD_DIM)
    n_steps = bsz * N_ROW_TILES
    tile = lambda w: pl.BlockSpec((1, ROW_TILE, w), lambda b, j: (b, j, 0))
    mod_spec = pl.BlockSpec((1, 1, 6 * D_MODEL), lambda b, j: (_mod_row(b, j), 0, 0))
    slot_spec = lambda ahead: pl.BlockSpec(
        (1, TOP_K, ROW_TILE), lambda b, j: (jnp.minimum(b * N_ROW_TILES + j + ahead, n_steps - 1), 0, 0),
        memory_space=pltpu.SMEM)
    return pl.pallas_call(
        _odd_proj_kernel,
        grid=(bsz, N_ROW_TILES),
        in_specs=[
            slot_spec(0), slot_spec(1), tile(D_MODEL), tile(LANES), mod_spec, mod_spec,
            pl.BlockSpec((ROW_TILE, 768), lambda b, j: (j, 0)),
            _resident(gains.shape), _resident(win.shape),
            pl.BlockSpec(memory_space=pl.ANY),
        ],
        out_specs=[tile(D_MODEL)] + [tile(w) for w in widths],
        out_shape=[jax.ShapeDtypeStruct((bsz, TOK, D_MODEL), F32)]
                  + [jax.ShapeDtypeStruct((bsz, TOK, w), BF16) for w in widths],
        scratch_shapes=[pltpu.VMEM((2, TOP_K, ROW_TILE, PACKED_W), U32), pltpu.SemaphoreType.DMA((2,))],
        compiler_params=pltpu.CompilerParams(
            dimension_semantics=("arbitrary", "arbitrary"), vmem_limit_bytes=VMEM_LIMIT),
        name="odd_proj",
    )(slots, slots, hn, rt, mod_prev, mod, tab, gains, win, ys)


KEY_CHUNK = 768


def _attn_kernel(*refs, nkv, g, dk, dv, tq, n_keys, diff_lambda_init):
    if diff_lambda_init is None:
        q_ref, k_ref, v_ref, o_ref = refs
    else:
        q_ref, k_ref, v_ref, aux_ref, o_ref = refs
    for kv in range(nkv):
        q = jnp.concatenate(
            [q_ref[0, :, (kv * g + gi) * dk:(kv * g + gi + 1) * dk] for gi in range(g)], axis=0)
        m = l = acc = None
        for c0 in range(0, n_keys, KEY_CHUNK):
            c1 = min(c0 + KEY_CHUNK, n_keys)
            s = _dot_nt(q, k_ref[0, c0:c1, kv * dk:(kv + 1) * dk])
            v = v_ref[0, c0:c1, kv * dv:(kv + 1) * dv]
            m_c = jnp.max(s, axis=-1, keepdims=True)
            if m is None:
                m = m_c
                p = jnp.exp2((s - m).astype(BF16))
                l = jnp.sum(p.astype(F32), axis=-1, keepdims=True)
                acc = _dot(p, v)
            else:
                m_new = jnp.maximum(m, m_c)
                alpha = jnp.exp2(m - m_new)
                p = jnp.exp2((s - m_new).astype(BF16))
                l = alpha * l + jnp.sum(p.astype(F32), axis=-1, keepdims=True)
                acc = alpha * acc + _dot(p, v)
                m = m_new
        o = acc / l
        if diff_lambda_init is None:
            for gi in range(g):
                o_ref[0, :, (kv * g + gi) * dv:(kv * g + gi + 1) * dv] = o[gi * tq:(gi + 1) * tq].astype(BF16)
        else:
            lam = (jnp.exp(jnp.sum(aux_ref[0:1, :] * aux_ref[1:2, :], axis=-1, keepdims=True))
                   - jnp.exp(jnp.sum(aux_ref[2:3, :] * aux_ref[3:4, :], axis=-1, keepdims=True))
                   + diff_lambda_init)
            od = o[0:tq] - lam * o[tq:2 * tq]
            od = _rms(od, dv) * aux_ref[4:5, :] * (1.0 - diff_lambda_init)
            o_ref[0, :, kv * dv:(kv + 1) * dv] = od.astype(BF16)


def _attn_call(q, k, v, *, n_heads_kv, g, dk, dv, nkv, tq, q_tile0, n_q_tiles, key_tile0, n_keys,
               aux=None, diff_lambda_init=None, name="attn"):
    bsz = q.shape[0]
    n_out_heads = n_heads_kv * (g if diff_lambda_init is None else 1)
    out_w = nkv * (g if diff_lambda_init is None else 1) * dv
    kern = functools.partial(_attn_kernel, nkv=nkv, g=g, dk=dk, dv=dv, tq=tq, n_keys=n_keys,
                             diff_lambda_init=diff_lambda_init)
    in_specs = [
        pl.BlockSpec((1, tq, nkv * g * dk), lambda b, hh, qi: (b, qi + q_tile0, hh)),
        pl.BlockSpec((1, n_keys, nkv * dk), lambda b, hh, qi: (b, key_tile0, hh)),
        pl.BlockSpec((1, n_keys, nkv * dv), lambda b, hh, qi: (b, key_tile0, hh)),
    ]
    args = [q, k, v]
    if aux is not None:
        in_specs.append(pl.BlockSpec(aux.shape, lambda b, hh, qi: (0, 0)))
        args.append(aux)
    return pl.pallas_call(
        kern,
        grid=(bsz, n_heads_kv // nkv, n_q_tiles),
        in_specs=in_specs,
        out_specs=pl.BlockSpec((1, tq, out_w), lambda b, hh, qi: (b, qi, hh)),
        out_shape=jax.ShapeDtypeStruct((bsz, n_q_tiles * tq, n_out_heads * dv), BF16),
        compiler_params=pltpu.CompilerParams(
            dimension_semantics=("parallel", "parallel", "parallel"), vmem_limit_bytes=VMEM_LIMIT),
        name=name,
    )(*args)


WIN_TQ = 256
WIN_BAND = WIN_TQ + 2 * WINDOW
NEG_BIG = -1e30


WIN_Q_TILES = SEQ // WIN_TQ


def _window_band(tile):
    q0 = (tile % WIN_Q_TILES) * WIN_TQ
    return q0, pl.multiple_of(jnp.clip(q0 - WINDOW, 0, SEQ - WIN_BAND), WINDOW)


def _window_kernel(q_ref, k_ref, v_ref, sink_ref, o_ref, sloc0, sctx0, sloc1, sctx1, *, n_total):
    g = D_HEADS // D_KV_HEADS
    dk = HEAD_DIM
    s = pl.program_id(0)

    @pl.when(s == 0)
    def _():
        sloc1[...] = jnp.zeros_like(sloc1)
        sctx1[...] = jnp.zeros_like(sctx1)

    def step(write, read):
        sloc_w, sctx_w = write
        sloc_r, sctx_r = read
        q0, start = _window_band(jnp.minimum(s, n_total - 1))
        q = jnp.concatenate([q_ref[0, :, gi * dk:(gi + 1) * dk] for gi in range(g)], axis=0)
        row = (lax.broadcasted_iota(jnp.int32, (g * WIN_TQ, WIN_BAND), 0) & (WIN_TQ - 1)) + q0
        col = lax.broadcasted_iota(jnp.int32, (g * WIN_TQ, WIN_BAND), 1) + start
        sloc_w[...] = jnp.where(jnp.abs(row - col) <= WINDOW,
                                _dot_nt(q, k_ref[0, pl.ds(start, WIN_BAND), :]), NEG_BIG)
        sctx_w[...] = _dot_nt(q, k_ref[0, SEQ:TOK, :])

        _, start = _window_band(jnp.maximum(s - 1, 0))
        s_loc = sloc_r[...]
        s_ctx = sctx_r[...]
        sink = sink_ref[0]
        m = jnp.maximum(jnp.maximum(jnp.max(s_loc, axis=-1, keepdims=True),
                                    jnp.max(s_ctx, axis=-1, keepdims=True)), sink)
        p_loc = jnp.exp2(s_loc - m)
        p_ctx = jnp.exp2(s_ctx - m)
        l = (jnp.sum(p_loc, axis=-1, keepdims=True) + jnp.sum(p_ctx, axis=-1, keepdims=True)
             + jnp.exp2(sink - m))
        acc = (_dot(p_ctx.astype(BF16), v_ref[0, SEQ:TOK, :])
               + _dot(p_loc.astype(BF16), v_ref[0, pl.ds(start, WIN_BAND), :]))
        o = acc / l
        for gi in range(g):
            o_ref[0, :, gi * dk:(gi + 1) * dk] = o[gi * WIN_TQ:(gi + 1) * WIN_TQ].astype(BF16)

    @pl.when(s % 2 == 0)
    def _():
        step((sloc0, sctx0), (sloc1, sctx1))

    @pl.when(s % 2 == 1)
    def _():
        step((sloc1, sctx1), (sloc0, sctx0))


def _window_call(q, k, v, sink_col):
    bsz = q.shape[0]
    g = D_HEADS // D_KV_HEADS
    n_total = bsz * D_KV_HEADS * WIN_Q_TILES
    split = lambda t: (t // (D_KV_HEADS * WIN_Q_TILES), (t // WIN_Q_TILES) % D_KV_HEADS, t % WIN_Q_TILES)
    stage_a = lambda s: split(jnp.minimum(s, n_total - 1))
    stage_b = lambda s: split(jnp.maximum(s - 1, 0))

    def q_index(s):
        b, hh, qi = stage_a(s)
        return (b, qi, hh)

    def k_index(s):
        b, hh, _ = stage_a(s)
        return (b, 0, hh)

    def v_index(s):
        b, hh, _ = stage_b(s)
        return (b, 0, hh)

    def o_index(s):
        b, hh, qi = stage_b(s)
        return (b, qi, hh)

    m_rows = g * WIN_TQ
    return pl.pallas_call(
        functools.partial(_window_kernel, n_total=n_total),
        grid=(n_total + 1,),
        in_specs=[
            pl.BlockSpec((1, WIN_TQ, g * HEAD_DIM), q_index),
            pl.BlockSpec((1, TOK, HEAD_DIM), k_index),
            pl.BlockSpec((1, TOK, HEAD_DIM), v_index),
            pl.BlockSpec((1, m_rows, 1), lambda s: (stage_b(s)[1], 0, 0)),
        ],
        out_specs=pl.BlockSpec((1, WIN_TQ, g * HEAD_DIM), o_index),
        out_shape=jax.ShapeDtypeStruct((bsz, SEQ, D_HEADS * HEAD_DIM), BF16),
        scratch_shapes=[pltpu.VMEM((m_rows, WIN_BAND), F32), pltpu.VMEM((m_rows, CTX_LEN), F32),
                        pltpu.VMEM((m_rows, WIN_BAND), F32), pltpu.VMEM((m_rows, CTX_LEN), F32)],
        compiler_params=pltpu.CompilerParams(
            dimension_semantics=("arbitrary",), vmem_limit_bytes=VMEM_LIMIT),
        name="window_attn",
    )(q, k, v, sink_col)


def _route(logits, lane):
    lane_f = lane.astype(F32)
    lg = jnp.where(lane < N_GROUPS, logits, NEG_BIG)
    g_max = jnp.max(lg, axis=-1, keepdims=True)
    p_top = 1.0 / jnp.sum(jnp.exp(lg - g_max), axis=-1, keepdims=True)
    g_idx = jnp.min(jnp.where(lg == g_max, lane_f, float(LANES)), axis=-1, keepdims=True)
    e_lane = lane - N_GROUPS
    in_group = (e_lane >= 0) & (e_lane < N_EXPERTS) & ((e_lane // EXPERTS_PER_GROUP).astype(F32) == g_idx)
    le = jnp.where(in_group, logits, NEG_BIG)
    m1 = jnp.max(le, axis=-1, keepdims=True)
    i1 = jnp.min(jnp.where(le == m1, lane_f, float(LANES)), axis=-1, keepdims=True)
    le2 = jnp.where(lane_f == i1, NEG_BIG, le)
    m2 = jnp.max(le2, axis=-1, keepdims=True)
    i2 = jnp.min(jnp.where(le2 == m2, lane_f, float(LANES)), axis=-1, keepdims=True)
    t = jnp.exp(m2 - m1)
    w1 = p_top / (1.0 + t)
    w2 = p_top * t / (1.0 + t)
    return i1, i2, w1, w2


def _outproj_kernel(*refs, has_ctx, n_tiles, n_total):
    if has_ctx:
        (o1_ref, o2_ref, h_ref, o1c_ref, o2c_ref, hc_ref, mod_ref, w1_ref, w2_ref, rw_ref, rb_ref,
         hn_ref, fx_ref, rt_ref, cnt_ref, run_ref, logit_ref) = refs
    else:
        (o1_ref, o2_ref, h_ref, mod_ref, w1_ref, w2_ref, rw_ref, rb_ref,
         hn_ref, fx_ref, rt_ref, cnt_ref, run_ref, logit_ref) = refs
    d = D_MODEL
    s = pl.program_id(0)

    @pl.when(s == 0)
    def _():
        run_ref[...] = jnp.zeros_like(run_ref)
        logit_ref[...] = jnp.zeros_like(logit_ref)

    logits = logit_ref[...]
    lane = lax.broadcasted_iota(jnp.int32, (ROW_TILE, LANES), 1)
    lane_f = lane.astype(F32)
    i1, i2, w1, w2 = _route(logits, lane)

    o1, o2, h = o1_ref[0], o2_ref[0], h_ref[0]
    if has_ctx:
        is_ctx = jnp.minimum(s, n_total - 1) % n_tiles == CTX_TILE
        o1 = jnp.where(is_ctx, o1c_ref[0], o1)
        o2 = jnp.where(is_ctx, o2c_ref[0], o2)
        h = jnp.where(is_ctx, hc_ref[0], h)
    acc = _dot(o1, w1_ref[...]) + _dot(o2, w2_ref[...])
    hn = h + mod_ref[0, :, 2 * d:3 * d] * acc
    hn_ref[0] = hn
    fx = _modulate(hn, mod_ref[0, :, 3 * d:4 * d], mod_ref[0, :, 4 * d:5 * d])
    fx_ref[0] = _pack_bf16_pairs(fx)
    hi = fx.astype(BF16)
    lo = (fx - hi.astype(F32)).astype(BF16)
    part = _dot(jnp.concatenate([hi, lo], axis=0), rw_ref[...])
    logit_ref[...] = (part[:ROW_TILE, :LANES] + part[:ROW_TILE, LANES:]
                      + part[ROW_TILE:, :LANES] + part[ROW_TILE:, LANES:] + rb_ref[...])

    hit1, hit2 = lane_f == i1, lane_f == i2
    multi_hot = jnp.where(s > 0, jnp.where(hit1, 1.0, 0.0) + jnp.where(hit2, 1.0, 0.0), 0.0)
    tri = (lax.broadcasted_iota(jnp.int32, (ROW_TILE, ROW_TILE), 0)
           > lax.broadcasted_iota(jnp.int32, (ROW_TILE, ROW_TILE), 1))
    before = _dot(jnp.where(tri, 1.0, 0.0).astype(BF16), multi_hot.astype(BF16)) + run_ref[...]
    r1 = jnp.sum(jnp.where(hit1, before, 0.0), axis=-1, keepdims=True)
    r2 = jnp.sum(jnp.where(hit2, before, 0.0), axis=-1, keepdims=True)
    run_ref[...] = run_ref[...] + jnp.sum(multi_hot, axis=0, keepdims=True)
    cnt_ref[...] = jnp.broadcast_to(run_ref[...], cnt_ref.shape)

    out = jnp.where(lane == 0, i1 - N_GROUPS, 0.0)
    out = jnp.where(lane == 1, i2 - N_GROUPS, out)
    out = jnp.where(lane == 2, w1, out)
    out = jnp.where(lane == 3, w2, out)
    out = jnp.where(lane == 4, r1, out)
    out = jnp.where(lane == 5, r2, out)
    rt_ref[0] = out


def _outproj_call(lat, ctx, mod, w1, w2, rw, rb, n_tiles):
    bsz = lat[2].shape[0]
    has_ctx = ctx is not None
    n_total = bsz * n_tiles
    stage_a = lambda s: jnp.minimum(s, n_total - 1)
    stage_b = lambda s: jnp.maximum(s - 1, 0)
    bj = lambda t: (t // n_tiles, t % n_tiles)

    def a_tile(w, kind):
        def index(s):
            b, j = bj(stage_a(s))
            return (b, {"all": j, "lat": jnp.minimum(j, CTX_TILE - 1), "ctx": 0}[kind], 0)
        return pl.BlockSpec((1, ROW_TILE, w), index)

    def b_tile(w):
        return pl.BlockSpec((1, ROW_TILE, w), lambda s: (*bj(stage_b(s)), 0))

    in_specs = [a_tile(a.shape[-1], "lat") for a in lat]
    args = list(lat)
    if has_ctx:
        in_specs += [a_tile(a.shape[-1], "ctx") for a in ctx]
        args += list(ctx)
    in_specs += [
        pl.BlockSpec((1, 1, 6 * D_MODEL), lambda s: (_mod_row(*bj(stage_a(s))), 0, 0)),
        _resident(w1.shape), _resident(w2.shape), _resident(rw.shape), _resident(rb.shape),
    ]
    args += [mod, w1, w2, rw, rb]
    n_rows = n_tiles * ROW_TILE
    return pl.pallas_call(
        functools.partial(_outproj_kernel, has_ctx=has_ctx, n_tiles=n_tiles, n_total=n_total),
        grid=(n_total + 1,),
        in_specs=in_specs,
        out_specs=[a_tile(D_MODEL, "all"), a_tile(PACKED_W, "all"), b_tile(LANES),
                   pl.BlockSpec((8, LANES), lambda s: (0, 0))],
        out_shape=[jax.ShapeDtypeStruct((bsz, n_rows, D_MODEL), F32),
                   jax.ShapeDtypeStruct((bsz, n_rows, PACKED_W), U32),
                   jax.ShapeDtypeStruct((bsz, n_rows, LANES), F32),
                   jax.ShapeDtypeStruct((8, LANES), F32)],
        scratch_shapes=[pltpu.VMEM((1, LANES), F32), pltpu.VMEM((ROW_TILE, LANES), F32)],
        compiler_params=pltpu.CompilerParams(
            dimension_semantics=("arbitrary",), vmem_limit_bytes=VMEM_LIMIT),
        name="outproj_router",
    )(*args)


def _dispatch_kernel(lo_ref, hi_ref, slot_ref, fx_ref, xs_hbm, zbuf, stage, sems, zsem):
    n_steps = pl.num_programs(0) * pl.num_programs(1)
    step = pl.program_id(0) * pl.num_programs(1) + pl.program_id(1)
    cur = step % 2
    src = stage.at[cur]
    src[...] = fx_ref[0]
    for r in range(ROW_TILE):
        for k in range(TOP_K):
            pltpu.make_async_copy(src.at[pl.ds(r, 1)], xs_hbm.at[pl.ds(slot_ref[0, k, r], 1)],
                                  sems.at[cur]).start(priority=k % 2)

    def wait_tile(slot):
        for _ in range(TOP_K):
            pltpu.make_async_copy(stage.at[slot], xs_hbm.at[pl.ds(0, ROW_TILE)], sems.at[slot]).wait()

    @pl.when(step > 0)
    def _():
        wait_tile(1 - cur)

    @pl.when(step == n_steps - 1)
    def _():
        zbuf[...] = jnp.zeros_like(zbuf)

        def fill(s, carry):
            pltpu.make_async_copy(zbuf.at[pl.ds(0, 1)], xs_hbm.at[pl.ds(s, 1)], zsem).start()
            return carry

        def drain(s, carry):
            pltpu.make_async_copy(zbuf.at[pl.ds(0, 1)], xs_hbm.at[pl.ds(0, 1)], zsem).wait()
            return carry
        for e in range(N_EXPERTS):
            lax.fori_loop(lo_ref[e], hi_ref[e], fill, 0)
        for e in range(N_EXPERTS):
            lax.fori_loop(lo_ref[e], hi_ref[e], drain, 0)
        wait_tile(cur)


def _dispatch_call(fx, slots, fill_lo, fill_hi, n_blocks):
    bsz, n_rows, _ = fx.shape
    n_tiles = n_rows // ROW_TILE
    grid_spec = pltpu.PrefetchScalarGridSpec(
        num_scalar_prefetch=2,
        grid=(bsz, n_tiles),
        in_specs=[
            pl.BlockSpec((1, TOP_K, ROW_TILE), lambda b, j, lo, hi: (b * n_tiles + j, 0, 0),
                         memory_space=pltpu.SMEM),
            pl.BlockSpec((1, ROW_TILE, PACKED_W), lambda b, j, lo, hi: (b, j, 0)),
        ],
        out_specs=pl.BlockSpec(memory_space=pl.ANY),
        scratch_shapes=[pltpu.VMEM((8, PACKED_W), U32), pltpu.VMEM((2, ROW_TILE, PACKED_W), U32),
                        pltpu.SemaphoreType.DMA((2,)), pltpu.SemaphoreType.DMA(())],
    )
    return pl.pallas_call(
        _dispatch_kernel,
        grid_spec=grid_spec,
        out_shape=jax.ShapeDtypeStruct((n_blocks * MOE_BLOCK, PACKED_W), U32),
        compiler_params=pltpu.CompilerParams(
            dimension_semantics=("arbitrary", "arbitrary"), vmem_limit_bytes=VMEM_LIMIT),
        name="moe_dispatch",
    )(fill_lo, fill_hi, slots, fx)


CAST_ROWS = 256


def _moe_kernel(be_ref, nu_ref, nx_ref, x_ref, wg_hbm, wu_hbm, wd_hbm, y_ref,
                sg, su, sd, wg, wu, wd, sems, *, layer):
    i = pl.program_id(0)
    e = be_ref[i]
    used = i < nu_ref[0]
    staged = ((wg_hbm, sg, wg, 0), (wu_hbm, su, wu, 1), (wd_hbm, sd, wd, 2))

    def fetch(expert):
        for hbm, stage, _, s in staged:
            pltpu.make_async_copy(hbm.at[layer, expert], stage, sems.at[s]).start(priority=1)

    @pl.when(i == 0)
    def _():
        fetch(e)

    first_of_expert = (i == 0) | (e != be_ref[jnp.maximum(i - 1, 0)])

    @pl.when(used & first_of_expert)
    def _():
        for hbm, stage, dst, s in staged:
            pltpu.make_async_copy(hbm.at[layer, 0], stage, sems.at[s]).wait()

            def cast(c, carry):
                rows = pl.ds(pl.multiple_of(c * CAST_ROWS, CAST_ROWS), CAST_ROWS)
                dst[rows, :] = stage[rows, :].astype(BF16)
                return carry
            lax.fori_loop(0, stage.shape[0] // CAST_ROWS, cast, 0)
        nxt = nx_ref[e]

        @pl.when(nxt >= 0)
        def _():
            fetch(nxt)

    @pl.when(used)
    def _():
        x = _unpack_bf16_pairs(x_ref[...]).astype(BF16)
        gt = _dot(x, wg[...])
        up = _dot(x, wu[...])
        u = (_silu(gt) * up).astype(BF16)
        y_ref[...] = _pack_bf16_pairs(_dot(u, wd[...]))


def _moe_call(xs, block_expert, n_used, next_expert, wg, wu, wd, layer):
    n_blocks = block_expert.shape[0]
    row_blk = pl.BlockSpec((MOE_BLOCK, PACKED_W), lambda i, be, nu, nx: (jnp.minimum(i, nu[0] - 1), 0))
    hbm = pl.BlockSpec(memory_space=pl.ANY)
    grid_spec = pltpu.PrefetchScalarGridSpec(
        num_scalar_prefetch=3,
        grid=(n_blocks,),
        in_specs=[row_blk, hbm, hbm, hbm],
        out_specs=row_blk,
        scratch_shapes=[
            pltpu.VMEM((D_MODEL, D_EXPERT), F32), pltpu.VMEM((D_MODEL, D_EXPERT), F32),
            pltpu.VMEM((D_EXPERT, D_MODEL), F32),
            pltpu.VMEM((D_MODEL, D_EXPERT), BF16), pltpu.VMEM((D_MODEL, D_EXPERT), BF16),
            pltpu.VMEM((D_EXPERT, D_MODEL), BF16),
            pltpu.SemaphoreType.DMA((3,)),
        ],
    )
    return pl.pallas_call(
        functools.partial(_moe_kernel, layer=layer),
        grid_spec=grid_spec,
        out_shape=jax.ShapeDtypeStruct((n_blocks * MOE_BLOCK, PACKED_W), U32),
        compiler_params=pltpu.CompilerParams(
            dimension_semantics=("arbitrary",), vmem_limit_bytes=VMEM_LIMIT),
        name="moe_experts",
    )(block_expert, n_used, next_expert, xs, wg, wu, wd)


def _combine_kernel(slot_ref, slot_next_ref, h_ref, rt_ref, mod_ref, ys_hbm, o_ref, ybuf, sems):
    d = D_MODEL
    n_steps = pl.num_programs(0) * pl.num_programs(1)
    step = pl.program_id(0) * pl.num_programs(1) + pl.program_id(1)
    cur = step % 2

    @pl.when(step == 0)
    def _():
        _start_row_gathers(slot_ref, ys_hbm, ybuf.at[0], sems.at[0])

    _start_row_gathers(slot_next_ref, ys_hbm, ybuf.at[1 - cur], sems.at[1 - cur])
    _wait_row_gathers(ys_hbm, ybuf.at[cur], sems.at[cur])
    rt = rt_ref[0]
    y = rt[:, 2:3] * _unpack_bf16_pairs(ybuf[cur, 0]) + rt[:, 3:4] * _unpack_bf16_pairs(ybuf[cur, 1])
    o_ref[0] = h_ref[0] + mod_ref[0, :, 5 * d:6 * d] * y

    @pl.when(step == n_steps - 1)
    def _():
        _wait_row_gathers(ys_hbm, ybuf.at[1 - cur], sems.at[1 - cur])


def _combine_call(h, ys, slots, rt, mod, n_tiles):
    bsz = h.shape[0]
    n_steps = bsz * n_tiles
    tile = lambda w: pl.BlockSpec((1, ROW_TILE, w), lambda b, j: (b, j, 0))
    slot_spec = lambda ahead: pl.BlockSpec(
        (1, TOP_K, ROW_TILE), lambda b, j: (jnp.minimum(b * n_tiles + j + ahead, n_steps - 1), 0, 0),
        memory_space=pltpu.SMEM)
    return pl.pallas_call(
        _combine_kernel,
        grid=(bsz, n_tiles),
        in_specs=[
            slot_spec(0), slot_spec(1),
            tile(D_MODEL), tile(LANES),
            pl.BlockSpec((1, 1, 6 * D_MODEL), lambda b, j: (_mod_row(b, j), 0, 0)),
            pl.BlockSpec(memory_space=pl.ANY),
        ],
        out_specs=tile(D_MODEL),
        out_shape=jax.ShapeDtypeStruct((bsz, n_tiles * ROW_TILE, D_MODEL), F32),
        scratch_shapes=[pltpu.VMEM((2, TOP_K, ROW_TILE, PACKED_W), U32), pltpu.SemaphoreType.DMA((2,))],
        compiler_params=pltpu.CompilerParams(
            dimension_semantics=("arbitrary", "arbitrary"), vmem_limit_bytes=VMEM_LIMIT),
        name="moe_combine",
    )(slots, slots, h, rt, mod, ys)


def _rope_tables():
    rows = SEQ // GRID_W
    row = jnp.repeat(jnp.arange(rows), GRID_W).astype(F32)
    col = jnp.tile(jnp.arange(GRID_W), rows).astype(F32)

    def tables(dim):
        n_freq = dim // 4
        inv = 1.0 / (ROPE_THETA ** (jnp.arange(n_freq, dtype=F32) / n_freq))
        ang_r = row[:, None] * inv
        ang_c = col[:, None] * inv
        ang = jnp.concatenate([ang_r, ang_r, ang_c, ang_c], axis=-1)
        sign = jnp.tile(jnp.concatenate([-jnp.ones((n_freq,), F32), jnp.ones((n_freq,), F32)]), 2)
        return jnp.cos(ang), jnp.sin(ang) * sign

    cos128, sin128 = tables(HEAD_DIM)
    cos64, sin64 = tables(B_ROPE)
    ones, zeros = jnp.ones((SEQ, 64), F32), jnp.zeros((SEQ, 64), F32)
    lat = jnp.concatenate([cos128, sin128,
                           cos64, cos64, sin64, sin64,
                           cos64, ones, sin64, zeros], axis=-1)
    ident = jnp.concatenate([jnp.ones((CTX_LEN, 128), F32), jnp.zeros((CTX_LEN, 128), F32)], axis=-1)
    return jnp.concatenate([lat, jnp.tile(ident, (1, 3))], axis=0)


def _pad_lanes(v, n):
    return jnp.pad(v, (0, n - v.shape[0]))


def _router_weights(wr_g, br_g, wr_e, br_e):
    w = jnp.pad(jnp.concatenate([wr_g, wr_e], axis=1), ((0, 0), (0, LANES - N_GROUPS - N_EXPERTS)))
    hi = w.astype(BF16)
    lo = (w - hi.astype(F32)).astype(BF16)
    b = _pad_lanes(jnp.concatenate([br_g, br_e]), LANES).reshape(1, LANES)
    return jnp.concatenate([hi, lo], axis=1), b


def _moe_experts(fx, rt, cnt, wg, wu, wd, layer):
    bsz, n_rows, _ = fx.shape
    n_tiles = n_rows // ROW_TILE
    n_blocks = -(-(bsz * n_rows * TOP_K) // MOE_BLOCK) + N_EXPERTS
    experts = jnp.arange(N_EXPERTS, dtype=jnp.int32)
    counts = cnt[0, N_GROUPS:N_GROUPS + N_EXPERTS].astype(jnp.int32)
    padded = (counts + MOE_BLOCK - 1) // MOE_BLOCK * MOE_BLOCK
    pad_end = jnp.cumsum(padded)
    pad_start = pad_end - padded
    e_idx = rt[..., 0:TOP_K].astype(jnp.int32)
    rank = rt[..., 4:4 + TOP_K].astype(jnp.int32)
    slot = rank + jnp.sum(jnp.where(e_idx[..., None] == experts, pad_start, 0), axis=-1)
    slots = slot.reshape(bsz * n_tiles, ROW_TILE, TOP_K).transpose(0, 2, 1)
    block_start = jnp.arange(n_blocks, dtype=jnp.int32) * MOE_BLOCK
    block_expert = jnp.minimum(jnp.sum(pad_end[None, :] <= block_start[:, None], axis=1), N_EXPERTS - 1)
    n_used = pad_end[-1:] // MOE_BLOCK
    later = (experts[None, :] > experts[:, None]) & (counts[None, :] > 0)
    next_expert = jnp.where(jnp.any(later, axis=1), jnp.argmax(later, axis=1), -1)
    i32 = lambda a: a.astype(jnp.int32)
    xs = _dispatch_call(fx, i32(slots), i32(pad_start + counts), i32(pad_end), n_blocks)
    ys = _moe_call(xs, i32(block_expert), i32(n_used), i32(next_expert), wg, wu, wd, layer)
    return ys, i32(slots)


def kernel(x, c, ctx, c_ctx, mod_w, mod_b, even_w_in, even_w_out, a_q_norm, a_k_norm, b_cq_norm, b_w_uq,
           b_ckv_norm, b_w_ukv, b_q_norm, b_k_norm, odd_w_in, odd_w_out, c_q_norm, c_k_norm, c_lambda_q1,
           c_lambda_k1, c_lambda_q2, c_lambda_k2, c_subln, d_q_norm, d_k_norm, d_sink, moe_wr_group,
           moe_br_group, moe_wr_expert, moe_br_expert, moe_w_gate, moe_w_up, moe_w_down):
    bsz = x.shape[0]
    c_all = jnp.zeros((MOD_ROWS, D_MODEL), F32).at[:bsz].set(c).at[MOD_CTX_ROW].set(c_ctx)
    mod_all = _mod_call(c_all, mod_w, mod_b)
    tab = _rope_tables()

    i = 0
    mod = mod_all[0].reshape(MOD_ROWS, 1, 6 * D_MODEL)
    scale_a = HEAD_DIM ** -0.5 * LOG2_E
    scale_b = (B_NOPE + B_ROPE) ** -0.5 * LOG2_E
    win = jnp.pad(even_w_in[i], ((0, 0), (0, EVEN_IN_PAD - even_w_in.shape[-1]))).astype(BF16)
    wuq = b_w_uq[i].reshape(B_Q_LORA, B_HEADS, B_NOPE + B_ROPE)
    wuq = jnp.pad(wuq, ((0, 0), (0, 0), (0, B_QK_PAD - B_NOPE - B_ROPE))).reshape(B_Q_LORA, -1).astype(BF16)
    wukv = b_w_ukv[i].reshape(B_KV_LORA, B_HEADS, B_NOPE + B_V)
    wukv = jnp.concatenate([wukv[:, :, :B_NOPE].reshape(B_KV_LORA, -1),
                            wukv[:, :, B_NOPE:].reshape(B_KV_LORA, -1)], axis=1).astype(BF16)
    gains = jnp.stack([
        _pad_lanes(a_q_norm[i] * scale_a, 512), _pad_lanes(a_k_norm[i], 512),
        b_cq_norm[i], b_ckv_norm[i],
        _pad_lanes(b_q_norm[i] * scale_b, 512), _pad_lanes(b_k_norm[i], 512),
        jnp.zeros((512,), F32), jnp.zeros((512,), F32)])
    qa, ka, va, qb, kb, vb = _even_proj_call(x, ctx, mod, tab, gains, win, wuq, wukv)

    g_a = A_HEADS // A_KV_HEADS
    lat = dict(q_tile0=0, key_tile0=0, n_keys=TOK)
    oa = _attn_call(qa, ka, va, n_heads_kv=A_KV_HEADS, g=g_a, dk=HEAD_DIM, dv=HEAD_DIM, nkv=2,
                    tq=256, n_q_tiles=SEQ // 256, name="gqa_latent", **lat)
    ob = _attn_call(qb, kb, vb, n_heads_kv=B_HEADS, g=1, dk=B_QK_PAD, dv=B_V, nkv=2,
                    tq=512, n_q_tiles=SEQ // 512, name="mla_latent", **lat)
    cx = dict(tq=CTX_LEN, q_tile0=SEQ // CTX_LEN, n_q_tiles=1, key_tile0=SEQ // CTX_LEN, n_keys=CTX_LEN)
    oa_c = _attn_call(qa, ka, va, n_heads_kv=A_KV_HEADS, g=g_a, dk=HEAD_DIM, dv=HEAD_DIM, nkv=2,
                      name="gqa_context", **cx)
    ob_c = _attn_call(qb, kb, vb, n_heads_kv=B_HEADS, g=1, dk=B_QK_PAD, dv=B_V, nkv=2,
                      name="mla_context", **cx)

    w_out = even_w_out[i].astype(BF16)
    rw, rb = _router_weights(moe_wr_group[0], moe_br_group[0], moe_wr_expert[0], moe_br_expert[0])
    hn, fx, rt, cnt = _outproj_call((oa, ob, x), (oa_c, ob_c, ctx), mod,
                                    w_out[:A_HEADS * HEAD_DIM], w_out[A_HEADS * HEAD_DIM:],
                                    rw, rb, N_ROW_TILES)
    ys, slots = _moe_experts(fx, rt, cnt, moe_w_gate, moe_w_up, moe_w_down, 0)
    hn0, rt0, mod0 = hn, rt, mod

    layer = 1
    mod = mod_all[1].reshape(MOD_ROWS, 1, 6 * D_MODEL)
    lambda_init = 0.8 - 0.6 * math.exp(-0.3 * layer)
    scale_c = C_HD ** -0.5 * LOG2_E
    scale_d = HEAD_DIM ** -0.5 * LOG2_E
    win = odd_w_in[i].astype(BF16)
    gains = jnp.stack([
        jnp.tile(c_q_norm[i] * scale_c, 2), jnp.tile(c_k_norm[i], 2),
        d_q_norm[i] * scale_d, d_k_norm[i],
        jnp.zeros((128,), F32), jnp.zeros((128,), F32), jnp.zeros((128,), F32), jnp.zeros((128,), F32)])
    h, qc, kc, vc, qd, kd, vd = _odd_proj_call(hn0, ys, slots, rt0, mod0, mod, tab, gains, win)

    aux = jnp.stack([
        _pad_lanes(c_lambda_q1[i], 128), _pad_lanes(c_lambda_k1[i], 128),
        _pad_lanes(c_lambda_q2[i], 128), _pad_lanes(c_lambda_k2[i], 128),
        c_subln[i], jnp.zeros((128,), F32), jnp.zeros((128,), F32), jnp.zeros((128,), F32)])
    oc = _attn_call(qc, kc, vc, n_heads_kv=C_HEADS, g=2, dk=LANES, dv=C_V, nkv=2,
                    tq=512, n_q_tiles=SEQ // 512, aux=aux, diff_lambda_init=lambda_init,
                    name="diff_latent", **lat)
    g_d = D_HEADS // D_KV_HEADS
    sink_col = jnp.repeat((d_sink[i] * LOG2_E).reshape(D_KV_HEADS, g_d), WIN_TQ, axis=1)
    sink_col = sink_col.reshape(D_KV_HEADS, g_d * WIN_TQ, 1)
    od = _window_call(qd, kd, vd, sink_col)

    w_out = odd_w_out[i].astype(BF16)
    rw, rb = _router_weights(moe_wr_group[1], moe_br_group[1], moe_wr_expert[1], moe_br_expert[1])
    n_lat_tiles = SEQ // ROW_TILE
    hn, fx, rt, cnt = _outproj_call((oc, od, h), None, mod, w_out[:C_HEADS * C_V], w_out[C_HEADS * C_V:],
                                    rw, rb, n_lat_tiles)
    ys, slots = _moe_experts(fx, rt, cnt, moe_w_gate, moe_w_up, moe_w_down, 1)
    return _combine_call(hn, ys, slots, rt, mod, n_lat_tiles)
```

```python
import functools
import math

import jax
import jax.numpy as jnp
import numpy as np
from jax import lax
from jax.experimental import pallas as pl
from jax.experimental.pallas import tpu as pltpu

F32 = jnp.float32
BF16 = jnp.bfloat16

D_MODEL = 2048
BATCH = 8
SEQ = 2048
DEPTH = 2
GRID_W = 64
CTX_LEN = 256
TOK = SEQ + CTX_LEN
HEAD_DIM = 128
ROPE_THETA = 10000.0
NORM_EPS = 1e-6
A_HEADS = 8
A_KV_HEADS = 2
B_HEADS = 8
B_Q_LORA = 512
B_KV_LORA = 512
B_NOPE = 128
B_ROPE = 64
B_V = 128
B_QK_PAD = 256
C_HEADS = 8
C_HD = 64
C_V = 128
D_HEADS = 8
D_KV_HEADS = 2
WINDOW = 128
N_GROUPS = 4
EXPERTS_PER_GROUP = 8
N_EXPERTS = 32
D_EXPERT = 1024
MOE_BLOCK = 256
TOP_K = 2
LANES = 128
LOG2_E = math.log2(math.e)

ROW_TILE = 256
N_ROW_TILES = TOK // ROW_TILE
CTX_TILE = SEQ // ROW_TILE
MOD_ROWS = 16
MOD_CTX_ROW = BATCH
VMEM_LIMIT = 56 * 1024 * 1024


def _dot(a, b):
    return jnp.dot(a, b, preferred_element_type=F32)


def _dot_nt(a, b):
    return lax.dot_general(a, b, (((1,), (1,)), ((), ())), preferred_element_type=F32)


def _silu(x):
    return x / (1.0 + jnp.exp(-x))


def _rms(x, n):
    return x * lax.rsqrt(jnp.sum(x * x, axis=-1, keepdims=True) * (1.0 / n) + NORM_EPS)


def _modulate(x, shift, scale):
    return _rms(x, x.shape[-1]) * (1.0 + scale) + shift


def _rope(x, cos, sin_signed, quarter, lane):
    fwd = pltpu.roll(x, LANES - quarter, axis=1)
    bwd = pltpu.roll(x, quarter, axis=1)
    rot = jnp.where((lane // quarter) % 2 == 0, fwd, bwd)
    return x * cos + rot * sin_signed


PACKED_W = D_MODEL // 2
U32 = jnp.uint32


def _pack_bf16_pairs(x):
    n = x.shape[-1] // 2
    lo = lax.bitcast_convert_type(x[:, :n].astype(BF16).astype(F32), U32) >> 16
    hi = lax.bitcast_convert_type(x[:, n:].astype(BF16).astype(F32), U32) & U32(0xFFFF0000)
    return lo | hi


def _unpack_bf16_pairs(p):
    lo = lax.bitcast_convert_type(p << 16, F32)
    hi = lax.bitcast_convert_type(p & U32(0xFFFF0000), F32)
    return jnp.concatenate([lo, hi], axis=-1)


def _start_row_gathers(slot_ref, src_hbm, dst_ref, sem):
    for r in range(ROW_TILE):
        for k in range(TOP_K):
            pltpu.make_async_copy(src_hbm.at[pl.ds(slot_ref[0, k, r], 1)], dst_ref.at[k, pl.ds(r, 1)],
                                  sem).start(priority=k % 2)


def _wait_row_gathers(src_hbm, dst_ref, sem):
    for k in range(TOP_K):
        pltpu.make_async_copy(src_hbm.at[pl.ds(0, ROW_TILE)], dst_ref.at[k], sem).wait()


MOD_TN = 1024


def _mod_kernel(c_ref, w_ref, b_ref, o_ref):
    a = _silu(c_ref[...]).astype(BF16)
    o_ref[0] = _dot(a, w_ref[0].astype(BF16)) + b_ref[0]


def _mod_call(c_all, mod_w, mod_b):
    d6 = 6 * D_MODEL
    return pl.pallas_call(
        _mod_kernel,
        grid=(DEPTH, d6 // MOD_TN),
        in_specs=[
            pl.BlockSpec((MOD_ROWS, D_MODEL), lambda l, n: (0, 0)),
            pl.BlockSpec((1, D_MODEL, MOD_TN), lambda l, n: (l, 0, n)),
            pl.BlockSpec((1, 1, MOD_TN), lambda l, n: (l, 0, n)),
        ],
        out_specs=pl.BlockSpec((1, MOD_ROWS, MOD_TN), lambda l, n: (l, 0, n)),
        out_shape=jax.ShapeDtypeStruct((DEPTH, MOD_ROWS, d6), F32),
        compiler_params=pltpu.CompilerParams(
            dimension_semantics=("parallel", "parallel"), vmem_limit_bytes=VMEM_LIMIT),
        name="mod_vectors",
    )(c_all, mod_w, mod_b.reshape(DEPTH, 1, d6))


def _mod_row(b, j):
    return jnp.where(j == CTX_TILE, MOD_CTX_ROW, b)


def _resident(shape):
    nd = len(shape)
    return pl.BlockSpec(shape, lambda *_: (0,) * nd, pipeline_mode=pl.Buffered(1))


EVEN_IN_PAD = 2688


def _even_proj_kernel(hx_ref, hc_ref, mod_ref, tab_ref, g_ref, win_ref, wuq_ref, wukv_ref,
                      qa_ref, ka_ref, va_ref, qb_ref, kb_ref, vb_ref):
    d = D_MODEL
    x = jnp.where(pl.program_id(1) == CTX_TILE, hc_ref[0], hx_ref[0])
    a = _modulate(x, mod_ref[0, :, 0:d], mod_ref[0, :, d:2 * d]).astype(BF16)
    lane = lax.broadcasted_iota(jnp.int32, (ROW_TILE, LANES), 1)
    cos128, sin128 = tab_ref[:, 0:128], tab_ref[:, 128:256]
    cos64p, sin64p = tab_ref[:, 512:640], tab_ref[:, 640:768]

    z = _dot(a, win_ref[:, 0:1024])
    g_q = g_ref[0:1, 0:128]
    for h in range(A_HEADS):
        blk = _rms(z[:, h * 128:(h + 1) * 128], HEAD_DIM) * g_q
        qa_ref[0, :, h * 128:(h + 1) * 128] = _rope(blk, cos128, sin128, 32, lane).astype(BF16)
    z = _dot(a, win_ref[:, 1024:1536])
    g_k = g_ref[1:2, 0:128]
    for h in range(A_KV_HEADS):
        blk = _rms(z[:, h * 128:(h + 1) * 128], HEAD_DIM) * g_k
        ka_ref[0, :, h * 128:(h + 1) * 128] = _rope(blk, cos128, sin128, 32, lane).astype(BF16)
    va_ref[0] = z[:, 256:512].astype(BF16)

    z = _dot(a, win_ref[:, 1536:2048])
    cq = (_rms(z, B_Q_LORA) * g_ref[2:3, :]).astype(BF16)
    zq = _dot(cq, wuq_ref[...])
    gq0, gq1 = g_ref[4:5, 0:128], g_ref[4:5, 128:256]
    n_qk = float(B_NOPE + B_ROPE)
    for h in range(B_HEADS):
        b0 = zq[:, h * 256:h * 256 + 128]
        b1 = zq[:, h * 256 + 128:(h + 1) * 256]
        ss = jnp.sum(b0 * b0, axis=-1, keepdims=True) + jnp.sum(b1 * b1, axis=-1, keepdims=True)
        r = lax.rsqrt(ss * (1.0 / n_qk) + NORM_EPS)
        qb_ref[0, :, h * 256:h * 256 + 128] = (b0 * r * gq0).astype(BF16)
        qb_ref[0, :, h * 256 + 128:(h + 1) * 256] = _rope(b1 * r * gq1, cos64p, sin64p, 16, lane).astype(BF16)

    z = _dot(a, win_ref[:, 2048:2688])
    ckv = (_rms(z[:, 0:512], B_KV_LORA) * g_ref[3:4, :]).astype(BF16)
    kr = z[:, 512:640]
    ss_kr = jnp.sum(kr * kr, axis=-1, keepdims=True)
    kr_rot = _rope(kr * g_ref[5:6, 128:256], cos64p, sin64p, 16, lane)
    zkv = _dot(ckv, wukv_ref[...])
    vb_ref[0] = zkv[:, 1024:2048].astype(BF16)
    gk0 = g_ref[5:6, 0:128]
    for h in range(B_HEADS):
        kn = zkv[:, h * 128:(h + 1) * 128]
        ss = jnp.sum(kn * kn, axis=-1, keepdims=True) + ss_kr
        r = lax.rsqrt(ss * (1.0 / n_qk) + NORM_EPS)
        kb_ref[0, :, h * 256:h * 256 + 128] = (kn * r * gk0).astype(BF16)
        kb_ref[0, :, h * 256 + 128:(h + 1) * 256] = (kr_rot * r).astype(BF16)


def _even_proj_call(h_lat, h_ctx, mod, tab, gains, win, wuq, wukv):
    bsz = h_lat.shape[0]
    widths = (A_HEADS * HEAD_DIM, A_KV_HEADS * HEAD_DIM, A_KV_HEADS * HEAD_DIM,
              B_HEADS * B_QK_PAD, B_HEADS * B_QK_PAD, B_HEADS * B_V)
    tile = lambda w: pl.BlockSpec((1, ROW_TILE, w), lambda b, j: (b, j, 0))
    return pl.pallas_call(
        _even_proj_kernel,
        grid=(bsz, N_ROW_TILES),
        in_specs=[
            pl.BlockSpec((1, ROW_TILE, D_MODEL), lambda b, j: (b, jnp.minimum(j, CTX_TILE - 1), 0)),
            pl.BlockSpec((1, ROW_TILE, D_MODEL), lambda b, j: (b, 0, 0)),
            pl.BlockSpec((1, 1, 6 * D_MODEL), lambda b, j: (_mod_row(b, j), 0, 0)),
            pl.BlockSpec((ROW_TILE, 768), lambda b, j: (j, 0)),
            _resident(gains.shape), _resident(win.shape), _resident(wuq.shape), _resident(wukv.shape),
        ],
        out_specs=[tile(w) for w in widths],
        out_shape=[jax.ShapeDtypeStruct((bsz, TOK, w), BF16) for w in widths],
        compiler_params=pltpu.CompilerParams(
            dimension_semantics=("parallel", "parallel"), vmem_limit_bytes=VMEM_LIMIT),
        name="even_proj",
    )(h_lat, h_ctx, mod, tab, gains, win, wuq, wukv)


def _rms_halves(x, lane):
    x2 = x * x
    s_lo = jnp.sum(jnp.where(lane < 64, x2, 0.0), axis=-1, keepdims=True)
    s_hi = jnp.sum(jnp.where(lane < 64, 0.0, x2), axis=-1, keepdims=True)
    r = jnp.where(lane < 64, lax.rsqrt(s_lo * (1.0 / C_HD) + NORM_EPS), lax.rsqrt(s_hi * (1.0 / C_HD) + NORM_EPS))
    return x * r


def _odd_proj_kernel(slot_ref, slot_next_ref, hn_ref, rt_ref, mod_prev_ref, mod_ref, tab_ref, g_ref, win_ref,
                     ys_hbm, h_ref, qc_ref, kc_ref, vc_ref, qd_ref, kd_ref, vd_ref, ybuf, sems):
    d = D_MODEL
    n_steps = pl.num_programs(0) * pl.num_programs(1)
    step = pl.program_id(0) * pl.num_programs(1) + pl.program_id(1)
    cur = step % 2

    @pl.when(step == 0)
    def _():
        _start_row_gathers(slot_ref, ys_hbm, ybuf.at[0], sems.at[0])

    @pl.when(step < n_steps)
    def _():
        _start_row_gathers(slot_next_ref, ys_hbm, ybuf.at[1 - cur], sems.at[1 - cur])

    _wait_row_gathers(ys_hbm, ybuf.at[cur], sems.at[cur])
    rt = rt_ref[0]
    y = rt[:, 2:3] * _unpack_bf16_pairs(ybuf[cur, 0]) + rt[:, 3:4] * _unpack_bf16_pairs(ybuf[cur, 1])
    x = hn_ref[0] + mod_prev_ref[0, :, 5 * d:6 * d] * y
    h_ref[0] = x
    a = _modulate(x, mod_ref[0, :, 0:d], mod_ref[0, :, d:2 * d]).astype(BF16)
    lane = lax.broadcasted_iota(jnp.int32, (ROW_TILE, LANES), 1)
    cos128, sin128 = tab_ref[:, 0:128], tab_ref[:, 128:256]
    cos64, sin64 = tab_ref[:, 256:384], tab_ref[:, 384:512]

    z = _dot(a, win_ref[:, 0:1024])
    g_q = g_ref[0:1, :]
    for h in range(C_HEADS):
        blk = _rope(_rms_halves(z[:, h * 128:(h + 1) * 128], lane) * g_q, cos64, sin64, 16, lane)
        qc_ref[0, :, h * 256:h * 256 + 128] = jnp.where(lane < 64, blk, 0.0).astype(BF16)
        qc_ref[0, :, h * 256 + 128:(h + 1) * 256] = jnp.where(lane < 64, 0.0, blk).astype(BF16)
    z = _dot(a, win_ref[:, 1024:2048])
    g_k = g_ref[1:2, :]
    for h in range(C_HEADS):
        blk = _rope(_rms_halves(z[:, h * 128:(h + 1) * 128], lane) * g_k, cos64, sin64, 16, lane)
        kc_ref[0, :, h * 128:(h + 1) * 128] = blk.astype(BF16)
    vc_ref[0] = _dot(a, win_ref[:, 2048:3072]).astype(BF16)

    z = _dot(a, win_ref[:, 3072:4096])
    g_q = g_ref[2:3, :]
    for h in range(D_HEADS):
        blk = _rms(z[:, h * 128:(h + 1) * 128], HEAD_DIM) * g_q
        qd_ref[0, :, h * 128:(h + 1) * 128] = _rope(blk, cos128, sin128, 32, lane).astype(BF16)
    z = _dot(a, win_ref[:, 4096:4608])
    g_k = g_ref[3:4, :]
    for h in range(D_KV_HEADS):
        blk = _rms(z[:, h * 128:(h + 1) * 128], HEAD_DIM) * g_k
        kd_ref[0, :, h * 128:(h + 1) * 128] = _rope(blk, cos128, sin128, 32, lane).astype(BF16)
    vd_ref[0] = z[:, 256:512].astype(BF16)

    @pl.when(step == n_steps - 1)
    def _():
        _wait_row_gathers(ys_hbm, ybuf.at[1 - cur], sems.at[1 - cur])


def _odd_proj_call(hn, ys, slots, rt, mod_prev, mod, tab, gains, win):
    bsz = hn.shape[0]
    widths = (C_HEADS * 2 * LANES, C_HEADS * LANES, C_HEADS * C_V,
              D_HEADS * HEAD_DIM, D_KV_HEADS * HEAD_DIM, D_KV_HEADS * HEAD_DIM)
    n_steps = bsz * N_ROW_TILES
    tile = lambda w: pl.BlockSpec((1, ROW_TILE, w), lambda b, j: (b, j, 0))
    mod_spec = pl.BlockSpec((1, 1, 6 * D_MODEL), lambda b, j: (_mod_row(b, j), 0, 0))
    slot_spec = lambda ahead: pl.BlockSpec(
        (1, TOP_K, ROW_TILE), lambda b, j: (jnp.minimum(b * N_ROW_TILES + j + ahead, n_steps - 1), 0, 0),
        memory_space=pltpu.SMEM)
    return pl.pallas_call(
        _odd_proj_kernel,
        grid=(bsz, N_ROW_TILES),
        in_specs=[
            slot_spec(0), slot_spec(1), tile(D_MODEL), tile(LANES), mod_spec, mod_spec,
            pl.BlockSpec((ROW_TILE, 768), lambda b, j: (j, 0)),
            _resident(gains.shape), _resident(win.shape),
            pl.BlockSpec(memory_space=pl.ANY),
        ],
        out_specs=[tile(D_MODEL)] + [tile(w) for w in widths],
        out_shape=[jax.ShapeDtypeStruct((bsz, TOK, D_MODEL), F32)]
                  + [jax.ShapeDtypeStruct((bsz, TOK, w), BF16) for w in widths],
        scratch_shapes=[pltpu.VMEM((2, TOP_K, ROW_TILE, PACKED_W), U32), pltpu.SemaphoreType.DMA((2,))],
        compiler_params=pltpu.CompilerParams(
            dimension_semantics=("arbitrary", "arbitrary"), vmem_limit_bytes=VMEM_LIMIT),
        name="odd_proj",
    )(slots, slots, hn, rt, mod_prev, mod, tab, gains, win, ys)


KEY_CHUNK = 768


def _attn_kernel(*refs, nkv, g, dk, dv, tq, n_keys, diff_lambda_init):
    if diff_lambda_init is None:
        q_ref, k_ref, v_ref, o_ref = refs
    else:
        q_ref, k_ref, v_ref, aux_ref, o_ref = refs
    for kv in range(nkv):
        q = jnp.concatenate(
            [q_ref[0, :, (kv * g + gi) * dk:(kv * g + gi + 1) * dk] for gi in range(g)], axis=0)
        m = l = acc = None
        for c0 in range(0, n_keys, KEY_CHUNK):
            c1 = min(c0 + KEY_CHUNK, n_keys)
            s = _dot_nt(q, k_ref[0, c0:c1, kv * dk:(kv + 1) * dk])
            v = v_ref[0, c0:c1, kv * dv:(kv + 1) * dv]
            m_c = jnp.max(s, axis=-1, keepdims=True)
            if m is None:
                m = m_c
                p = jnp.exp2((s - m).astype(BF16))
                l = jnp.sum(p.astype(F32), axis=-1, keepdims=True)
                acc = _dot(p, v)
            else:
                m_new = jnp.maximum(m, m_c)
                alpha = jnp.exp2(m - m_new)
                p = jnp.exp2((s - m_new).astype(BF16))
                l = alpha * l + jnp.sum(p.astype(F32), axis=-1, keepdims=True)
                acc = alpha * acc + _dot(p, v)
                m = m_new
        o = acc / l
        if diff_lambda_init is None:
            for gi in range(g):
                o_ref[0, :, (kv * g + gi) * dv:(kv * g + gi + 1) * dv] = o[gi * tq:(gi + 1) * tq].astype(BF16)
        else:
            lam = (jnp.exp(jnp.sum(aux_ref[0:1, :] * aux_ref[1:2, :], axis=-1, keepdims=True))
                   - jnp.exp(jnp.sum(aux_ref[2:3, :] * aux_ref[3:4, :], axis=-1, keepdims=True))
                   + diff_lambda_init)
            od = o[0:tq] - lam * o[tq:2 * tq]
            od = _rms(od, dv) * aux_ref[4:5, :] * (1.0 - diff_lambda_init)
            o_ref[0, :, kv * dv:(kv + 1) * dv] = od.astype(BF16)


def _attn_call(q, k, v, *, n_heads_kv, g, dk, dv, nkv, tq, q_tile0, n_q_tiles, key_tile0, n_keys,
               aux=None, diff_lambda_init=None, name="attn"):
    bsz = q.shape[0]
    n_out_heads = n_heads_kv * (g if diff_lambda_init is None else 1)
    out_w = nkv * (g if diff_lambda_init is None else 1) * dv
    kern = functools.partial(_attn_kernel, nkv=nkv, g=g, dk=dk, dv=dv, tq=tq, n_keys=n_keys,
                             diff_lambda_init=diff_lambda_init)
    in_specs = [
        pl.BlockSpec((1, tq, nkv * g * dk), lambda b, hh, qi: (b, qi + q_tile0, hh)),
        pl.BlockSpec((1, n_keys, nkv * dk), lambda b, hh, qi: (b, key_tile0, hh)),
        pl.BlockSpec((1, n_keys, nkv * dv), lambda b, hh, qi: (b, key_tile0, hh)),
    ]
    args = [q, k, v]
    if aux is not None:
        in_specs.append(pl.BlockSpec(aux.shape, lambda b, hh, qi: (0, 0)))
        args.append(aux)
    return pl.pallas_call(
        kern,
        grid=(bsz, n_heads_kv // nkv, n_q_tiles),
        in_specs=in_specs,
        out_specs=pl.BlockSpec((1, tq, out_w), lambda b, hh, qi: (b, qi, hh)),
        out_shape=jax.ShapeDtypeStruct((bsz, n_q_tiles * tq, n_out_heads * dv), BF16),
        compiler_params=pltpu.CompilerParams(
            dimension_semantics=("parallel", "parallel", "parallel"), vmem_limit_bytes=VMEM_LIMIT),
        name=name,
    )(*args)


WIN_TQ = 256
WIN_BAND = WIN_TQ + 2 * WINDOW
NEG_BIG = -1e30


WIN_Q_TILES = SEQ // WIN_TQ


def _window_band(tile):
    q0 = (tile % WIN_Q_TILES) * WIN_TQ
    return q0, pl.multiple_of(jnp.clip(q0 - WINDOW, 0, SEQ - WIN_BAND), WINDOW)


def _window_kernel(q_ref, k_ref, v_ref, sink_ref, o_ref, sloc0, sctx0, sloc1, sctx1, *, n_total):
    g = D_HEADS // D_KV_HEADS
    dk = HEAD_DIM
    s = pl.program_id(0)

    @pl.when(s == 0)
    def _():
        sloc1[...] = jnp.zeros_like(sloc1)
        sctx1[...] = jnp.zeros_like(sctx1)

    def step(write, read):
        sloc_w, sctx_w = write
        sloc_r, sctx_r = read
        q0, start = _window_band(jnp.minimum(s, n_total - 1))
        q = jnp.concatenate([q_ref[0, :, gi * dk:(gi + 1) * dk] for gi in range(g)], axis=0)
        row = (lax.broadcasted_iota(jnp.int32, (g * WIN_TQ, WIN_BAND), 0) & (WIN_TQ - 1)) + q0
        col = lax.broadcasted_iota(jnp.int32, (g * WIN_TQ, WIN_BAND), 1) + start
        sloc_w[...] = jnp.where(jnp.abs(row - col) <= WINDOW,
                                _dot_nt(q, k_ref[0, pl.ds(start, WIN_BAND), :]), NEG_BIG)
        sctx_w[...] = _dot_nt(q, k_ref[0, SEQ:TOK, :])

        _, start = _window_band(jnp.maximum(s - 1, 0))
        s_loc = sloc_r[...]
        s_ctx = sctx_r[...]
        sink = sink_ref[0]
        m = jnp.maximum(jnp.maximum(jnp.max(s_loc, axis=-1, keepdims=True),
                                    jnp.max(s_ctx, axis=-1, keepdims=True)), sink)
        p_loc = jnp.exp2(s_loc - m)
        p_ctx = jnp.exp2(s_ctx - m)
        l = (jnp.sum(p_loc, axis=-1, keepdims=True) + jnp.sum(p_ctx, axis=-1, keepdims=True)
             + jnp.exp2(sink - m))
        acc = (_dot(p_ctx.astype(BF16), v_ref[0, SEQ:TOK, :])
               + _dot(p_loc.astype(BF16), v_ref[0, pl.ds(start, WIN_BAND), :]))
        o = acc / l
        for gi in range(g):
            o_ref[0, :, gi * dk:(gi + 1) * dk] = o[gi * WIN_TQ:(gi + 1) * WIN_TQ].astype(BF16)

    @pl.when(s % 2 == 0)
    def _():
        step((sloc0, sctx0), (sloc1, sctx1))

    @pl.when(s % 2 == 1)
    def _():
        step((sloc1, sctx1), (sloc0, sctx0))


def _window_call(q, k, v, sink_col):
    bsz = q.shape[0]
    g = D_HEADS // D_KV_HEADS
    n_total = bsz * D_KV_HEADS * WIN_Q_TILES
    split = lambda t: (t // (D_KV_HEADS * WIN_Q_TILES), (t // WIN_Q_TILES) % D_KV_HEADS, t % WIN_Q_TILES)
    stage_a = lambda s: split(jnp.minimum(s, n_total - 1))
    stage_b = lambda s: split(jnp.maximum(s - 1, 0))

    def q_index(s):
        b, hh, qi = stage_a(s)
        return (b, qi, hh)

    def k_index(s):
        b, hh, _ = stage_a(s)
        return (b, 0, hh)

    def v_index(s):
        b, hh, _ = stage_b(s)
        return (b, 0, hh)

    def o_index(s):
        b, hh, qi = stage_b(s)
        return (b, qi, hh)

    m_rows = g * WIN_TQ
    return pl.pallas_call(
        functools.partial(_window_kernel, n_total=n_total),
        grid=(n_total + 1,),
        in_specs=[
            pl.BlockSpec((1, WIN_TQ, g * HEAD_DIM), q_index),
            pl.BlockSpec((1, TOK, HEAD_DIM), k_index),
            pl.BlockSpec((1, TOK, HEAD_DIM), v_index),
            pl.BlockSpec((1, m_rows, 1), lambda s: (stage_b(s)[1], 0, 0)),
        ],
        out_specs=pl.BlockSpec((1, WIN_TQ, g * HEAD_DIM), o_index),
        out_shape=jax.ShapeDtypeStruct((bsz, SEQ, D_HEADS * HEAD_DIM), BF16),
        scratch_shapes=[pltpu.VMEM((m_rows, WIN_BAND), F32), pltpu.VMEM((m_rows, CTX_LEN), F32),
                        pltpu.VMEM((m_rows, WIN_BAND), F32), pltpu.VMEM((m_rows, CTX_LEN), F32)],
        compiler_params=pltpu.CompilerParams(
            dimension_semantics=("arbitrary",), vmem_limit_bytes=VMEM_LIMIT),
        name="window_attn",
    )(q, k, v, sink_col)


def _route(logits, lane):
    lane_f = lane.astype(F32)
    lg = jnp.where(lane < N_GROUPS, logits, NEG_BIG)
    g_max = jnp.max(lg, axis=-1, keepdims=True)
    p_top = 1.0 / jnp.sum(jnp.exp(lg - g_max), axis=-1, keepdims=True)
    g_idx = jnp.min(jnp.where(lg == g_max, lane_f, float(LANES)), axis=-1, keepdims=True)
    e_lane = lane - N_GROUPS
    in_group = (e_lane >= 0) & (e_lane < N_EXPERTS) & ((e_lane // EXPERTS_PER_GROUP).astype(F32) == g_idx)
    le = jnp.where(in_group, logits, NEG_BIG)
    m1 = jnp.max(le, axis=-1, keepdims=True)
    i1 = jnp.min(jnp.where(le == m1, lane_f, float(LANES)), axis=-1, keepdims=True)
    le2 = jnp.where(lane_f == i1, NEG_BIG, le)
    m2 = jnp.max(le2, axis=-1, keepdims=True)
    i2 = jnp.min(jnp.where(le2 == m2, lane_f, float(LANES)), axis=-1, keepdims=True)
    t = jnp.exp(m2 - m1)
    w1 = p_top / (1.0 + t)
    w2 = p_top * t / (1.0 + t)
    return i1, i2, w1, w2


def _outproj_kernel(*refs, has_ctx, n_tiles, n_total):
    if has_ctx:
        (o1_ref, o2_ref, h_ref, o1c_ref, o2c_ref, hc_ref, mod_ref, w1_ref, w2_ref, rw_ref, rb_ref,
         hn_ref, fx_ref, rt_ref, cnt_ref, run_ref, logit_ref) = refs
    else:
        (o1_ref, o2_ref, h_ref, mod_ref, w1_ref, w2_ref, rw_ref, rb_ref,
         hn_ref, fx_ref, rt_ref, cnt_ref, run_ref, logit_ref) = refs
    d = D_MODEL
    s = pl.program_id(0)

    @pl.when(s == 0)
    def _():
        run_ref[...] = jnp.zeros_like(run_ref)
        logit_ref[...] = jnp.zeros_like(logit_ref)

    logits = logit_ref[...]
    lane = lax.broadcasted_iota(jnp.int32, (ROW_TILE, LANES), 1)
    lane_f = lane.astype(F32)
    i1, i2, w1, w2 = _route(logits, lane)

    o1, o2, h = o1_ref[0], o2_ref[0], h_ref[0]
    if has_ctx:
        is_ctx = jnp.minimum(s, n_total - 1) % n_tiles == CTX_TILE
        o1 = jnp.where(is_ctx, o1c_ref[0], o1)
        o2 = jnp.where(is_ctx, o2c_ref[0], o2)
        h = jnp.where(is_ctx, hc_ref[0], h)
    acc = _dot(o1, w1_ref[...]) + _dot(o2, w2_ref[...])
    hn = h + mod_ref[0, :, 2 * d:3 * d] * acc
    hn_ref[0] = hn
    fx = _modulate(hn, mod_ref[0, :, 3 * d:4 * d], mod_ref[0, :, 4 * d:5 * d])
    fx_ref[0] = _pack_bf16_pairs(fx)
    hi = fx.astype(BF16)
    lo = (fx - hi.astype(F32)).astype(BF16)
    part = _dot(jnp.concatenate([hi, lo], axis=0), rw_ref[...])
    logit_ref[...] = (part[:ROW_TILE, :LANES] + part[:ROW_TILE, LANES:]
                      + part[ROW_TILE:, :LANES] + part[ROW_TILE:, LANES:] + rb_ref[...])

    hit1, hit2 = lane_f == i1, lane_f == i2
    multi_hot = jnp.where(s > 0, jnp.where(hit1, 1.0, 0.0) + jnp.where(hit2, 1.0, 0.0), 0.0)
    tri = (lax.broadcasted_iota(jnp.int32, (ROW_TILE, ROW_TILE), 0)
           > lax.broadcasted_iota(jnp.int32, (ROW_TILE, ROW_TILE), 1))
    before = _dot(jnp.where(tri, 1.0, 0.0).astype(BF16), multi_hot.astype(BF16)) + run_ref[...]
    r1 = jnp.sum(jnp.where(hit1, before, 0.0), axis=-1, keepdims=True)
    r2 = jnp.sum(jnp.where(hit2, before, 0.0), axis=-1, keepdims=True)
    run_ref[...] = run_ref[...] + jnp.sum(multi_hot, axis=0, keepdims=True)
    cnt_ref[...] = jnp.broadcast_to(run_ref[...], cnt_ref.shape)

    out = jnp.where(lane == 0, i1 - N_GROUPS, 0.0)
    out = jnp.where(lane == 1, i2 - N_GROUPS, out)
    out = jnp.where(lane == 2, w1, out)
    out = jnp.where(lane == 3, w2, out)
    out = jnp.where(lane == 4, r1, out)
    out = jnp.where(lane == 5, r2, out)
    rt_ref[0] = out


def _outproj_call(lat, ctx, mod, w1, w2, rw, rb, n_tiles):
    bsz = lat[2].shape[0]
    has_ctx = ctx is not None
    n_total = bsz * n_tiles
    stage_a = lambda s: jnp.minimum(s, n_total - 1)
    stage_b = lambda s: jnp.maximum(s - 1, 0)
    bj = lambda t: (t // n_tiles, t % n_tiles)

    def a_tile(w, kind):
        def index(s):
            b, j = bj(stage_a(s))
            return (b, {"all": j, "lat": jnp.minimum(j, CTX_TILE - 1), "ctx": 0}[kind], 0)
        return pl.BlockSpec((1, ROW_TILE, w), index)

    def b_tile(w):
        return pl.BlockSpec((1, ROW_TILE, w), lambda s: (*bj(stage_b(s)), 0))

    in_specs = [a_tile(a.shape[-1], "lat") for a in lat]
    args = list(lat)
    if has_ctx:
        in_specs += [a_tile(a.shape[-1], "ctx") for a in ctx]
        args += list(ctx)
    in_specs += [
        pl.BlockSpec((1, 1, 6 * D_MODEL), lambda s: (_mod_row(*bj(stage_a(s))), 0, 0)),
        _resident(w1.shape), _resident(w2.shape), _resident(rw.shape), _resident(rb.shape),
    ]
    args += [mod, w1, w2, rw, rb]
    n_rows = n_tiles * ROW_TILE
    return pl.pallas_call(
        functools.partial(_outproj_kernel, has_ctx=has_ctx, n_tiles=n_tiles, n_total=n_total),
        grid=(n_total + 1,),
        in_specs=in_specs,
        out_specs=[a_tile(D_MODEL, "all"), a_tile(PACKED_W, "all"), b_tile(LANES),
                   pl.BlockSpec((8, LANES), lambda s: (0, 0))],
        out_shape=[jax.ShapeDtypeStruct((bsz, n_rows, D_MODEL), F32),
                   jax.ShapeDtypeStruct((bsz, n_rows, PACKED_W), U32),
                   jax.ShapeDtypeStruct((bsz, n_rows, LANES), F32),
                   jax.ShapeDtypeStruct((8, LANES), F32)],
        scratch_shapes=[pltpu.VMEM((1, LANES), F32), pltpu.VMEM((ROW_TILE, LANES), F32)],
        compiler_params=pltpu.CompilerParams(
            dimension_semantics=("arbitrary",), vmem_limit_bytes=VMEM_LIMIT),
        name="outproj_router",
    )(*args)


def _dispatch_kernel(lo_ref, hi_ref, slot_ref, fx_ref, xs_hbm, zbuf, stage, sems, zsem):
    n_steps = pl.num_programs(0) * pl.num_programs(1)
    step = pl.program_id(0) * pl.num_programs(1) + pl.program_id(1)
    cur = step % 2
    src = stage.at[cur]
    src[...] = fx_ref[0]
    for r in range(ROW_TILE):
        for k in range(TOP_K):
            pltpu.make_async_copy(src.at[pl.ds(r, 1)], xs_hbm.at[pl.ds(slot_ref[0, k, r], 1)],
                                  sems.at[cur]).start(priority=k % 2)

    def wait_tile(slot):
        for _ in range(TOP_K):
            pltpu.make_async_copy(stage.at[slot], xs_hbm.at[pl.ds(0, ROW_TILE)], sems.at[slot]).wait()

    @pl.when(step > 0)
    def _():
        wait_tile(1 - cur)

    @pl.when(step == n_steps - 1)
    def _():
        zbuf[...] = jnp.zeros_like(zbuf)

        rows8 = zbuf.shape[0]

        def one(s, start):
            cp = pltpu.make_async_copy(zbuf.at[pl.ds(0, 1)], xs_hbm.at[pl.ds(s if start else 0, 1)], zsem)
            cp.start() if start else cp.wait()

        def eight(s, start):
            at = pl.multiple_of(s, rows8) if start else 0
            cp = pltpu.make_async_copy(zbuf, xs_hbm.at[pl.ds(at, rows8)], zsem)
            cp.start() if start else cp.wait()

        def sweep(start):
            def per_expert(e, carry):
                lo, hi = lo_ref[e], hi_ref[e]
                head_end = jnp.minimum((lo + rows8 - 1) // rows8 * rows8, hi)
                n_groups = (hi - head_end) // rows8
                body_end = head_end + n_groups * rows8
                def single(s, c):
                    one(s, start)
                    return c

                def group(i, c):
                    eight(head_end + i * rows8, start)
                    return c
                lax.fori_loop(lo, head_end, single, 0)
                lax.fori_loop(0, n_groups, group, 0)
                lax.fori_loop(body_end, hi, single, 0)
                return carry
            lax.fori_loop(0, N_EXPERTS, per_expert, 0)

        sweep(True)
        sweep(False)
        wait_tile(cur)


def _dispatch_call(fx, slots, fill_lo, fill_hi, n_blocks):
    bsz, n_rows, _ = fx.shape
    n_tiles = n_rows // ROW_TILE
    grid_spec = pltpu.PrefetchScalarGridSpec(
        num_scalar_prefetch=2,
        grid=(bsz, n_tiles),
        in_specs=[
            pl.BlockSpec((1, TOP_K, ROW_TILE), lambda b, j, lo, hi: (b * n_tiles + j, 0, 0),
                         memory_space=pltpu.SMEM),
            pl.BlockSpec((1, ROW_TILE, PACKED_W), lambda b, j, lo, hi: (b, j, 0)),
        ],
        out_specs=pl.BlockSpec(memory_space=pl.ANY),
        scratch_shapes=[pltpu.VMEM((8, PACKED_W), U32), pltpu.VMEM((2, ROW_TILE, PACKED_W), U32),
                        pltpu.SemaphoreType.DMA((2,)), pltpu.SemaphoreType.DMA(())],
    )
    return pl.pallas_call(
        _dispatch_kernel,
        grid_spec=grid_spec,
        out_shape=jax.ShapeDtypeStruct((n_blocks * MOE_BLOCK, PACKED_W), U32),
        compiler_params=pltpu.CompilerParams(
            dimension_semantics=("arbitrary", "arbitrary"), vmem_limit_bytes=VMEM_LIMIT),
        name="moe_dispatch",
    )(fill_lo, fill_hi, slots, fx)


CAST_ROWS = 256


def _moe_kernel(be_ref, nu_ref, nx_ref, x_ref, wg_hbm, wu_hbm, wd_hbm, y_ref,
                sg, su, sd, wg, wu, wd, sems, *, layer):
    i = pl.program_id(0)
    e = be_ref[i]
    used = i < nu_ref[0]
    staged = ((wg_hbm, sg, wg, 0), (wu_hbm, su, wu, 1), (wd_hbm, sd, wd, 2))

    def fetch(expert):
        for hbm, stage, _, s in staged:
            pltpu.make_async_copy(hbm.at[layer, expert], stage, sems.at[s]).start(priority=1)

    @pl.when(i == 0)
    def _():
        fetch(e)

    first_of_expert = (i == 0) | (e != be_ref[jnp.maximum(i - 1, 0)])

    @pl.when(used & first_of_expert)
    def _():
        for hbm, stage, dst, s in staged:
            pltpu.make_async_copy(hbm.at[layer, 0], stage, sems.at[s]).wait()

            def cast(c, carry):
                rows = pl.ds(pl.multiple_of(c * CAST_ROWS, CAST_ROWS), CAST_ROWS)
                dst[rows, :] = stage[rows, :].astype(BF16)
                return carry
            lax.fori_loop(0, stage.shape[0] // CAST_ROWS, cast, 0)
        nxt = nx_ref[e]

        @pl.when(nxt >= 0)
        def _():
            fetch(nxt)

    @pl.when(used)
    def _():
        x = _unpack_bf16_pairs(x_ref[...]).astype(BF16)
        gt = _dot(x, wg[...])
        up = _dot(x, wu[...])
        u = (_silu(gt) * up).astype(BF16)
        y_ref[...] = _pack_bf16_pairs(_dot(u, wd[...]))


def _moe_call(xs, block_expert, n_used, next_expert, wg, wu, wd, layer):
    n_blocks = block_expert.shape[0]
    row_blk = pl.BlockSpec((MOE_BLOCK, PACKED_W), lambda i, be, nu, nx: (jnp.minimum(i, nu[0] - 1), 0))
    hbm = pl.BlockSpec(memory_space=pl.ANY)
    grid_spec = pltpu.PrefetchScalarGridSpec(
        num_scalar_prefetch=3,
        grid=(n_blocks,),
        in_specs=[row_blk, hbm, hbm, hbm],
        out_specs=row_blk,
        scratch_shapes=[
            pltpu.VMEM((D_MODEL, D_EXPERT), F32), pltpu.VMEM((D_MODEL, D_EXPERT), F32),
            pltpu.VMEM((D_EXPERT, D_MODEL), F32),
            pltpu.VMEM((D_MODEL, D_EXPERT), BF16), pltpu.VMEM((D_MODEL, D_EXPERT), BF16),
            pltpu.VMEM((D_EXPERT, D_MODEL), BF16),
            pltpu.SemaphoreType.DMA((3,)),
        ],
    )
    return pl.pallas_call(
        functools.partial(_moe_kernel, layer=layer),
        grid_spec=grid_spec,
        out_shape=jax.ShapeDtypeStruct((n_blocks * MOE_BLOCK, PACKED_W), U32),
        compiler_params=pltpu.CompilerParams(
            dimension_semantics=("arbitrary",), vmem_limit_bytes=VMEM_LIMIT),
        name="moe_experts",
    )(block_expert, n_used, next_expert, xs, wg, wu, wd)


def _combine_kernel(slot_ref, slot_next_ref, h_ref, rt_ref, mod_ref, ys_hbm, o_ref, ybuf, sems):
    d = D_MODEL
    n_steps = pl.num_programs(0) * pl.num_programs(1)
    step = pl.program_id(0) * pl.num_programs(1) + pl.program_id(1)
    cur = step % 2

    @pl.when(step == 0)
    def _():
        _start_row_gathers(slot_ref, ys_hbm, ybuf.at[0], sems.at[0])

    _start_row_gathers(slot_next_ref, ys_hbm, ybuf.at[1 - cur], sems.at[1 - cur])
    _wait_row_gathers(ys_hbm, ybuf.at[cur], sems.at[cur])
    rt = rt_ref[0]
    y = rt[:, 2:3] * _unpack_bf16_pairs(ybuf[cur, 0]) + rt[:, 3:4] * _unpack_bf16_pairs(ybuf[cur, 1])
    o_ref[0] = h_ref[0] + mod_ref[0, :, 5 * d:6 * d] * y

    @pl.when(step == n_steps - 1)
    def _():
        _wait_row_gathers(ys_hbm, ybuf.at[1 - cur], sems.at[1 - cur])


def _combine_call(h, ys, slots, rt, mod, n_tiles):
    bsz = h.shape[0]
    n_steps = bsz * n_tiles
    tile = lambda w: pl.BlockSpec((1, ROW_TILE, w), lambda b, j: (b, j, 0))
    slot_spec = lambda ahead: pl.BlockSpec(
        (1, TOP_K, ROW_TILE), lambda b, j: (jnp.minimum(b * n_tiles + j + ahead, n_steps - 1), 0, 0),
        memory_space=pltpu.SMEM)
    return pl.pallas_call(
        _combine_kernel,
        grid=(bsz, n_tiles),
        in_specs=[
            slot_spec(0), slot_spec(1),
            tile(D_MODEL), tile(LANES),
            pl.BlockSpec((1, 1, 6 * D_MODEL), lambda b, j: (_mod_row(b, j), 0, 0)),
            pl.BlockSpec(memory_space=pl.ANY),
        ],
        out_specs=tile(D_MODEL),
        out_shape=jax.ShapeDtypeStruct((bsz, n_tiles * ROW_TILE, D_MODEL), F32),
        scratch_shapes=[pltpu.VMEM((2, TOP_K, ROW_TILE, PACKED_W), U32), pltpu.SemaphoreType.DMA((2,))],
        compiler_params=pltpu.CompilerParams(
            dimension_semantics=("arbitrary", "arbitrary"), vmem_limit_bytes=VMEM_LIMIT),
        name="moe_combine",
    )(slots, slots, h, rt, mod, ys)


def _rope_tables():
    rows = SEQ // GRID_W
    row = np.repeat(np.arange(rows), GRID_W).astype(np.float32)
    col = np.tile(np.arange(GRID_W), rows).astype(np.float32)

    def tables(dim):
        n_freq = dim // 4
        inv = (1.0 / (np.float32(ROPE_THETA) ** (np.arange(n_freq, dtype=np.float32) / np.float32(n_freq))))
        inv = inv.astype(np.float32)
        ang_r = row[:, None] * inv
        ang_c = col[:, None] * inv
        ang = np.concatenate([ang_r, ang_r, ang_c, ang_c], axis=-1)
        sign = np.tile(np.concatenate([-np.ones((n_freq,), np.float32), np.ones((n_freq,), np.float32)]), 2)
        return np.cos(ang).astype(np.float32), (np.sin(ang) * sign).astype(np.float32)

    cos128, sin128 = tables(HEAD_DIM)
    cos64, sin64 = tables(B_ROPE)
    ones, zeros = np.ones((SEQ, 64), np.float32), np.zeros((SEQ, 64), np.float32)
    lat = np.concatenate([cos128, sin128,
                          cos64, cos64, sin64, sin64,
                          cos64, ones, sin64, zeros], axis=-1)
    ident = np.concatenate([np.ones((CTX_LEN, 128), np.float32), np.zeros((CTX_LEN, 128), np.float32)], axis=-1)
    return jnp.asarray(np.concatenate([lat, np.tile(ident, (1, 3))], axis=0))


def _pad_lanes(v, n):
    return jnp.pad(v, (0, n - v.shape[0]))


def _router_weights(wr_g, br_g, wr_e, br_e):
    w = jnp.pad(jnp.concatenate([wr_g, wr_e], axis=1), ((0, 0), (0, LANES - N_GROUPS - N_EXPERTS)))
    hi = w.astype(BF16)
    lo = (w - hi.astype(F32)).astype(BF16)
    b = _pad_lanes(jnp.concatenate([br_g, br_e]), LANES).reshape(1, LANES)
    return jnp.concatenate([hi, lo], axis=1), b


def _moe_experts(fx, rt, cnt, wg, wu, wd, layer):
    bsz, n_rows, _ = fx.shape
    n_tiles = n_rows // ROW_TILE
    n_blocks = -(-(bsz * n_rows * TOP_K) // MOE_BLOCK) + N_EXPERTS
    experts = jnp.arange(N_EXPERTS, dtype=jnp.int32)
    counts = cnt[0, N_GROUPS:N_GROUPS + N_EXPERTS].astype(jnp.int32)
    padded = (counts + MOE_BLOCK - 1) // MOE_BLOCK * MOE_BLOCK
    pad_end = jnp.cumsum(padded)
    pad_start = pad_end - padded
    e_idx = rt[..., 0:TOP_K].astype(jnp.int32)
    rank = rt[..., 4:4 + TOP_K].astype(jnp.int32)
    slot = rank + jnp.sum(jnp.where(e_idx[..., None] == experts, pad_start, 0), axis=-1)
    slots = slot.reshape(bsz * n_tiles, ROW_TILE, TOP_K).transpose(0, 2, 1)
    block_start = jnp.arange(n_blocks, dtype=jnp.int32) * MOE_BLOCK
    block_expert = jnp.minimum(jnp.sum(pad_end[None, :] <= block_start[:, None], axis=1), N_EXPERTS - 1)
    n_used = pad_end[-1:] // MOE_BLOCK
    later = (experts[None, :] > experts[:, None]) & (counts[None, :] > 0)
    next_expert = jnp.where(jnp.any(later, axis=1), jnp.argmax(later, axis=1), -1)
    i32 = lambda a: a.astype(jnp.int32)
    xs = _dispatch_call(fx, i32(slots), i32(pad_start + counts), i32(pad_end), n_blocks)
    ys = _moe_call(xs, i32(block_expert), i32(n_used), i32(next_expert), wg, wu, wd, layer)
    return ys, i32(slots)


def kernel(x, c, ctx, c_ctx, mod_w, mod_b, even_w_in, even_w_out, a_q_norm, a_k_norm, b_cq_norm, b_w_uq,
           b_ckv_norm, b_w_ukv, b_q_norm, b_k_norm, odd_w_in, odd_w_out, c_q_norm, c_k_norm, c_lambda_q1,
           c_lambda_k1, c_lambda_q2, c_lambda_k2, c_subln, d_q_norm, d_k_norm, d_sink, moe_wr_group,
           moe_br_group, moe_wr_expert, moe_br_expert, moe_w_gate, moe_w_up, moe_w_down):
    bsz = x.shape[0]
    c_all = jnp.zeros((MOD_ROWS, D_MODEL), F32).at[:bsz].set(c).at[MOD_CTX_ROW].set(c_ctx)
    mod_all = _mod_call(c_all, mod_w, mod_b)
    tab = _rope_tables()

    i = 0
    mod = mod_all[0].reshape(MOD_ROWS, 1, 6 * D_MODEL)
    scale_a = HEAD_DIM ** -0.5 * LOG2_E
    scale_b = (B_NOPE + B_ROPE) ** -0.5 * LOG2_E
    win = jnp.pad(even_w_in[i], ((0, 0), (0, EVEN_IN_PAD - even_w_in.shape[-1]))).astype(BF16)
    wuq = b_w_uq[i].reshape(B_Q_LORA, B_HEADS, B_NOPE + B_ROPE)
    wuq = jnp.pad(wuq, ((0, 0), (0, 0), (0, B_QK_PAD - B_NOPE - B_ROPE))).reshape(B_Q_LORA, -1).astype(BF16)
    wukv = b_w_ukv[i].reshape(B_KV_LORA, B_HEADS, B_NOPE + B_V)
    wukv = jnp.concatenate([wukv[:, :, :B_NOPE].reshape(B_KV_LORA, -1),
                            wukv[:, :, B_NOPE:].reshape(B_KV_LORA, -1)], axis=1).astype(BF16)
    gains = jnp.stack([
        _pad_lanes(a_q_norm[i] * scale_a, 512), _pad_lanes(a_k_norm[i], 512),
        b_cq_norm[i], b_ckv_norm[i],
        _pad_lanes(b_q_norm[i] * scale_b, 512), _pad_lanes(b_k_norm[i], 512),
        jnp.zeros((512,), F32), jnp.zeros((512,), F32)])
    qa, ka, va, qb, kb, vb = _even_proj_call(x, ctx, mod, tab, gains, win, wuq, wukv)

    g_a = A_HEADS // A_KV_HEADS
    lat = dict(q_tile0=0, key_tile0=0, n_keys=TOK)
    oa = _attn_call(qa, ka, va, n_heads_kv=A_KV_HEADS, g=g_a, dk=HEAD_DIM, dv=HEAD_DIM, nkv=2,
                    tq=256, n_q_tiles=SEQ // 256, name="gqa_latent", **lat)
    ob = _attn_call(qb, kb, vb, n_heads_kv=B_HEADS, g=1, dk=B_QK_PAD, dv=B_V, nkv=2,
                    tq=512, n_q_tiles=SEQ // 512, name="mla_latent", **lat)
    cx = dict(tq=CTX_LEN, q_tile0=SEQ // CTX_LEN, n_q_tiles=1, key_tile0=SEQ // CTX_LEN, n_keys=CTX_LEN)
    oa_c = _attn_call(qa, ka, va, n_heads_kv=A_KV_HEADS, g=g_a, dk=HEAD_DIM, dv=HEAD_DIM, nkv=2,
                      name="gqa_context", **cx)
    ob_c = _attn_call(qb, kb, vb, n_heads_kv=B_HEADS, g=1, dk=B_QK_PAD, dv=B_V, nkv=2,
                      name="mla_context", **cx)

    w_out = even_w_out[i].astype(BF16)
    rw, rb = _router_weights(moe_wr_group[0], moe_br_group[0], moe_wr_expert[0], moe_br_expert[0])
    hn, fx, rt, cnt = _outproj_call((oa, ob, x), (oa_c, ob_c, ctx), mod,
                                    w_out[:A_HEADS * HEAD_DIM], w_out[A_HEADS * HEAD_DIM:],
                                    rw, rb, N_ROW_TILES)
    ys, slots = _moe_experts(fx, rt, cnt, moe_w_gate, moe_w_up, moe_w_down, 0)
    hn0, rt0, mod0 = hn, rt, mod

    layer = 1
    mod = mod_all[1].reshape(MOD_ROWS, 1, 6 * D_MODEL)
    lambda_init = 0.8 - 0.6 * math.exp(-0.3 * layer)
    scale_c = C_HD ** -0.5 * LOG2_E
    scale_d = HEAD_DIM ** -0.5 * LOG2_E
    win = odd_w_in[i].astype(BF16)
    gains = jnp.stack([
        jnp.tile(c_q_norm[i] * scale_c, 2), jnp.tile(c_k_norm[i], 2),
        d_q_norm[i] * scale_d, d_k_norm[i],
        jnp.zeros((128,), F32), jnp.zeros((128,), F32), jnp.zeros((128,), F32), jnp.zeros((128,), F32)])
    h, qc, kc, vc, qd, kd, vd = _odd_proj_call(hn0, ys, slots, rt0, mod0, mod, tab, gains, win)

    aux = jnp.stack([
        _pad_lanes(c_lambda_q1[i], 128), _pad_lanes(c_lambda_k1[i], 128),
        _pad_lanes(c_lambda_q2[i], 128), _pad_lanes(c_lambda_k2[i], 128),
        c_subln[i], jnp.zeros((128,), F32), jnp.zeros((128,), F32), jnp.zeros((128,), F32)])
    oc = _attn_call(qc, kc, vc, n_heads_kv=C_HEADS, g=2, dk=LANES, dv=C_V, nkv=2,
                    tq=512, n_q_tiles=SEQ // 512, aux=aux, diff_lambda_init=lambda_init,
                    name="diff_latent", **lat)
    g_d = D_HEADS // D_KV_HEADS
    sink_col = jnp.repeat((d_sink[i] * LOG2_E).reshape(D_KV_HEADS, g_d), WIN_TQ, axis=1)
    sink_col = sink_col.reshape(D_KV_HEADS, g_d * WIN_TQ, 1)
    od = _window_call(qd, kd, vd, sink_col)

    w_out = odd_w_out[i].astype(BF16)
    rw, rb = _router_weights(moe_wr_group[1], moe_br_group[1], moe_wr_expert[1], moe_br_expert[1])
    n_lat_tiles = SEQ // ROW_TILE
    hn, fx, rt, cnt = _outproj_call((oc, od, h), None, mod, w_out[:C_HEADS * C_V], w_out[C_HEADS * C_V:],
                                    rw, rb, n_lat_tiles)
    ys, slots = _moe_experts(fx, rt, cnt, moe_w_gate, moe_w_up, moe_w_down, 1)
    return _combine_call(hn, ys, slots, rt, mod, n_lat_tiles)
```

```python
import functools
import math

import jax
import jax.numpy as jnp
import numpy as np
from jax import lax
from jax.experimental import pallas as pl
from jax.experimental.pallas import tpu as pltpu

F32 = jnp.float32
BF16 = jnp.bfloat16

D_MODEL = 2048
BATCH = 8
SEQ = 2048
DEPTH = 2
GRID_W = 64
CTX_LEN = 256
TOK = SEQ + CTX_LEN
HEAD_DIM = 128
ROPE_THETA = 10000.0
NORM_EPS = 1e-6
A_HEADS = 8
A_KV_HEADS = 2
B_HEADS = 8
B_Q_LORA = 512
B_KV_LORA = 512
B_NOPE = 128
B_ROPE = 64
B_V = 128
B_QK_PAD = 256
C_HEADS = 8
C_HD = 64
C_V = 128
D_HEADS = 8
D_KV_HEADS = 2
WINDOW = 128
N_GROUPS = 4
EXPERTS_PER_GROUP = 8
N_EXPERTS = 32
D_EXPERT = 1024
MOE_BLOCK = 256
TOP_K = 2
LANES = 128
LOG2_E = math.log2(math.e)

ROW_TILE = 256
N_ROW_TILES = TOK // ROW_TILE
CTX_TILE = SEQ // ROW_TILE
MOD_ROWS = 16
MOD_CTX_ROW = BATCH
VMEM_LIMIT = 56 * 1024 * 1024


def _dot(a, b):
    return jnp.dot(a, b, preferred_element_type=F32)


def _dot_nt(a, b):
    return lax.dot_general(a, b, (((1,), (1,)), ((), ())), preferred_element_type=F32)


def _silu(x):
    return x / (1.0 + jnp.exp(-x))


def _rms(x, n):
    return x * lax.rsqrt(jnp.sum(x * x, axis=-1, keepdims=True) * (1.0 / n) + NORM_EPS)


def _modulate(x, shift, scale):
    return _rms(x, x.shape[-1]) * (1.0 + scale) + shift


def _rope(x, cos, sin_signed, quarter, lane):
    fwd = pltpu.roll(x, LANES - quarter, axis=1)
    bwd = pltpu.roll(x, quarter, axis=1)
    rot = jnp.where((lane // quarter) % 2 == 0, fwd, bwd)
    return x * cos + rot * sin_signed


PACKED_W = D_MODEL // 2
U32 = jnp.uint32


def _pack_bf16_pairs(x):
    n = x.shape[-1] // 2
    lo = lax.bitcast_convert_type(x[:, :n].astype(BF16).astype(F32), U32) >> 16
    hi = lax.bitcast_convert_type(x[:, n:].astype(BF16).astype(F32), U32) & U32(0xFFFF0000)
    return lo | hi


def _unpack_bf16_pairs(p):
    lo = lax.bitcast_convert_type(p << 16, F32)
    hi = lax.bitcast_convert_type(p & U32(0xFFFF0000), F32)
    return jnp.concatenate([lo, hi], axis=-1)


def _start_row_gathers(slot_ref, src_hbm, dst_ref, sem):
    for r in range(ROW_TILE):
        for k in range(TOP_K):
            pltpu.make_async_copy(src_hbm.at[pl.ds(slot_ref[0, k, r], 1)], dst_ref.at[k, pl.ds(r, 1)],
                                  sem).start(priority=k % 2)


def _wait_row_gathers(src_hbm, dst_ref, sem):
    for k in range(TOP_K):
        pltpu.make_async_copy(src_hbm.at[pl.ds(0, ROW_TILE)], dst_ref.at[k], sem).wait()


MOD_TN = 1024


def _mod_kernel(c_ref, w_ref, b_ref, o_ref):
    a = _silu(c_ref[...]).astype(BF16)
    o_ref[0] = _dot(a, w_ref[0].astype(BF16)) + b_ref[0]


def _mod_call(c_all, mod_w, mod_b):
    d6 = 6 * D_MODEL
    return pl.pallas_call(
        _mod_kernel,
        grid=(DEPTH, d6 // MOD_TN),
        in_specs=[
            pl.BlockSpec((MOD_ROWS, D_MODEL), lambda l, n: (0, 0)),
            pl.BlockSpec((1, D_MODEL, MOD_TN), lambda l, n: (l, 0, n)),
            pl.BlockSpec((1, 1, MOD_TN), lambda l, n: (l, 0, n)),
        ],
        out_specs=pl.BlockSpec((1, MOD_ROWS, MOD_TN), lambda l, n: (l, 0, n)),
        out_shape=jax.ShapeDtypeStruct((DEPTH, MOD_ROWS, d6), F32),
        compiler_params=pltpu.CompilerParams(
            dimension_semantics=("parallel", "parallel"), vmem_limit_bytes=VMEM_LIMIT),
        name="mod_vectors",
    )(c_all, mod_w, mod_b.reshape(DEPTH, 1, d6))


def _mod_row(b, j):
    return jnp.where(j == CTX_TILE, MOD_CTX_ROW, b)


def _resident(shape):
    nd = len(shape)
    return pl.BlockSpec(shape, lambda *_: (0,) * nd, pipeline_mode=pl.Buffered(1))


EVEN_IN_PAD = 2688


def _even_proj_kernel(hx_ref, hc_ref, mod_ref, tab_ref, g_ref, win_ref, wuq_ref, wukv_ref,
                      qa_ref, ka_ref, va_ref, qb_ref, kb_ref, vb_ref):
    d = D_MODEL
    x = jnp.where(pl.program_id(1) == CTX_TILE, hc_ref[0], hx_ref[0])
    a = _modulate(x, mod_ref[0, :, 0:d], mod_ref[0, :, d:2 * d]).astype(BF16)
    lane = lax.broadcasted_iota(jnp.int32, (ROW_TILE, LANES), 1)
    cos128, sin128 = tab_ref[:, 0:128], tab_ref[:, 128:256]
    cos64p, sin64p = tab_ref[:, 512:640], tab_ref[:, 640:768]

    z = _dot(a, win_ref[:, 0:1024])
    g_q = g_ref[0:1, 0:128]
    for h in range(A_HEADS):
        blk = _rms(z[:, h * 128:(h + 1) * 128], HEAD_DIM) * g_q
        qa_ref[0, :, h * 128:(h + 1) * 128] = _rope(blk, cos128, sin128, 32, lane).astype(BF16)
    z = _dot(a, win_ref[:, 1024:1536])
    g_k = g_ref[1:2, 0:128]
    for h in range(A_KV_HEADS):
        blk = _rms(z[:, h * 128:(h + 1) * 128], HEAD_DIM) * g_k
        ka_ref[0, :, h * 128:(h + 1) * 128] = _rope(blk, cos128, sin128, 32, lane).astype(BF16)
    va_ref[0] = z[:, 256:512].astype(BF16)

    z = _dot(a, win_ref[:, 1536:2048])
    cq = (_rms(z, B_Q_LORA) * g_ref[2:3, :]).astype(BF16)
    zq = _dot(cq, wuq_ref[...])
    gq0, gq1 = g_ref[4:5, 0:128], g_ref[4:5, 128:256]
    n_qk = float(B_NOPE + B_ROPE)
    for h in range(B_HEADS):
        b0 = zq[:, h * 256:h * 256 + 128]
        b1 = zq[:, h * 256 + 128:(h + 1) * 256]
        ss = jnp.sum(b0 * b0, axis=-1, keepdims=True) + jnp.sum(b1 * b1, axis=-1, keepdims=True)
        r = lax.rsqrt(ss * (1.0 / n_qk) + NORM_EPS)
        qb_ref[0, :, h * 256:h * 256 + 128] = (b0 * r * gq0).astype(BF16)
        qb_ref[0, :, h * 256 + 128:(h + 1) * 256] = _rope(b1 * r * gq1, cos64p, sin64p, 16, lane).astype(BF16)

    z = _dot(a, win_ref[:, 2048:2688])
    ckv = (_rms(z[:, 0:512], B_KV_LORA) * g_ref[3:4, :]).astype(BF16)
    kr = z[:, 512:640]
    ss_kr = jnp.sum(kr * kr, axis=-1, keepdims=True)
    kr_rot = _rope(kr * g_ref[5:6, 128:256], cos64p, sin64p, 16, lane)
    zkv = _dot(ckv, wukv_ref[...])
    vb_ref[0] = zkv[:, 1024:2048].astype(BF16)
    gk0 = g_ref[5:6, 0:128]
    for h in range(B_HEADS):
        kn = zkv[:, h * 128:(h + 1) * 128]
        ss = jnp.sum(kn * kn, axis=-1, keepdims=True) + ss_kr
        r = lax.rsqrt(ss * (1.0 / n_qk) + NORM_EPS)
        kb_ref[0, :, h * 256:h * 256 + 128] = (kn * r * gk0).astype(BF16)
        kb_ref[0, :, h * 256 + 128:(h + 1) * 256] = (kr_rot * r).astype(BF16)


def _even_proj_call(h_lat, h_ctx, mod, tab, gains, win, wuq, wukv):
    bsz = h_lat.shape[0]
    widths = (A_HEADS * HEAD_DIM, A_KV_HEADS * HEAD_DIM, A_KV_HEADS * HEAD_DIM,
              B_HEADS * B_QK_PAD, B_HEADS * B_QK_PAD, B_HEADS * B_V)
    tile = lambda w: pl.BlockSpec((1, ROW_TILE, w), lambda b, j: (b, j, 0))
    return pl.pallas_call(
        _even_proj_kernel,
        grid=(bsz, N_ROW_TILES),
        in_specs=[
            pl.BlockSpec((1, ROW_TILE, D_MODEL), lambda b, j: (b, jnp.minimum(j, CTX_TILE - 1), 0)),
            pl.BlockSpec((1, ROW_TILE, D_MODEL), lambda b, j: (b, 0, 0)),
            pl.BlockSpec((1, 1, 6 * D_MODEL), lambda b, j: (_mod_row(b, j), 0, 0)),
            pl.BlockSpec((ROW_TILE, 768), lambda b, j: (j, 0)),
            _resident(gains.shape), _resident(win.shape), _resident(wuq.shape), _resident(wukv.shape),
        ],
        out_specs=[tile(w) for w in widths],
        out_shape=[jax.ShapeDtypeStruct((bsz, TOK, w), BF16) for w in widths],
        compiler_params=pltpu.CompilerParams(
            dimension_semantics=("parallel", "parallel"), vmem_limit_bytes=VMEM_LIMIT),
        name="even_proj",
    )(h_lat, h_ctx, mod, tab, gains, win, wuq, wukv)


def _rms_halves(x, lane):
    x2 = x * x
    s_lo = jnp.sum(jnp.where(lane < 64, x2, 0.0), axis=-1, keepdims=True)
    s_hi = jnp.sum(jnp.where(lane < 64, 0.0, x2), axis=-1, keepdims=True)
    r = jnp.where(lane < 64, lax.rsqrt(s_lo * (1.0 / C_HD) + NORM_EPS), lax.rsqrt(s_hi * (1.0 / C_HD) + NORM_EPS))
    return x * r


def _odd_proj_kernel(slot_ref, slot_next_ref, hn_ref, rt_ref, mod_prev_ref, mod_ref, tab_ref, g_ref, win_ref,
                     ys_hbm, h_ref, qc_ref, kc_ref, vc_ref, qd_ref, kd_ref, vd_ref, ybuf, sems):
    d = D_MODEL
    n_steps = pl.num_programs(0) * pl.num_programs(1)
    step = pl.program_id(0) * pl.num_programs(1) + pl.program_id(1)
    cur = step % 2

    @pl.when(step == 0)
    def _():
        _start_row_gathers(slot_ref, ys_hbm, ybuf.at[0], sems.at[0])

    @pl.when(step < n_steps)
    def _():
        _start_row_gathers(slot_next_ref, ys_hbm, ybuf.at[1 - cur], sems.at[1 - cur])

    _wait_row_gathers(ys_hbm, ybuf.at[cur], sems.at[cur])
    rt = rt_ref[0]
    y = rt[:, 2:3] * _unpack_bf16_pairs(ybuf[cur, 0]) + rt[:, 3:4] * _unpack_bf16_pairs(ybuf[cur, 1])
    x = hn_ref[0] + mod_prev_ref[0, :, 5 * d:6 * d] * y
    h_ref[0] = x
    a = _modulate(x, mod_ref[0, :, 0:d], mod_ref[0, :, d:2 * d]).astype(BF16)
    lane = lax.broadcasted_iota(jnp.int32, (ROW_TILE, LANES), 1)
    cos128, sin128 = tab_ref[:, 0:128], tab_ref[:, 128:256]
    cos64, sin64 = tab_ref[:, 256:384], tab_ref[:, 384:512]

    z = _dot(a, win_ref[:, 0:1024])
    g_q = g_ref[0:1, :]
    for h in range(C_HEADS):
        blk = _rope(_rms_halves(z[:, h * 128:(h + 1) * 128], lane) * g_q, cos64, sin64, 16, lane)
        qc_ref[0, :, h * 256:h * 256 + 128] = jnp.where(lane < 64, blk, 0.0).astype(BF16)
        qc_ref[0, :, h * 256 + 128:(h + 1) * 256] = jnp.where(lane < 64, 0.0, blk).astype(BF16)
    z = _dot(a, win_ref[:, 1024:2048])
    g_k = g_ref[1:2, :]
    for h in range(C_HEADS):
        blk = _rope(_rms_halves(z[:, h * 128:(h + 1) * 128], lane) * g_k, cos64, sin64, 16, lane)
        kc_ref[0, :, h * 128:(h + 1) * 128] = blk.astype(BF16)
    vc_ref[0] = _dot(a, win_ref[:, 2048:3072]).astype(BF16)

    z = _dot(a, win_ref[:, 3072:4096])
    g_q = g_ref[2:3, :]
    for h in range(D_HEADS):
        blk = _rms(z[:, h * 128:(h + 1) * 128], HEAD_DIM) * g_q
        qd_ref[0, :, h * 128:(h + 1) * 128] = _rope(blk, cos128, sin128, 32, lane).astype(BF16)
    z = _dot(a, win_ref[:, 4096:4608])
    g_k = g_ref[3:4, :]
    for h in range(D_KV_HEADS):
        blk = _rms(z[:, h * 128:(h + 1) * 128], HEAD_DIM) * g_k
        kd_ref[0, :, h * 128:(h + 1) * 128] = _rope(blk, cos128, sin128, 32, lane).astype(BF16)
    vd_ref[0] = z[:, 256:512].astype(BF16)

    @pl.when(step == n_steps - 1)
    def _():
        _wait_row_gathers(ys_hbm, ybuf.at[1 - cur], sems.at[1 - cur])


def _odd_proj_call(hn, ys, slots, rt, mod_prev, mod, tab, gains, win):
    bsz = hn.shape[0]
    widths = (C_HEADS * 2 * LANES, C_HEADS * LANES, C_HEADS * C_V,
              D_HEADS * HEAD_DIM, D_KV_HEADS * HEAD_DIM, D_KV_HEADS * HEAD_DIM)
    n_steps = bsz * N_ROW_TILES
    tile = lambda w: pl.BlockSpec((1, ROW_TILE, w), lambda b, j: (b, j, 0))
    mod_spec = pl.BlockSpec((1, 1, 6 * D_MODEL), lambda b, j: (_mod_row(b, j), 0, 0))
    slot_spec = lambda ahead: pl.BlockSpec(
        (1, TOP_K, ROW_TILE), lambda b, j: (jnp.minimum(b * N_ROW_TILES + j + ahead, n_steps - 1), 0, 0),
        memory_space=pltpu.SMEM)
    return pl.pallas_call(
        _odd_proj_kernel,
        grid=(bsz, N_ROW_TILES),
        in_specs=[
            slot_spec(0), slot_spec(1), tile(D_MODEL), tile(LANES), mod_spec, mod_spec,
            pl.BlockSpec((ROW_TILE, 768), lambda b, j: (j, 0)),
            _resident(gains.shape), _resident(win.shape),
            pl.BlockSpec(memory_space=pl.ANY),
        ],
        out_specs=[tile(D_MODEL)] + [tile(w) for w in widths],
        out_shape=[jax.ShapeDtypeStruct((bsz, TOK, D_MODEL), F32)]
                  + [jax.ShapeDtypeStruct((bsz, TOK, w), BF16) for w in widths],
        scratch_shapes=[pltpu.VMEM((2, TOP_K, ROW_TILE, PACKED_W), U32), pltpu.SemaphoreType.DMA((2,))],
        compiler_params=pltpu.CompilerParams(
            dimension_semantics=("arbitrary", "arbitrary"), vmem_limit_bytes=VMEM_LIMIT),
        name="odd_proj",
    )(slots, slots, hn, rt, mod_prev, mod, tab, gains, win, ys)


KEY_CHUNK = 768


def _attn_kernel(*refs, nkv, g, dk, dv, tq, n_keys, diff_lambda_init):
    if diff_lambda_init is None:
        q_ref, k_ref, v_ref, o_ref = refs
    else:
        q_ref, k_ref, v_ref, aux_ref, o_ref = refs
    for kv in range(nkv):
        q = jnp.concatenate(
            [q_ref[0, :, (kv * g + gi) * dk:(kv * g + gi + 1) * dk] for gi in range(g)], axis=0)
        m = l = acc = None
        for c0 in range(0, n_keys, KEY_CHUNK):
            c1 = min(c0 + KEY_CHUNK, n_keys)
            s = _dot_nt(q, k_ref[0, c0:c1, kv * dk:(kv + 1) * dk])
            v = v_ref[0, c0:c1, kv * dv:(kv + 1) * dv]
            m_c = jnp.max(s, axis=-1, keepdims=True)
            if m is None:
                m = m_c
                p = jnp.exp2((s - m).astype(BF16))
                l = jnp.sum(p.astype(F32), axis=-1, keepdims=True)
                acc = _dot(p, v)
            else:
                m_new = jnp.maximum(m, m_c)
                alpha = jnp.exp2(m - m_new)
                p = jnp.exp2((s - m_new).astype(BF16))
                l = alpha * l + jnp.sum(p.astype(F32), axis=-1, keepdims=True)
                acc = alpha * acc + _dot(p, v)
                m = m_new
        o = acc / l
        if diff_lambda_init is None:
            for gi in range(g):
                o_ref[0, :, (kv * g + gi) * dv:(kv * g + gi + 1) * dv] = o[gi * tq:(gi + 1) * tq].astype(BF16)
        else:
            lam = (jnp.exp(jnp.sum(aux_ref[0:1, :] * aux_ref[1:2, :], axis=-1, keepdims=True))
                   - jnp.exp(jnp.sum(aux_ref[2:3, :] * aux_ref[3:4, :], axis=-1, keepdims=True))
                   + diff_lambda_init)
            od = o[0:tq] - lam * o[tq:2 * tq]
            od = _rms(od, dv) * aux_ref[4:5, :] * (1.0 - diff_lambda_init)
            o_ref[0, :, kv * dv:(kv + 1) * dv] = od.astype(BF16)


def _attn_call(q, k, v, *, n_heads_kv, g, dk, dv, nkv, tq, q_tile0, n_q_tiles, key_tile0, n_keys,
               aux=None, diff_lambda_init=None, name="attn"):
    bsz = q.shape[0]
    n_out_heads = n_heads_kv * (g if diff_lambda_init is None else 1)
    out_w = nkv * (g if diff_lambda_init is None else 1) * dv
    kern = functools.partial(_attn_kernel, nkv=nkv, g=g, dk=dk, dv=dv, tq=tq, n_keys=n_keys,
                             diff_lambda_init=diff_lambda_init)
    in_specs = [
        pl.BlockSpec((1, tq, nkv * g * dk), lambda b, hh, qi: (b, qi + q_tile0, hh)),
        pl.BlockSpec((1, n_keys, nkv * dk), lambda b, hh, qi: (b, key_tile0, hh)),
        pl.BlockSpec((1, n_keys, nkv * dv), lambda b, hh, qi: (b, key_tile0, hh)),
    ]
    args = [q, k, v]
    if aux is not None:
        in_specs.append(pl.BlockSpec(aux.shape, lambda b, hh, qi: (0, 0)))
        args.append(aux)
    return pl.pallas_call(
        kern,
        grid=(bsz, n_heads_kv // nkv, n_q_tiles),
        in_specs=in_specs,
        out_specs=pl.BlockSpec((1, tq, out_w), lambda b, hh, qi: (b, qi, hh)),
        out_shape=jax.ShapeDtypeStruct((bsz, n_q_tiles * tq, n_out_heads * dv), BF16),
        compiler_params=pltpu.CompilerParams(
            dimension_semantics=("parallel", "parallel", "parallel"), vmem_limit_bytes=VMEM_LIMIT),
        name=name,
    )(*args)


WIN_TQ = 256
WIN_BAND = WIN_TQ + 2 * WINDOW
NEG_BIG = -1e30


WIN_Q_TILES = SEQ // WIN_TQ


def _window_band(tile):
    q0 = (tile % WIN_Q_TILES) * WIN_TQ
    return q0, pl.multiple_of(jnp.clip(q0 - WINDOW, 0, SEQ - WIN_BAND), WINDOW)


def _window_kernel(q_ref, k_ref, v_ref, sink_ref, o_ref, sloc0, sctx0, sloc1, sctx1, *, n_total):
    g = D_HEADS // D_KV_HEADS
    dk = HEAD_DIM
    s = pl.program_id(0)

    @pl.when(s == 0)
    def _():
        sloc1[...] = jnp.zeros_like(sloc1)
        sctx1[...] = jnp.zeros_like(sctx1)

    def step(write, read):
        sloc_w, sctx_w = write
        sloc_r, sctx_r = read
        q0, start = _window_band(jnp.minimum(s, n_total - 1))
        q = jnp.concatenate([q_ref[0, :, gi * dk:(gi + 1) * dk] for gi in range(g)], axis=0)
        row = (lax.broadcasted_iota(jnp.int32, (g * WIN_TQ, WIN_BAND), 0) & (WIN_TQ - 1)) + q0
        col = lax.broadcasted_iota(jnp.int32, (g * WIN_TQ, WIN_BAND), 1) + start
        sloc_w[...] = jnp.where(jnp.abs(row - col) <= WINDOW,
                                _dot_nt(q, k_ref[0, pl.ds(start, WIN_BAND), :]), NEG_BIG)
        sctx_w[...] = _dot_nt(q, k_ref[0, SEQ:TOK, :])

        _, start = _window_band(jnp.maximum(s - 1, 0))
        s_loc = sloc_r[...]
        s_ctx = sctx_r[...]
        sink = sink_ref[0]
        m = jnp.maximum(jnp.maximum(jnp.max(s_loc, axis=-1, keepdims=True),
                                    jnp.max(s_ctx, axis=-1, keepdims=True)), sink)
        p_loc = jnp.exp2(s_loc - m)
        p_ctx = jnp.exp2(s_ctx - m)
        l = (jnp.sum(p_loc, axis=-1, keepdims=True) + jnp.sum(p_ctx, axis=-1, keepdims=True)
             + jnp.exp2(sink - m))
        acc = (_dot(p_ctx.astype(BF16), v_ref[0, SEQ:TOK, :])
               + _dot(p_loc.astype(BF16), v_ref[0, pl.ds(start, WIN_BAND), :]))
        o = acc / l
        for gi in range(g):
            o_ref[0, :, gi * dk:(gi + 1) * dk] = o[gi * WIN_TQ:(gi + 1) * WIN_TQ].astype(BF16)

    @pl.when(s % 2 == 0)
    def _():
        step((sloc0, sctx0), (sloc1, sctx1))

    @pl.when(s % 2 == 1)
    def _():
        step((sloc1, sctx1), (sloc0, sctx0))


def _window_call(q, k, v, sink_col):
    bsz = q.shape[0]
    g = D_HEADS // D_KV_HEADS
    n_total = bsz * D_KV_HEADS * WIN_Q_TILES
    split = lambda t: (t // (D_KV_HEADS * WIN_Q_TILES), (t // WIN_Q_TILES) % D_KV_HEADS, t % WIN_Q_TILES)
    stage_a = lambda s: split(jnp.minimum(s, n_total - 1))
    stage_b = lambda s: split(jnp.maximum(s - 1, 0))

    def q_index(s):
        b, hh, qi = stage_a(s)
        return (b, qi, hh)

    def k_index(s):
        b, hh, _ = stage_a(s)
        return (b, 0, hh)

    def v_index(s):
        b, hh, _ = stage_b(s)
        return (b, 0, hh)

    def o_index(s):
        b, hh, qi = stage_b(s)
        return (b, qi, hh)

    m_rows = g * WIN_TQ
    return pl.pallas_call(
        functools.partial(_window_kernel, n_total=n_total),
        grid=(n_total + 1,),
        in_specs=[
            pl.BlockSpec((1, WIN_TQ, g * HEAD_DIM), q_index),
            pl.BlockSpec((1, TOK, HEAD_DIM), k_index),
            pl.BlockSpec((1, TOK, HEAD_DIM), v_index),
            pl.BlockSpec((1, m_rows, 1), lambda s: (stage_b(s)[1], 0, 0)),
        ],
        out_specs=pl.BlockSpec((1, WIN_TQ, g * HEAD_DIM), o_index),
        out_shape=jax.ShapeDtypeStruct((bsz, SEQ, D_HEADS * HEAD_DIM), BF16),
        scratch_shapes=[pltpu.VMEM((m_rows, WIN_BAND), F32), pltpu.VMEM((m_rows, CTX_LEN), F32),
                        pltpu.VMEM((m_rows, WIN_BAND), F32), pltpu.VMEM((m_rows, CTX_LEN), F32)],
        compiler_params=pltpu.CompilerParams(
            dimension_semantics=("arbitrary",), vmem_limit_bytes=VMEM_LIMIT),
        name="window_attn",
    )(q, k, v, sink_col)


def _route(logits, lane):
    lane_f = lane.astype(F32)
    lg = jnp.where(lane < N_GROUPS, logits, NEG_BIG)
    g_max = jnp.max(lg, axis=-1, keepdims=True)
    p_top = 1.0 / jnp.sum(jnp.exp(lg - g_max), axis=-1, keepdims=True)
    g_idx = jnp.min(jnp.where(lg == g_max, lane_f, float(LANES)), axis=-1, keepdims=True)
    e_lane = lane - N_GROUPS
    in_group = (e_lane >= 0) & (e_lane < N_EXPERTS) & ((e_lane // EXPERTS_PER_GROUP).astype(F32) == g_idx)
    le = jnp.where(in_group, logits, NEG_BIG)
    m1 = jnp.max(le, axis=-1, keepdims=True)
    i1 = jnp.min(jnp.where(le == m1, lane_f, float(LANES)), axis=-1, keepdims=True)
    le2 = jnp.where(lane_f == i1, NEG_BIG, le)
    m2 = jnp.max(le2, axis=-1, keepdims=True)
    i2 = jnp.min(jnp.where(le2 == m2, lane_f, float(LANES)), axis=-1, keepdims=True)
    t = jnp.exp(m2 - m1)
    w1 = p_top / (1.0 + t)
    w2 = p_top * t / (1.0 + t)
    return i1, i2, w1, w2


def _outproj_kernel(*refs, has_ctx, n_tiles, n_total):
    if has_ctx:
        (o1_ref, o2_ref, h_ref, o1c_ref, o2c_ref, hc_ref, mod_ref, w1_ref, w2_ref, rw_ref, rb_ref,
         hn_ref, fx_ref, rt_ref, rtt_ref, cnt_ref, run_ref, logit_ref) = refs
    else:
        (o1_ref, o2_ref, h_ref, mod_ref, w1_ref, w2_ref, rw_ref, rb_ref,
         hn_ref, fx_ref, rt_ref, rtt_ref, cnt_ref, run_ref, logit_ref) = refs
    d = D_MODEL
    s = pl.program_id(0)

    @pl.when(s == 0)
    def _():
        run_ref[...] = jnp.zeros_like(run_ref)
        logit_ref[...] = jnp.zeros_like(logit_ref)

    logits = logit_ref[...]
    lane = lax.broadcasted_iota(jnp.int32, (ROW_TILE, LANES), 1)
    lane_f = lane.astype(F32)
    i1, i2, w1, w2 = _route(logits, lane)

    o1, o2, h = o1_ref[0], o2_ref[0], h_ref[0]
    if has_ctx:
        is_ctx = jnp.minimum(s, n_total - 1) % n_tiles == CTX_TILE
        o1 = jnp.where(is_ctx, o1c_ref[0], o1)
        o2 = jnp.where(is_ctx, o2c_ref[0], o2)
        h = jnp.where(is_ctx, hc_ref[0], h)
    acc = _dot(o1, w1_ref[...]) + _dot(o2, w2_ref[...])
    hn = h + mod_ref[0, :, 2 * d:3 * d] * acc
    hn_ref[0] = hn
    fx = _modulate(hn, mod_ref[0, :, 3 * d:4 * d], mod_ref[0, :, 4 * d:5 * d])
    fx_ref[0] = _pack_bf16_pairs(fx)
    hi = fx.astype(BF16)
    lo = (fx - hi.astype(F32)).astype(BF16)
    part = _dot(jnp.concatenate([hi, lo], axis=0), rw_ref[...])
    logit_ref[...] = (part[:ROW_TILE, :LANES] + part[:ROW_TILE, LANES:]
                      + part[ROW_TILE:, :LANES] + part[ROW_TILE:, LANES:] + rb_ref[...])

    hit1, hit2 = lane_f == i1, lane_f == i2
    multi_hot = jnp.where(s > 0, jnp.where(hit1, 1.0, 0.0) + jnp.where(hit2, 1.0, 0.0), 0.0)
    tri = (lax.broadcasted_iota(jnp.int32, (ROW_TILE, ROW_TILE), 0)
           > lax.broadcasted_iota(jnp.int32, (ROW_TILE, ROW_TILE), 1))
    before = _dot(jnp.where(tri, 1.0, 0.0).astype(BF16), multi_hot.astype(BF16)) + run_ref[...]
    r1 = jnp.sum(jnp.where(hit1, before, 0.0), axis=-1, keepdims=True)
    r2 = jnp.sum(jnp.where(hit2, before, 0.0), axis=-1, keepdims=True)
    run_ref[...] = run_ref[...] + jnp.sum(multi_hot, axis=0, keepdims=True)
    cnt_ref[...] = jnp.broadcast_to(run_ref[...], cnt_ref.shape)

    out = jnp.where(lane == 0, i1 - N_GROUPS, 0.0)
    out = jnp.where(lane == 1, i2 - N_GROUPS, out)
    out = jnp.where(lane == 2, w1, out)
    out = jnp.where(lane == 3, w2, out)
    out = jnp.where(lane == 4, r1, out)
    out = jnp.where(lane == 5, r2, out)
    rt_ref[0] = out
    rtt_ref[0] = jnp.transpose(out)[0:8, :]


def _outproj_call(lat, ctx, mod, w1, w2, rw, rb, n_tiles):
    bsz = lat[2].shape[0]
    has_ctx = ctx is not None
    n_total = bsz * n_tiles
    stage_a = lambda s: jnp.minimum(s, n_total - 1)
    stage_b = lambda s: jnp.maximum(s - 1, 0)
    bj = lambda t: (t // n_tiles, t % n_tiles)

    def a_tile(w, kind):
        def index(s):
            b, j = bj(stage_a(s))
            return (b, {"all": j, "lat": jnp.minimum(j, CTX_TILE - 1), "ctx": 0}[kind], 0)
        return pl.BlockSpec((1, ROW_TILE, w), index)

    def b_tile(w):
        return pl.BlockSpec((1, ROW_TILE, w), lambda s: (*bj(stage_b(s)), 0))

    in_specs = [a_tile(a.shape[-1], "lat") for a in lat]
    args = list(lat)
    if has_ctx:
        in_specs += [a_tile(a.shape[-1], "ctx") for a in ctx]
        args += list(ctx)
    in_specs += [
        pl.BlockSpec((1, 1, 6 * D_MODEL), lambda s: (_mod_row(*bj(stage_a(s))), 0, 0)),
        _resident(w1.shape), _resident(w2.shape), _resident(rw.shape), _resident(rb.shape),
    ]
    args += [mod, w1, w2, rw, rb]
    n_rows = n_tiles * ROW_TILE
    return pl.pallas_call(
        functools.partial(_outproj_kernel, has_ctx=has_ctx, n_tiles=n_tiles, n_total=n_total),
        grid=(n_total + 1,),
        in_specs=in_specs,
        out_specs=[a_tile(D_MODEL, "all"), a_tile(PACKED_W, "all"), b_tile(LANES),
                   pl.BlockSpec((1, 8, ROW_TILE), lambda s: (stage_b(s), 0, 0)),
                   pl.BlockSpec((8, LANES), lambda s: (0, 0))],
        out_shape=[jax.ShapeDtypeStruct((bsz, n_rows, D_MODEL), F32),
                   jax.ShapeDtypeStruct((bsz, n_rows, PACKED_W), U32),
                   jax.ShapeDtypeStruct((bsz, n_rows, LANES), F32),
                   jax.ShapeDtypeStruct((n_total, 8, ROW_TILE), F32),
                   jax.ShapeDtypeStruct((8, LANES), F32)],
        scratch_shapes=[pltpu.VMEM((1, LANES), F32), pltpu.VMEM((ROW_TILE, LANES), F32)],
        compiler_params=pltpu.CompilerParams(
            dimension_semantics=("arbitrary",), vmem_limit_bytes=VMEM_LIMIT),
        name="outproj_router",
    )(*args)


def _dispatch_kernel(lo_ref, hi_ref, slot_ref, fx_ref, xs_hbm, zbuf, stage, sems, zsem):
    n_steps = pl.num_programs(0) * pl.num_programs(1)
    step = pl.program_id(0) * pl.num_programs(1) + pl.program_id(1)
    cur = step % 2
    src = stage.at[cur]
    src[...] = fx_ref[0]
    for r in range(ROW_TILE):
        for k in range(TOP_K):
            pltpu.make_async_copy(src.at[pl.ds(r, 1)], xs_hbm.at[pl.ds(slot_ref[0, k, r], 1)],
                                  sems.at[cur]).start(priority=k % 2)

    def wait_tile(slot):
        for _ in range(TOP_K):
            pltpu.make_async_copy(stage.at[slot], xs_hbm.at[pl.ds(0, ROW_TILE)], sems.at[slot]).wait()

    @pl.when(step > 0)
    def _():
        wait_tile(1 - cur)

    @pl.when(step == n_steps - 1)
    def _():
        zbuf[...] = jnp.zeros_like(zbuf)

        rows8 = zbuf.shape[0]

        def one(s, start):
            cp = pltpu.make_async_copy(zbuf.at[pl.ds(0, 1)], xs_hbm.at[pl.ds(s if start else 0, 1)], zsem)
            cp.start() if start else cp.wait()

        def eight(s, start):
            at = pl.multiple_of(s, rows8) if start else 0
            cp = pltpu.make_async_copy(zbuf, xs_hbm.at[pl.ds(at, rows8)], zsem)
            cp.start() if start else cp.wait()

        def sweep(start):
            def per_expert(e, carry):
                lo, hi = lo_ref[e], hi_ref[e]
                head_end = jnp.minimum((lo + rows8 - 1) // rows8 * rows8, hi)
                n_groups = (hi - head_end) // rows8
                body_end = head_end + n_groups * rows8
                def single(s, c):
                    one(s, start)
                    return c

                def group(i, c):
                    eight(head_end + i * rows8, start)
                    return c
                lax.fori_loop(lo, head_end, single, 0)
                lax.fori_loop(0, n_groups, group, 0)
                lax.fori_loop(body_end, hi, single, 0)
                return carry
            lax.fori_loop(0, N_EXPERTS, per_expert, 0)

        sweep(True)
        sweep(False)
        wait_tile(cur)


def _dispatch_call(fx, slots, fill_lo, fill_hi, n_blocks):
    bsz, n_rows, _ = fx.shape
    n_tiles = n_rows // ROW_TILE
    grid_spec = pltpu.PrefetchScalarGridSpec(
        num_scalar_prefetch=2,
        grid=(bsz, n_tiles),
        in_specs=[
            pl.BlockSpec((1, TOP_K, ROW_TILE), lambda b, j, lo, hi: (b * n_tiles + j, 0, 0),
                         memory_space=pltpu.SMEM),
            pl.BlockSpec((1, ROW_TILE, PACKED_W), lambda b, j, lo, hi: (b, j, 0)),
        ],
        out_specs=pl.BlockSpec(memory_space=pl.ANY),
        scratch_shapes=[pltpu.VMEM((8, PACKED_W), U32), pltpu.VMEM((2, ROW_TILE, PACKED_W), U32),
                        pltpu.SemaphoreType.DMA((2,)), pltpu.SemaphoreType.DMA(())],
    )
    return pl.pallas_call(
        _dispatch_kernel,
        grid_spec=grid_spec,
        out_shape=jax.ShapeDtypeStruct((n_blocks * MOE_BLOCK, PACKED_W), U32),
        compiler_params=pltpu.CompilerParams(
            dimension_semantics=("arbitrary", "arbitrary"), vmem_limit_bytes=VMEM_LIMIT),
        name="moe_dispatch",
    )(fill_lo, fill_hi, slots, fx)


CAST_ROWS = 256


def _moe_kernel(be_ref, nu_ref, nx_ref, x_ref, wg_hbm, wu_hbm, wd_hbm, y_ref,
                sg, su, sd, wg, wu, wd, sems, *, layer):
    i = pl.program_id(0)
    e = be_ref[i]
    used = i < nu_ref[0]
    staged = ((wg_hbm, sg, wg, 0), (wu_hbm, su, wu, 1), (wd_hbm, sd, wd, 2))

    def fetch(expert):
        for hbm, stage, _, s in staged:
            pltpu.make_async_copy(hbm.at[layer, expert], stage, sems.at[s]).start(priority=1)

    @pl.when(i == 0)
    def _():
        fetch(e)

    first_of_expert = (i == 0) | (e != be_ref[jnp.maximum(i - 1, 0)])

    @pl.when(used & first_of_expert)
    def _():
        for hbm, stage, dst, s in staged:
            pltpu.make_async_copy(hbm.at[layer, 0], stage, sems.at[s]).wait()

            def cast(c, carry):
                rows = pl.ds(pl.multiple_of(c * CAST_ROWS, CAST_ROWS), CAST_ROWS)
                dst[rows, :] = stage[rows, :].astype(BF16)
                return carry
            lax.fori_loop(0, stage.shape[0] // CAST_ROWS, cast, 0)
        nxt = nx_ref[e]

        @pl.when(nxt >= 0)
        def _():
            fetch(nxt)

    @pl.when(used)
    def _():
        x = _unpack_bf16_pairs(x_ref[...]).astype(BF16)
        gt = _dot(x, wg[...])
        up = _dot(x, wu[...])
        u = (_silu(gt) * up).astype(BF16)
        y_ref[...] = _pack_bf16_pairs(_dot(u, wd[...]))


def _moe_call(xs, block_expert, n_used, next_expert, wg, wu, wd, layer):
    n_blocks = block_expert.shape[0]
    row_blk = pl.BlockSpec((MOE_BLOCK, PACKED_W), lambda i, be, nu, nx: (jnp.minimum(i, nu[0] - 1), 0))
    hbm = pl.BlockSpec(memory_space=pl.ANY)
    grid_spec = pltpu.PrefetchScalarGridSpec(
        num_scalar_prefetch=3,
        grid=(n_blocks,),
        in_specs=[row_blk, hbm, hbm, hbm],
        out_specs=row_blk,
        scratch_shapes=[
            pltpu.VMEM((D_MODEL, D_EXPERT), F32), pltpu.VMEM((D_MODEL, D_EXPERT), F32),
            pltpu.VMEM((D_EXPERT, D_MODEL), F32),
            pltpu.VMEM((D_MODEL, D_EXPERT), BF16), pltpu.VMEM((D_MODEL, D_EXPERT), BF16),
            pltpu.VMEM((D_EXPERT, D_MODEL), BF16),
            pltpu.SemaphoreType.DMA((3,)),
        ],
    )
    return pl.pallas_call(
        functools.partial(_moe_kernel, layer=layer),
        grid_spec=grid_spec,
        out_shape=jax.ShapeDtypeStruct((n_blocks * MOE_BLOCK, PACKED_W), U32),
        compiler_params=pltpu.CompilerParams(
            dimension_semantics=("arbitrary",), vmem_limit_bytes=VMEM_LIMIT),
        name="moe_experts",
    )(block_expert, n_used, next_expert, xs, wg, wu, wd)


def _combine_kernel(slot_ref, slot_next_ref, h_ref, rt_ref, mod_ref, ys_hbm, o_ref, ybuf, sems):
    d = D_MODEL
    n_steps = pl.num_programs(0) * pl.num_programs(1)
    step = pl.program_id(0) * pl.num_programs(1) + pl.program_id(1)
    cur = step % 2

    @pl.when(step == 0)
    def _():
        _start_row_gathers(slot_ref, ys_hbm, ybuf.at[0], sems.at[0])

    _start_row_gathers(slot_next_ref, ys_hbm, ybuf.at[1 - cur], sems.at[1 - cur])
    _wait_row_gathers(ys_hbm, ybuf.at[cur], sems.at[cur])
    rt = rt_ref[0]
    y = rt[:, 2:3] * _unpack_bf16_pairs(ybuf[cur, 0]) + rt[:, 3:4] * _unpack_bf16_pairs(ybuf[cur, 1])
    o_ref[0] = h_ref[0] + mod_ref[0, :, 5 * d:6 * d] * y

    @pl.when(step == n_steps - 1)
    def _():
        _wait_row_gathers(ys_hbm, ybuf.at[1 - cur], sems.at[1 - cur])


def _combine_call(h, ys, slots, rt, mod, n_tiles):
    bsz = h.shape[0]
    n_steps = bsz * n_tiles
    tile = lambda w: pl.BlockSpec((1, ROW_TILE, w), lambda b, j: (b, j, 0))
    slot_spec = lambda ahead: pl.BlockSpec(
        (1, TOP_K, ROW_TILE), lambda b, j: (jnp.minimum(b * n_tiles + j + ahead, n_steps - 1), 0, 0),
        memory_space=pltpu.SMEM)
    return pl.pallas_call(
        _combine_kernel,
        grid=(bsz, n_tiles),
        in_specs=[
            slot_spec(0), slot_spec(1),
            tile(D_MODEL), tile(LANES),
            pl.BlockSpec((1, 1, 6 * D_MODEL), lambda b, j: (_mod_row(b, j), 0, 0)),
            pl.BlockSpec(memory_space=pl.ANY),
        ],
        out_specs=tile(D_MODEL),
        out_shape=jax.ShapeDtypeStruct((bsz, n_tiles * ROW_TILE, D_MODEL), F32),
        scratch_shapes=[pltpu.VMEM((2, TOP_K, ROW_TILE, PACKED_W), U32), pltpu.SemaphoreType.DMA((2,))],
        compiler_params=pltpu.CompilerParams(
            dimension_semantics=("arbitrary", "arbitrary"), vmem_limit_bytes=VMEM_LIMIT),
        name="moe_combine",
    )(slots, slots, h, rt, mod, ys)


def _rope_tables():
    rows = SEQ // GRID_W
    row = np.repeat(np.arange(rows), GRID_W).astype(np.float32)
    col = np.tile(np.arange(GRID_W), rows).astype(np.float32)

    def tables(dim):
        n_freq = dim // 4
        inv = (1.0 / (np.float32(ROPE_THETA) ** (np.arange(n_freq, dtype=np.float32) / np.float32(n_freq))))
        inv = inv.astype(np.float32)
        ang_r = row[:, None] * inv
        ang_c = col[:, None] * inv
        ang = np.concatenate([ang_r, ang_r, ang_c, ang_c], axis=-1)
        sign = np.tile(np.concatenate([-np.ones((n_freq,), np.float32), np.ones((n_freq,), np.float32)]), 2)
        return np.cos(ang).astype(np.float32), (np.sin(ang) * sign).astype(np.float32)

    cos128, sin128 = tables(HEAD_DIM)
    cos64, sin64 = tables(B_ROPE)
    ones, zeros = np.ones((SEQ, 64), np.float32), np.zeros((SEQ, 64), np.float32)
    lat = np.concatenate([cos128, sin128,
                          cos64, cos64, sin64, sin64,
                          cos64, ones, sin64, zeros], axis=-1)
    ident = np.concatenate([np.ones((CTX_LEN, 128), np.float32), np.zeros((CTX_LEN, 128), np.float32)], axis=-1)
    return jnp.asarray(np.concatenate([lat, np.tile(ident, (1, 3))], axis=0))


def _pad_lanes(v, n):
    return jnp.pad(v, (0, n - v.shape[0]))


def _router_weights(wr_g, br_g, wr_e, br_e):
    w = jnp.pad(jnp.concatenate([wr_g, wr_e], axis=1), ((0, 0), (0, LANES - N_GROUPS - N_EXPERTS)))
    hi = w.astype(BF16)
    lo = (w - hi.astype(F32)).astype(BF16)
    b = _pad_lanes(jnp.concatenate([br_g, br_e]), LANES).reshape(1, LANES)
    return jnp.concatenate([hi, lo], axis=1), b


def _moe_experts(fx, rtt, cnt, wg, wu, wd, layer):
    bsz, n_rows, _ = fx.shape
    n_blocks = -(-(bsz * n_rows * TOP_K) // MOE_BLOCK) + N_EXPERTS
    experts = jnp.arange(N_EXPERTS, dtype=jnp.int32)
    counts = cnt[0, N_GROUPS:N_GROUPS + N_EXPERTS].astype(jnp.int32)
    padded = (counts + MOE_BLOCK - 1) // MOE_BLOCK * MOE_BLOCK
    pad_end = jnp.cumsum(padded)
    pad_start = pad_end - padded
    e_idx = rtt[:, 0:TOP_K, :].astype(jnp.int32)
    rank = rtt[:, 4:4 + TOP_K, :].astype(jnp.int32)
    slots = rank + jnp.sum(jnp.where(e_idx[..., None] == experts, pad_start, 0), axis=-1)
    block_start = jnp.arange(n_blocks, dtype=jnp.int32) * MOE_BLOCK
    block_expert = jnp.minimum(jnp.sum(pad_end[None, :] <= block_start[:, None], axis=1), N_EXPERTS - 1)
    n_used = pad_end[-1:] // MOE_BLOCK
    later = (experts[None, :] > experts[:, None]) & (counts[None, :] > 0)
    next_expert = jnp.where(jnp.any(later, axis=1), jnp.argmax(later, axis=1), -1)
    i32 = lambda a: a.astype(jnp.int32)
    xs = _dispatch_call(fx, i32(slots), i32(pad_start + counts), i32(pad_end), n_blocks)
    ys = _moe_call(xs, i32(block_expert), i32(n_used), i32(next_expert), wg, wu, wd, layer)
    return ys, i32(slots)


def kernel(x, c, ctx, c_ctx, mod_w, mod_b, even_w_in, even_w_out, a_q_norm, a_k_norm, b_cq_norm, b_w_uq,
           b_ckv_norm, b_w_ukv, b_q_norm, b_k_norm, odd_w_in, odd_w_out, c_q_norm, c_k_norm, c_lambda_q1,
           c_lambda_k1, c_lambda_q2, c_lambda_k2, c_subln, d_q_norm, d_k_norm, d_sink, moe_wr_group,
           moe_br_group, moe_wr_expert, moe_br_expert, moe_w_gate, moe_w_up, moe_w_down):
    bsz = x.shape[0]
    c_all = jnp.zeros((MOD_ROWS, D_MODEL), F32).at[:bsz].set(c).at[MOD_CTX_ROW].set(c_ctx)
    mod_all = _mod_call(c_all, mod_w, mod_b)
    tab = _rope_tables()

    i = 0
    mod = mod_all[0].reshape(MOD_ROWS, 1, 6 * D_MODEL)
    scale_a = HEAD_DIM ** -0.5 * LOG2_E
    scale_b = (B_NOPE + B_ROPE) ** -0.5 * LOG2_E
    win = jnp.pad(even_w_in[i], ((0, 0), (0, EVEN_IN_PAD - even_w_in.shape[-1]))).astype(BF16)
    wuq = b_w_uq[i].reshape(B_Q_LORA, B_HEADS, B_NOPE + B_ROPE)
    wuq = jnp.pad(wuq, ((0, 0), (0, 0), (0, B_QK_PAD - B_NOPE - B_ROPE))).reshape(B_Q_LORA, -1).astype(BF16)
    wukv = b_w_ukv[i].reshape(B_KV_LORA, B_HEADS, B_NOPE + B_V)
    wukv = jnp.concatenate([wukv[:, :, :B_NOPE].reshape(B_KV_LORA, -1),
                            wukv[:, :, B_NOPE:].reshape(B_KV_LORA, -1)], axis=1).astype(BF16)
    gains = jnp.stack([
        _pad_lanes(a_q_norm[i] * scale_a, 512), _pad_lanes(a_k_norm[i], 512),
        b_cq_norm[i], b_ckv_norm[i],
        _pad_lanes(b_q_norm[i] * scale_b, 512), _pad_lanes(b_k_norm[i], 512),
        jnp.zeros((512,), F32), jnp.zeros((512,), F32)])
    qa, ka, va, qb, kb, vb = _even_proj_call(x, ctx, mod, tab, gains, win, wuq, wukv)

    g_a = A_HEADS // A_KV_HEADS
    lat = dict(q_tile0=0, key_tile0=0, n_keys=TOK)
    oa = _attn_call(qa, ka, va, n_heads_kv=A_KV_HEADS, g=g_a, dk=HEAD_DIM, dv=HEAD_DIM, nkv=2,
                    tq=256, n_q_tiles=SEQ // 256, name="gqa_latent", **lat)
    ob = _attn_call(qb, kb, vb, n_heads_kv=B_HEADS, g=1, dk=B_QK_PAD, dv=B_V, nkv=2,
                    tq=512, n_q_tiles=SEQ // 512, name="mla_latent", **lat)
    cx = dict(tq=CTX_LEN, q_tile0=SEQ // CTX_LEN, n_q_tiles=1, key_tile0=SEQ // CTX_LEN, n_keys=CTX_LEN)
    oa_c = _attn_call(qa, ka, va, n_heads_kv=A_KV_HEADS, g=g_a, dk=HEAD_DIM, dv=HEAD_DIM, nkv=2,
                      name="gqa_context", **cx)
    ob_c = _attn_call(qb, kb, vb, n_heads_kv=B_HEADS, g=1, dk=B_QK_PAD, dv=B_V, nkv=B_HEADS,
                      name="mla_context", **cx)

    w_out = even_w_out[i].astype(BF16)
    rw, rb = _router_weights(moe_wr_group[0], moe_br_group[0], moe_wr_expert[0], moe_br_expert[0])
    hn, fx, rt, rtt, cnt = _outproj_call((oa, ob, x), (oa_c, ob_c, ctx), mod,
                                         w_out[:A_HEADS * HEAD_DIM], w_out[A_HEADS * HEAD_DIM:],
                                         rw, rb, N_ROW_TILES)
    ys, slots = _moe_experts(fx, rtt, cnt, moe_w_gate, moe_w_up, moe_w_down, 0)
    hn0, rt0, mod0 = hn, rt, mod

    layer = 1
    mod = mod_all[1].reshape(MOD_ROWS, 1, 6 * D_MODEL)
    lambda_init = 0.8 - 0.6 * math.exp(-0.3 * layer)
    scale_c = C_HD ** -0.5 * LOG2_E
    scale_d = HEAD_DIM ** -0.5 * LOG2_E
    win = odd_w_in[i].astype(BF16)
    gains = jnp.stack([
        jnp.tile(c_q_norm[i] * scale_c, 2), jnp.tile(c_k_norm[i], 2),
        d_q_norm[i] * scale_d, d_k_norm[i],
        jnp.zeros((128,), F32), jnp.zeros((128,), F32), jnp.zeros((128,), F32), jnp.zeros((128,), F32)])
    h, qc, kc, vc, qd, kd, vd = _odd_proj_call(hn0, ys, slots, rt0, mod0, mod, tab, gains, win)

    aux = jnp.stack([
        _pad_lanes(c_lambda_q1[i], 128), _pad_lanes(c_lambda_k1[i], 128),
        _pad_lanes(c_lambda_q2[i], 128), _pad_lanes(c_lambda_k2[i], 128),
        c_subln[i], jnp.zeros((128,), F32), jnp.zeros((128,), F32), jnp.zeros((128,), F32)])
    oc = _attn_call(qc, kc, vc, n_heads_kv=C_HEADS, g=2, dk=LANES, dv=C_V, nkv=2,
                    tq=512, n_q_tiles=SEQ // 512, aux=aux, diff_lambda_init=lambda_init,
                    name="diff_latent", **lat)
    g_d = D_HEADS // D_KV_HEADS
    sink_col = jnp.repeat((d_sink[i] * LOG2_E).reshape(D_KV_HEADS, g_d), WIN_TQ, axis=1)
    sink_col = sink_col.reshape(D_KV_HEADS, g_d * WIN_TQ, 1)
    od = _window_call(qd, kd, vd, sink_col)

    w_out = odd_w_out[i].astype(BF16)
    rw, rb = _router_weights(moe_wr_group[1], moe_br_group[1], moe_wr_expert[1], moe_br_expert[1])
    n_lat_tiles = SEQ // ROW_TILE
    hn, fx, rt, rtt, cnt = _outproj_call((oc, od, h), None, mod, w_out[:C_HEADS * C_V], w_out[C_HEADS * C_V:],
                                         rw, rb, n_lat_tiles)
    ys, slots = _moe_experts(fx, rtt, cnt, moe_w_gate, moe_w_up, moe_w_down, 1)
    return _combine_call(hn, ys, slots, rt, mod, n_lat_tiles)
```

```python
import functools
import math

import jax
import jax.numpy as jnp
import numpy as np
from jax import lax
from jax.experimental import pallas as pl
from jax.experimental.pallas import tpu as pltpu

F32 = jnp.float32
BF16 = jnp.bfloat16

D_MODEL = 2048
BATCH = 8
SEQ = 2048
DEPTH = 2
GRID_W = 64
CTX_LEN = 256
TOK = SEQ + CTX_LEN
HEAD_DIM = 128
ROPE_THETA = 10000.0
NORM_EPS = 1e-6
A_HEADS = 8
A_KV_HEADS = 2
B_HEADS = 8
B_Q_LORA = 512
B_KV_LORA = 512
B_NOPE = 128
B_ROPE = 64
B_V = 128
B_QK_PAD = 256
C_HEADS = 8
C_HD = 64
C_V = 128
D_HEADS = 8
D_KV_HEADS = 2
WINDOW = 128
N_GROUPS = 4
EXPERTS_PER_GROUP = 8
N_EXPERTS = 32
D_EXPERT = 1024
MOE_BLOCK = 256
TOP_K = 2
LANES = 128
LOG2_E = math.log2(math.e)

ROW_TILE = 256
N_ROW_TILES = TOK // ROW_TILE
CTX_TILE = SEQ // ROW_TILE
MOD_ROWS = 16
MOD_CTX_ROW = BATCH
VMEM_LIMIT = 56 * 1024 * 1024


def _dot(a, b):
    return jnp.dot(a, b, preferred_element_type=F32)


def _dot_nt(a, b):
    return lax.dot_general(a, b, (((1,), (1,)), ((), ())), preferred_element_type=F32)


def _silu(x):
    return x / (1.0 + jnp.exp(-x))


def _rms(x, n):
    return x * lax.rsqrt(jnp.sum(x * x, axis=-1, keepdims=True) * (1.0 / n) + NORM_EPS)


def _modulate(x, shift, scale):
    return _rms(x, x.shape[-1]) * (1.0 + scale) + shift


def _rope(x, cos, sin_signed, quarter, lane):
    fwd = pltpu.roll(x, LANES - quarter, axis=1)
    bwd = pltpu.roll(x, quarter, axis=1)
    rot = jnp.where((lane // quarter) % 2 == 0, fwd, bwd)
    return x * cos + rot * sin_signed


PACKED_W = D_MODEL // 2
U32 = jnp.uint32


def _pack_bf16_pairs(x):
    n = x.shape[-1] // 2
    lo = lax.bitcast_convert_type(x[:, :n].astype(BF16).astype(F32), U32) >> 16
    hi = lax.bitcast_convert_type(x[:, n:].astype(BF16).astype(F32), U32) & U32(0xFFFF0000)
    return lo | hi


def _unpack_bf16_pairs(p):
    lo = lax.bitcast_convert_type(p << 16, F32)
    hi = lax.bitcast_convert_type(p & U32(0xFFFF0000), F32)
    return jnp.concatenate([lo, hi], axis=-1)


def _start_row_gathers(slot_ref, src_hbm, dst_ref, sem):
    for r in range(ROW_TILE):
        for k in range(TOP_K):
            pltpu.make_async_copy(src_hbm.at[pl.ds(slot_ref[0, k, r], 1)], dst_ref.at[k, pl.ds(r, 1)],
                                  sem).start(priority=k % 2)


def _wait_row_gathers(src_hbm, dst_ref, sem):
    for k in range(TOP_K):
        pltpu.make_async_copy(src_hbm.at[pl.ds(0, ROW_TILE)], dst_ref.at[k], sem).wait()


MOD_TN = 1024


def _mod_kernel(c_ref, w_ref, b_ref, o_ref):
    a = _silu(c_ref[...]).astype(BF16)
    o_ref[0] = _dot(a, w_ref[0].astype(BF16)) + b_ref[0]


def _mod_call(c_all, mod_w, mod_b):
    d6 = 6 * D_MODEL
    return pl.pallas_call(
        _mod_kernel,
        grid=(DEPTH, d6 // MOD_TN),
        in_specs=[
            pl.BlockSpec((MOD_ROWS, D_MODEL), lambda l, n: (0, 0)),
            pl.BlockSpec((1, D_MODEL, MOD_TN), lambda l, n: (l, 0, n)),
            pl.BlockSpec((1, 1, MOD_TN), lambda l, n: (l, 0, n)),
        ],
        out_specs=pl.BlockSpec((1, MOD_ROWS, MOD_TN), lambda l, n: (l, 0, n)),
        out_shape=jax.ShapeDtypeStruct((DEPTH, MOD_ROWS, d6), F32),
        compiler_params=pltpu.CompilerParams(
            dimension_semantics=("parallel", "parallel"), vmem_limit_bytes=VMEM_LIMIT),
        name="mod_vectors",
    )(c_all, mod_w, mod_b.reshape(DEPTH, 1, d6))


def _mod_row(b, j):
    return jnp.where(j == CTX_TILE, MOD_CTX_ROW, b)


def _resident(shape):
    nd = len(shape)
    return pl.BlockSpec(shape, lambda *_: (0,) * nd, pipeline_mode=pl.Buffered(1))


EVEN_IN_PAD = 2688


def _even_proj_kernel(hx_ref, hc_ref, mod_ref, tab_ref, g_ref, win_ref, wuq_ref, wukv_ref,
                      qa_ref, ka_ref, va_ref, qb_ref, kb_ref, vb_ref):
    d = D_MODEL
    x = jnp.where(pl.program_id(1) == CTX_TILE, hc_ref[0], hx_ref[0])
    a = _modulate(x, mod_ref[0, :, 0:d], mod_ref[0, :, d:2 * d]).astype(BF16)
    lane = lax.broadcasted_iota(jnp.int32, (ROW_TILE, LANES), 1)
    cos128, sin128 = tab_ref[:, 0:128], tab_ref[:, 128:256]
    cos64p, sin64p = tab_ref[:, 512:640], tab_ref[:, 640:768]

    z = _dot(a, win_ref[:, 0:1024])
    g_q = g_ref[0:1, 0:128]
    for h in range(A_HEADS):
        blk = _rms(z[:, h * 128:(h + 1) * 128], HEAD_DIM) * g_q
        qa_ref[0, :, h * 128:(h + 1) * 128] = _rope(blk, cos128, sin128, 32, lane).astype(BF16)
    z = _dot(a, win_ref[:, 1024:1536])
    g_k = g_ref[1:2, 0:128]
    for h in range(A_KV_HEADS):
        blk = _rms(z[:, h * 128:(h + 1) * 128], HEAD_DIM) * g_k
        ka_ref[0, :, h * 128:(h + 1) * 128] = _rope(blk, cos128, sin128, 32, lane).astype(BF16)
    va_ref[0] = z[:, 256:512].astype(BF16)

    z = _dot(a, win_ref[:, 1536:2048])
    cq = (_rms(z, B_Q_LORA) * g_ref[2:3, :]).astype(BF16)
    zq = _dot(cq, wuq_ref[...])
    gq0, gq1 = g_ref[4:5, 0:128], g_ref[4:5, 128:256]
    n_qk = float(B_NOPE + B_ROPE)
    for h in range(B_HEADS):
        b0 = zq[:, h * 256:h * 256 + 128]
        b1 = zq[:, h * 256 + 128:(h + 1) * 256]
        ss = jnp.sum(b0 * b0, axis=-1, keepdims=True) + jnp.sum(b1 * b1, axis=-1, keepdims=True)
        r = lax.rsqrt(ss * (1.0 / n_qk) + NORM_EPS)
        qb_ref[0, :, h * 256:h * 256 + 128] = (b0 * r * gq0).astype(BF16)
        qb_ref[0, :, h * 256 + 128:(h + 1) * 256] = _rope(b1 * r * gq1, cos64p, sin64p, 16, lane).astype(BF16)

    z = _dot(a, win_ref[:, 2048:2688])
    ckv = (_rms(z[:, 0:512], B_KV_LORA) * g_ref[3:4, :]).astype(BF16)
    kr = z[:, 512:640]
    ss_kr = jnp.sum(kr * kr, axis=-1, keepdims=True)
    kr_rot = _rope(kr * g_ref[5:6, 128:256], cos64p, sin64p, 16, lane)
    zkv = _dot(ckv, wukv_ref[...])
    vb_ref[0] = zkv[:, 1024:2048].astype(BF16)
    gk0 = g_ref[5:6, 0:128]
    for h in range(B_HEADS):
        kn = zkv[:, h * 128:(h + 1) * 128]
        ss = jnp.sum(kn * kn, axis=-1, keepdims=True) + ss_kr
        r = lax.rsqrt(ss * (1.0 / n_qk) + NORM_EPS)
        kb_ref[0, :, h * 256:h * 256 + 128] = (kn * r * gk0).astype(BF16)
        kb_ref[0, :, h * 256 + 128:(h + 1) * 256] = (kr_rot * r).astype(BF16)


def _even_proj_call(h_lat, h_ctx, mod, tab, gains, win, wuq, wukv):
    bsz = h_lat.shape[0]
    widths = (A_HEADS * HEAD_DIM, A_KV_HEADS * HEAD_DIM, A_KV_HEADS * HEAD_DIM,
              B_HEADS * B_QK_PAD, B_HEADS * B_QK_PAD, B_HEADS * B_V)
    tile = lambda w: pl.BlockSpec((1, ROW_TILE, w), lambda b, j: (b, j, 0))
    return pl.pallas_call(
        _even_proj_kernel,
        grid=(bsz, N_ROW_TILES),
        in_specs=[
            pl.BlockSpec((1, ROW_TILE, D_MODEL), lambda b, j: (b, jnp.minimum(j, CTX_TILE - 1), 0)),
            pl.BlockSpec((1, ROW_TILE, D_MODEL), lambda b, j: (b, 0, 0)),
            pl.BlockSpec((1, 1, 6 * D_MODEL), lambda b, j: (_mod_row(b, j), 0, 0)),
            pl.BlockSpec((ROW_TILE, 768), lambda b, j: (j, 0)),
            _resident(gains.shape), _resident(win.shape), _resident(wuq.shape), _resident(wukv.shape),
        ],
        out_specs=[tile(w) for w in widths],
        out_shape=[jax.ShapeDtypeStruct((bsz, TOK, w), BF16) for w in widths],
        compiler_params=pltpu.CompilerParams(
            dimension_semantics=("parallel", "parallel"), vmem_limit_bytes=VMEM_LIMIT),
        name="even_proj",
    )(h_lat, h_ctx, mod, tab, gains, win, wuq, wukv)


def _rms_halves(x, lane):
    x2 = x * x
    s_lo = jnp.sum(jnp.where(lane < 64, x2, 0.0), axis=-1, keepdims=True)
    s_hi = jnp.sum(jnp.where(lane < 64, 0.0, x2), axis=-1, keepdims=True)
    r = jnp.where(lane < 64, lax.rsqrt(s_lo * (1.0 / C_HD) + NORM_EPS), lax.rsqrt(s_hi * (1.0 / C_HD) + NORM_EPS))
    return x * r


def _odd_proj_kernel(slot_ref, slot_next_ref, hn_ref, rt_ref, mod_prev_ref, mod_ref, tab_ref, g_ref, win_ref,
                     ys_hbm, h_ref, qc_ref, kc_ref, vc_ref, qd_ref, kd_ref, vd_ref, ybuf, sems):
    d = D_MODEL
    n_steps = pl.num_programs(0) * pl.num_programs(1)
    step = pl.program_id(0) * pl.num_programs(1) + pl.program_id(1)
    cur = step % 2

    @pl.when(step == 0)
    def _():
        _start_row_gathers(slot_ref, ys_hbm, ybuf.at[0], sems.at[0])

    @pl.when(step < n_steps)
    def _():
        _start_row_gathers(slot_next_ref, ys_hbm, ybuf.at[1 - cur], sems.at[1 - cur])

    _wait_row_gathers(ys_hbm, ybuf.at[cur], sems.at[cur])
    rt = rt_ref[0]
    y = rt[:, 2:3] * _unpack_bf16_pairs(ybuf[cur, 0]) + rt[:, 3:4] * _unpack_bf16_pairs(ybuf[cur, 1])
    x = hn_ref[0] + mod_prev_ref[0, :, 5 * d:6 * d] * y
    h_ref[0] = x
    a = _modulate(x, mod_ref[0, :, 0:d], mod_ref[0, :, d:2 * d]).astype(BF16)
    lane = lax.broadcasted_iota(jnp.int32, (ROW_TILE, LANES), 1)
    cos128, sin128 = tab_ref[:, 0:128], tab_ref[:, 128:256]
    cos64, sin64 = tab_ref[:, 256:384], tab_ref[:, 384:512]

    z = _dot(a, win_ref[:, 0:1024])
    g_q = g_ref[0:1, :]
    for h in range(C_HEADS):
        blk = _rope(_rms_halves(z[:, h * 128:(h + 1) * 128], lane) * g_q, cos64, sin64, 16, lane)
        qc_ref[0, :, h * 256:h * 256 + 128] = jnp.where(lane < 64, blk, 0.0).astype(BF16)
        qc_ref[0, :, h * 256 + 128:(h + 1) * 256] = jnp.where(lane < 64, 0.0, blk).astype(BF16)
    z = _dot(a, win_ref[:, 1024:2048])
    g_k = g_ref[1:2, :]
    for h in range(C_HEADS):
        blk = _rope(_rms_halves(z[:, h * 128:(h + 1) * 128], lane) * g_k, cos64, sin64, 16, lane)
        kc_ref[0, :, h * 128:(h + 1) * 128] = blk.astype(BF16)
    vc_ref[0] = _dot(a, win_ref[:, 2048:3072]).astype(BF16)

    z = _dot(a, win_ref[:, 3072:4096])
    g_q = g_ref[2:3, :]
    for h in range(D_HEADS):
        blk = _rms(z[:, h * 128:(h + 1) * 128], HEAD_DIM) * g_q
        qd_ref[0, :, h * 128:(h + 1) * 128] = _rope(blk, cos128, sin128, 32, lane).astype(BF16)
    z = _dot(a, win_ref[:, 4096:4608])
    g_k = g_ref[3:4, :]
    for h in range(D_KV_HEADS):
        blk = _rms(z[:, h * 128:(h + 1) * 128], HEAD_DIM) * g_k
        kd_ref[0, :, h * 128:(h + 1) * 128] = _rope(blk, cos128, sin128, 32, lane).astype(BF16)
    vd_ref[0] = z[:, 256:512].astype(BF16)

    @pl.when(step == n_steps - 1)
    def _():
        _wait_row_gathers(ys_hbm, ybuf.at[1 - cur], sems.at[1 - cur])


def _odd_proj_call(hn, ys, slots, rt, mod_prev, mod, tab, gains, win):
    bsz = hn.shape[0]
    widths = (C_HEADS * 2 * LANES, C_HEADS * LANES, C_HEADS * C_V,
              D_HEADS * HEAD_DIM, D_KV_HEADS * HEAD_DIM, D_KV_HEADS * HEAD_DIM)
    n_steps = bsz * N_ROW_TILES
    tile = lambda w: pl.BlockSpec((1, ROW_TILE, w), lambda b, j: (b, j, 0))
    mod_spec = pl.BlockSpec((1, 1, 6 * D_MODEL), lambda b, j: (_mod_row(b, j), 0, 0))
    slot_spec = lambda ahead: pl.BlockSpec(
        (1, TOP_K, ROW_TILE), lambda b, j: (jnp.minimum(b * N_ROW_TILES + j + ahead, n_steps - 1), 0, 0),
        memory_space=pltpu.SMEM)
    return pl.pallas_call(
        _odd_proj_kernel,
        grid=(bsz, N_ROW_TILES),
        in_specs=[
            slot_spec(0), slot_spec(1), tile(D_MODEL), tile(LANES), mod_spec, mod_spec,
            pl.BlockSpec((ROW_TILE, 768), lambda b, j: (j, 0)),
            _resident(gains.shape), _resident(win.shape),
            pl.BlockSpec(memory_space=pl.ANY),
        ],
        out_specs=[tile(D_MODEL)] + [tile(w) for w in widths],
        out_shape=[jax.ShapeDtypeStruct((bsz, TOK, D_MODEL), F32)]
                  + [jax.ShapeDtypeStruct((bsz, TOK, w), BF16) for w in widths],
        scratch_shapes=[pltpu.VMEM((2, TOP_K, ROW_TILE, PACKED_W), U32), pltpu.SemaphoreType.DMA((2,))],
        compiler_params=pltpu.CompilerParams(
            dimension_semantics=("arbitrary", "arbitrary"), vmem_limit_bytes=VMEM_LIMIT),
        name="odd_proj",
    )(slots, slots, hn, rt, mod_prev, mod, tab, gains, win, ys)


KEY_CHUNK = 768


def _attn_kernel(*refs, nkv, g, dk, dv, tq, n_keys, diff_lambda_init):
    if diff_lambda_init is None:
        q_ref, k_ref, v_ref, o_ref = refs
    else:
        q_ref, k_ref, v_ref, aux_ref, o_ref = refs
    for kv in range(nkv):
        q = jnp.concatenate(
            [q_ref[0, :, (kv * g + gi) * dk:(kv * g + gi + 1) * dk] for gi in range(g)], axis=0)
        m = l = acc = None
        for c0 in range(0, n_keys, KEY_CHUNK):
            c1 = min(c0 + KEY_CHUNK, n_keys)
            s = _dot_nt(q, k_ref[0, c0:c1, kv * dk:(kv + 1) * dk])
            v = v_ref[0, c0:c1, kv * dv:(kv + 1) * dv]
            m_c = jnp.max(s, axis=-1, keepdims=True)
            if m is None:
                m = m_c
                p = jnp.exp2((s - m).astype(BF16))
                l = jnp.sum(p.astype(F32), axis=-1, keepdims=True)
                acc = _dot(p, v)
            else:
                m_new = jnp.maximum(m, m_c)
                alpha = jnp.exp2(m - m_new)
                p = jnp.exp2((s - m_new).astype(BF16))
                l = alpha * l + jnp.sum(p.astype(F32), axis=-1, keepdims=True)
                acc = alpha * acc + _dot(p, v)
                m = m_new
        o = acc / l
        if diff_lambda_init is None:
            for gi in range(g):
                o_ref[0, :, (kv * g + gi) * dv:(kv * g + gi + 1) * dv] = o[gi * tq:(gi + 1) * tq].astype(BF16)
        else:
            lam = (jnp.exp(jnp.sum(aux_ref[0:1, :] * aux_ref[1:2, :], axis=-1, keepdims=True))
                   - jnp.exp(jnp.sum(aux_ref[2:3, :] * aux_ref[3:4, :], axis=-1, keepdims=True))
                   + diff_lambda_init)
            od = o[0:tq] - lam * o[tq:2 * tq]
            od = _rms(od, dv) * aux_ref[4:5, :] * (1.0 - diff_lambda_init)
            o_ref[0, :, kv * dv:(kv + 1) * dv] = od.astype(BF16)


def _attn_call(q, k, v, *, n_heads_kv, g, dk, dv, nkv, tq, q_tile0, n_q_tiles, key_tile0, n_keys,
               aux=None, diff_lambda_init=None, name="attn"):
    bsz = q.shape[0]
    n_out_heads = n_heads_kv * (g if diff_lambda_init is None else 1)
    out_w = nkv * (g if diff_lambda_init is None else 1) * dv
    kern = functools.partial(_attn_kernel, nkv=nkv, g=g, dk=dk, dv=dv, tq=tq, n_keys=n_keys,
                             diff_lambda_init=diff_lambda_init)
    in_specs = [
        pl.BlockSpec((1, tq, nkv * g * dk), lambda b, hh, qi: (b, qi + q_tile0, hh)),
        pl.BlockSpec((1, n_keys, nkv * dk), lambda b, hh, qi: (b, key_tile0, hh)),
        pl.BlockSpec((1, n_keys, nkv * dv), lambda b, hh, qi: (b, key_tile0, hh)),
    ]
    args = [q, k, v]
    if aux is not None:
        in_specs.append(pl.BlockSpec(aux.shape, lambda b, hh, qi: (0, 0)))
        args.append(aux)
    return pl.pallas_call(
        kern,
        grid=(bsz, n_heads_kv // nkv, n_q_tiles),
        in_specs=in_specs,
        out_specs=pl.BlockSpec((1, tq, out_w), lambda b, hh, qi: (b, qi, hh)),
        out_shape=jax.ShapeDtypeStruct((bsz, n_q_tiles * tq, n_out_heads * dv), BF16),
        compiler_params=pltpu.CompilerParams(
            dimension_semantics=("parallel", "parallel", "parallel"), vmem_limit_bytes=VMEM_LIMIT),
        name=name,
    )(*args)


WIN_TQ = 256
WIN_BAND = WIN_TQ + 2 * WINDOW
NEG_BIG = -1e30


WIN_Q_TILES = SEQ // WIN_TQ


def _window_band(tile):
    q0 = (tile % WIN_Q_TILES) * WIN_TQ
    return q0, pl.multiple_of(jnp.clip(q0 - WINDOW, 0, SEQ - WIN_BAND), WINDOW)


def _window_kernel(q_ref, k_ref, v_ref, sink_ref, o_ref, sloc0, sctx0, sloc1, sctx1, *, n_total):
    g = D_HEADS // D_KV_HEADS
    dk = HEAD_DIM
    s = pl.program_id(0)

    @pl.when(s == 0)
    def _():
        sloc1[...] = jnp.zeros_like(sloc1)
        sctx1[...] = jnp.zeros_like(sctx1)

    def step(write, read):
        sloc_w, sctx_w = write
        sloc_r, sctx_r = read
        q0, start = _window_band(jnp.minimum(s, n_total - 1))
        q = jnp.concatenate([q_ref[0, :, gi * dk:(gi + 1) * dk] for gi in range(g)], axis=0)
        row = (lax.broadcasted_iota(jnp.int32, (g * WIN_TQ, WIN_BAND), 0) & (WIN_TQ - 1)) + q0
        col = lax.broadcasted_iota(jnp.int32, (g * WIN_TQ, WIN_BAND), 1) + start
        sloc_w[...] = jnp.where(jnp.abs(row - col) <= WINDOW,
                                _dot_nt(q, k_ref[0, pl.ds(start, WIN_BAND), :]), NEG_BIG)
        sctx_w[...] = _dot_nt(q, k_ref[0, SEQ:TOK, :])

        _, start = _window_band(jnp.maximum(s - 1, 0))
        s_loc = sloc_r[...]
        s_ctx = sctx_r[...]
        sink = sink_ref[0]
        m = jnp.maximum(jnp.maximum(jnp.max(s_loc, axis=-1, keepdims=True),
                                    jnp.max(s_ctx, axis=-1, keepdims=True)), sink)
        p_loc = jnp.exp2(s_loc - m)
        p_ctx = jnp.exp2(s_ctx - m)
        l = (jnp.sum(p_loc, axis=-1, keepdims=True) + jnp.sum(p_ctx, axis=-1, keepdims=True)
             + jnp.exp2(sink - m))
        acc = (_dot(p_ctx.astype(BF16), v_ref[0, SEQ:TOK, :])
               + _dot(p_loc.astype(BF16), v_ref[0, pl.ds(start, WIN_BAND), :]))
        o = acc / l
        for gi in range(g):
            o_ref[0, :, gi * dk:(gi + 1) * dk] = o[gi * WIN_TQ:(gi + 1) * WIN_TQ].astype(BF16)

    @pl.when(s % 2 == 0)
    def _():
        step((sloc0, sctx0), (sloc1, sctx1))

    @pl.when(s % 2 == 1)
    def _():
        step((sloc1, sctx1), (sloc0, sctx0))


def _window_call(q, k, v, sink_col):
    bsz = q.shape[0]
    g = D_HEADS // D_KV_HEADS
    n_total = bsz * D_KV_HEADS * WIN_Q_TILES
    split = lambda t: (t // (D_KV_HEADS * WIN_Q_TILES), (t // WIN_Q_TILES) % D_KV_HEADS, t % WIN_Q_TILES)
    stage_a = lambda s: split(jnp.minimum(s, n_total - 1))
    stage_b = lambda s: split(jnp.maximum(s - 1, 0))

    def q_index(s):
        b, hh, qi = stage_a(s)
        return (b, qi, hh)

    def k_index(s):
        b, hh, _ = stage_a(s)
        return (b, 0, hh)

    def v_index(s):
        b, hh, _ = stage_b(s)
        return (b, 0, hh)

    def o_index(s):
        b, hh, qi = stage_b(s)
        return (b, qi, hh)

    m_rows = g * WIN_TQ
    return pl.pallas_call(
        functools.partial(_window_kernel, n_total=n_total),
        grid=(n_total + 1,),
        in_specs=[
            pl.BlockSpec((1, WIN_TQ, g * HEAD_DIM), q_index),
            pl.BlockSpec((1, TOK, HEAD_DIM), k_index),
            pl.BlockSpec((1, TOK, HEAD_DIM), v_index),
            pl.BlockSpec((1, m_rows, 1), lambda s: (stage_b(s)[1], 0, 0)),
        ],
        out_specs=pl.BlockSpec((1, WIN_TQ, g * HEAD_DIM), o_index),
        out_shape=jax.ShapeDtypeStruct((bsz, SEQ, D_HEADS * HEAD_DIM), BF16),
        scratch_shapes=[pltpu.VMEM((m_rows, WIN_BAND), F32), pltpu.VMEM((m_rows, CTX_LEN), F32),
                        pltpu.VMEM((m_rows, WIN_BAND), F32), pltpu.VMEM((m_rows, CTX_LEN), F32)],
        compiler_params=pltpu.CompilerParams(
            dimension_semantics=("arbitrary",), vmem_limit_bytes=VMEM_LIMIT),
        name="window_attn",
    )(q, k, v, sink_col)


def _route(logits, lane):
    lane_f = lane.astype(F32)
    lg = jnp.where(lane < N_GROUPS, logits, NEG_BIG)
    g_max = jnp.max(lg, axis=-1, keepdims=True)
    p_top = 1.0 / jnp.sum(jnp.exp(lg - g_max), axis=-1, keepdims=True)
    g_idx = jnp.min(jnp.where(lg == g_max, lane_f, float(LANES)), axis=-1, keepdims=True)
    e_lane = lane - N_GROUPS
    in_group = (e_lane >= 0) & (e_lane < N_EXPERTS) & ((e_lane // EXPERTS_PER_GROUP).astype(F32) == g_idx)
    le = jnp.where(in_group, logits, NEG_BIG)
    m1 = jnp.max(le, axis=-1, keepdims=True)
    i1 = jnp.min(jnp.where(le == m1, lane_f, float(LANES)), axis=-1, keepdims=True)
    le2 = jnp.where(lane_f == i1, NEG_BIG, le)
    m2 = jnp.max(le2, axis=-1, keepdims=True)
    i2 = jnp.min(jnp.where(le2 == m2, lane_f, float(LANES)), axis=-1, keepdims=True)
    t = jnp.exp(m2 - m1)
    w1 = p_top / (1.0 + t)
    w2 = p_top * t / (1.0 + t)
    return i1, i2, w1, w2


def _outproj_kernel(*refs, has_ctx, n_tiles, n_total):
    if has_ctx:
        (o1_ref, o2_ref, h_ref, o1c_ref, o2c_ref, hc_ref, mod_ref, w1_ref, w2_ref, rw_ref, rb_ref,
         hn_ref, fx_ref, rt_ref, rtt_ref, cnt_ref, run_ref, logit_ref) = refs
    else:
        (o1_ref, o2_ref, h_ref, mod_ref, w1_ref, w2_ref, rw_ref, rb_ref,
         hn_ref, fx_ref, rt_ref, rtt_ref, cnt_ref, run_ref, logit_ref) = refs
    d = D_MODEL
    s = pl.program_id(0)

    @pl.when(s == 0)
    def _():
        run_ref[...] = jnp.zeros_like(run_ref)
        logit_ref[...] = jnp.zeros_like(logit_ref)

    logits = logit_ref[...]
    lane = lax.broadcasted_iota(jnp.int32, (ROW_TILE, LANES), 1)
    lane_f = lane.astype(F32)
    i1, i2, w1, w2 = _route(logits, lane)

    o1, o2, h = o1_ref[0], o2_ref[0], h_ref[0]
    if has_ctx:
        is_ctx = jnp.minimum(s, n_total - 1) % n_tiles == CTX_TILE
        o1 = jnp.where(is_ctx, o1c_ref[0], o1)
        o2 = jnp.where(is_ctx, o2c_ref[0], o2)
        h = jnp.where(is_ctx, hc_ref[0], h)
    acc = _dot(o1, w1_ref[...]) + _dot(o2, w2_ref[...])
    hn = h + mod_ref[0, :, 2 * d:3 * d] * acc
    hn_ref[0] = hn
    fx = _modulate(hn, mod_ref[0, :, 3 * d:4 * d], mod_ref[0, :, 4 * d:5 * d])
    fx_ref[0] = _pack_bf16_pairs(fx)
    hi = fx.astype(BF16)
    lo = (fx - hi.astype(F32)).astype(BF16)
    part = _dot(jnp.concatenate([hi, lo], axis=0), rw_ref[...])
    logit_ref[...] = (part[:ROW_TILE, :LANES] + part[:ROW_TILE, LANES:]
                      + part[ROW_TILE:, :LANES] + part[ROW_TILE:, LANES:] + rb_ref[...])

    hit1, hit2 = lane_f == i1, lane_f == i2
    multi_hot = jnp.where(s > 0, jnp.where(hit1, 1.0, 0.0) + jnp.where(hit2, 1.0, 0.0), 0.0)
    tri = (lax.broadcasted_iota(jnp.int32, (ROW_TILE, ROW_TILE), 0)
           > lax.broadcasted_iota(jnp.int32, (ROW_TILE, ROW_TILE), 1))
    before = _dot(jnp.where(tri, 1.0, 0.0).astype(BF16), multi_hot.astype(BF16)) + run_ref[...]
    r1 = jnp.sum(jnp.where(hit1, before, 0.0), axis=-1, keepdims=True)
    r2 = jnp.sum(jnp.where(hit2, before, 0.0), axis=-1, keepdims=True)
    run_ref[...] = run_ref[...] + jnp.sum(multi_hot, axis=0, keepdims=True)
    cnt_ref[...] = jnp.broadcast_to(run_ref[...], cnt_ref.shape)

    out = jnp.where(lane == 0, i1 - N_GROUPS, 0.0)
    out = jnp.where(lane == 1, i2 - N_GROUPS, out)
    out = jnp.where(lane == 2, w1, out)
    out = jnp.where(lane == 3, w2, out)
    out = jnp.where(lane == 4, r1, out)
    out = jnp.where(lane == 5, r2, out)
    rt_ref[0] = out
    rtt_ref[0] = jnp.transpose(out)[0:8, :]


def _outproj_call(lat, ctx, mod, w1, w2, rw, rb, n_tiles):
    bsz = lat[2].shape[0]
    has_ctx = ctx is not None
    n_total = bsz * n_tiles
    stage_a = lambda s: jnp.minimum(s, n_total - 1)
    stage_b = lambda s: jnp.maximum(s - 1, 0)
    bj = lambda t: (t // n_tiles, t % n_tiles)

    def a_tile(w, kind):
        def index(s):
            b, j = bj(stage_a(s))
            return (b, {"all": j, "lat": jnp.minimum(j, CTX_TILE - 1), "ctx": 0}[kind], 0)
        return pl.BlockSpec((1, ROW_TILE, w), index)

    def b_tile(w):
        return pl.BlockSpec((1, ROW_TILE, w), lambda s: (*bj(stage_b(s)), 0))

    in_specs = [a_tile(a.shape[-1], "lat") for a in lat]
    args = list(lat)
    if has_ctx:
        in_specs += [a_tile(a.shape[-1], "ctx") for a in ctx]
        args += list(ctx)
    in_specs += [
        pl.BlockSpec((1, 1, 6 * D_MODEL), lambda s: (_mod_row(*bj(stage_a(s))), 0, 0)),
        _resident(w1.shape), _resident(w2.shape), _resident(rw.shape), _resident(rb.shape),
    ]
    args += [mod, w1, w2, rw, rb]
    n_rows = n_tiles * ROW_TILE
    return pl.pallas_call(
        functools.partial(_outproj_kernel, has_ctx=has_ctx, n_tiles=n_tiles, n_total=n_total),
        grid=(n_total + 1,),
        in_specs=in_specs,
        out_specs=[a_tile(D_MODEL, "all"), a_tile(PACKED_W, "all"), b_tile(LANES),
                   pl.BlockSpec((1, 8, ROW_TILE), lambda s: (stage_b(s), 0, 0)),
                   pl.BlockSpec((8, LANES), lambda s: (0, 0))],
        out_shape=[jax.ShapeDtypeStruct((bsz, n_rows, D_MODEL), F32),
                   jax.ShapeDtypeStruct((bsz, n_rows, PACKED_W), U32),
                   jax.ShapeDtypeStruct((bsz, n_rows, LANES), F32),
                   jax.ShapeDtypeStruct((n_total, 8, ROW_TILE), F32),
                   jax.ShapeDtypeStruct((8, LANES), F32)],
        scratch_shapes=[pltpu.VMEM((1, LANES), F32), pltpu.VMEM((ROW_TILE, LANES), F32)],
        compiler_params=pltpu.CompilerParams(
            dimension_semantics=("arbitrary",), vmem_limit_bytes=VMEM_LIMIT),
        name="outproj_router",
    )(*args)


def _dispatch_kernel(lo_ref, hi_ref, slot_ref, fx_ref, xs_hbm, zbuf, stage, sems, zsem):
    n_steps = pl.num_programs(0) * pl.num_programs(1)
    step = pl.program_id(0) * pl.num_programs(1) + pl.program_id(1)
    cur = step % 2
    src = stage.at[cur]
    src[...] = fx_ref[0]
    for r in range(ROW_TILE):
        for k in range(TOP_K):
            pltpu.make_async_copy(src.at[pl.ds(r, 1)], xs_hbm.at[pl.ds(slot_ref[0, k, r], 1)],
                                  sems.at[cur]).start(priority=k % 2)

    def wait_tile(slot):
        for _ in range(TOP_K):
            pltpu.make_async_copy(stage.at[slot], xs_hbm.at[pl.ds(0, ROW_TILE)], sems.at[slot]).wait()

    @pl.when(step > 0)
    def _():
        wait_tile(1 - cur)

    @pl.when(step == n_steps - 1)
    def _():
        zbuf[...] = jnp.zeros_like(zbuf)

        rows8 = zbuf.shape[0]

        def one(s, start):
            cp = pltpu.make_async_copy(zbuf.at[pl.ds(0, 1)], xs_hbm.at[pl.ds(s if start else 0, 1)], zsem)
            cp.start() if start else cp.wait()

        def eight(s, start):
            at = pl.multiple_of(s, rows8) if start else 0
            cp = pltpu.make_async_copy(zbuf, xs_hbm.at[pl.ds(at, rows8)], zsem)
            cp.start() if start else cp.wait()

        def sweep(start):
            def per_expert(e, carry):
                lo, hi = lo_ref[e], hi_ref[e]
                head_end = jnp.minimum((lo + rows8 - 1) // rows8 * rows8, hi)
                n_groups = (hi - head_end) // rows8
                body_end = head_end + n_groups * rows8
                def single(s, c):
                    one(s, start)
                    return c

                def group(i, c):
                    eight(head_end + i * rows8, start)
                    return c
                lax.fori_loop(lo, head_end, single, 0)
                lax.fori_loop(0, n_groups, group, 0)
                lax.fori_loop(body_end, hi, single, 0)
                return carry
            lax.fori_loop(0, N_EXPERTS, per_expert, 0)

        sweep(True)
        sweep(False)
        wait_tile(cur)


def _dispatch_call(fx, slots, fill_lo, fill_hi, n_blocks):
    bsz, n_rows, _ = fx.shape
    n_tiles = n_rows // ROW_TILE
    grid_spec = pltpu.PrefetchScalarGridSpec(
        num_scalar_prefetch=2,
        grid=(bsz, n_tiles),
        in_specs=[
            pl.BlockSpec((1, TOP_K, ROW_TILE), lambda b, j, lo, hi: (b * n_tiles + j, 0, 0),
                         memory_space=pltpu.SMEM),
            pl.BlockSpec((1, ROW_TILE, PACKED_W), lambda b, j, lo, hi: (b, j, 0)),
        ],
        out_specs=pl.BlockSpec(memory_space=pl.ANY),
        scratch_shapes=[pltpu.VMEM((8, PACKED_W), U32), pltpu.VMEM((2, ROW_TILE, PACKED_W), U32),
                        pltpu.SemaphoreType.DMA((2,)), pltpu.SemaphoreType.DMA(())],
    )
    return pl.pallas_call(
        _dispatch_kernel,
        grid_spec=grid_spec,
        out_shape=jax.ShapeDtypeStruct((n_blocks * MOE_BLOCK, PACKED_W), U32),
        compiler_params=pltpu.CompilerParams(
            dimension_semantics=("arbitrary", "arbitrary"), vmem_limit_bytes=VMEM_LIMIT),
        name="moe_dispatch",
    )(fill_lo, fill_hi, slots, fx)


CAST_ROWS = 256


def _moe_kernel(be_ref, nu_ref, nx_ref, x_ref, wg_hbm, wu_hbm, wd_hbm, y_ref,
                sg, su, sd, wg, wu, wd, sems, *, layer):
    i = pl.program_id(0)
    e = be_ref[i]
    used = i < nu_ref[0]
    staged = ((wg_hbm, sg, wg, 0), (wu_hbm, su, wu, 1), (wd_hbm, sd, wd, 2))

    def fetch(expert):
        for hbm, stage, _, s in staged:
            pltpu.make_async_copy(hbm.at[layer, expert], stage, sems.at[s]).start(priority=1)

    @pl.when(i == 0)
    def _():
        fetch(e)

    first_of_expert = (i == 0) | (e != be_ref[jnp.maximum(i - 1, 0)])

    @pl.when(used & first_of_expert)
    def _():
        for hbm, stage, dst, s in staged:
            pltpu.make_async_copy(hbm.at[layer, 0], stage, sems.at[s]).wait()

            def cast(c, carry):
                rows = pl.ds(pl.multiple_of(c * CAST_ROWS, CAST_ROWS), CAST_ROWS)
                dst[rows, :] = stage[rows, :].astype(BF16)
                return carry
            lax.fori_loop(0, stage.shape[0] // CAST_ROWS, cast, 0)
        nxt = nx_ref[e]

        @pl.when(nxt >= 0)
        def _():
            fetch(nxt)

    @pl.when(used)
    def _():
        x = _unpack_bf16_pairs(x_ref[...]).astype(BF16)
        gt = _dot(x, wg[...])
        up = _dot(x, wu[...])
        u = (_silu(gt) * up).astype(BF16)
        y_ref[...] = _pack_bf16_pairs(_dot(u, wd[...]))


def _moe_call(xs, block_expert, n_used, next_expert, wg, wu, wd, layer):
    n_blocks = block_expert.shape[0]
    row_blk = pl.BlockSpec((MOE_BLOCK, PACKED_W), lambda i, be, nu, nx: (jnp.minimum(i, nu[0] - 1), 0))
    hbm = pl.BlockSpec(memory_space=pl.ANY)
    grid_spec = pltpu.PrefetchScalarGridSpec(
        num_scalar_prefetch=3,
        grid=(n_blocks,),
        in_specs=[row_blk, hbm, hbm, hbm],
        out_specs=row_blk,
        scratch_shapes=[
            pltpu.VMEM((D_MODEL, D_EXPERT), F32), pltpu.VMEM((D_MODEL, D_EXPERT), F32),
            pltpu.VMEM((D_EXPERT, D_MODEL), F32),
            pltpu.VMEM((D_MODEL, D_EXPERT), BF16), pltpu.VMEM((D_MODEL, D_EXPERT), BF16),
            pltpu.VMEM((D_EXPERT, D_MODEL), BF16),
            pltpu.SemaphoreType.DMA((3,)),
        ],
    )
    return pl.pallas_call(
        functools.partial(_moe_kernel, layer=layer),
        grid_spec=grid_spec,
        out_shape=jax.ShapeDtypeStruct((n_blocks * MOE_BLOCK, PACKED_W), U32),
        compiler_params=pltpu.CompilerParams(
            dimension_semantics=("arbitrary",), vmem_limit_bytes=VMEM_LIMIT),
        name="moe_experts",
    )(block_expert, n_used, next_expert, xs, wg, wu, wd)


def _combine_kernel(slot_ref, slot_next_ref, h_ref, rt_ref, mod_ref, ys_hbm, o_ref, ybuf, sems):
    d = D_MODEL
    n_steps = pl.num_programs(0) * pl.num_programs(1)
    step = pl.program_id(0) * pl.num_programs(1) + pl.program_id(1)
    cur = step % 2

    @pl.when(step == 0)
    def _():
        _start_row_gathers(slot_ref, ys_hbm, ybuf.at[0], sems.at[0])

    _start_row_gathers(slot_next_ref, ys_hbm, ybuf.at[1 - cur], sems.at[1 - cur])
    _wait_row_gathers(ys_hbm, ybuf.at[cur], sems.at[cur])
    rt = rt_ref[0]
    y = rt[:, 2:3] * _unpack_bf16_pairs(ybuf[cur, 0]) + rt[:, 3:4] * _unpack_bf16_pairs(ybuf[cur, 1])
    o_ref[0] = h_ref[0] + mod_ref[0, :, 5 * d:6 * d] * y

    @pl.when(step == n_steps - 1)
    def _():
        _wait_row_gathers(ys_hbm, ybuf.at[1 - cur], sems.at[1 - cur])


def _combine_call(h, ys, slots, rt, mod, n_tiles):
    bsz = h.shape[0]
    n_steps = bsz * n_tiles
    tile = lambda w: pl.BlockSpec((1, ROW_TILE, w), lambda b, j: (b, j, 0))
    slot_spec = lambda ahead: pl.BlockSpec(
        (1, TOP_K, ROW_TILE), lambda b, j: (jnp.minimum(b * n_tiles + j + ahead, n_steps - 1), 0, 0),
        memory_space=pltpu.SMEM)
    return pl.pallas_call(
        _combine_kernel,
        grid=(bsz, n_tiles),
        in_specs=[
            slot_spec(0), slot_spec(1),
            tile(D_MODEL), tile(LANES),
            pl.BlockSpec((1, 1, 6 * D_MODEL), lambda b, j: (_mod_row(b, j), 0, 0)),
            pl.BlockSpec(memory_space=pl.ANY),
        ],
        out_specs=tile(D_MODEL),
        out_shape=jax.ShapeDtypeStruct((bsz, n_tiles * ROW_TILE, D_MODEL), F32),
        scratch_shapes=[pltpu.VMEM((2, TOP_K, ROW_TILE, PACKED_W), U32), pltpu.SemaphoreType.DMA((2,))],
        compiler_params=pltpu.CompilerParams(
            dimension_semantics=("arbitrary", "arbitrary"), vmem_limit_bytes=VMEM_LIMIT),
        name="moe_combine",
    )(slots, slots, h, rt, mod, ys)


def _rope_tables():
    rows = SEQ // GRID_W
    row = np.repeat(np.arange(rows), GRID_W).astype(np.float32)
    col = np.tile(np.arange(GRID_W), rows).astype(np.float32)

    def tables(dim):
        n_freq = dim // 4
        inv = (1.0 / (np.float32(ROPE_THETA) ** (np.arange(n_freq, dtype=np.float32) / np.float32(n_freq))))
        inv = inv.astype(np.float32)
        ang_r = row[:, None] * inv
        ang_c = col[:, None] * inv
        ang = np.concatenate([ang_r, ang_r, ang_c, ang_c], axis=-1)
        sign = np.tile(np.concatenate([-np.ones((n_freq,), np.float32), np.ones((n_freq,), np.float32)]), 2)
        return np.cos(ang).astype(np.float32), (np.sin(ang) * sign).astype(np.float32)

    cos128, sin128 = tables(HEAD_DIM)
    cos64, sin64 = tables(B_ROPE)
    ones, zeros = np.ones((SEQ, 64), np.float32), np.zeros((SEQ, 64), np.float32)
    lat = np.concatenate([cos128, sin128,
                          cos64, cos64, sin64, sin64,
                          cos64, ones, sin64, zeros], axis=-1)
    ident = np.concatenate([np.ones((CTX_LEN, 128), np.float32), np.zeros((CTX_LEN, 128), np.float32)], axis=-1)
    return jnp.asarray(np.concatenate([lat, np.tile(ident, (1, 3))], axis=0))


def _pad_lanes(v, n):
    return jnp.pad(v, (0, n - v.shape[0]))


def _router_weights(wr_g, br_g, wr_e, br_e):
    w = jnp.pad(jnp.concatenate([wr_g, wr_e], axis=1), ((0, 0), (0, LANES - N_GROUPS - N_EXPERTS)))
    hi = w.astype(BF16)
    lo = (w - hi.astype(F32)).astype(BF16)
    b = _pad_lanes(jnp.concatenate([br_g, br_e]), LANES).reshape(1, LANES)
    return jnp.concatenate([hi, lo], axis=1), b


def _moe_experts(fx, rtt, cnt, wg, wu, wd, layer):
    bsz, n_rows, _ = fx.shape
    n_blocks = -(-(bsz * n_rows * TOP_K) // MOE_BLOCK) + N_EXPERTS
    experts = jnp.arange(N_EXPERTS, dtype=jnp.int32)
    counts = cnt[0, N_GROUPS:N_GROUPS + N_EXPERTS].astype(jnp.int32)
    padded = (counts + MOE_BLOCK - 1) // MOE_BLOCK * MOE_BLOCK
    pad_end = jnp.cumsum(padded)
    pad_start = pad_end - padded
    e_idx = rtt[:, 0:TOP_K, :].astype(jnp.int32)
    rank = rtt[:, 4:4 + TOP_K, :].astype(jnp.int32)
    slots = rank + jnp.sum(jnp.where(e_idx[..., None] == experts, pad_start, 0), axis=-1)
    block_start = jnp.arange(n_blocks, dtype=jnp.int32) * MOE_BLOCK
    block_expert = jnp.minimum(jnp.sum(pad_end[None, :] <= block_start[:, None], axis=1), N_EXPERTS - 1)
    n_used = pad_end[-1:] // MOE_BLOCK
    later = (experts[None, :] > experts[:, None]) & (counts[None, :] > 0)
    next_expert = jnp.where(jnp.any(later, axis=1), jnp.argmax(later, axis=1), -1)
    i32 = lambda a: a.astype(jnp.int32)
    xs = _dispatch_call(fx, i32(slots), i32(pad_start + counts), i32(pad_end), n_blocks)
    ys = _moe_call(xs, i32(block_expert), i32(n_used), i32(next_expert), wg, wu, wd, layer)
    return ys, i32(slots)


def kernel(x, c, ctx, c_ctx, mod_w, mod_b, even_w_in, even_w_out, a_q_norm, a_k_norm, b_cq_norm, b_w_uq,
           b_ckv_norm, b_w_ukv, b_q_norm, b_k_norm, odd_w_in, odd_w_out, c_q_norm, c_k_norm, c_lambda_q1,
           c_lambda_k1, c_lambda_q2, c_lambda_k2, c_subln, d_q_norm, d_k_norm, d_sink, moe_wr_group,
           moe_br_group, moe_wr_expert, moe_br_expert, moe_w_gate, moe_w_up, moe_w_down):
    bsz = x.shape[0]
    c_all = jnp.zeros((MOD_ROWS, D_MODEL), F32).at[:bsz].set(c).at[MOD_CTX_ROW].set(c_ctx)
    mod_all = _mod_call(c_all, mod_w, mod_b)
    tab = _rope_tables()

    i = 0
    mod = mod_all[0].reshape(MOD_ROWS, 1, 6 * D_MODEL)
    scale_a = HEAD_DIM ** -0.5 * LOG2_E
    scale_b = (B_NOPE + B_ROPE) ** -0.5 * LOG2_E
    win = jnp.pad(even_w_in[i], ((0, 0), (0, EVEN_IN_PAD - even_w_in.shape[-1]))).astype(BF16)
    wuq = b_w_uq[i].reshape(B_Q_LORA, B_HEADS, B_NOPE + B_ROPE)
    wuq = jnp.pad(wuq, ((0, 0), (0, 0), (0, B_QK_PAD - B_NOPE - B_ROPE))).reshape(B_Q_LORA, -1).astype(BF16)
    wukv = b_w_ukv[i].reshape(B_KV_LORA, B_HEADS, B_NOPE + B_V)
    wukv = jnp.concatenate([wukv[:, :, :B_NOPE].reshape(B_KV_LORA, -1),
                            wukv[:, :, B_NOPE:].reshape(B_KV_LORA, -1)], axis=1).astype(BF16)
    gains = jnp.stack([
        _pad_lanes(a_q_norm[i] * scale_a, 512), _pad_lanes(a_k_norm[i], 512),
        b_cq_norm[i], b_ckv_norm[i],
        _pad_lanes(b_q_norm[i] * scale_b, 512), _pad_lanes(b_k_norm[i], 512),
        jnp.zeros((512,), F32), jnp.zeros((512,), F32)])
    qa, ka, va, qb, kb, vb = _even_proj_call(x, ctx, mod, tab, gains, win, wuq, wukv)

    g_a = A_HEADS // A_KV_HEADS
    lat = dict(q_tile0=0, key_tile0=0, n_keys=TOK)
    oa = _attn_call(qa, ka, va, n_heads_kv=A_KV_HEADS, g=g_a, dk=HEAD_DIM, dv=HEAD_DIM, nkv=2,
                    tq=256, n_q_tiles=SEQ // 256, name="gqa_latent", **lat)
    ob = _attn_call(qb, kb, vb, n_heads_kv=B_HEADS, g=1, dk=B_QK_PAD, dv=B_V, nkv=4,
                    tq=512, n_q_tiles=SEQ // 512, name="mla_latent", **lat)
    cx = dict(tq=CTX_LEN, q_tile0=SEQ // CTX_LEN, n_q_tiles=1, key_tile0=SEQ // CTX_LEN, n_keys=CTX_LEN)
    oa_c = _attn_call(qa, ka, va, n_heads_kv=A_KV_HEADS, g=g_a, dk=HEAD_DIM, dv=HEAD_DIM, nkv=2,
                      name="gqa_context", **cx)
    ob_c = _attn_call(qb, kb, vb, n_heads_kv=B_HEADS, g=1, dk=B_QK_PAD, dv=B_V, nkv=B_HEADS,
                      name="mla_context", **cx)

    w_out = even_w_out[i].astype(BF16)
    rw, rb = _router_weights(moe_wr_group[0], moe_br_group[0], moe_wr_expert[0], moe_br_expert[0])
    hn, fx, rt, rtt, cnt = _outproj_call((oa, ob, x), (oa_c, ob_c, ctx), mod,
                                         w_out[:A_HEADS * HEAD_DIM], w_out[A_HEADS * HEAD_DIM:],
                                         rw, rb, N_ROW_TILES)
    ys, slots = _moe_experts(fx, rtt, cnt, moe_w_gate, moe_w_up, moe_w_down, 0)
    hn0, rt0, mod0 = hn, rt, mod

    layer = 1
    mod = mod_all[1].reshape(MOD_ROWS, 1, 6 * D_MODEL)
    lambda_init = 0.8 - 0.6 * math.exp(-0.3 * layer)
    scale_c = C_HD ** -0.5 * LOG2_E
    scale_d = HEAD_DIM ** -0.5 * LOG2_E
    win = odd_w_in[i].astype(BF16)
    gains = jnp.stack([
        jnp.tile(c_q_norm[i] * scale_c, 2), jnp.tile(c_k_norm[i], 2),
        d_q_norm[i] * scale_d, d_k_norm[i],
        jnp.zeros((128,), F32), jnp.zeros((128,), F32), jnp.zeros((128,), F32), jnp.zeros((128,), F32)])
    h, qc, kc, vc, qd, kd, vd = _odd_proj_call(hn0, ys, slots, rt0, mod0, mod, tab, gains, win)

    aux = jnp.stack([
        _pad_lanes(c_lambda_q1[i], 128), _pad_lanes(c_lambda_k1[i], 128),
        _pad_lanes(c_lambda_q2[i], 128), _pad_lanes(c_lambda_k2[i], 128),
        c_subln[i], jnp.zeros((128,), F32), jnp.zeros((128,), F32), jnp.zeros((128,), F32)])
    oc = _attn_call(qc, kc, vc, n_heads_kv=C_HEADS, g=2, dk=LANES, dv=C_V, nkv=4,
                    tq=512, n_q_tiles=SEQ // 512, aux=aux, diff_lambda_init=lambda_init,
                    name="diff_latent", **lat)
    g_d = D_HEADS // D_KV_HEADS
    sink_col = jnp.repeat((d_sink[i] * LOG2_E).reshape(D_KV_HEADS, g_d), WIN_TQ, axis=1)
    sink_col = sink_col.reshape(D_KV_HEADS, g_d * WIN_TQ, 1)
    od = _window_call(qd, kd, vd, sink_col)

    w_out = odd_w_out[i].astype(BF16)
    rw, rb = _router_weights(moe_wr_group[1], moe_br_group[1], moe_wr_expert[1], moe_br_expert[1])
    n_lat_tiles = SEQ // ROW_TILE
    hn, fx, rt, rtt, cnt = _outproj_call((oc, od, h), None, mod, w_out[:C_HEADS * C_V], w_out[C_HEADS * C_V:],
                                         rw, rb, n_lat_tiles)
    ys, slots = _moe_experts(fx, rtt, cnt, moe_w_gate, moe_w_up, moe_w_down, 1)
    return _combine_call(hn, ys, slots, rt, mod, n_lat_tiles)
```
